```python
import jax, jax.numpy as jnp
from jax import lax
import numpy as np

D_MODEL = 1024
BATCH = 2
SEQ = 16384
DEPTH = 1
DEC_BATCH = 8
DEC_SEQ = 4096
PAST_LEN = 128

FOURIER_WIDTH = D_MODEL // 4
FOURIER_GROUPS = 4
FOURIER_GROUP_DIM = FOURIER_WIDTH // FOURIER_GROUPS
HGRN_WIDTH = D_MODEL - FOURIER_WIDTH
HGRN_HEAD_DIM = 128
HGRN_HEADS = HGRN_WIDTH // HGRN_HEAD_DIM
CHUNK = 64
IN_WIDTH = FOURIER_WIDTH + 5 * HGRN_WIDTH
N_EXPERTS = 32
TOP_K = 4
D_FF = D_MODEL
SWIGLU_LIMIT = 7.0
SWIGLU_ALPHA = 1.702
MOE_BLOCK = 256
LN_EPS = 1e-5
RMS_EPS = 1e-6
DEEPNORM_ALPHA = (2 * DEPTH) ** 0.25
DEEPNORM_BETA = (8 * DEPTH) ** -0.25

kernel_name = "hybrid_fnet_hgrn2_moe_encoder"


def layer_norm(x, g, b):
    xf = x.astype(jnp.float32)
    mu = xf.mean(-1, keepdims=True)
    var = jnp.square(xf - mu).mean(-1, keepdims=True)
    return ((xf - mu) * lax.rsqrt(var + LN_EPS) * g.astype(jnp.float32) + b.astype(jnp.float32)).astype(x.dtype)


def fourier_mix(u, g):
    B, L, _ = u.shape
    uf = u.astype(jnp.float32).reshape(B, L, FOURIER_GROUPS, FOURIER_GROUP_DIM)
    z = jnp.fft.fft2(uf, axes=(1, 3), norm="ortho").real.reshape(B, L, FOURIER_WIDTH)
    z = z * lax.rsqrt(jnp.mean(jnp.square(z), -1, keepdims=True) + RMS_EPS) * g.astype(jnp.float32)
    return z.astype(u.dtype)


def gla_chunked(q, k, v, logf):
    B, H, L, K = q.shape
    V = v.shape[-1]
    N = L // CHUNK
    q, k, v, logf = (t.reshape(B, H, N, CHUNK, t.shape[-1]) for t in (q, k, v, logf))
    b = jnp.cumsum(logf, axis=3)
    b_mid = b[:, :, :, CHUNK // 2 - 1:CHUNK // 2, :]
    b_last = b[:, :, :, -1:, :]
    scores = jnp.einsum('bhntk,bhnsk->bhnts', q * jnp.exp(b - b_mid), k * jnp.exp(b_mid - b))
    causal = jnp.tril(jnp.ones((CHUNK, CHUNK), dtype=bool))
    scores = jnp.where(causal, scores, 0.0)
    o_intra = jnp.einsum('bhnts,bhnsv->bhntv', scores, v)
    delta = jnp.einsum('bhnsk,bhnsv->bhnkv', k * jnp.exp(b_last - b), v)
    decay = jnp.exp(b_last[:, :, :, 0, :])

    def step(S, inp):
        d, dl = inp
        return d[..., None] * S + dl, S

    S0 = jnp.zeros((B, H, K, V), jnp.float32)
    _, S_prev = lax.scan(step, S0, (jnp.moveaxis(decay, 2, 0), jnp.moveaxis(delta, 2, 0)))
    S_prev = jnp.moveaxis(S_prev, 0, 2)
    o_inter = jnp.einsum('bhntk,bhnkv->bhntv', q * jnp.exp(b), S_prev)
    return (o_intra + o_inter).reshape(B, H, L, V)


def hgrn2_mix(q, i, f_fwd, f_bwd, gate, lb_fwd, lb_bwd, norm_g):
    B, L, _ = q.shape
    f32 = jnp.float32

    def heads(t):
        return t.reshape(B, L, HGRN_HEADS, HGRN_HEAD_DIM).transpose(0, 2, 1, 3)

    qh = heads(jax.nn.silu(q.astype(f32))) * (HGRN_HEAD_DIM ** -0.5)
    vh = heads(i.astype(f32))

    def direction(f_pre, lb, flip):
        fg = lb + (1.0 - lb) * jax.nn.sigmoid(f_pre.astype(f32))
        kh, logf = heads(1.0 - fg), heads(jnp.log(fg))
        if flip:
            o = gla_chunked(jnp.flip(qh, 2), jnp.flip(kh, 2), jnp.flip(vh, 2), jnp.flip(logf, 2))
            return jnp.flip(o, 2)
        return gla_chunked(qh, kh, vh, logf)

    o = direction(f_fwd, lb_fwd, False) + direction(f_bwd, lb_bwd, True)
    o = o.transpose(0, 2, 1, 3)
    g = gate.astype(f32).reshape(B, L, HGRN_HEADS, HGRN_HEAD_DIM)
    o = o * lax.rsqrt(jnp.mean(jnp.square(o), -1, keepdims=True) + RMS_EPS) * norm_g.astype(f32) * jax.nn.silu(g)
    return o.reshape(B, L, HGRN_WIDTH).astype(q.dtype)


def moe(h, router_w, router_b, w_gu, b_gu, w_dn, b_dn):
    B, L, D = h.shape
    T = B * L
    xf = h.reshape(T, D)
    logits = (xf @ router_w + router_b).astype(jnp.float32)
    top_val, top_idx = lax.top_k(logits, TOP_K)
    gates = jax.nn.softmax(top_val, axis=-1)
    A = T * TOP_K
    e_flat = top_idx.reshape(A).astype(jnp.int32)
    tok_flat = jnp.repeat(jnp.arange(T, dtype=jnp.int32), TOP_K)
    g_flat = gates.reshape(A)
    order = jnp.argsort(e_flat)
    e_s, tok_s, g_s = e_flat[order], tok_flat[order], g_flat[order]
    counts = jnp.bincount(e_flat, length=N_EXPERTS)
    padded = ((counts + MOE_BLOCK - 1) // MOE_BLOCK) * MOE_BLOCK
    start = jnp.cumsum(counts) - counts
    pend = jnp.cumsum(padded)
    pstart = pend - padded
    dest = pstart[e_s] + (jnp.arange(A, dtype=jnp.int32) - start[e_s])
    P = ((A + MOE_BLOCK - 1) // MOE_BLOCK) * MOE_BLOCK + N_EXPERTS * MOE_BLOCK
    nb = P // MOE_BLOCK
    tok_buf = jnp.full((P,), T, jnp.int32).at[dest].set(tok_s)
    g_buf = jnp.zeros((P,), jnp.float32).at[dest].set(g_s)
    block_expert = jnp.searchsorted(pend, jnp.arange(nb) * MOE_BLOCK, side='right')
    block_expert = jnp.minimum(block_expert, N_EXPERTS - 1).astype(jnp.int32)
    x_pad = jnp.concatenate([xf, jnp.zeros((1, D), xf.dtype)], axis=0)
    xb = x_pad[tok_buf].reshape(nb, MOE_BLOCK, D)

    def expert_block(args):
        xblk, e = args
        gu = xblk @ w_gu[e] + b_gu[e]
        g_, up = gu[:, :D_FF], gu[:, D_FF:]
        g_ = jnp.minimum(g_, SWIGLU_LIMIT)
        up = jnp.clip(up, -SWIGLU_LIMIT, SWIGLU_LIMIT)
        act = (up + 1.0) * g_ * jax.nn.sigmoid(SWIGLU_ALPHA * g_)
        return act @ w_dn[e] + b_dn[e]

    yb = lax.map(expert_block, (xb, block_expert)).reshape(P, D)
    y = jnp.zeros((T + 1, D), jnp.float32).at[tok_buf].add(yb.astype(jnp.float32) * g_buf[:, None])
    return y[:T].astype(h.dtype).reshape(B, L, D)


def trunk(x, ln_in_g, ln_in_b, w_in, fourier_norm_g, lb_gamma, hgrn_norm_g, w_out,
          ln1_g, ln1_b, router_w, router_b, w_gate_up, b_gate_up, w_down, b_down, ln2_g, ln2_b):
    x = layer_norm(x, ln_in_g, ln_in_b)
    lb_all = jnp.cumsum(jax.nn.softmax(lb_gamma.astype(jnp.float32), axis=1), axis=1)
    FW, HW = FOURIER_WIDTH, HGRN_WIDTH
    for l in range(DEPTH):
        proj = x @ w_in[l]
        u, q, i, f_f, f_b, g = jnp.split(proj, [FW, FW + HW, FW + 2 * HW, FW + 3 * HW, FW + 4 * HW], axis=-1)
        mixed = jnp.concatenate([
            fourier_mix(u, fourier_norm_g[l]),
            hgrn2_mix(q, i, f_f, f_b, g, lb_all[0, l], lb_all[1, l], hgrn_norm_g[l]),
        ], axis=-1) @ w_out[l]
        h = layer_norm(DEEPNORM_ALPHA * x + mixed, ln1_g[l], ln1_b[l])
        ff = moe(h, router_w[l], router_b[l], w_gate_up[l], b_gate_up[l], w_down[l], b_down[l])
        x = layer_norm(DEEPNORM_ALPHA * h + ff, ln2_g[l], ln2_b[l])
    return x


def setup_inputs(seed: int = 0) -> dict:
    key = jax.random.key(seed)
    ks = jax.random.split(key, 24)
    f32 = jnp.float32
    D, E, F = D_MODEL, N_EXPERTS, D_FF
    nrm = lambda k, s: jax.random.normal(k, s, f32)
    col_scale = jnp.concatenate([
        jnp.ones((FOURIER_WIDTH + HGRN_WIDTH,), f32),
        jnp.full((HGRN_WIDTH,), DEEPNORM_BETA, f32),
        jnp.ones((3 * HGRN_WIDTH,), f32)])
    return {
        "x_prompt": nrm(ks[0], (BATCH, SEQ, D)),
        "x_sample": nrm(ks[1], (DEC_BATCH, DEC_SEQ, D)),
        "ln_in_g": 1.0 + 0.02 * nrm(ks[2], (D,)),
        "ln_in_b": 0.02 * nrm(ks[3], (D,)),
        "w_in": nrm(ks[4], (DEPTH, D, IN_WIDTH)) * (D ** -0.5) * col_scale,
        "fourier_norm_g": 1.0 + 0.02 * nrm(ks[5], (DEPTH, FOURIER_WIDTH)),
        "lb_gamma": 0.1 * nrm(ks[6], (2, DEPTH + 1, HGRN_WIDTH)),
        "hgrn_norm_g": 1.0 + 0.02 * nrm(ks[7], (DEPTH, HGRN_HEAD_DIM)),
        "w_out": nrm(ks[8], (DEPTH, D, D)) * (D ** -0.5) * DEEPNORM_BETA,
        "ln1_g": 1.0 + 0.02 * nrm(ks[9], (DEPTH, D)),
        "ln1_b": 0.02 * nrm(ks[10], (DEPTH, D)),
        "router_w": nrm(ks[11], (DEPTH, D, E)) * (D ** -0.5),
        "router_b": 0.01 * nrm(ks[12], (DEPTH, E)),
        "w_gate_up": nrm(ks[13], (DEPTH, E, D, 2 * F)) * (D ** -0.5) * DEEPNORM_BETA,
        "b_gate_up": 0.01 * nrm(ks[14], (DEPTH, E, 2 * F)),
        "w_down": nrm(ks[15], (DEPTH, E, F, D)) * (F ** -0.5) * DEEPNORM_BETA,
        "b_down": 0.01 * nrm(ks[16], (DEPTH, E, D)),
        "ln2_g": 1.0 + 0.02 * nrm(ks[17], (DEPTH, D)),
        "ln2_b": 0.02 * nrm(ks[18], (DEPTH, D)),
    }


def reference(x_prompt, x_sample, ln_in_g, ln_in_b, w_in, fourier_norm_g, lb_gamma, hgrn_norm_g, w_out,
              ln1_g, ln1_b, router_w, router_b, w_gate_up, b_gate_up, w_down, b_down, ln2_g, ln2_b):
    y_prompt = trunk(x_prompt, ln_in_g, ln_in_b, w_in, fourier_norm_g, lb_gamma, hgrn_norm_g, w_out,
                     ln1_g, ln1_b, router_w, router_b, w_gate_up, b_gate_up, w_down, b_down, ln2_g, ln2_b)
    y_sample = trunk(x_sample, ln_in_g, ln_in_b, w_in, fourier_norm_g, lb_gamma, hgrn_norm_g, w_out,
                     ln1_g, ln1_b, router_w, router_b, w_gate_up, b_gate_up, w_down, b_down, ln2_g, ln2_b)
    return (y_prompt, y_sample)
```

```python
import functools
import math

import numpy as np
import jax
import jax.numpy as jnp
from jax import lax
from jax.experimental import pallas as pl
from jax.experimental.pallas import tpu as pltpu

D_MODEL = 1024
FOURIER_WIDTH = 256
FOURIER_GROUP_DIM = 64
HGRN_WIDTH = 768
HEAD_DIM = 128
HEADS = 6
CHUNK = 64
N_EXPERTS = 32
TOP_K = 4
D_FF = 1024
SWIGLU_LIMIT = 7.0
SWIGLU_ALPHA = 1.702
MOE_BLOCK = 256
LN_EPS = 1e-5
RMS_EPS = 1e-6
DEEPNORM_ALPHA = 2.0 ** 0.25

LANES = 128
VMEM_LIMIT_BYTES = 56 * 1024 * 1024

F32 = jnp.float32
BF16 = jnp.bfloat16


def _cparams(*sem):
    return pltpu.CompilerParams(dimension_semantics=sem, vmem_limit_bytes=VMEM_LIMIT_BYTES)


def _layer_norm_rows(x, g, b):
    mu = jnp.mean(x, axis=-1, keepdims=True)
    xc = x - mu
    var = jnp.mean(xc * xc, axis=-1, keepdims=True)
    return xc * lax.rsqrt(var + LN_EPS) * g + b


IN_TM = 512
IN_TN = 512


def _in_proj_kernel(x_ref, g_ref, b_ref, w_ref, xn_ref, u_ref, qi_ref, f_ref, og_ref):
    xn = _layer_norm_rows(x_ref[...], g_ref[...], b_ref[...])
    xn_ref[...] = xn
    xb = xn.astype(BF16)
    col = 0
    for ref in (u_ref, qi_ref, f_ref, og_ref):
        width = ref.shape[1]
        for c0 in range(0, width, IN_TN):
            cw = min(IN_TN, width - c0)
            acc = jnp.dot(xb, w_ref[:, col + c0:col + c0 + cw], preferred_element_type=F32)
            ref[:, c0:c0 + cw] = acc.astype(ref.dtype)
        col += width


def _in_proj(x2d, ln_g, ln_b, w_bf16):
    t = x2d.shape[0]
    tm = min(IN_TM, t)
    widths = (FOURIER_WIDTH, 2 * HGRN_WIDTH, 2 * HGRN_WIDTH, HGRN_WIDTH)
    dtypes = (BF16, BF16, F32, BF16)
    row = lambda i: (i, 0)
    const = lambda i: (0, 0)
    return pl.pallas_call(
        _in_proj_kernel,
        grid=(t // tm,),
        in_specs=[
            pl.BlockSpec((tm, D_MODEL), row),
            pl.BlockSpec((1, D_MODEL), const),
            pl.BlockSpec((1, D_MODEL), const),
            pl.BlockSpec(w_bf16.shape, const),
        ],
        out_specs=[pl.BlockSpec((tm, D_MODEL), row)] + [pl.BlockSpec((tm, w), row) for w in widths],
        out_shape=[jax.ShapeDtypeStruct((t, D_MODEL), F32)]
        + [jax.ShapeDtypeStruct((t, w), dt) for w, dt in zip(widths, dtypes)],
        compiler_params=_cparams("parallel"),
        name="in_proj",
    )(x2d, ln_g.reshape(1, -1), ln_b.reshape(1, -1), w_bf16)


def _fft_split(seq_len):
    n1 = 1 << ((seq_len.bit_length() - 1 + 1) // 2)
    return n1, seq_len // n1


@functools.lru_cache(maxsize=None)
def _fft_tables(seq_len):
    n1, n2 = _fft_split(seq_len)
    k1 = np.arange(n1)
    ang1 = 2.0 * np.pi * ((k1[:, None] * k1[None, :]) % n1) / n1
    s1 = 1.0 / math.sqrt(n1)
    c1, s1m = np.cos(ang1) * s1, np.sin(ang1) * s1
    l1p = np.arange(n1)[:, None, None]
    l2p = np.arange(n2)[None, :, None]
    l2 = np.arange(n2)[None, None, :]
    ang2 = 2.0 * np.pi * ((l2 * (l1p + n1 * l2p)) % seq_len) / seq_len
    s2 = 1.0 / math.sqrt(n2)
    gc, gs = np.cos(ang2) * s2, np.sin(ang2) * s2
    kc = np.arange(FOURIER_GROUP_DIM)
    angc = 2.0 * np.pi * ((kc[:, None] * kc[None, :]) % FOURIER_GROUP_DIM) / FOURIER_GROUP_DIM
    sc = 1.0 / math.sqrt(FOURIER_GROUP_DIM)
    groups = FOURIER_WIDTH // FOURIER_GROUP_DIM
    bc = np.kron(np.eye(groups), np.cos(angc) * sc)
    bs = np.kron(np.eye(groups), np.sin(angc) * sc)
    as_bf16 = lambda a: jnp.asarray(a, dtype=F32).astype(BF16)
    return tuple(as_bf16(a) for a in (c1, s1m, gc, gs, bc, bs))


def _fft1_kernel(c_ref, s_ref, u_ref, ar_ref, ai_ref):
    u = u_ref[0]
    ar_ref[0] = jnp.dot(c_ref[...], u, preferred_element_type=F32).astype(BF16)
    ai_ref[0] = (-jnp.dot(s_ref[...], u, preferred_element_type=F32)).astype(BF16)


FFT1_TN = 4096


def _fft_stage1(u3, c1, s1):
    b, n1, width = u3.shape
    tn = min(FFT1_TN, width)
    blk = pl.BlockSpec((1, n1, tn), lambda i, j: (i, 0, j))
    mat = pl.BlockSpec((n1, n1), lambda i, j: (0, 0))
    return pl.pallas_call(
        _fft1_kernel,
        grid=(b, width // tn),
        in_specs=[mat, mat, blk],
        out_specs=[blk, blk],
        out_shape=[jax.ShapeDtypeStruct(u3.shape, BF16)] * 2,
        compiler_params=_cparams("parallel", "parallel"),
        name="fft_stage1",
    )(c1, s1, u3)


FFT2_T1 = 8


def _fft2_kernel(gc_ref, gs_ref, bc_ref, bs_ref, g_ref, ar_ref, ai_ref, z_ref):
    t1 = ar_ref.shape[1]
    for j in range(t1):
        ar, ai = ar_ref[0, j], ai_ref[0, j]
        gc, gs = gc_ref[j], gs_ref[j]
        xr = jnp.dot(gc, ar, preferred_element_type=F32) + jnp.dot(gs, ai, preferred_element_type=F32)
        xi = jnp.dot(gc, ai, preferred_element_type=F32) - jnp.dot(gs, ar, preferred_element_type=F32)
        z = jnp.dot(xr.astype(BF16), bc_ref[...], preferred_element_type=F32)
        z += jnp.dot(xi.astype(BF16), bs_ref[...], preferred_element_type=F32)
        z = z * lax.rsqrt(jnp.mean(z * z, axis=-1, keepdims=True) + RMS_EPS) * g_ref[...]
        z_ref[0, j] = z.astype(BF16)


def _fft_stage2(ar4, ai4, gc, gs, bc, bs, gain):
    b, n1, n2, w = ar4.shape
    t1 = min(FFT2_T1, n1)
    a_blk = pl.BlockSpec((1, t1, n2, w), lambda i, j: (i, j, 0, 0))
    g_blk = pl.BlockSpec((t1, n2, n2), lambda i, j: (j, 0, 0))
    c_blk = pl.BlockSpec((w, w), lambda i, j: (0, 0))
    return pl.pallas_call(
        _fft2_kernel,
        grid=(b, n1 // t1),
        in_specs=[g_blk, g_blk, c_blk, c_blk, pl.BlockSpec((1, w), lambda i, j: (0, 0)), a_blk, a_blk],
        out_specs=a_blk,
        out_shape=jax.ShapeDtypeStruct(ar4.shape, BF16),
        compiler_params=_cparams("parallel", "parallel"),
        name="fft_stage2",
    )(gc, gs, bc, bs, gain.reshape(1, -1), ar4, ai4)


def _fourier_mix(u2d, batch, seq_len, gain):
    n1, n2 = _fft_split(seq_len)
    c1, s1, gc, gs, bc, bs = _fft_tables(seq_len)
    u3 = u2d.reshape(batch, n1, n2 * FOURIER_WIDTH)
    ar, ai = _fft_stage1(u3, c1, s1)
    shape4 = (batch, n1, n2, FOURIER_WIDTH)
    zt = _fft_stage2(ar.reshape(shape4), ai.reshape(shape4), gc, gs, bc, bs, gain)
    return zt.reshape(batch, n1, n2 * FOURIER_WIDTH)


GLA_LB = 512


def _chunk_cumsum(x, reverse):
    n = x.shape[0]
    row = lax.broadcasted_iota(jnp.int32, x.shape, 0) % CHUNK
    s = 1
    while s < CHUNK:
        if reverse:
            x = x + jnp.where(row < CHUNK - s, pltpu.roll(x, n - s, axis=0), 0.0)
        else:
            x = x + jnp.where(row >= s, pltpu.roll(x, s, axis=0), 0.0)
        s *= 2
    return x


def _gla_direction(q_raw, v_raw, f_pre, lb, st_ref, reverse):
    n = q_raw.shape[0]
    q = q_raw.astype(F32)
    q = q * jax.nn.sigmoid(q) * (HEAD_DIM ** -0.5)
    fg = lb + (1.0 - lb) * jax.nn.sigmoid(f_pre)
    k = 1.0 - fg
    b = _chunk_cumsum(jnp.log(fg), reverse)
    mid = CHUNK // 2 if reverse else CHUNK // 2 - 1
    last = 0 if reverse else CHUNK - 1
    t_idx = lax.broadcasted_iota(jnp.int32, (CHUNK, CHUNK), 0)
    s_idx = lax.broadcasted_iota(jnp.int32, (CHUNK, CHUNK), 1)
    visible = (t_idx <= s_idx) if reverse else (t_idx >= s_idx)
    nchunks = n // CHUNK
    order = range(nchunks - 1, -1, -1) if reverse else range(nchunks)
    outs = [None] * nchunks
    st = st_ref[...]
    for c in order:
        rows = slice(c * CHUNK, (c + 1) * CHUNK)
        bc, qc, kc = b[rows], q[rows], k[rows]
        vc = v_raw[rows]
        b_mid = bc[mid:mid + 1]
        b_last = bc[last:last + 1]
        e_up = jnp.exp(bc - b_mid)
        e_dn = jnp.exp(b_mid - bc)
        scores = lax.dot_general((qc * e_up).astype(BF16), (kc * e_dn).astype(BF16),
                                 (((1,), (1,)), ((), ())), preferred_element_type=F32)
        scores = jnp.where(visible, scores, 0.0)
        o = jnp.dot(scores.astype(BF16), vc, preferred_element_type=F32)
        q_state = (qc * (e_up * jnp.exp(b_mid))).astype(BF16)
        o += lax.dot_general(q_state, st.astype(BF16), (((1,), (1,)), ((), ())), preferred_element_type=F32)
        k_state = (kc * (e_dn * jnp.exp(b_last - b_mid))).astype(BF16)
        delta_t = lax.dot_general(vc, k_state, (((0,), (0,)), ((), ())), preferred_element_type=F32)
        st = st * jnp.exp(b_last) + delta_t
        outs[c] = o
    st_ref[...] = st
    return jnp.concatenate(outs, axis=0)


def _gla_kernel(lbf_ref, lbb_ref, qf_ref, vf_ref, ff_ref, qb_ref, vb_ref, fb_ref, o_ref, sf_ref, sb_ref):
    j = pl.program_id(2)
    nblk = pl.num_programs(2)
    lb_rows = qf_ref.shape[1]

    @pl.when(j == 0)
    def _():
        o_ref[...] = jnp.zeros(o_ref.shape, o_ref.dtype)
        sf_ref[...] = jnp.zeros(sf_ref.shape, sf_ref.dtype)
        sb_ref[...] = jnp.zeros(sb_ref.shape, sb_ref.dtype)

    o_f = _gla_direction(qf_ref[0], vf_ref[0], ff_ref[0], lbf_ref[...], sf_ref, False)
    start_f = pl.multiple_of(j * lb_rows, lb_rows)
    o_ref[0, pl.ds(start_f, lb_rows), :] += o_f
    o_b = _gla_direction(qb_ref[0], vb_ref[0], fb_ref[0], lbb_ref[...], sb_ref, True)
    start_b = pl.multiple_of((nblk - 1 - j) * lb_rows, lb_rows)
    o_ref[0, pl.ds(start_b, lb_rows), :] += o_b


def _gla(qi3, f3, lb_fwd, lb_bwd):
    b, seq_len, _ = qi3.shape
    lbk = min(GLA_LB, seq_len)
    nblk = seq_len // lbk
    blk = lambda col, rev: pl.BlockSpec(
        (1, lbk, HEAD_DIM),
        (lambda i, h, j: (i, nblk - 1 - j, col + h)) if rev else (lambda i, h, j: (i, j, col + h)))
    lb_spec = pl.BlockSpec((1, HEAD_DIM), lambda i, h, j: (0, h))
    return pl.pallas_call(
        _gla_kernel,
        grid=(b, HEADS, nblk),
        in_specs=[lb_spec, lb_spec,
                  blk(0, False), blk(HEADS, False), blk(0, False),
                  blk(0, True), blk(HEADS, True), blk(HEADS, True)],
        out_specs=pl.BlockSpec((1, seq_len, HEAD_DIM), lambda i, h, j: (i, 0, h)),
        out_shape=jax.ShapeDtypeStruct((b, seq_len, HGRN_WIDTH), F32),
        scratch_shapes=[pltpu.VMEM((HEAD_DIM, HEAD_DIM), F32)] * 2,
        compiler_params=_cparams("parallel", "parallel", "arbitrary"),
        name="gla",
    )(lb_fwd.reshape(1, -1), lb_bwd.reshape(1, -1), qi3, qi3, f3, qi3, qi3, f3)


HALF_D = D_MODEL // 2


def _pack_bf16_pairs(x):
    bits = lax.bitcast_convert_type(x.astype(BF16).astype(F32), jnp.uint32)
    return bits[:, :HALF_D] | (bits[:, HALF_D:] >> 16)


def _unpack_bf16_pairs(words):
    hi = lax.bitcast_convert_type(words & jnp.uint32(0xFFFF0000), F32)
    lo = lax.bitcast_convert_type(words << 16, F32)
    return hi, lo


OUT_TM = 512


def _out_proj_kernel(*refs, nz):
    z_refs = refs[:nz]
    (o_ref, g_ref, xn_ref, ng_ref, wz_ref, wh_ref, l1g_ref, l1b_ref, rwh_ref, rwl_ref, rb_ref,
     h_ref, hpk_ref, route_ref, gate_ref, cnt_ref, base_ref) = refs[nz:]
    tm = o_ref.shape[0]

    @pl.when(pl.program_id(0) == 0)
    def _():
        base_ref[...] = jnp.zeros(base_ref.shape, base_ref.dtype)

    z = jnp.concatenate([r[0] for r in z_refs], axis=0) if nz > 1 else z_refs[0][0]
    o = o_ref[...]
    g = g_ref[...].astype(F32)
    normed = []
    for hd in range(HEADS):
        oh = o[:, hd * HEAD_DIM:(hd + 1) * HEAD_DIM]
        normed.append(oh * lax.rsqrt(jnp.mean(oh * oh, axis=-1, keepdims=True) + RMS_EPS))
    hg = jnp.concatenate(normed, axis=1) * ng_ref[...] * (g * jax.nn.sigmoid(g))
    mixed = jnp.dot(z, wz_ref[...], preferred_element_type=F32)
    mixed += jnp.dot(hg.astype(BF16), wh_ref[...], preferred_element_type=F32)
    h = _layer_norm_rows(DEEPNORM_ALPHA * xn_ref[...] + mixed, l1g_ref[...], l1b_ref[...])
    h_ref[...] = h
    hpk_ref[...] = _pack_bf16_pairs(h)

    h_hi = h.astype(BF16)
    h_lo = (h - h_hi.astype(F32)).astype(BF16)
    logits = jnp.dot(h_hi, rwh_ref[...], preferred_element_type=F32)
    logits += jnp.dot(h_lo, rwh_ref[...], preferred_element_type=F32)
    logits += jnp.dot(h_hi, rwl_ref[...], preferred_element_type=F32)
    logits += rb_ref[...]

    lane = lax.broadcasted_iota(jnp.int32, (tm, LANES), 1)
    work = jnp.where(lane < N_EXPERTS, logits, -jnp.inf)
    vals, idxs, hits = [], [], []
    for _ in range(TOP_K):
        m = jnp.max(work, axis=-1, keepdims=True)
        idx = jnp.min(jnp.where(work == m, lane, LANES), axis=-1, keepdims=True)
        hit = lane == idx
        work = jnp.where(hit, -jnp.inf, work)
        vals.append(m)
        idxs.append(idx)
        hits.append(hit)
    exps = [jnp.exp(v - vals[0]) for v in vals]
    denom = exps[0] + exps[1] + exps[2] + exps[3]

    member = jnp.zeros((tm, LANES), F32)
    for hit in hits:
        member = member + jnp.where(hit, 1.0, 0.0)
    t_idx = lax.broadcasted_iota(jnp.int32, (tm, tm), 0)
    s_idx = lax.broadcasted_iota(jnp.int32, (tm, tm), 1)
    earlier = jnp.where(s_idx < t_idx, 1.0, 0.0).astype(BF16)
    base = base_ref[...]
    before = jnp.dot(earlier, member.astype(BF16), preferred_element_type=F32) + base

    route = jnp.zeros((tm, LANES), jnp.int32)
    gate = jnp.zeros((tm, LANES), F32)
    for k in range(TOP_K):
        rank = jnp.sum(jnp.where(hits[k], before, 0.0), axis=-1, keepdims=True).astype(jnp.int32)
        route = route + jnp.where(lane == k, idxs[k], 0) + jnp.where(lane == TOP_K + k, rank, 0)
        gate = gate + jnp.where(lane == k, exps[k] / denom, 0.0)
    route_ref[...] = route
    gate_ref[...] = gate
    total = base + jnp.sum(member, axis=0, keepdims=True)
    base_ref[...] = total
    cnt_ref[...] = total


def _out_proj(zt, o2d, g2d, xn2d, seq_len, norm_g6, w_out_bf16, ln_g, ln_b, rw_hi, rw_lo, rb_pad):
    t = o2d.shape[0]
    n1 = zt.shape[1]
    tm = max(min(OUT_TM, seq_len), n1)
    nz = tm // n1
    row = lambda i: (i, 0)
    const = lambda i: (0, 0)

    def z_spec(k):
        return pl.BlockSpec((1, n1, FOURIER_WIDTH),
                            lambda i: ((i * tm) // seq_len, 0, ((i * tm) % seq_len) // n1 + k))

    in_specs = [z_spec(k) for k in range(nz)] + [
        pl.BlockSpec((tm, HGRN_WIDTH), row),
        pl.BlockSpec((tm, HGRN_WIDTH), row),
        pl.BlockSpec((tm, D_MODEL), row),
        pl.BlockSpec((1, HGRN_WIDTH), const),
        pl.BlockSpec((FOURIER_WIDTH, D_MODEL), const),
        pl.BlockSpec((HGRN_WIDTH, D_MODEL), const),
        pl.BlockSpec((1, D_MODEL), const),
        pl.BlockSpec((1, D_MODEL), const),
        pl.BlockSpec((D_MODEL, LANES), const),
        pl.BlockSpec((D_MODEL, LANES), const),
        pl.BlockSpec((1, LANES), const),
    ]
    out_specs = [
        pl.BlockSpec((tm, D_MODEL), row),
        pl.BlockSpec((tm, HALF_D), row),
        pl.BlockSpec((tm, LANES), row),
        pl.BlockSpec((tm, LANES), row),
        pl.BlockSpec((1, LANES), const),
    ]
    out_shape = [
        jax.ShapeDtypeStruct((t, D_MODEL), F32),
        jax.ShapeDtypeStruct((t, HALF_D), jnp.uint32),
        jax.ShapeDtypeStruct((t, LANES), jnp.int32),
        jax.ShapeDtypeStruct((t, LANES), F32),
        jax.ShapeDtypeStruct((1, LANES), F32),
    ]
    return pl.pallas_call(
        functools.partial(_out_proj_kernel, nz=nz),
        grid=(t // tm,),
        in_specs=in_specs,
        out_specs=out_specs,
        out_shape=out_shape,
        scratch_shapes=[pltpu.VMEM((1, LANES), F32)],
        compiler_params=_cparams("arbitrary"),
        name="out_proj",
    )(*([zt] * nz), o2d, g2d, xn2d, norm_g6, w_out_bf16[:FOURIER_WIDTH], w_out_bf16[FOURIER_WIDTH:],
      ln_g.reshape(1, -1), ln_b.reshape(1, -1), rw_hi, rw_lo, rb_pad)


DISPATCH_TM = 256
COMBINE_TM = 128


def _dispatch_kernel(dest_ref, hpk_ref, xb_in_ref, xb_ref, sem):
    del xb_in_ref
    tm = hpk_ref.shape[0]

    def row_copy(r, k):
        slot = dest_ref[0, 0, r * TOP_K + k]
        return pltpu.make_async_copy(hpk_ref.at[pl.ds(r, 1), :], xb_ref.at[pl.ds(slot, 1), :], sem)

    def issue(r, carry):
        for k in range(TOP_K):
            row_copy(r, k).start()
        return carry

    def drain(r, carry):
        for k in range(TOP_K):
            row_copy(r, k).wait()
        return carry

    lax.fori_loop(0, tm, issue, 0)
    lax.fori_loop(0, tm, drain, 0)


def _dispatch(dest, hpk, n_slots):
    t = hpk.shape[0]
    tm = min(DISPATCH_TM, t)
    dest3 = dest.reshape(t // tm, 1, tm * TOP_K)
    xb0 = jnp.zeros((n_slots, HALF_D), jnp.uint32)
    return pl.pallas_call(
        _dispatch_kernel,
        grid=(t // tm,),
        in_specs=[
            pl.BlockSpec((1, 1, tm * TOP_K), lambda i: (i, 0, 0), memory_space=pltpu.SMEM),
            pl.BlockSpec((tm, HALF_D), lambda i: (i, 0)),
            pl.BlockSpec(memory_space=pl.ANY),
        ],
        out_specs=pl.BlockSpec(memory_space=pl.ANY),
        out_shape=jax.ShapeDtypeStruct((n_slots, HALF_D), jnp.uint32),
        scratch_shapes=[pltpu.SemaphoreType.DMA(())],
        input_output_aliases={2: 0},
        compiler_params=_cparams("arbitrary"),
        name="moe_dispatch",
    )(dest3, hpk, xb0)


def _experts_kernel(be_ref, nused_ref, xb_ref, wgu_ref, bgu_ref, wdn_ref, bdn_ref, yb_ref):
    @pl.when(pl.program_id(0) < nused_ref[0])
    def _():
        x_hi, x_lo = _unpack_bf16_pairs(xb_ref[...])
        gu = jnp.dot(x_hi.astype(BF16), wgu_ref[0, :HALF_D, :], preferred_element_type=F32)
        gu += jnp.dot(x_lo.astype(BF16), wgu_ref[0, HALF_D:, :], preferred_element_type=F32)
        gu += bgu_ref[0]
        gate = jnp.minimum(gu[:, :D_FF], SWIGLU_LIMIT)
        up = jnp.clip(gu[:, D_FF:], -SWIGLU_LIMIT, SWIGLU_LIMIT)
        act = (up + 1.0) * gate * jax.nn.sigmoid(SWIGLU_ALPHA * gate)
        y = jnp.dot(act.astype(BF16), wdn_ref[0], preferred_element_type=F32) + bdn_ref[0]
        yb_ref[...] = _pack_bf16_pairs(y)

    @pl.when(pl.program_id(0) >= nused_ref[0])
    def _():
        yb_ref[...] = jnp.zeros(yb_ref.shape, yb_ref.dtype)


def _experts(block_expert, n_used, xb, wgu_bf16, bgu, wdn_bf16, bdn):
    n_slots = xb.shape[0]
    nb = n_slots // MOE_BLOCK
    blk = pl.BlockSpec((MOE_BLOCK, HALF_D), lambda i, be, nu: (i, 0))
    per_expert = lambda shape: pl.BlockSpec((1,) + shape, lambda i, be, nu: (be[i], 0, 0))
    grid_spec = pltpu.PrefetchScalarGridSpec(
        num_scalar_prefetch=2,
        grid=(nb,),
        in_specs=[blk, per_expert((D_MODEL, 2 * D_FF)), per_expert((1, 2 * D_FF)),
                  per_expert((D_FF, D_MODEL)), per_expert((1, D_MODEL))],
        out_specs=blk,
    )
    return pl.pallas_call(
        _experts_kernel,
        grid_spec=grid_spec,
        out_shape=jax.ShapeDtypeStruct((n_slots, HALF_D), jnp.uint32),
        compiler_params=_cparams("arbitrary"),
        name="moe_experts",
    )(block_expert, n_used, xb, wgu_bf16, bgu, wdn_bf16, bdn)


def _combine_kernel(dest_ref, gate_ref, h_ref, g_ref, b_ref, yb_ref, out_ref, rows_ref, sem):
    tm = h_ref.shape[0]

    def row_copy(r, k):
        slot = dest_ref[0, 0, r * TOP_K + k]
        return pltpu.make_async_copy(yb_ref.at[pl.ds(slot, 1), :], rows_ref.at[k, pl.ds(r, 1), :], sem)

    def issue(r, carry):
        for k in range(TOP_K):
            row_copy(r, k).start()
        return carry

    def drain(r, carry):
        for k in range(TOP_K):
            row_copy(r, k).wait()
        return carry

    lax.fori_loop(0, tm, issue, 0)
    lax.fori_loop(0, tm, drain, 0)

    gate = gate_ref[...]
    ff_hi = jnp.zeros((tm, HALF_D), F32)
    ff_lo = jnp.zeros((tm, HALF_D), F32)
    for k in range(TOP_K):
        hi, lo = _unpack_bf16_pairs(rows_ref[k])
        gk = gate[:, k:k + 1]
        ff_hi = ff_hi + gk * hi
        ff_lo = ff_lo + gk * lo
    ff = jnp.concatenate([ff_hi, ff_lo], axis=1)
    out_ref[...] = _layer_norm_rows(DEEPNORM_ALPHA * h_ref[...] + ff, g_ref[...], b_ref[...])


def _combine(dest, gates, h2d, ln_g, ln_b, yb):
    t = h2d.shape[0]
    tm = min(COMBINE_TM, t)
    dest3 = dest.reshape(t // tm, 1, tm * TOP_K)
    row = lambda i: (i, 0)
    const = lambda i: (0, 0)
    return pl.pallas_call(
        _combine_kernel,
        grid=(t // tm,),
        in_specs=[
            pl.BlockSpec((1, 1, tm * TOP_K), lambda i: (i, 0, 0), memory_space=pltpu.SMEM),
            pl.BlockSpec((tm, LANES), row),
            pl.BlockSpec((tm, D_MODEL), row),
            pl.BlockSpec((1, D_MODEL), const),
            pl.BlockSpec((1, D_MODEL), const),
            pl.BlockSpec(memory_space=pl.ANY),
        ],
        out_specs=pl.BlockSpec((tm, D_MODEL), row),
        out_shape=jax.ShapeDtypeStruct((t, D_MODEL), F32),
        scratch_shapes=[pltpu.VMEM((TOP_K, tm, HALF_D), jnp.uint32), pltpu.SemaphoreType.DMA(())],
        compiler_params=_cparams("arbitrary"),
        name="moe_combine",
    )(dest3, gates, h2d, ln_g.reshape(1, -1), ln_b.reshape(1, -1), yb)


def _moe_plan(route, counts_f32, n_tokens):
    eid = route[:, :TOP_K]
    rank = route[:, TOP_K:2 * TOP_K]
    counts = counts_f32[0, :N_EXPERTS].astype(jnp.int32)
    padded = ((counts + MOE_BLOCK - 1) // MOE_BLOCK) * MOE_BLOCK
    pend = jnp.cumsum(padded)
    pstart = pend - padded
    dest = pstart[eid] + rank
    n_assign = n_tokens * TOP_K
    n_slots = ((n_assign + MOE_BLOCK - 1) // MOE_BLOCK) * MOE_BLOCK + N_EXPERTS * MOE_BLOCK
    nb = n_slots // MOE_BLOCK
    block_expert = jnp.searchsorted(pend, jnp.arange(nb, dtype=jnp.int32) * MOE_BLOCK, side="right")
    block_expert = jnp.minimum(block_expert, N_EXPERTS - 1).astype(jnp.int32)
    n_used = (pend[-1:] // MOE_BLOCK).astype(jnp.int32)
    return dest.astype(jnp.int32), block_expert, n_used, n_slots


def _trunk(x, p):
    batch, seq_len, _ = x.shape
    t = batch * seq_len
    xn, u, qi, f, og = _in_proj(x.reshape(t, D_MODEL), p["ln_in_g"], p["ln_in_b"], p["w_in"])
    zt = _fourier_mix(u, batch, seq_len, p["fourier_norm_g"])
    o = _gla(qi.reshape(batch, seq_len, -1), f.reshape(batch, seq_len, -1), p["lb_fwd"], p["lb_bwd"])
    h, hpk, route, gates, counts = _out_proj(
        zt, o.reshape(t, HGRN_WIDTH), og, xn, seq_len, p["norm_g6"], p["w_out"], p["ln1_g"], p["ln1_b"],
        p["rw_hi"], p["rw_lo"], p["rb_pad"])
    dest, block_expert, n_used, n_slots = _moe_plan(route, counts, t)
    xb = _dispatch(dest, hpk, n_slots)
    yb = _experts(block_expert, n_used, xb, p["w_gu"], p["b_gu"], p["w_dn"], p["b_dn"])
    y = _combine(dest, gates, h, p["ln2_g"], p["ln2_b"], yb)
    return y.reshape(batch, seq_len, D_MODEL)


def _prepare_params(ln_in_g, ln_in_b, w_in, fourier_norm_g, lb_gamma, hgrn_norm_g, w_out, ln1_g, ln1_b,
                    router_w, router_b, w_gate_up, b_gate_up, w_down, b_down, ln2_g, ln2_b):
    fw, hw = FOURIER_WIDTH, HGRN_WIDTH
    w0 = w_in[0]
    lb_all = jnp.cumsum(jax.nn.softmax(lb_gamma.astype(F32), axis=1), axis=1)
    rw = router_w[0].astype(F32)
    rw_hi = rw.astype(BF16)
    rw_lo = (rw - rw_hi.astype(F32)).astype(BF16)
    pad = ((0, 0), (0, LANES - N_EXPERTS))
    del fw, hw
    return dict(
        ln_in_g=ln_in_g, ln_in_b=ln_in_b, w_in=w0.astype(BF16),
        fourier_norm_g=fourier_norm_g[0],
        lb_fwd=lb_all[0, 0], lb_bwd=lb_all[1, 0],
        norm_g6=jnp.tile(hgrn_norm_g[0].astype(F32), HEADS).reshape(1, -1),
        w_out=w_out[0].astype(BF16), ln1_g=ln1_g[0], ln1_b=ln1_b[0],
        rw_hi=jnp.pad(rw_hi, pad), rw_lo=jnp.pad(rw_lo, pad),
        rb_pad=jnp.pad(router_b[0].astype(F32).reshape(1, -1), pad),
        w_gu=w_gate_up[0].astype(BF16), b_gu=b_gate_up[0].reshape(N_EXPERTS, 1, -1),
        w_dn=w_down[0].astype(BF16), b_dn=b_down[0].reshape(N_EXPERTS, 1, -1),
        ln2_g=ln2_g[0], ln2_b=ln2_b[0],
    )


def kernel(x_prompt, x_sample, ln_in_g, ln_in_b, w_in, fourier_norm_g, lb_gamma, hgrn_norm_g, w_out,
           ln1_g, ln1_b, router_w, router_b, w_gate_up, b_gate_up, w_down, b_down, ln2_g, ln2_b):
    p = _prepare_params(ln_in_g, ln_in_b, w_in, fourier_norm_g, lb_gamma, hgrn_norm_g, w_out, ln1_g, ln1_b,
                        router_w, router_b, w_gate_up, b_gate_up, w_down, b_down, ln2_g, ln2_b)
    return (_trunk(x_prompt, p), _trunk(x_sample, p))
```

```python
import functools
import math

import numpy as np
import jax
import jax.numpy as jnp
from jax import lax
from jax.experimental import pallas as pl
from jax.experimental.pallas import tpu as pltpu

D_MODEL = 1024
FOURIER_WIDTH = 256
FOURIER_GROUP_DIM = 64
HGRN_WIDTH = 768
HEAD_DIM = 128
HEADS = 6
CHUNK = 64
N_EXPERTS = 32
TOP_K = 4
D_FF = 1024
SWIGLU_LIMIT = 7.0
SWIGLU_ALPHA = 1.702
MOE_BLOCK = 512
LN_EPS = 1e-5
RMS_EPS = 1e-6
DEEPNORM_ALPHA = 2.0 ** 0.25

LANES = 128
VMEM_LIMIT_BYTES = 56 * 1024 * 1024

F32 = jnp.float32
BF16 = jnp.bfloat16


def _cparams(*sem):
    return pltpu.CompilerParams(dimension_semantics=sem, vmem_limit_bytes=VMEM_LIMIT_BYTES)


def _layer_norm_rows(x, g, b):
    mu = jnp.mean(x, axis=-1, keepdims=True)
    xc = x - mu
    var = jnp.mean(xc * xc, axis=-1, keepdims=True)
    return xc * lax.rsqrt(var + LN_EPS) * g + b


IN_TM = 512
IN_TN = 512


def _in_proj_kernel(x_ref, g_ref, b_ref, lb_ref, w_ref, xn_ref, u_ref, qv_ref, lf_ref, kk_ref, og_ref):
    xn = _layer_norm_rows(x_ref[...], g_ref[...], b_ref[...])
    xn_ref[...] = xn
    xb = xn.astype(BF16)
    hw = HGRN_WIDTH

    def chunks(col, width):
        for c0 in range(0, width, IN_TN):
            cw = min(IN_TN, width - c0)
            yield c0, cw, jnp.dot(xb, w_ref[:, col + c0:col + c0 + cw], preferred_element_type=F32)

    for c0, cw, acc in chunks(0, FOURIER_WIDTH):
        u_ref[:, c0:c0 + cw] = acc.astype(BF16)
    for c0, cw, acc in chunks(FOURIER_WIDTH, hw):
        qv_ref[:, c0:c0 + cw] = (acc * jax.nn.sigmoid(acc) * (HEAD_DIM ** -0.5)).astype(BF16)
    for c0, cw, acc in chunks(FOURIER_WIDTH + hw, hw):
        qv_ref[:, hw + c0:hw + c0 + cw] = acc.astype(BF16)
    for c0, cw, acc in chunks(FOURIER_WIDTH + 2 * hw, 2 * hw):
        lb = lb_ref[:, c0:c0 + cw]
        fg = lb + (1.0 - lb) * jax.nn.sigmoid(acc)
        lf_ref[:, c0:c0 + cw] = jnp.log(fg)
        kk_ref[:, c0:c0 + cw] = (1.0 - fg).astype(BF16)
    for c0, cw, acc in chunks(FOURIER_WIDTH + 4 * hw, hw):
        og_ref[:, c0:c0 + cw] = acc.astype(BF16)


def _in_proj(x2d, ln_g, ln_b, lb2, w_bf16):
    t = x2d.shape[0]
    tm = min(IN_TM, t)
    widths = (FOURIER_WIDTH, 2 * HGRN_WIDTH, 2 * HGRN_WIDTH, 2 * HGRN_WIDTH, HGRN_WIDTH)
    dtypes = (BF16, BF16, F32, BF16, BF16)
    row = lambda i: (i, 0)
    const = lambda i: (0, 0)
    return pl.pallas_call(
        _in_proj_kernel,
        grid=(t // tm,),
        in_specs=[
            pl.BlockSpec((tm, D_MODEL), row),
            pl.BlockSpec((1, D_MODEL), const),
            pl.BlockSpec((1, D_MODEL), const),
            pl.BlockSpec((1, 2 * HGRN_WIDTH), const),
            pl.BlockSpec(w_bf16.shape, const),
        ],
        out_specs=[pl.BlockSpec((tm, D_MODEL), row)] + [pl.BlockSpec((tm, w), row) for w in widths],
        out_shape=[jax.ShapeDtypeStruct((t, D_MODEL), F32)]
        + [jax.ShapeDtypeStruct((t, w), dt) for w, dt in zip(widths, dtypes)],
        compiler_params=_cparams("parallel"),
        name="in_proj",
    )(x2d, ln_g.reshape(1, -1), ln_b.reshape(1, -1), lb2, w_bf16)


def _fft_split(seq_len):
    n1 = 1 << ((seq_len.bit_length() - 1 + 1) // 2)
    return n1, seq_len // n1


@functools.lru_cache(maxsize=None)
def _fft_tables(seq_len):
    n1, n2 = _fft_split(seq_len)
    k1 = np.arange(n1)
    ang1 = 2.0 * np.pi * ((k1[:, None] * k1[None, :]) % n1) / n1
    s1 = 1.0 / math.sqrt(n1)
    c1, s1m = np.cos(ang1) * s1, np.sin(ang1) * s1
    l1p = np.arange(n1)[:, None, None]
    l2p = np.arange(n2)[None, :, None]
    l2 = np.arange(n2)[None, None, :]
    ang2 = 2.0 * np.pi * ((l2 * (l1p + n1 * l2p)) % seq_len) / seq_len
    s2 = 1.0 / math.sqrt(n2)
    gc, gs = np.cos(ang2) * s2, np.sin(ang2) * s2
    kc = np.arange(FOURIER_GROUP_DIM)
    angc = 2.0 * np.pi * ((kc[:, None] * kc[None, :]) % FOURIER_GROUP_DIM) / FOURIER_GROUP_DIM
    sc = 1.0 / math.sqrt(FOURIER_GROUP_DIM)
    groups = FOURIER_WIDTH // FOURIER_GROUP_DIM
    bc = np.kron(np.eye(groups), np.cos(angc) * sc)
    bs = np.kron(np.eye(groups), np.sin(angc) * sc)
    as_bf16 = lambda a: jnp.asarray(a, dtype=F32).astype(BF16)
    return tuple(as_bf16(a) for a in (c1, s1m, gc, gs, bc, bs))


def _fft1_kernel(c_ref, s_ref, u_ref, ar_ref, ai_ref):
    u = u_ref[0]
    ar_ref[0] = jnp.dot(c_ref[...], u, preferred_element_type=F32).astype(BF16)
    ai_ref[0] = (-jnp.dot(s_ref[...], u, preferred_element_type=F32)).astype(BF16)


FFT1_TN = 4096


def _fft_stage1(u3, c1, s1):
    b, n1, width = u3.shape
    tn = min(FFT1_TN, width)
    blk = pl.BlockSpec((1, n1, tn), lambda i, j: (i, 0, j))
    mat = pl.BlockSpec((n1, n1), lambda i, j: (0, 0))
    return pl.pallas_call(
        _fft1_kernel,
        grid=(b, width // tn),
        in_specs=[mat, mat, blk],
        out_specs=[blk, blk],
        out_shape=[jax.ShapeDtypeStruct(u3.shape, BF16)] * 2,
        compiler_params=_cparams("parallel", "parallel"),
        name="fft_stage1",
    )(c1, s1, u3)


FFT2_T1 = 8


def _fft2_kernel(gc_ref, gs_ref, bc_ref, bs_ref, g_ref, ar_ref, ai_ref, z_ref):
    t1 = ar_ref.shape[1]
    for j in range(t1):
        ar, ai = ar_ref[0, j], ai_ref[0, j]
        gc, gs = gc_ref[j], gs_ref[j]
        xr = jnp.dot(gc, ar, preferred_element_type=F32) + jnp.dot(gs, ai, preferred_element_type=F32)
        xi = jnp.dot(gc, ai, preferred_element_type=F32) - jnp.dot(gs, ar, preferred_element_type=F32)
        z = jnp.dot(xr.astype(BF16), bc_ref[...], preferred_element_type=F32)
        z += jnp.dot(xi.astype(BF16), bs_ref[...], preferred_element_type=F32)
        z = z * lax.rsqrt(jnp.mean(z * z, axis=-1, keepdims=True) + RMS_EPS) * g_ref[...]
        z_ref[0, j] = z.astype(BF16)


def _fft_stage2(ar4, ai4, gc, gs, bc, bs, gain):
    b, n1, n2, w = ar4.shape
    t1 = min(FFT2_T1, n1)
    a_blk = pl.BlockSpec((1, t1, n2, w), lambda i, j: (i, j, 0, 0))
    g_blk = pl.BlockSpec((t1, n2, n2), lambda i, j: (j, 0, 0))
    c_blk = pl.BlockSpec((w, w), lambda i, j: (0, 0))
    return pl.pallas_call(
        _fft2_kernel,
        grid=(b, n1 // t1),
        in_specs=[g_blk, g_blk, c_blk, c_blk, pl.BlockSpec((1, w), lambda i, j: (0, 0)), a_blk, a_blk],
        out_specs=a_blk,
        out_shape=jax.ShapeDtypeStruct(ar4.shape, BF16),
        compiler_params=_cparams("parallel", "parallel"),
        name="fft_stage2",
    )(gc, gs, bc, bs, gain.reshape(1, -1), ar4, ai4)


def _fourier_mix(u2d, batch, seq_len, gain):
    n1, n2 = _fft_split(seq_len)
    c1, s1, gc, gs, bc, bs = _fft_tables(seq_len)
    u3 = u2d.reshape(batch, n1, n2 * FOURIER_WIDTH)
    ar, ai = _fft_stage1(u3, c1, s1)
    shape4 = (batch, n1, n2, FOURIER_WIDTH)
    zt = _fft_stage2(ar.reshape(shape4), ai.reshape(shape4), gc, gs, bc, bs, gain)
    return zt.reshape(batch, n1, n2 * FOURIER_WIDTH)


GLA_LB = 512
CUMSUM_ROWS = 256


@functools.lru_cache(maxsize=None)
def _cumsum_matrices():
    r = np.arange(CUMSUM_ROWS)
    same_chunk = (r[:, None] // CHUNK) == (r[None, :] // CHUNK)
    prefix = same_chunk & (r[None, :] <= r[:, None])
    suffix = same_chunk & (r[None, :] >= r[:, None])
    return (jnp.asarray(prefix, dtype=F32).astype(BF16), jnp.asarray(suffix, dtype=F32).astype(BF16))


def _chunk_cumsum(x, tri):
    hi = x.astype(BF16)
    lo = (x - hi.astype(F32)).astype(BF16)
    parts = []
    for r0 in range(0, x.shape[0], CUMSUM_ROWS):
        rows = slice(r0, r0 + CUMSUM_ROWS)
        parts.append(jnp.dot(tri, hi[rows], preferred_element_type=F32)
                     + jnp.dot(tri, lo[rows], preferred_element_type=F32))
    return parts[0] if len(parts) == 1 else jnp.concatenate(parts, axis=0)


def _gla_direction(q, v, k, logf, tri, st_ref, reverse):
    n = q.shape[0]
    b = _chunk_cumsum(logf, tri)
    mid = CHUNK // 2 if reverse else CHUNK // 2 - 1
    last = 0 if reverse else CHUNK - 1
    t_idx = lax.broadcasted_iota(jnp.int32, (CHUNK, CHUNK), 0)
    s_idx = lax.broadcasted_iota(jnp.int32, (CHUNK, CHUNK), 1)
    visible = (t_idx <= s_idx) if reverse else (t_idx >= s_idx)
    nchunks = n // CHUNK
    order = range(nchunks - 1, -1, -1) if reverse else range(nchunks)
    nt = (((1,), (1,)), ((), ()))
    tn = (((0,), (0,)), ((), ()))
    rows = [slice(c * CHUNK, (c + 1) * CHUNK) for c in range(nchunks)]
    qe, ke, e_mid, e_last, e_gap = [], [], [], [], []
    for c in range(nchunks):
        bc = b[rows[c]]
        b_mid = bc[mid:mid + 1]
        b_last = bc[last:last + 1]
        qe.append((q[rows[c]].astype(F32) * jnp.exp(bc - b_mid)).astype(BF16))
        ke.append((k[rows[c]].astype(F32) * jnp.exp(b_mid - bc)).astype(BF16))
        e_mid.append(jnp.exp(b_mid))
        e_last.append(jnp.exp(b_last))
        e_gap.append(jnp.exp(b_last - b_mid))
    scores, delta_t = [], []
    for c in range(nchunks):
        s = lax.dot_general(qe[c], ke[c], nt, preferred_element_type=F32)
        scores.append(jnp.where(visible, s, 0.0).astype(BF16))
        delta_t.append(lax.dot_general(v[rows[c]], ke[c], tn, preferred_element_type=F32) * e_gap[c])
    st = st_ref[...]
    st_in = [None] * nchunks
    for c in order:
        st_in[c] = (st * e_mid[c]).astype(BF16)
        st = st * e_last[c] + delta_t[c]
    st_ref[...] = st
    outs = [jnp.dot(scores[c], v[rows[c]], preferred_element_type=F32)
            + lax.dot_general(qe[c], st_in[c], nt, preferred_element_type=F32) for c in range(nchunks)]
    return jnp.concatenate(outs, axis=0)


def _gla_kernel(trif_ref, trib_ref, qf_ref, vf_ref, kf_ref, lf_ref, qb_ref, vb_ref, kb_ref, lb_ref,
                o_ref, sf_ref, sb_ref):
    j = pl.program_id(2)
    nblk = pl.num_programs(2)
    lb_rows = qf_ref.shape[1]

    @pl.when(j == 0)
    def _():
        o_ref[...] = jnp.zeros(o_ref.shape, o_ref.dtype)
        sf_ref[...] = jnp.zeros(sf_ref.shape, sf_ref.dtype)
        sb_ref[...] = jnp.zeros(sb_ref.shape, sb_ref.dtype)

    o_f = _gla_direction(qf_ref[0], vf_ref[0], kf_ref[0], lf_ref[0], trif_ref[...], sf_ref, False)
    start_f = pl.multiple_of(j * lb_rows, lb_rows)
    o_ref[0, pl.ds(start_f, lb_rows), :] += o_f
    o_b = _gla_direction(qb_ref[0], vb_ref[0], kb_ref[0], lb_ref[0], trib_ref[...], sb_ref, True)
    start_b = pl.multiple_of((nblk - 1 - j) * lb_rows, lb_rows)
    o_ref[0, pl.ds(start_b, lb_rows), :] += o_b


def _gla(qv3, lf3, kk3):
    b, seq_len, _ = qv3.shape
    lbk = min(GLA_LB, seq_len)
    assert lbk % CUMSUM_ROWS == 0 and seq_len % lbk == 0
    nblk = seq_len // lbk
    blk = lambda col, rev: pl.BlockSpec(
        (1, lbk, HEAD_DIM),
        (lambda i, h, j: (i, nblk - 1 - j, col + h)) if rev else (lambda i, h, j: (i, j, col + h)))
    tri_spec = pl.BlockSpec((CUMSUM_ROWS, CUMSUM_ROWS), lambda i, h, j: (0, 0))
    tri_f, tri_b = _cumsum_matrices()
    return pl.pallas_call(
        _gla_kernel,
        grid=(b, HEADS, nblk),
        in_specs=[tri_spec, tri_spec,
                  blk(0, False), blk(HEADS, False), blk(0, False), blk(0, False),
                  blk(0, True), blk(HEADS, True), blk(HEADS, True), blk(HEADS, True)],
        out_specs=pl.BlockSpec((1, seq_len, HEAD_DIM), lambda i, h, j: (i, 0, h)),
        out_shape=jax.ShapeDtypeStruct((b, seq_len, HGRN_WIDTH), F32),
        scratch_shapes=[pltpu.VMEM((HEAD_DIM, HEAD_DIM), F32)] * 2,
        compiler_params=_cparams("parallel", "parallel", "arbitrary"),
        name="gla",
    )(tri_f, tri_b, qv3, qv3, kk3, lf3, qv3, qv3, kk3, lf3)


HALF_D = D_MODEL // 2


def _pack_bf16_pairs(x):
    bits = lax.bitcast_convert_type(x.astype(BF16).astype(F32), jnp.uint32)
    return bits[:, :HALF_D] | (bits[:, HALF_D:] >> 16)


def _unpack_bf16_pairs(words):
    hi = lax.bitcast_convert_type(words & jnp.uint32(0xFFFF0000), F32)
    lo = lax.bitcast_convert_type(words << 16, F32)
    return hi, lo


OUT_TM = 512


def _out_proj_kernel(*refs, nz):
    z_refs = refs[:nz]
    (o_ref, g_ref, xn_ref, ng_ref, wz_ref, wh_ref, l1g_ref, l1b_ref, rwh_ref, rwl_ref, rb_ref,
     h_ref, hpk_ref, route_ref, gate_ref, cnt_ref, base_ref) = refs[nz:]
    tm = o_ref.shape[0]

    @pl.when(pl.program_id(0) == 0)
    def _():
        base_ref[...] = jnp.zeros(base_ref.shape, base_ref.dtype)

    z = jnp.concatenate([r[0] for r in z_refs], axis=0) if nz > 1 else z_refs[0][0]
    o = o_ref[...]
    g = g_ref[...].astype(F32)
    normed = []
    for hd in range(HEADS):
        oh = o[:, hd * HEAD_DIM:(hd + 1) * HEAD_DIM]
        normed.append(oh * lax.rsqrt(jnp.mean(oh * oh, axis=-1, keepdims=True) + RMS_EPS))
    hg = jnp.concatenate(normed, axis=1) * ng_ref[...] * (g * jax.nn.sigmoid(g))
    mixed = jnp.dot(z, wz_ref[...], preferred_element_type=F32)
    mixed += jnp.dot(hg.astype(BF16), wh_ref[...], preferred_element_type=F32)
    h = _layer_norm_rows(DEEPNORM_ALPHA * xn_ref[...] + mixed, l1g_ref[...], l1b_ref[...])
    h_ref[...] = h
    hpk_ref[...] = _pack_bf16_pairs(h)

    h_hi = h.astype(BF16)
    h_lo = (h - h_hi.astype(F32)).astype(BF16)
    logits = jnp.dot(h_hi, rwh_ref[...], preferred_element_type=F32)
    logits += jnp.dot(h_lo, rwh_ref[...], preferred_element_type=F32)
    logits += jnp.dot(h_hi, rwl_ref[...], preferred_element_type=F32)
    logits += rb_ref[...]

    lane = lax.broadcasted_iota(jnp.int32, (tm, LANES), 1)
    work = jnp.where(lane < N_EXPERTS, logits, -jnp.inf)
    vals, idxs, hits = [], [], []
    for _ in range(TOP_K):
        m = jnp.max(work, axis=-1, keepdims=True)
        idx = jnp.min(jnp.where(work == m, lane, LANES), axis=-1, keepdims=True)
        hit = lane == idx
        work = jnp.where(hit, -jnp.inf, work)
        vals.append(m)
        idxs.append(idx)
        hits.append(hit)
    exps = [jnp.exp(v - vals[0]) for v in vals]
    denom = exps[0] + exps[1] + exps[2] + exps[3]

    member = jnp.zeros((tm, LANES), F32)
    for hit in hits:
        member = member + jnp.where(hit, 1.0, 0.0)
    t_idx = lax.broadcasted_iota(jnp.int32, (tm, tm), 0)
    s_idx = lax.broadcasted_iota(jnp.int32, (tm, tm), 1)
    earlier = jnp.where(s_idx < t_idx, 1.0, 0.0).astype(BF16)
    base = base_ref[...]
    before = jnp.dot(earlier, member.astype(BF16), preferred_element_type=F32) + base

    route = jnp.zeros((tm, LANES), jnp.int32)
    gate = jnp.zeros((tm, LANES), F32)
    for k in range(TOP_K):
        rank = jnp.sum(jnp.where(hits[k], before, 0.0), axis=-1, keepdims=True).astype(jnp.int32)
        route = route + jnp.where(lane == k, idxs[k], 0) + jnp.where(lane == TOP_K + k, rank, 0)
        gate = gate + jnp.where(lane == k, exps[k] / denom, 0.0)
    route_ref[...] = route
    gate_ref[...] = gate
    total = base + jnp.sum(member, axis=0, keepdims=True)
    base_ref[...] = total
    cnt_ref[...] = total


def _out_proj(zt, o2d, g2d, xn2d, seq_len, norm_g6, w_out_bf16, ln_g, ln_b, rw_hi, rw_lo, rb_pad):
    t = o2d.shape[0]
    n1 = zt.shape[1]
    tm = max(min(OUT_TM, seq_len), n1)
    nz = tm // n1
    row = lambda i: (i, 0)
    const = lambda i: (0, 0)

    def z_spec(k):
        return pl.BlockSpec((1, n1, FOURIER_WIDTH),
                            lambda i: ((i * tm) // seq_len, 0, ((i * tm) % seq_len) // n1 + k))

    in_specs = [z_spec(k) for k in range(nz)] + [
        pl.BlockSpec((tm, HGRN_WIDTH), row),
        pl.BlockSpec((tm, HGRN_WIDTH), row),
        pl.BlockSpec((tm, D_MODEL), row),
        pl.BlockSpec((1, HGRN_WIDTH), const),
        pl.BlockSpec((FOURIER_WIDTH, D_MODEL), const),
        pl.BlockSpec((HGRN_WIDTH, D_MODEL), const),
        pl.BlockSpec((1, D_MODEL), const),
        pl.BlockSpec((1, D_MODEL), const),
        pl.BlockSpec((D_MODEL, LANES), const),
        pl.BlockSpec((D_MODEL, LANES), const),
        pl.BlockSpec((1, LANES), const),
    ]
    out_specs = [
        pl.BlockSpec((tm, D_MODEL), row),
        pl.BlockSpec((tm, HALF_D), row),
        pl.BlockSpec((tm, LANES), row),
        pl.BlockSpec((tm, LANES), row),
        pl.BlockSpec((1, LANES), const),
    ]
    out_shape = [
        jax.ShapeDtypeStruct((t, D_MODEL), F32),
        jax.ShapeDtypeStruct((t, HALF_D), jnp.uint32),
        jax.ShapeDtypeStruct((t, LANES), jnp.int32),
        jax.ShapeDtypeStruct((t, LANES), F32),
        jax.ShapeDtypeStruct((1, LANES), F32),
    ]
    return pl.pallas_call(
        functools.partial(_out_proj_kernel, nz=nz),
        grid=(t // tm,),
        in_specs=in_specs,
        out_specs=out_specs,
        out_shape=out_shape,
        scratch_shapes=[pltpu.VMEM((1, LANES), F32)],
        compiler_params=_cparams("arbitrary"),
        name="out_proj",
    )(*([zt] * nz), o2d, g2d, xn2d, norm_g6, w_out_bf16[:FOURIER_WIDTH], w_out_bf16[FOURIER_WIDTH:],
      ln_g.reshape(1, -1), ln_b.reshape(1, -1), rw_hi, rw_lo, rb_pad)


DISPATCH_TM = 256
COMBINE_TM = 128


def _dispatch_kernel(dest_ref, hpk_ref, xb_in_ref, xb_ref, sem):
    del xb_in_ref
    tm = hpk_ref.shape[0]

    def row_copy(r, k):
        slot = dest_ref[0, 0, r * TOP_K + k]
        return pltpu.make_async_copy(hpk_ref.at[pl.ds(r, 1), :], xb_ref.at[pl.ds(slot, 1), :], sem)

    def issue(r, carry):
        for k in range(TOP_K):
            row_copy(r, k).start()
        return carry

    def drain(r, carry):
        for k in range(TOP_K):
            row_copy(r, k).wait()
        return carry

    lax.fori_loop(0, tm, issue, 0)
    lax.fori_loop(0, tm, drain, 0)


def _dispatch(dest, hpk, n_slots):
    t = hpk.shape[0]
    tm = min(DISPATCH_TM, t)
    dest3 = dest.reshape(t // tm, 1, tm * TOP_K)
    xb0 = jnp.zeros((n_slots, HALF_D), jnp.uint32)
    return pl.pallas_call(
        _dispatch_kernel,
        grid=(t // tm,),
        in_specs=[
            pl.BlockSpec((1, 1, tm * TOP_K), lambda i: (i, 0, 0), memory_space=pltpu.SMEM),
            pl.BlockSpec((tm, HALF_D), lambda i: (i, 0)),
            pl.BlockSpec(memory_space=pl.ANY),
        ],
        out_specs=pl.BlockSpec(memory_space=pl.ANY),
        out_shape=jax.ShapeDtypeStruct((n_slots, HALF_D), jnp.uint32),
        scratch_shapes=[pltpu.SemaphoreType.DMA(())],
        input_output_aliases={2: 0},
        compiler_params=_cparams("arbitrary"),
        name="moe_dispatch",
    )(dest3, hpk, xb0)


def _experts_kernel(be_ref, nused_ref, xb_ref, wgu_ref, bgu_ref, wdn_ref, bdn_ref, yb_ref):
    @pl.when(pl.program_id(0) < nused_ref[0])
    def _():
        x_hi, x_lo = _unpack_bf16_pairs(xb_ref[...])
        gu = jnp.dot(x_hi.astype(BF16), wgu_ref[0, :HALF_D, :], preferred_element_type=F32)
        gu += jnp.dot(x_lo.astype(BF16), wgu_ref[0, HALF_D:, :], preferred_element_type=F32)
        gu += bgu_ref[0]
        gate = jnp.minimum(gu[:, :D_FF], SWIGLU_LIMIT)
        up = jnp.clip(gu[:, D_FF:], -SWIGLU_LIMIT, SWIGLU_LIMIT)
        act = (up + 1.0) * gate * jax.nn.sigmoid(SWIGLU_ALPHA * gate)
        y = jnp.dot(act.astype(BF16), wdn_ref[0], preferred_element_type=F32) + bdn_ref[0]
        yb_ref[...] = _pack_bf16_pairs(y)

    @pl.when(pl.program_id(0) >= nused_ref[0])
    def _():
        yb_ref[...] = jnp.zeros(yb_ref.shape, yb_ref.dtype)


def _experts(block_expert, n_used, xb, wgu_bf16, bgu, wdn_bf16, bdn):
    n_slots = xb.shape[0]
    nb = n_slots // MOE_BLOCK
    blk = pl.BlockSpec((MOE_BLOCK, HALF_D), lambda i, be, nu: (i, 0))
    per_expert = lambda shape: pl.BlockSpec((1,) + shape, lambda i, be, nu: (be[i], 0, 0))
    grid_spec = pltpu.PrefetchScalarGridSpec(
        num_scalar_prefetch=2,
        grid=(nb,),
        in_specs=[blk, per_expert((D_MODEL, 2 * D_FF)), per_expert((1, 2 * D_FF)),
                  per_expert((D_FF, D_MODEL)), per_expert((1, D_MODEL))],
        out_specs=blk,
    )
    return pl.pallas_call(
        _experts_kernel,
        grid_spec=grid_spec,
        out_shape=jax.ShapeDtypeStruct((n_slots, HALF_D), jnp.uint32),
        compiler_params=_cparams("arbitrary"),
        name="moe_experts",
    )(block_expert, n_used, xb, wgu_bf16, bgu, wdn_bf16, bdn)


def _combine_kernel(dest_ref, gate_ref, h_ref, g_ref, b_ref, yb_ref, out_ref, rows_ref, sem):
    tm = h_ref.shape[0]

    def row_copy(r, k):
        slot = dest_ref[0, 0, r * TOP_K + k]
        return pltpu.make_async_copy(yb_ref.at[pl.ds(slot, 1), :], rows_ref.at[k, pl.ds(r, 1), :], sem)

    def issue(r, carry):
        for k in range(TOP_K):
            row_copy(r, k).start()
        return carry

    def drain(r, carry):
        for k in range(TOP_K):
            row_copy(r, k).wait()
        return carry

    lax.fori_loop(0, tm, issue, 0)
    lax.fori_loop(0, tm, drain, 0)

    gate = gate_ref[...]
    ff_hi = jnp.zeros((tm, HALF_D), F32)
    ff_lo = jnp.zeros((tm, HALF_D), F32)
    for k in range(TOP_K):
        hi, lo = _unpack_bf16_pairs(rows_ref[k])
        gk = gate[:, k:k + 1]
        ff_hi = ff_hi + gk * hi
        ff_lo = ff_lo + gk * lo
    ff = jnp.concatenate([ff_hi, ff_lo], axis=1)
    out_ref[...] = _layer_norm_rows(DEEPNORM_ALPHA * h_ref[...] + ff, g_ref[...], b_ref[...])


def _combine(dest, gates, h2d, ln_g, ln_b, yb):
    t = h2d.shape[0]
    tm = min(COMBINE_TM, t)
    dest3 = dest.reshape(t // tm, 1, tm * TOP_K)
    row = lambda i: (i, 0)
    const = lambda i: (0, 0)
    return pl.pallas_call(
        _combine_kernel,
        grid=(t // tm,),
        in_specs=[
            pl.BlockSpec((1, 1, tm * TOP_K), lambda i: (i, 0, 0), memory_space=pltpu.SMEM),
            pl.BlockSpec((tm, LANES), row),
            pl.BlockSpec((tm, D_MODEL), row),
            pl.BlockSpec((1, D_MODEL), const),
            pl.BlockSpec((1, D_MODEL), const),
            pl.BlockSpec(memory_space=pl.ANY),
        ],
        out_specs=pl.BlockSpec((tm, D_MODEL), row),
        out_shape=jax.ShapeDtypeStruct((t, D_MODEL), F32),
        scratch_shapes=[pltpu.VMEM((TOP_K, tm, HALF_D), jnp.uint32), pltpu.SemaphoreType.DMA(())],
        compiler_params=_cparams("arbitrary"),
        name="moe_combine",
    )(dest3, gates, h2d, ln_g.reshape(1, -1), ln_b.reshape(1, -1), yb)


def _moe_plan(route, counts_f32, n_tokens):
    eid = route[:, :TOP_K]
    rank = route[:, TOP_K:2 * TOP_K]
    counts = counts_f32[0, :N_EXPERTS].astype(jnp.int32)
    padded = ((counts + MOE_BLOCK - 1) // MOE_BLOCK) * MOE_BLOCK
    pend = jnp.cumsum(padded)
    pstart = pend - padded
    dest = pstart[eid] + rank
    n_assign = n_tokens * TOP_K
    n_slots = ((n_assign + MOE_BLOCK - 1) // MOE_BLOCK) * MOE_BLOCK + N_EXPERTS * MOE_BLOCK
    nb = n_slots // MOE_BLOCK
    block_start = jnp.arange(nb, dtype=jnp.int32) * MOE_BLOCK
    block_expert = jnp.sum((pend[None, :] <= block_start[:, None]).astype(jnp.int32), axis=1)
    block_expert = jnp.minimum(block_expert, N_EXPERTS - 1).astype(jnp.int32)
    n_used = (pend[-1:] // MOE_BLOCK).astype(jnp.int32)
    return dest.astype(jnp.int32), block_expert, n_used, n_slots


def _trunk(x, p):
    batch, seq_len, _ = x.shape
    t = batch * seq_len
    xn, u, qv, lf, kk, og = _in_proj(x.reshape(t, D_MODEL), p["ln_in_g"], p["ln_in_b"], p["lb2"], p["w_in"])
    zt = _fourier_mix(u, batch, seq_len, p["fourier_norm_g"])
    per_seq = lambda a: a.reshape(batch, seq_len, -1)
    o = _gla(per_seq(qv), per_seq(lf), per_seq(kk))
    h, hpk, route, gates, counts = _out_proj(
        zt, o.reshape(t, HGRN_WIDTH), og, xn, seq_len, p["norm_g6"], p["w_out"], p["ln1_g"], p["ln1_b"],
        p["rw_hi"], p["rw_lo"], p["rb_pad"])
    dest, block_expert, n_used, n_slots = _moe_plan(route, counts, t)
    xb = _dispatch(dest, hpk, n_slots)
    yb = _experts(block_expert, n_used, xb, p["w_gu"], p["b_gu"], p["w_dn"], p["b_dn"])
    y = _combine(dest, gates, h, p["ln2_g"], p["ln2_b"], yb)
    return y.reshape(batch, seq_len, D_MODEL)


def _prepare_params(ln_in_g, ln_in_b, w_in, fourier_norm_g, lb_gamma, hgrn_norm_g, w_out, ln1_g, ln1_b,
                    router_w, router_b, w_gate_up, b_gate_up, w_down, b_down, ln2_g, ln2_b):
    fw, hw = FOURIER_WIDTH, HGRN_WIDTH
    w0 = w_in[0]
    lb_all = jnp.cumsum(jax.nn.softmax(lb_gamma.astype(F32), axis=1), axis=1)
    rw = router_w[0].astype(F32)
    rw_hi = rw.astype(BF16)
    rw_lo = (rw - rw_hi.astype(F32)).astype(BF16)
    pad = ((0, 0), (0, LANES - N_EXPERTS))
    del fw, hw
    return dict(
        ln_in_g=ln_in_g, ln_in_b=ln_in_b, w_in=w0.astype(BF16),
        fourier_norm_g=fourier_norm_g[0],
        lb2=jnp.concatenate([lb_all[0, 0], lb_all[1, 0]]).reshape(1, -1),
        norm_g6=jnp.tile(hgrn_norm_g[0].astype(F32), HEADS).reshape(1, -1),
        w_out=w_out[0].astype(BF16), ln1_g=ln1_g[0], ln1_b=ln1_b[0],
        rw_hi=jnp.pad(rw_hi, pad), rw_lo=jnp.pad(rw_lo, pad),
        rb_pad=jnp.pad(router_b[0].astype(F32).reshape(1, -1), pad),
        w_gu=w_gate_up[0].astype(BF16), b_gu=b_gate_up[0].reshape(N_EXPERTS, 1, -1),
        w_dn=w_down[0].astype(BF16), b_dn=b_down[0].reshape(N_EXPERTS, 1, -1),
        ln2_g=ln2_g[0], ln2_b=ln2_b[0],
    )


def kernel(x_prompt, x_sample, ln_in_g, ln_in_b, w_in, fourier_norm_g, lb_gamma, hgrn_norm_g, w_out,
           ln1_g, ln1_b, router_w, router_b, w_gate_up, b_gate_up, w_down, b_down, ln2_g, ln2_b):
    p = _prepare_params(ln_in_g, ln_in_b, w_in, fourier_norm_g, lb_gamma, hgrn_norm_g, w_out, ln1_g, ln1_b,
                        router_w, router_b, w_gate_up, b_gate_up, w_down, b_down, ln2_g, ln2_b)
    return (_trunk(x_prompt, p), _trunk(x_sample, p))
```

```python
import functools
import math

import numpy as np
import jax
import jax.numpy as jnp
from jax import lax
from jax.experimental import pallas as pl
from jax.experimental.pallas import tpu as pltpu

D_MODEL = 1024
FOURIER_WIDTH = 256
FOURIER_GROUP_DIM = 64
HGRN_WIDTH = 768
HEAD_DIM = 128
HEADS = 6
CHUNK = 64
N_EXPERTS = 32
TOP_K = 4
D_FF = 1024
SWIGLU_LIMIT = 7.0
SWIGLU_ALPHA = 1.702
MOE_BLOCK = 512
LN_EPS = 1e-5
RMS_EPS = 1e-6
DEEPNORM_ALPHA = 2.0 ** 0.25

LANES = 128
VMEM_LIMIT_BYTES = 56 * 1024 * 1024

F32 = jnp.float32
BF16 = jnp.bfloat16


def _cparams(*sem):
    return pltpu.CompilerParams(dimension_semantics=sem, vmem_limit_bytes=VMEM_LIMIT_BYTES)


def _layer_norm_rows(x, g, b):
    mu = jnp.mean(x, axis=-1, keepdims=True)
    xc = x - mu
    var = jnp.mean(xc * xc, axis=-1, keepdims=True)
    return xc * lax.rsqrt(var + LN_EPS) * g + b


IN_TM = 512
IN_TN = 512


def _in_proj_kernel(x_ref, g_ref, b_ref, lb_ref, w_ref, xn_ref, u_ref, qv_ref, lf_ref, kk_ref, og_ref):
    xn = _layer_norm_rows(x_ref[...], g_ref[...], b_ref[...])
    xn_ref[...] = xn
    xb = xn.astype(BF16)
    hw = HGRN_WIDTH

    def chunks(col, width):
        for c0 in range(0, width, IN_TN):
            cw = min(IN_TN, width - c0)
            yield c0, cw, jnp.dot(xb, w_ref[:, col + c0:col + c0 + cw], preferred_element_type=F32)

    def store_heads(ref, first_head, c0, val):
        for j in range(val.shape[1] // HEAD_DIM):
            ref[first_head + c0 // HEAD_DIM + j] = val[:, j * HEAD_DIM:(j + 1) * HEAD_DIM]

    for c0, cw, acc in chunks(0, FOURIER_WIDTH):
        u_ref[:, c0:c0 + cw] = acc.astype(BF16)
    for c0, cw, acc in chunks(FOURIER_WIDTH, hw):
        store_heads(qv_ref, 0, c0, (acc * jax.nn.sigmoid(acc) * (HEAD_DIM ** -0.5)).astype(BF16))
    for c0, cw, acc in chunks(FOURIER_WIDTH + hw, hw):
        store_heads(qv_ref, HEADS, c0, acc.astype(BF16))
    for c0, cw, acc in chunks(FOURIER_WIDTH + 2 * hw, 2 * hw):
        lb = lb_ref[:, c0:c0 + cw]
        fg = lb + (1.0 - lb) * jax.nn.sigmoid(acc)
        store_heads(lf_ref, 0, c0, jnp.log(fg))
        store_heads(kk_ref, 0, c0, (1.0 - fg).astype(BF16))
    for c0, cw, acc in chunks(FOURIER_WIDTH + 4 * hw, hw):
        og_ref[:, c0:c0 + cw] = acc.astype(BF16)


def _in_proj(x2d, ln_g, ln_b, lb2, w_bf16):
    t = x2d.shape[0]
    tm = min(IN_TM, t)
    row = lambda i: (i, 0)
    const = lambda i: (0, 0)
    flat = lambda w, dt: (pl.BlockSpec((tm, w), row), jax.ShapeDtypeStruct((t, w), dt))
    head_major = lambda nh, dt: (pl.BlockSpec((nh, tm, HEAD_DIM), lambda i: (0, i, 0)),
                                 jax.ShapeDtypeStruct((nh, t, HEAD_DIM), dt))
    outs = [flat(D_MODEL, F32), flat(FOURIER_WIDTH, BF16), head_major(2 * HEADS, BF16),
            head_major(2 * HEADS, F32), head_major(2 * HEADS, BF16), flat(HGRN_WIDTH, BF16)]
    return pl.pallas_call(
        _in_proj_kernel,
        grid=(t // tm,),
        in_specs=[
            pl.BlockSpec((tm, D_MODEL), row),
            pl.BlockSpec((1, D_MODEL), const),
            pl.BlockSpec((1, D_MODEL), const),
            pl.BlockSpec((1, 2 * HGRN_WIDTH), const),
            pl.BlockSpec(w_bf16.shape, const),
        ],
        out_specs=[o[0] for o in outs],
        out_shape=[o[1] for o in outs],
        compiler_params=_cparams("parallel"),
        name="in_proj",
    )(x2d, ln_g.reshape(1, -1), ln_b.reshape(1, -1), lb2, w_bf16)


def _fft_split(seq_len):
    n1 = 1 << ((seq_len.bit_length() - 1 + 1) // 2)
    return n1, seq_len // n1


@functools.lru_cache(maxsize=None)
def _fft_tables(seq_len):
    n1, n2 = _fft_split(seq_len)
    k1 = np.arange(n1)
    ang1 = 2.0 * np.pi * ((k1[:, None] * k1[None, :]) % n1) / n1
    s1 = 1.0 / math.sqrt(n1)
    c1, s1m = np.cos(ang1) * s1, np.sin(ang1) * s1
    l1p = np.arange(n1)[:, None, None]
    l2p = np.arange(n2)[None, :, None]
    l2 = np.arange(n2)[None, None, :]
    ang2 = 2.0 * np.pi * ((l2 * (l1p + n1 * l2p)) % seq_len) / seq_len
    s2 = 1.0 / math.sqrt(n2)
    gc, gs = np.cos(ang2) * s2, np.sin(ang2) * s2
    kc = np.arange(FOURIER_GROUP_DIM)
    angc = 2.0 * np.pi * ((kc[:, None] * kc[None, :]) % FOURIER_GROUP_DIM) / FOURIER_GROUP_DIM
    sc = 1.0 / math.sqrt(FOURIER_GROUP_DIM)
    groups = FOURIER_WIDTH // FOURIER_GROUP_DIM
    bc = np.kron(np.eye(groups), np.cos(angc) * sc)
    bs = np.kron(np.eye(groups), np.sin(angc) * sc)
    as_bf16 = lambda a: jnp.asarray(a, dtype=F32).astype(BF16)
    return tuple(as_bf16(a) for a in (c1, s1m, gc, gs, bc, bs))


def _fft1_kernel(c_ref, s_ref, u_ref, ar_ref, ai_ref):
    u = u_ref[0]
    ar_ref[0] = jnp.dot(c_ref[...], u, preferred_element_type=F32).astype(BF16)
    ai_ref[0] = (-jnp.dot(s_ref[...], u, preferred_element_type=F32)).astype(BF16)


FFT1_TN = 4096


def _fft_stage1(u3, c1, s1):
    b, n1, width = u3.shape
    tn = min(FFT1_TN, width)
    blk = pl.BlockSpec((1, n1, tn), lambda i, j: (i, 0, j))
    mat = pl.BlockSpec((n1, n1), lambda i, j: (0, 0))
    return pl.pallas_call(
        _fft1_kernel,
        grid=(b, width // tn),
        in_specs=[mat, mat, blk],
        out_specs=[blk, blk],
        out_shape=[jax.ShapeDtypeStruct(u3.shape, BF16)] * 2,
        compiler_params=_cparams("parallel", "parallel"),
        name="fft_stage1",
    )(c1, s1, u3)


FFT2_T1 = 8


def _fft2_kernel(gc_ref, gs_ref, bc_ref, bs_ref, g_ref, ar_ref, ai_ref, z_ref):
    t1 = ar_ref.shape[1]
    for j in range(t1):
        ar, ai = ar_ref[0, j], ai_ref[0, j]
        gc, gs = gc_ref[j], gs_ref[j]
        xr = jnp.dot(gc, ar, preferred_element_type=F32) + jnp.dot(gs, ai, preferred_element_type=F32)
        xi = jnp.dot(gc, ai, preferred_element_type=F32) - jnp.dot(gs, ar, preferred_element_type=F32)
        z = jnp.dot(xr.astype(BF16), bc_ref[...], preferred_element_type=F32)
        z += jnp.dot(xi.astype(BF16), bs_ref[...], preferred_element_type=F32)
        z = z * lax.rsqrt(jnp.mean(z * z, axis=-1, keepdims=True) + RMS_EPS) * g_ref[...]
        z_ref[0, j] = z.astype(BF16)


def _fft_stage2(ar4, ai4, gc, gs, bc, bs, gain):
    b, n1, n2, w = ar4.shape
    t1 = min(FFT2_T1, n1)
    a_blk = pl.BlockSpec((1, t1, n2, w), lambda i, j: (i, j, 0, 0))
    g_blk = pl.BlockSpec((t1, n2, n2), lambda i, j: (j, 0, 0))
    c_blk = pl.BlockSpec((w, w), lambda i, j: (0, 0))
    return pl.pallas_call(
        _fft2_kernel,
        grid=(b, n1 // t1),
        in_specs=[g_blk, g_blk, c_blk, c_blk, pl.BlockSpec((1, w), lambda i, j: (0, 0)), a_blk, a_blk],
        out_specs=a_blk,
        out_shape=jax.ShapeDtypeStruct(ar4.shape, BF16),
        compiler_params=_cparams("parallel", "parallel"),
        name="fft_stage2",
    )(gc, gs, bc, bs, gain.reshape(1, -1), ar4, ai4)


def _fourier_mix(u2d, batch, seq_len, gain):
    n1, n2 = _fft_split(seq_len)
    c1, s1, gc, gs, bc, bs = _fft_tables(seq_len)
    u3 = u2d.reshape(batch, n1, n2 * FOURIER_WIDTH)
    ar, ai = _fft_stage1(u3, c1, s1)
    shape4 = (batch, n1, n2, FOURIER_WIDTH)
    zt = _fft_stage2(ar.reshape(shape4), ai.reshape(shape4), gc, gs, bc, bs, gain)
    return zt.reshape(batch, n1, n2 * FOURIER_WIDTH)


GLA_LB = 512
CUMSUM_ROWS = 256


@functools.lru_cache(maxsize=None)
def _cumsum_matrices():
    r = np.arange(CUMSUM_ROWS)
    same_chunk = (r[:, None] // CHUNK) == (r[None, :] // CHUNK)
    prefix = same_chunk & (r[None, :] <= r[:, None])
    suffix = same_chunk & (r[None, :] >= r[:, None])
    return (jnp.asarray(prefix, dtype=F32).astype(BF16), jnp.asarray(suffix, dtype=F32).astype(BF16))


def _chunk_cumsum(x, tri):
    hi = x.astype(BF16)
    lo = (x - hi.astype(F32)).astype(BF16)
    parts = []
    for r0 in range(0, x.shape[0], CUMSUM_ROWS):
        rows = slice(r0, r0 + CUMSUM_ROWS)
        parts.append(jnp.dot(tri, hi[rows], preferred_element_type=F32)
                     + jnp.dot(tri, lo[rows], preferred_element_type=F32))
    return parts[0] if len(parts) == 1 else jnp.concatenate(parts, axis=0)


def _gla_direction(q, v, k, logf, tri, st_ref, reverse):
    n = q.shape[0]
    b = _chunk_cumsum(logf, tri)
    mid = CHUNK // 2 if reverse else CHUNK // 2 - 1
    last = 0 if reverse else CHUNK - 1
    t_idx = lax.broadcasted_iota(jnp.int32, (CHUNK, CHUNK), 0)
    s_idx = lax.broadcasted_iota(jnp.int32, (CHUNK, CHUNK), 1)
    visible = (t_idx <= s_idx) if reverse else (t_idx >= s_idx)
    nchunks = n // CHUNK
    order = range(nchunks - 1, -1, -1) if reverse else range(nchunks)
    nt = (((1,), (1,)), ((), ()))
    tn = (((0,), (0,)), ((), ()))
    rows = [slice(c * CHUNK, (c + 1) * CHUNK) for c in range(nchunks)]
    qe, ke, e_mid, e_last, e_gap = [], [], [], [], []
    for c in range(nchunks):
        bc = b[rows[c]]
        b_mid = bc[mid:mid + 1]
        b_last = bc[last:last + 1]
        qe.append((q[rows[c]].astype(F32) * jnp.exp(bc - b_mid)).astype(BF16))
        ke.append((k[rows[c]].astype(F32) * jnp.exp(b_mid - bc)).astype(BF16))
        e_mid.append(jnp.exp(b_mid))
        e_last.append(jnp.exp(b_last))
        e_gap.append(jnp.exp(b_last - b_mid))
    scores, delta_t = [], []
    for c in range(nchunks):
        s = lax.dot_general(qe[c], ke[c], nt, preferred_element_type=F32)
        scores.append(jnp.where(visible, s, 0.0).astype(BF16))
        delta_t.append(lax.dot_general(v[rows[c]], ke[c], tn, preferred_element_type=F32) * e_gap[c])
    st = st_ref[...]
    st_in = [None] * nchunks
    for c in order:
        st_in[c] = (st * e_mid[c]).astype(BF16)
        st = st * e_last[c] + delta_t[c]
    st_ref[...] = st
    outs = [jnp.dot(scores[c], v[rows[c]], preferred_element_type=F32)
            + lax.dot_general(qe[c], st_in[c], nt, preferred_element_type=F32) for c in range(nchunks)]
    return jnp.concatenate(outs, axis=0)


def _gla_kernel(trif_ref, trib_ref, qf_ref, vf_ref, kf_ref, lf_ref, qb_ref, vb_ref, kb_ref, lb_ref,
                o_ref, sf_ref, sb_ref):
    j = pl.program_id(2)
    nblk = pl.num_programs(2)
    lb_rows = qf_ref.shape[2]

    @pl.when(j == 0)
    def _():
        o_ref[...] = jnp.zeros(o_ref.shape, o_ref.dtype)
        sf_ref[...] = jnp.zeros(sf_ref.shape, sf_ref.dtype)
        sb_ref[...] = jnp.zeros(sb_ref.shape, sb_ref.dtype)

    o_f = _gla_direction(qf_ref[0, 0], vf_ref[0, 0], kf_ref[0, 0], lf_ref[0, 0], trif_ref[...], sf_ref, False)
    start_f = pl.multiple_of(j * lb_rows, lb_rows)
    o_ref[0, 0, pl.ds(start_f, lb_rows), :] += o_f
    o_b = _gla_direction(qb_ref[0, 0], vb_ref[0, 0], kb_ref[0, 0], lb_ref[0, 0], trib_ref[...], sb_ref, True)
    start_b = pl.multiple_of((nblk - 1 - j) * lb_rows, lb_rows)
    o_ref[0, 0, pl.ds(start_b, lb_rows), :] += o_b


def _gla(qv4, lf4, kk4):
    _, b, seq_len, _ = qv4.shape
    lbk = min(GLA_LB, seq_len)
    assert lbk % CUMSUM_ROWS == 0 and seq_len % lbk == 0
    nblk = seq_len // lbk
    blk = lambda head0, rev: pl.BlockSpec(
        (1, 1, lbk, HEAD_DIM),
        (lambda i, h, j: (head0 + h, i, nblk - 1 - j, 0)) if rev else (lambda i, h, j: (head0 + h, i, j, 0)))
    tri_spec = pl.BlockSpec((CUMSUM_ROWS, CUMSUM_ROWS), lambda i, h, j: (0, 0))
    tri_f, tri_b = _cumsum_matrices()
    return pl.pallas_call(
        _gla_kernel,
        grid=(b, HEADS, nblk),
        in_specs=[tri_spec, tri_spec,
                  blk(0, False), blk(HEADS, False), blk(0, False), blk(0, False),
                  blk(0, True), blk(HEADS, True), blk(HEADS, True), blk(HEADS, True)],
        out_specs=pl.BlockSpec((1, 1, seq_len, HEAD_DIM), lambda i, h, j: (h, i, 0, 0)),
        out_shape=jax.ShapeDtypeStruct((HEADS, b, seq_len, HEAD_DIM), F32),
        scratch_shapes=[pltpu.VMEM((HEAD_DIM, HEAD_DIM), F32)] * 2,
        compiler_params=_cparams("parallel", "parallel", "arbitrary"),
        name="gla",
    )(tri_f, tri_b, qv4, qv4, kk4, lf4, qv4, qv4, kk4, lf4)


HALF_D = D_MODEL // 2


def _pack_bf16_pairs(x):
    bits = lax.bitcast_convert_type(x.astype(BF16).astype(F32), jnp.uint32)
    return bits[:, :HALF_D] | (bits[:, HALF_D:] >> 16)


def _unpack_bf16_pairs(words):
    hi = lax.bitcast_convert_type(words & jnp.uint32(0xFFFF0000), F32)
    lo = lax.bitcast_convert_type(words << 16, F32)
    return hi, lo


OUT_TM = 512


def _out_proj_kernel(*refs, nz):
    z_refs = refs[:nz]
    (o_ref, g_ref, xn_ref, ng_ref, wz_ref, wh_ref, l1g_ref, l1b_ref, rwh_ref, rwl_ref, rb_ref,
     h_ref, hpk_ref, route_ref, gate_ref, cnt_ref, base_ref) = refs[nz:]
    tm = o_ref.shape[1]

    @pl.when(pl.program_id(0) == 0)
    def _():
        base_ref[...] = jnp.zeros(base_ref.shape, base_ref.dtype)

    z = jnp.concatenate([r[0] for r in z_refs], axis=0) if nz > 1 else z_refs[0][0]
    g = g_ref[...].astype(F32)
    normed = []
    for hd in range(HEADS):
        oh = o_ref[hd]
        normed.append(oh * lax.rsqrt(jnp.mean(oh * oh, axis=-1, keepdims=True) + RMS_EPS))
    hg = jnp.concatenate(normed, axis=1) * ng_ref[...] * (g * jax.nn.sigmoid(g))
    mixed = jnp.dot(z, wz_ref[...], preferred_element_type=F32)
    mixed += jnp.dot(hg.astype(BF16), wh_ref[...], preferred_element_type=F32)
    h = _layer_norm_rows(DEEPNORM_ALPHA * xn_ref[...] + mixed, l1g_ref[...], l1b_ref[...])
    h_ref[...] = h
    hpk_ref[...] = _pack_bf16_pairs(h)

    h_hi = h.astype(BF16)
    h_lo = (h - h_hi.astype(F32)).astype(BF16)
    logits = jnp.dot(h_hi, rwh_ref[...], preferred_element_type=F32)
    logits += jnp.dot(h_lo, rwh_ref[...], preferred_element_type=F32)
    logits += jnp.dot(h_hi, rwl_ref[...], preferred_element_type=F32)
    logits += rb_ref[...]

    lane = lax.broadcasted_iota(jnp.int32, (tm, LANES), 1)
    work = jnp.where(lane < N_EXPERTS, logits, -jnp.inf)
    vals, idxs, hits = [], [], []
    for _ in range(TOP_K):
        m = jnp.max(work, axis=-1, keepdims=True)
        idx = jnp.min(jnp.where(work == m, lane, LANES), axis=-1, keepdims=True)
        hit = lane == idx
        work = jnp.where(hit, -jnp.inf, work)
        vals.append(m)
        idxs.append(idx)
        hits.append(hit)
    exps = [jnp.exp(v - vals[0]) for v in vals]
    denom = exps[0] + exps[1] + exps[2] + exps[3]

    member = jnp.zeros((tm, LANES), F32)
    for hit in hits:
        member = member + jnp.where(hit, 1.0, 0.0)
    t_idx = lax.broadcasted_iota(jnp.int32, (tm, tm), 0)
    s_idx = lax.broadcasted_iota(jnp.int32, (tm, tm), 1)
    earlier = jnp.where(s_idx < t_idx, 1.0, 0.0).astype(BF16)
    base = base_ref[...]
    before = jnp.dot(earlier, member.astype(BF16), preferred_element_type=F32) + base

    route = jnp.zeros((tm, LANES), jnp.int32)
    gate = jnp.zeros((tm, LANES), F32)
    for k in range(TOP_K):
        rank = jnp.sum(jnp.where(hits[k], before, 0.0), axis=-1, keepdims=True).astype(jnp.int32)
        route = route + jnp.where(lane == k, idxs[k], 0) + jnp.where(lane == TOP_K + k, rank, 0)
        gate = gate + jnp.where(lane == k, exps[k] / denom, 0.0)
    route_ref[...] = route
    gate_ref[...] = gate
    total = base + jnp.sum(member, axis=0, keepdims=True)
    base_ref[...] = total
    cnt_ref[...] = total


def _out_proj(zt, o3, g2d, xn2d, seq_len, norm_g6, w_out_bf16, ln_g, ln_b, rw_hi, rw_lo, rb_pad):
    t = o3.shape[1]
    n1 = zt.shape[1]
    tm = max(min(OUT_TM, seq_len), n1)
    nz = tm // n1
    row = lambda i: (i, 0)
    const = lambda i: (0, 0)

    def z_spec(k):
        return pl.BlockSpec((1, n1, FOURIER_WIDTH),
                            lambda i: ((i * tm) // seq_len, 0, ((i * tm) % seq_len) // n1 + k))

    in_specs = [z_spec(k) for k in range(nz)] + [
        pl.BlockSpec((HEADS, tm, HEAD_DIM), lambda i: (0, i, 0)),
        pl.BlockSpec((tm, HGRN_WIDTH), row),
        pl.BlockSpec((tm, D_MODEL), row),
        pl.BlockSpec((1, HGRN_WIDTH), const),
        pl.BlockSpec((FOURIER_WIDTH, D_MODEL), const),
        pl.BlockSpec((HGRN_WIDTH, D_MODEL), const),
        pl.BlockSpec((1, D_MODEL), const),
        pl.BlockSpec((1, D_MODEL), const),
        pl.BlockSpec((D_MODEL, LANES), const),
        pl.BlockSpec((D_MODEL, LANES), const),
        pl.BlockSpec((1, LANES), const),
    ]
    out_specs = [
        pl.BlockSpec((tm, D_MODEL), row),
        pl.BlockSpec((tm, HALF_D), row),
        pl.BlockSpec((tm, LANES), row),
        pl.BlockSpec((tm, LANES), row),
        pl.BlockSpec((1, LANES), const),
    ]
    out_shape = [
        jax.ShapeDtypeStruct((t, D_MODEL), F32),
        jax.ShapeDtypeStruct((t, HALF_D), jnp.uint32),
        jax.ShapeDtypeStruct((t, LANES), jnp.int32),
        jax.ShapeDtypeStruct((t, LANES), F32),
        jax.ShapeDtypeStruct((1, LANES), F32),
    ]
    return pl.pallas_call(
        functools.partial(_out_proj_kernel, nz=nz),
        grid=(t // tm,),
        in_specs=in_specs,
        out_specs=out_specs,
        out_shape=out_shape,
        scratch_shapes=[pltpu.VMEM((1, LANES), F32)],
        compiler_params=_cparams("arbitrary"),
        name="out_proj",
    )(*([zt] * nz), o3, g2d, xn2d, norm_g6, w_out_bf16[:FOURIER_WIDTH], w_out_bf16[FOURIER_WIDTH:],
      ln_g.reshape(1, -1), ln_b.reshape(1, -1), rw_hi, rw_lo, rb_pad)


DISPATCH_TM = 256
COMBINE_TM = 128


def _dispatch_kernel(dest_ref, hpk_ref, xb_in_ref, xb_ref, sem):
    del xb_in_ref
    tm = hpk_ref.shape[0]

    def row_copy(r, k):
        slot = dest_ref[0, 0, r * TOP_K + k]
        return pltpu.make_async_copy(hpk_ref.at[pl.ds(r, 1), :], xb_ref.at[pl.ds(slot, 1), :], sem)

    def issue(r, carry):
        for k in range(TOP_K):
            row_copy(r, k).start()
        return carry

    lax.fori_loop(0, tm, issue, 0)
    for _ in range(TOP_K):
        pltpu.make_async_copy(hpk_ref, hpk_ref, sem).wait()


def _dispatch(dest, hpk, n_slots):
    t = hpk.shape[0]
    tm = min(DISPATCH_TM, t)
    dest3 = dest.reshape(t // tm, 1, tm * TOP_K)
    xb0 = jnp.zeros((n_slots, HALF_D), jnp.uint32)
    return pl.pallas_call(
        _dispatch_kernel,
        grid=(t // tm,),
        in_specs=[
            pl.BlockSpec((1, 1, tm * TOP_K), lambda i: (i, 0, 0), memory_space=pltpu.SMEM),
            pl.BlockSpec((tm, HALF_D), lambda i: (i, 0)),
            pl.BlockSpec(memory_space=pl.ANY),
        ],
        out_specs=pl.BlockSpec(memory_space=pl.ANY),
        out_shape=jax.ShapeDtypeStruct((n_slots, HALF_D), jnp.uint32),
        scratch_shapes=[pltpu.SemaphoreType.DMA(())],
        input_output_aliases={2: 0},
        compiler_params=_cparams("arbitrary"),
        name="moe_dispatch",
    )(dest3, hpk, xb0)


def _experts_kernel(be_ref, nused_ref, xb_ref, wgu_ref, bgu_ref, wdn_ref, bdn_ref, yb_ref):
    @pl.when(pl.program_id(0) < nused_ref[0])
    def _():
        x_hi, x_lo = _unpack_bf16_pairs(xb_ref[...])
        gu = jnp.dot(x_hi.astype(BF16), wgu_ref[0, :HALF_D, :], preferred_element_type=F32)
        gu += jnp.dot(x_lo.astype(BF16), wgu_ref[0, HALF_D:, :], preferred_element_type=F32)
        gu += bgu_ref[0]
        gate = jnp.minimum(gu[:, :D_FF], SWIGLU_LIMIT)
        up = jnp.clip(gu[:, D_FF:], -SWIGLU_LIMIT, SWIGLU_LIMIT)
        act = (up + 1.0) * gate * jax.nn.sigmoid(SWIGLU_ALPHA * gate)
        y = jnp.dot(act.astype(BF16), wdn_ref[0], preferred_element_type=F32) + bdn_ref[0]
        yb_ref[...] = _pack_bf16_pairs(y)

    @pl.when(pl.program_id(0) >= nused_ref[0])
    def _():
        yb_ref[...] = jnp.zeros(yb_ref.shape, yb_ref.dtype)


def _experts(block_expert, n_used, xb, wgu_bf16, bgu, wdn_bf16, bdn):
    n_slots = xb.shape[0]
    nb = n_slots // MOE_BLOCK
    blk = pl.BlockSpec((MOE_BLOCK, HALF_D), lambda i, be, nu: (i, 0))
    per_expert = lambda shape: pl.BlockSpec((1,) + shape, lambda i, be, nu: (be[i], 0, 0))
    grid_spec = pltpu.PrefetchScalarGridSpec(
        num_scalar_prefetch=2,
        grid=(nb,),
        in_specs=[blk, per_expert((D_MODEL, 2 * D_FF)), per_expert((1, 2 * D_FF)),
                  per_expert((D_FF, D_MODEL)), per_expert((1, D_MODEL))],
        out_specs=blk,
    )
    return pl.pallas_call(
        _experts_kernel,
        grid_spec=grid_spec,
        out_shape=jax.ShapeDtypeStruct((n_slots, HALF_D), jnp.uint32),
        compiler_params=_cparams("arbitrary"),
        name="moe_experts",
    )(block_expert, n_used, xb, wgu_bf16, bgu, wdn_bf16, bdn)


def _combine_kernel(dest_ref, dest_next_ref, gate_ref, h_ref, g_ref, b_ref, yb_ref, out_ref, rows_ref, sem):
    i = pl.program_id(0)
    n = pl.num_programs(0)
    tm = h_ref.shape[0]
    buf = i % 2

    def issue_tile(d_ref, b):
        def issue(r, carry):
            for k in range(TOP_K):
                slot = d_ref[0, 0, r * TOP_K + k]
                pltpu.make_async_copy(yb_ref.at[pl.ds(slot, 1), :], rows_ref.at[b, k, pl.ds(r, 1), :],
                                      sem.at[b]).start()
            return carry

        lax.fori_loop(0, tm, issue, 0)

    @pl.when(i == 0)
    def _():
        issue_tile(dest_ref, 0)

    @pl.when(i + 1 < n)
    def _():
        issue_tile(dest_next_ref, 1 - buf)

    pltpu.make_async_copy(rows_ref.at[buf], rows_ref.at[buf], sem.at[buf]).wait()

    gate = gate_ref[...]
    ff_hi = jnp.zeros((tm, HALF_D), F32)
    ff_lo = jnp.zeros((tm, HALF_D), F32)
    for k in range(TOP_K):
        hi, lo = _unpack_bf16_pairs(rows_ref[buf, k])
        gk = gate[:, k:k + 1]
        ff_hi = ff_hi + gk * hi
        ff_lo = ff_lo + gk * lo
    ff = jnp.concatenate([ff_hi, ff_lo], axis=1)
    out_ref[...] = _layer_norm_rows(DEEPNORM_ALPHA * h_ref[...] + ff, g_ref[...], b_ref[...])


def _combine(dest, gates, h2d, ln_g, ln_b, yb):
    t = h2d.shape[0]
    tm = min(COMBINE_TM, t)
    nsteps = t // tm
    dest3 = dest.reshape(nsteps, 1, tm * TOP_K)
    row = lambda i: (i, 0)
    const = lambda i: (0, 0)
    return pl.pallas_call(
        _combine_kernel,
        grid=(nsteps,),
        in_specs=[
            pl.BlockSpec((1, 1, tm * TOP_K), lambda i: (i, 0, 0), memory_space=pltpu.SMEM),
            pl.BlockSpec((1, 1, tm * TOP_K), lambda i: (jnp.minimum(i + 1, nsteps - 1), 0, 0),
                         memory_space=pltpu.SMEM),
            pl.BlockSpec((tm, LANES), row),
            pl.BlockSpec((tm, D_MODEL), row),
            pl.BlockSpec((1, D_MODEL), const),
            pl.BlockSpec((1, D_MODEL), const),
            pl.BlockSpec(memory_space=pl.ANY),
        ],
        out_specs=pl.BlockSpec((tm, D_MODEL), row),
        out_shape=jax.ShapeDtypeStruct((t, D_MODEL), F32),
        scratch_shapes=[pltpu.VMEM((2, TOP_K, tm, HALF_D), jnp.uint32), pltpu.SemaphoreType.DMA((2,))],
        compiler_params=_cparams("arbitrary"),
        name="moe_combine",
    )(dest3, dest3, gates, h2d, ln_g.reshape(1, -1), ln_b.reshape(1, -1), yb)


def _moe_plan(route, counts_f32, n_tokens):
    eid = route[:, :TOP_K]
    rank = route[:, TOP_K:2 * TOP_K]
    counts = counts_f32[0, :N_EXPERTS].astype(jnp.int32)
    padded = ((counts + MOE_BLOCK - 1) // MOE_BLOCK) * MOE_BLOCK
    pend = jnp.cumsum(padded)
    pstart = pend - padded
    dest = pstart[eid] + rank
    n_assign = n_tokens * TOP_K
    n_slots = ((n_assign + MOE_BLOCK - 1) // MOE_BLOCK) * MOE_BLOCK + N_EXPERTS * MOE_BLOCK
    nb = n_slots // MOE_BLOCK
    block_start = jnp.arange(nb, dtype=jnp.int32) * MOE_BLOCK
    block_expert = jnp.sum((pend[None, :] <= block_start[:, None]).astype(jnp.int32), axis=1)
    block_expert = jnp.minimum(block_expert, N_EXPERTS - 1).astype(jnp.int32)
    n_used = (pend[-1:] // MOE_BLOCK).astype(jnp.int32)
    return dest.astype(jnp.int32), block_expert, n_used, n_slots


def _trunk(x, p):
    batch, seq_len, _ = x.shape
    t = batch * seq_len
    xn, u, qv, lf, kk, og = _in_proj(x.reshape(t, D_MODEL), p["ln_in_g"], p["ln_in_b"], p["lb2"], p["w_in"])
    zt = _fourier_mix(u, batch, seq_len, p["fourier_norm_g"])
    per_seq = lambda a: a.reshape(a.shape[0], batch, seq_len, HEAD_DIM)
    o = _gla(per_seq(qv), per_seq(lf), per_seq(kk))
    h, hpk, route, gates, counts = _out_proj(
        zt, o.reshape(HEADS, t, HEAD_DIM), og, xn, seq_len, p["norm_g6"], p["w_out"], p["ln1_g"], p["ln1_b"],
        p["rw_hi"], p["rw_lo"], p["rb_pad"])
    dest, block_expert, n_used, n_slots = _moe_plan(route, counts, t)
    xb = _dispatch(dest, hpk, n_slots)
    yb = _experts(block_expert, n_used, xb, p["w_gu"], p["b_gu"], p["w_dn"], p["b_dn"])
    y = _combine(dest, gates, h, p["ln2_g"], p["ln2_b"], yb)
    return y.reshape(batch, seq_len, D_MODEL)


def _prepare_params(ln_in_g, ln_in_b, w_in, fourier_norm_g, lb_gamma, hgrn_norm_g, w_out, ln1_g, ln1_b,
                    router_w, router_b, w_gate_up, b_gate_up, w_down, b_down, ln2_g, ln2_b):
    fw, hw = FOURIER_WIDTH, HGRN_WIDTH
    w0 = w_in[0]
    lb_all = jnp.cumsum(jax.nn.softmax(lb_gamma.astype(F32), axis=1), axis=1)
    rw = router_w[0].astype(F32)
    rw_hi = rw.astype(BF16)
    rw_lo = (rw - rw_hi.astype(F32)).astype(BF16)
    pad = ((0, 0), (0, LANES - N_EXPERTS))
    del fw, hw
    return dict(
        ln_in_g=ln_in_g, ln_in_b=ln_in_b, w_in=w0.astype(BF16),
        fourier_norm_g=fourier_norm_g[0],
        lb2=jnp.concatenate([lb_all[0, 0], lb_all[1, 0]]).reshape(1, -1),
        norm_g6=jnp.tile(hgrn_norm_g[0].astype(F32), HEADS).reshape(1, -1),
        w_out=w_out[0].astype(BF16), ln1_g=ln1_g[0], ln1_b=ln1_b[0],
        rw_hi=jnp.pad(rw_hi, pad), rw_lo=jnp.pad(rw_lo, pad),
        rb_pad=jnp.pad(router_b[0].astype(F32).reshape(1, -1), pad),
        w_gu=w_gate_up[0].astype(BF16), b_gu=b_gate_up[0].reshape(N_EXPERTS, 1, -1),
        w_dn=w_down[0].astype(BF16), b_dn=b_down[0].reshape(N_EXPERTS, 1, -1),
        ln2_g=ln2_g[0], ln2_b=ln2_b[0],
    )


def kernel(x_prompt, x_sample, ln_in_g, ln_in_b, w_in, fourier_norm_g, lb_gamma, hgrn_norm_g, w_out,
           ln1_g, ln1_b, router_w, router_b, w_gate_up, b_gate_up, w_down, b_down, ln2_g, ln2_b):
    p = _prepare_params(ln_in_g, ln_in_b, w_in, fourier_norm_g, lb_gamma, hgrn_norm_g, w_out, ln1_g, ln1_b,
                        router_w, router_b, w_gate_up, b_gate_up, w_down, b_down, ln2_g, ln2_b)
    return (_trunk(x_prompt, p), _trunk(x_sample, p))
```

```python
import functools
import math

import numpy as np
import jax
import jax.numpy as jnp
from jax import lax
from jax.experimental import pallas as pl
from jax.experimental.pallas import tpu as pltpu
from jax.experimental.pallas import tpu_sc as plsc

D_MODEL = 1024
FOURIER_WIDTH = 256
FOURIER_GROUP_DIM = 64
HGRN_WIDTH = 768
HEAD_DIM = 128
HEADS = 6
CHUNK = 64
N_EXPERTS = 32
TOP_K = 4
D_FF = 1024
SWIGLU_LIMIT = 7.0
SWIGLU_ALPHA = 1.702
MOE_BLOCK = 512
LN_EPS = 1e-5
RMS_EPS = 1e-6
DEEPNORM_ALPHA = 2.0 ** 0.25

LANES = 128
VMEM_LIMIT_BYTES = 56 * 1024 * 1024

F32 = jnp.float32
BF16 = jnp.bfloat16


def _cparams(*sem):
    return pltpu.CompilerParams(dimension_semantics=sem, vmem_limit_bytes=VMEM_LIMIT_BYTES)


def _layer_norm_rows(x, g, b):
    mu = jnp.mean(x, axis=-1, keepdims=True)
    xc = x - mu
    var = jnp.mean(xc * xc, axis=-1, keepdims=True)
    return xc * lax.rsqrt(var + LN_EPS) * g + b


IN_TM = 512
IN_TN = 512


def _in_proj_kernel(x_ref, g_ref, b_ref, lb_ref, w_ref, xn_ref, u_ref, qv_ref, lf_ref, kk_ref, og_ref):
    xn = _layer_norm_rows(x_ref[...], g_ref[...], b_ref[...])
    xn_ref[...] = xn
    xb = xn.astype(BF16)
    hw = HGRN_WIDTH

    def chunks(col, width):
        for c0 in range(0, width, IN_TN):
            cw = min(IN_TN, width - c0)
            yield c0, cw, jnp.dot(xb, w_ref[:, col + c0:col + c0 + cw], preferred_element_type=F32)

    def store_heads(ref, first_head, c0, val):
        for j in range(val.shape[1] // HEAD_DIM):
            ref[first_head + c0 // HEAD_DIM + j] = val[:, j * HEAD_DIM:(j + 1) * HEAD_DIM]

    for c0, cw, acc in chunks(0, FOURIER_WIDTH):
        u_ref[:, c0:c0 + cw] = acc.astype(BF16)
    for c0, cw, acc in chunks(FOURIER_WIDTH, hw):
        store_heads(qv_ref, 0, c0, (acc * jax.nn.sigmoid(acc) * (HEAD_DIM ** -0.5)).astype(BF16))
    for c0, cw, acc in chunks(FOURIER_WIDTH + hw, hw):
        store_heads(qv_ref, HEADS, c0, acc.astype(BF16))
    for c0, cw, acc in chunks(FOURIER_WIDTH + 2 * hw, 2 * hw):
        lb = lb_ref[:, c0:c0 + cw]
        fg = lb + (1.0 - lb) * jax.nn.sigmoid(acc)
        store_heads(lf_ref, 0, c0, jnp.log(fg))
        store_heads(kk_ref, 0, c0, (1.0 - fg).astype(BF16))
    for c0, cw, acc in chunks(FOURIER_WIDTH + 4 * hw, hw):
        og_ref[:, c0:c0 + cw] = acc.astype(BF16)


def _in_proj(x2d, ln_g, ln_b, lb2, w_bf16):
    t = x2d.shape[0]
    tm = min(IN_TM, t)
    row = lambda i: (i, 0)
    const = lambda i: (0, 0)
    flat = lambda w, dt: (pl.BlockSpec((tm, w), row), jax.ShapeDtypeStruct((t, w), dt))
    head_major = lambda nh, dt: (pl.BlockSpec((nh, tm, HEAD_DIM), lambda i: (0, i, 0)),
                                 jax.ShapeDtypeStruct((nh, t, HEAD_DIM), dt))
    outs = [flat(D_MODEL, F32), flat(FOURIER_WIDTH, BF16), head_major(2 * HEADS, BF16),
            head_major(2 * HEADS, F32), head_major(2 * HEADS, BF16), flat(HGRN_WIDTH, BF16)]
    return pl.pallas_call(
        _in_proj_kernel,
        grid=(t // tm,),
        in_specs=[
            pl.BlockSpec((tm, D_MODEL), row),
            pl.BlockSpec((1, D_MODEL), const),
            pl.BlockSpec((1, D_MODEL), const),
            pl.BlockSpec((1, 2 * HGRN_WIDTH), const),
            pl.BlockSpec(w_bf16.shape, const),
        ],
        out_specs=[o[0] for o in outs],
        out_shape=[o[1] for o in outs],
        compiler_params=_cparams("parallel"),
        name="in_proj",
    )(x2d, ln_g.reshape(1, -1), ln_b.reshape(1, -1), lb2, w_bf16)


def _fft_split(seq_len):
    n1 = 1 << ((seq_len.bit_length() - 1 + 1) // 2)
    return n1, seq_len // n1


@functools.lru_cache(maxsize=None)
def _fft_tables(seq_len):
    n1, n2 = _fft_split(seq_len)
    k1 = np.arange(n1)
    ang1 = 2.0 * np.pi * ((k1[:, None] * k1[None, :]) % n1) / n1
    s1 = 1.0 / math.sqrt(n1)
    c1, s1m = np.cos(ang1) * s1, np.sin(ang1) * s1
    l1p = np.arange(n1)[:, None, None]
    l2p = np.arange(n2)[None, :, None]
    l2 = np.arange(n2)[None, None, :]
    ang2 = 2.0 * np.pi * ((l2 * (l1p + n1 * l2p)) % seq_len) / seq_len
    s2 = 1.0 / math.sqrt(n2)
    gc, gs = np.cos(ang2) * s2, np.sin(ang2) * s2
    kc = np.arange(FOURIER_GROUP_DIM)
    angc = 2.0 * np.pi * ((kc[:, None] * kc[None, :]) % FOURIER_GROUP_DIM) / FOURIER_GROUP_DIM
    sc = 1.0 / math.sqrt(FOURIER_GROUP_DIM)
    groups = FOURIER_WIDTH // FOURIER_GROUP_DIM
    bc = np.kron(np.eye(groups), np.cos(angc) * sc)
    bs = np.kron(np.eye(groups), np.sin(angc) * sc)
    as_bf16 = lambda a: jnp.asarray(a, dtype=F32).astype(BF16)
    return tuple(as_bf16(a) for a in (c1, s1m, gc, gs, bc, bs))


def _fft1_kernel(c_ref, s_ref, u_ref, ar_ref, ai_ref):
    u = u_ref[0]
    ar_ref[0] = jnp.dot(c_ref[...], u, preferred_element_type=F32).astype(BF16)
    ai_ref[0] = (-jnp.dot(s_ref[...], u, preferred_element_type=F32)).astype(BF16)


FFT1_TN = 4096


def _fft_stage1(u3, c1, s1):
    b, n1, width = u3.shape
    tn = min(FFT1_TN, width)
    blk = pl.BlockSpec((1, n1, tn), lambda i, j: (i, 0, j))
    mat = pl.BlockSpec((n1, n1), lambda i, j: (0, 0))
    return pl.pallas_call(
        _fft1_kernel,
        grid=(b, width // tn),
        in_specs=[mat, mat, blk],
        out_specs=[blk, blk],
        out_shape=[jax.ShapeDtypeStruct(u3.shape, BF16)] * 2,
        compiler_params=_cparams("parallel", "parallel"),
        name="fft_stage1",
    )(c1, s1, u3)


FFT2_T1 = 8


def _fft2_kernel(gc_ref, gs_ref, bc_ref, bs_ref, g_ref, ar_ref, ai_ref, z_ref):
    t1 = ar_ref.shape[1]
    for j in range(t1):
        ar, ai = ar_ref[0, j], ai_ref[0, j]
        gc, gs = gc_ref[j], gs_ref[j]
        xr = jnp.dot(gc, ar, preferred_element_type=F32) + jnp.dot(gs, ai, preferred_element_type=F32)
        xi = jnp.dot(gc, ai, preferred_element_type=F32) - jnp.dot(gs, ar, preferred_element_type=F32)
        z = jnp.dot(xr.astype(BF16), bc_ref[...], preferred_element_type=F32)
        z += jnp.dot(xi.astype(BF16), bs_ref[...], preferred_element_type=F32)
        z = z * lax.rsqrt(jnp.mean(z * z, axis=-1, keepdims=True) + RMS_EPS) * g_ref[...]
        z_ref[0, j] = z.astype(BF16)


def _fft_stage2(ar4, ai4, gc, gs, bc, bs, gain):
    b, n1, n2, w = ar4.shape
    t1 = min(FFT2_T1, n1)
    a_blk = pl.BlockSpec((1, t1, n2, w), lambda i, j: (i, j, 0, 0))
    g_blk = pl.BlockSpec((t1, n2, n2), lambda i, j: (j, 0, 0))
    c_blk = pl.BlockSpec((w, w), lambda i, j: (0, 0))
    return pl.pallas_call(
        _fft2_kernel,
        grid=(b, n1 // t1),
        in_specs=[g_blk, g_blk, c_blk, c_blk, pl.BlockSpec((1, w), lambda i, j: (0, 0)), a_blk, a_blk],
        out_specs=a_blk,
        out_shape=jax.ShapeDtypeStruct(ar4.shape, BF16),
        compiler_params=_cparams("parallel", "parallel"),
        name="fft_stage2",
    )(gc, gs, bc, bs, gain.reshape(1, -1), ar4, ai4)


def _fourier_mix(u2d, batch, seq_len, gain):
    n1, n2 = _fft_split(seq_len)
    c1, s1, gc, gs, bc, bs = _fft_tables(seq_len)
    u3 = u2d.reshape(batch, n1, n2 * FOURIER_WIDTH)
    ar, ai = _fft_stage1(u3, c1, s1)
    shape4 = (batch, n1, n2, FOURIER_WIDTH)
    zt = _fft_stage2(ar.reshape(shape4), ai.reshape(shape4), gc, gs, bc, bs, gain)
    return zt.reshape(batch, n1, n2 * FOURIER_WIDTH)


GLA_LB = 512
CUMSUM_ROWS = 256


@functools.lru_cache(maxsize=None)
def _cumsum_matrices():
    r = np.arange(CUMSUM_ROWS)
    same_chunk = (r[:, None] // CHUNK) == (r[None, :] // CHUNK)
    prefix = same_chunk & (r[None, :] <= r[:, None])
    suffix = same_chunk & (r[None, :] >= r[:, None])
    return (jnp.asarray(prefix, dtype=F32).astype(BF16), jnp.asarray(suffix, dtype=F32).astype(BF16))


def _chunk_cumsum(x, tri):
    hi = x.astype(BF16)
    lo = (x - hi.astype(F32)).astype(BF16)
    parts = []
    for r0 in range(0, x.shape[0], CUMSUM_ROWS):
        rows = slice(r0, r0 + CUMSUM_ROWS)
        parts.append(jnp.dot(tri, hi[rows], preferred_element_type=F32)
                     + jnp.dot(tri, lo[rows], preferred_element_type=F32))
    return parts[0] if len(parts) == 1 else jnp.concatenate(parts, axis=0)


def _gla_direction(q, v, k, logf, tri, st_ref, reverse):
    n = q.shape[0]
    b = _chunk_cumsum(logf, tri)
    mid = CHUNK // 2 if reverse else CHUNK // 2 - 1
    last = 0 if reverse else CHUNK - 1
    t_idx = lax.broadcasted_iota(jnp.int32, (CHUNK, CHUNK), 0)
    s_idx = lax.broadcasted_iota(jnp.int32, (CHUNK, CHUNK), 1)
    visible = (t_idx <= s_idx) if reverse else (t_idx >= s_idx)
    nchunks = n // CHUNK
    order = range(nchunks - 1, -1, -1) if reverse else range(nchunks)
    nt = (((1,), (1,)), ((), ()))
    tn = (((0,), (0,)), ((), ()))
    rows = [slice(c * CHUNK, (c + 1) * CHUNK) for c in range(nchunks)]
    qe, ke, e_mid, e_last, e_gap = [], [], [], [], []
    for c in range(nchunks):
        bc = b[rows[c]]
        b_mid = bc[mid:mid + 1]
        b_last = bc[last:last + 1]
        qe.append((q[rows[c]].astype(F32) * jnp.exp(bc - b_mid)).astype(BF16))
        ke.append((k[rows[c]].astype(F32) * jnp.exp(b_mid - bc)).astype(BF16))
        e_mid.append(jnp.exp(b_mid))
        e_last.append(jnp.exp(b_last))
        e_gap.append(jnp.exp(b_last - b_mid))
    scores, delta_t = [], []
    for c in range(nchunks):
        s = lax.dot_general(qe[c], ke[c], nt, preferred_element_type=F32)
        scores.append(jnp.where(visible, s, 0.0).astype(BF16))
        delta_t.append(lax.dot_general(v[rows[c]], ke[c], tn, preferred_element_type=F32) * e_gap[c])
    st = st_ref[...]
    st_in = [None] * nchunks
    for c in order:
        st_in[c] = (st * e_mid[c]).astype(BF16)
        st = st * e_last[c] + delta_t[c]
    st_ref[...] = st
    outs = [jnp.dot(scores[c], v[rows[c]], preferred_element_type=F32)
            + lax.dot_general(qe[c], st_in[c], nt, preferred_element_type=F32) for c in range(nchunks)]
    return jnp.concatenate(outs, axis=0)


def _gla_kernel(trif_ref, trib_ref, qf_ref, vf_ref, kf_ref, lf_ref, qb_ref, vb_ref, kb_ref, lb_ref,
                o_ref, sf_ref, sb_ref):
    j = pl.program_id(2)
    nblk = pl.num_programs(2)
    lb_rows = qf_ref.shape[2]

    @pl.when(j == 0)
    def _():
        o_ref[...] = jnp.zeros(o_ref.shape, o_ref.dtype)
        sf_ref[...] = jnp.zeros(sf_ref.shape, sf_ref.dtype)
        sb_ref[...] = jnp.zeros(sb_ref.shape, sb_ref.dtype)

    o_f = _gla_direction(qf_ref[0, 0], vf_ref[0, 0], kf_ref[0, 0], lf_ref[0, 0], trif_ref[...], sf_ref, False)
    start_f = pl.multiple_of(j * lb_rows, lb_rows)
    o_ref[0, 0, pl.ds(start_f, lb_rows), :] += o_f
    o_b = _gla_direction(qb_ref[0, 0], vb_ref[0, 0], kb_ref[0, 0], lb_ref[0, 0], trib_ref[...], sb_ref, True)
    start_b = pl.multiple_of((nblk - 1 - j) * lb_rows, lb_rows)
    o_ref[0, 0, pl.ds(start_b, lb_rows), :] += o_b


def _gla(qv4, lf4, kk4):
    _, b, seq_len, _ = qv4.shape
    lbk = min(GLA_LB, seq_len)
    assert lbk % CUMSUM_ROWS == 0 and seq_len % lbk == 0
    nblk = seq_len // lbk
    blk = lambda head0, rev: pl.BlockSpec(
        (1, 1, lbk, HEAD_DIM),
        (lambda i, h, j: (head0 + h, i, nblk - 1 - j, 0)) if rev else (lambda i, h, j: (head0 + h, i, j, 0)))
    tri_spec = pl.BlockSpec((CUMSUM_ROWS, CUMSUM_ROWS), lambda i, h, j: (0, 0))
    tri_f, tri_b = _cumsum_matrices()
    return pl.pallas_call(
        _gla_kernel,
        grid=(b, HEADS, nblk),
        in_specs=[tri_spec, tri_spec,
                  blk(0, False), blk(HEADS, False), blk(0, False), blk(0, False),
                  blk(0, True), blk(HEADS, True), blk(HEADS, True), blk(HEADS, True)],
        out_specs=pl.BlockSpec((1, 1, seq_len, HEAD_DIM), lambda i, h, j: (h, i, 0, 0)),
        out_shape=jax.ShapeDtypeStruct((HEADS, b, seq_len, HEAD_DIM), F32),
        scratch_shapes=[pltpu.VMEM((HEAD_DIM, HEAD_DIM), F32)] * 2,
        compiler_params=_cparams("parallel", "parallel", "arbitrary"),
        name="gla",
    )(tri_f, tri_b, qv4, qv4, kk4, lf4, qv4, qv4, kk4, lf4)


HALF_D = D_MODEL // 2


def _pack_bf16_pairs(x):
    bits = lax.bitcast_convert_type(x.astype(BF16).astype(F32), jnp.uint32)
    return bits[:, :HALF_D] | (bits[:, HALF_D:] >> 16)


def _unpack_bf16_pairs(words):
    hi = lax.bitcast_convert_type(words & jnp.uint32(0xFFFF0000), F32)
    lo = lax.bitcast_convert_type(words << 16, F32)
    return hi, lo


OUT_TM = 512


def _out_proj_kernel(*refs, nz):
    z_refs = refs[:nz]
    (o_ref, g_ref, xn_ref, ng_ref, wz_ref, wh_ref, l1g_ref, l1b_ref, rwh_ref, rwl_ref, rb_ref,
     h_ref, hpk_ref, route_ref, gate_ref, cnt_ref, base_ref) = refs[nz:]
    tm = o_ref.shape[1]

    @pl.when(pl.program_id(0) == 0)
    def _():
        base_ref[...] = jnp.zeros(base_ref.shape, base_ref.dtype)

    z = jnp.concatenate([r[0] for r in z_refs], axis=0) if nz > 1 else z_refs[0][0]
    g = g_ref[...].astype(F32)
    normed = []
    for hd in range(HEADS):
        oh = o_ref[hd]
        normed.append(oh * lax.rsqrt(jnp.mean(oh * oh, axis=-1, keepdims=True) + RMS_EPS))
    hg = jnp.concatenate(normed, axis=1) * ng_ref[...] * (g * jax.nn.sigmoid(g))
    mixed = jnp.dot(z, wz_ref[...], preferred_element_type=F32)
    mixed += jnp.dot(hg.astype(BF16), wh_ref[...], preferred_element_type=F32)
    h = _layer_norm_rows(DEEPNORM_ALPHA * xn_ref[...] + mixed, l1g_ref[...], l1b_ref[...])
    h_ref[...] = h
    hpk_ref[...] = _pack_bf16_pairs(h)

    h_hi = h.astype(BF16)
    h_lo = (h - h_hi.astype(F32)).astype(BF16)
    logits = jnp.dot(h_hi, rwh_ref[...], preferred_element_type=F32)
    logits += jnp.dot(h_lo, rwh_ref[...], preferred_element_type=F32)
    logits += jnp.dot(h_hi, rwl_ref[...], preferred_element_type=F32)
    logits += rb_ref[...]

    lane = lax.broadcasted_iota(jnp.int32, (tm, LANES), 1)
    work = jnp.where(lane < N_EXPERTS, logits, -jnp.inf)
    vals, idxs, hits = [], [], []
    for _ in range(TOP_K):
        m = jnp.max(work, axis=-1, keepdims=True)
        idx = jnp.min(jnp.where(work == m, lane, LANES), axis=-1, keepdims=True)
        hit = lane == idx
        work = jnp.where(hit, -jnp.inf, work)
        vals.append(m)
        idxs.append(idx)
        hits.append(hit)
    exps = [jnp.exp(v - vals[0]) for v in vals]
    denom = exps[0] + exps[1] + exps[2] + exps[3]

    member = jnp.zeros((tm, LANES), F32)
    for hit in hits:
        member = member + jnp.where(hit, 1.0, 0.0)
    t_idx = lax.broadcasted_iota(jnp.int32, (tm, tm), 0)
    s_idx = lax.broadcasted_iota(jnp.int32, (tm, tm), 1)
    earlier = jnp.where(s_idx < t_idx, 1.0, 0.0).astype(BF16)
    base = base_ref[...]
    before = jnp.dot(earlier, member.astype(BF16), preferred_element_type=F32) + base

    route = jnp.zeros((tm, LANES), jnp.int32)
    gate = jnp.zeros((tm, LANES), F32)
    for k in range(TOP_K):
        rank = jnp.sum(jnp.where(hits[k], before, 0.0), axis=-1, keepdims=True).astype(jnp.int32)
        route = route + jnp.where(lane == k, idxs[k], 0) + jnp.where(lane == TOP_K + k, rank, 0)
        gate = gate + jnp.where(lane == k, exps[k] / denom, 0.0)
    route_ref[...] = route
    gate_ref[...] = gate
    total = base + jnp.sum(member, axis=0, keepdims=True)
    base_ref[...] = total
    cnt_ref[...] = total


def _out_proj(zt, o3, g2d, xn2d, seq_len, norm_g6, w_out_bf16, ln_g, ln_b, rw_hi, rw_lo, rb_pad):
    t = o3.shape[1]
    n1 = zt.shape[1]
    tm = max(min(OUT_TM, seq_len), n1)
    nz = tm // n1
    row = lambda i: (i, 0)
    const = lambda i: (0, 0)

    def z_spec(k):
        return pl.BlockSpec((1, n1, FOURIER_WIDTH),
                            lambda i: ((i * tm) // seq_len, 0, ((i * tm) % seq_len) // n1 + k))

    in_specs = [z_spec(k) for k in range(nz)] + [
        pl.BlockSpec((HEADS, tm, HEAD_DIM), lambda i: (0, i, 0)),
        pl.BlockSpec((tm, HGRN_WIDTH), row),
        pl.BlockSpec((tm, D_MODEL), row),
        pl.BlockSpec((1, HGRN_WIDTH), const),
        pl.BlockSpec((FOURIER_WIDTH, D_MODEL), const),
        pl.BlockSpec((HGRN_WIDTH, D_MODEL), const),
        pl.BlockSpec((1, D_MODEL), const),
        pl.BlockSpec((1, D_MODEL), const),
        pl.BlockSpec((D_MODEL, LANES), const),
        pl.BlockSpec((D_MODEL, LANES), const),
        pl.BlockSpec((1, LANES), const),
    ]
    out_specs = [
        pl.BlockSpec((tm, D_MODEL), row),
        pl.BlockSpec((tm, HALF_D), row),
        pl.BlockSpec((tm, LANES), row),
        pl.BlockSpec((tm, LANES), row),
        pl.BlockSpec((1, LANES), const),
    ]
    out_shape = [
        jax.ShapeDtypeStruct((t, D_MODEL), F32),
        jax.ShapeDtypeStruct((t, HALF_D), jnp.uint32),
        jax.ShapeDtypeStruct((t, LANES), jnp.int32),
        jax.ShapeDtypeStruct((t, LANES), F32),
        jax.ShapeDtypeStruct((1, LANES), F32),
    ]
    return pl.pallas_call(
        functools.partial(_out_proj_kernel, nz=nz),
        grid=(t // tm,),
        in_specs=in_specs,
        out_specs=out_specs,
        out_shape=out_shape,
        scratch_shapes=[pltpu.VMEM((1, LANES), F32)],
        compiler_params=_cparams("arbitrary"),
        name="out_proj",
    )(*([zt] * nz), o3, g2d, xn2d, norm_g6, w_out_bf16[:FOURIER_WIDTH], w_out_bf16[FOURIER_WIDTH:],
      ln_g.reshape(1, -1), ln_b.reshape(1, -1), rw_hi, rw_lo, rb_pad)


COMBINE_TM = 256
SC_WINDOW = 128


def _sc_mesh():
    return plsc.VectorSubcoreMesh(core_axis_name="core", subcore_axis_name="subcore")


def _dispatch(dest_kt, hpk, n_slots):
    t, d = hpk.shape
    rows = pl.BlockSpec((SC_WINDOW, d), index_map=lambda i: (i, 0), pipeline_mode=pl.Buffered(1))
    idx = pl.BlockSpec((1, SC_WINDOW), index_map=lambda i: (0, i))

    @pl.kernel(out_type=jax.ShapeDtypeStruct((n_slots, d), hpk.dtype), mesh=_sc_mesh(), name="moe_dispatch_sc")
    def scatter(x_hbm, i0_hbm, i1_hbm, i2_hbm, i3_hbm, o_hbm):
        def body(x_vmem, *idx_vmem):
            for iv in idx_vmem:
                pltpu.sync_copy(x_vmem, o_hbm.at[iv.at[0]])

        pltpu.emit_pipeline(
            body, grid=(t // SC_WINDOW,), in_specs=[rows] + [idx] * TOP_K, out_specs=[],
            core_axis_name=("core", "subcore"), dimension_semantics=(pltpu.PARALLEL,),
        )(x_hbm, i0_hbm, i1_hbm, i2_hbm, i3_hbm)

    return scatter(hpk, *[dest_kt[k].reshape(1, t) for k in range(TOP_K)])


def _gather_rows(yb, dest_kt):
    _, t = dest_kt.shape
    d = yb.shape[1]
    n = TOP_K * t

    @pl.kernel(out_type=jax.ShapeDtypeStruct((n, d), yb.dtype), mesh=_sc_mesh(), name="moe_gather_sc")
    def gather(y_hbm, i_hbm, o_hbm):
        def body(i_vmem, o_vmem):
            pltpu.sync_copy(y_hbm.at[i_vmem.at[0]], o_vmem)

        pltpu.emit_pipeline(
            body, grid=(n // SC_WINDOW,),
            in_specs=[pl.BlockSpec((1, SC_WINDOW), index_map=lambda i: (0, i))],
            out_specs=[pl.BlockSpec((SC_WINDOW, d), index_map=lambda i: (i, 0), pipeline_mode=pl.Buffered(1))],
            core_axis_name=("core", "subcore"), dimension_semantics=(pltpu.PARALLEL,),
        )(i_hbm, o_hbm)

    return gather(yb, dest_kt.reshape(1, n)).reshape(TOP_K, t, d)


def _experts_kernel(be_ref, nused_ref, xb_ref, wgu_ref, bgu_ref, wdn_ref, bdn_ref, yb_ref):
    @pl.when(pl.program_id(0) < nused_ref[0])
    def _():
        x_hi, x_lo = _unpack_bf16_pairs(xb_ref[...])
        gu = jnp.dot(x_hi.astype(BF16), wgu_ref[0, :HALF_D, :], preferred_element_type=F32)
        gu += jnp.dot(x_lo.astype(BF16), wgu_ref[0, HALF_D:, :], preferred_element_type=F32)
        gu += bgu_ref[0]
        gate = jnp.minimum(gu[:, :D_FF], SWIGLU_LIMIT)
        up = jnp.clip(gu[:, D_FF:], -SWIGLU_LIMIT, SWIGLU_LIMIT)
        act = (up + 1.0) * gate * jax.nn.sigmoid(SWIGLU_ALPHA * gate)
        y = jnp.dot(act.astype(BF16), wdn_ref[0], preferred_element_type=F32) + bdn_ref[0]
        yb_ref[...] = _pack_bf16_pairs(y)

    @pl.when(pl.program_id(0) >= nused_ref[0])
    def _():
        yb_ref[...] = jnp.zeros(yb_ref.shape, yb_ref.dtype)


def _experts(block_expert, n_used, xb, wgu_bf16, bgu, wdn_bf16, bdn):
    n_slots = xb.shape[0]
    nb = n_slots // MOE_BLOCK
    blk = pl.BlockSpec((MOE_BLOCK, HALF_D), lambda i, be, nu: (i, 0))
    per_expert = lambda shape: pl.BlockSpec((1,) + shape, lambda i, be, nu: (be[i], 0, 0))
    grid_spec = pltpu.PrefetchScalarGridSpec(
        num_scalar_prefetch=2,
        grid=(nb,),
        in_specs=[blk, per_expert((D_MODEL, 2 * D_FF)), per_expert((1, 2 * D_FF)),
                  per_expert((D_FF, D_MODEL)), per_expert((1, D_MODEL))],
        out_specs=blk,
    )
    return pl.pallas_call(
        _experts_kernel,
        grid_spec=grid_spec,
        out_shape=jax.ShapeDtypeStruct((n_slots, HALF_D), jnp.uint32),
        compiler_params=_cparams("arbitrary"),
        name="moe_experts",
    )(block_expert, n_used, xb, wgu_bf16, bgu, wdn_bf16, bdn)


def _combine_kernel(rows_ref, gate_ref, h_ref, g_ref, b_ref, out_ref):
    tm = h_ref.shape[0]
    gate = gate_ref[...]
    ff_hi = jnp.zeros((tm, HALF_D), F32)
    ff_lo = jnp.zeros((tm, HALF_D), F32)
    for k in range(TOP_K):
        hi, lo = _unpack_bf16_pairs(rows_ref[k])
        gk = gate[:, k:k + 1]
        ff_hi = ff_hi + gk * hi
        ff_lo = ff_lo + gk * lo
    ff = jnp.concatenate([ff_hi, ff_lo], axis=1)
    out_ref[...] = _layer_norm_rows(DEEPNORM_ALPHA * h_ref[...] + ff, g_ref[...], b_ref[...])


def _combine(rows_kt, gates, h2d, ln_g, ln_b):
    t = h2d.shape[0]
    tm = min(COMBINE_TM, t)
    row = lambda i: (i, 0)
    const = lambda i: (0, 0)
    return pl.pallas_call(
        _combine_kernel,
        grid=(t // tm,),
        in_specs=[
            pl.BlockSpec((TOP_K, tm, HALF_D), lambda i: (0, i, 0)),
            pl.BlockSpec((tm, LANES), row),
            pl.BlockSpec((tm, D_MODEL), row),
            pl.BlockSpec((1, D_MODEL), const),
            pl.BlockSpec((1, D_MODEL), const),
        ],
        out_specs=pl.BlockSpec((tm, D_MODEL), row),
        out_shape=jax.ShapeDtypeStruct((t, D_MODEL), F32),
        compiler_params=_cparams("parallel"),
        name="moe_combine",
    )(rows_kt, gates, h2d, ln_g.reshape(1, -1), ln_b.reshape(1, -1))


def _moe_plan(route, counts_f32, n_tokens):
    eid = route[:, :TOP_K]
    rank = route[:, TOP_K:2 * TOP_K]
    counts = counts_f32[0, :N_EXPERTS].astype(jnp.int32)
    padded = ((counts + MOE_BLOCK - 1) // MOE_BLOCK) * MOE_BLOCK
    pend = jnp.cumsum(padded)
    pstart = pend - padded
    dest = pstart[eid] + rank
    n_assign = n_tokens * TOP_K
    n_slots = ((n_assign + MOE_BLOCK - 1) // MOE_BLOCK) * MOE_BLOCK + N_EXPERTS * MOE_BLOCK
    nb = n_slots // MOE_BLOCK
    block_start = jnp.arange(nb, dtype=jnp.int32) * MOE_BLOCK
    block_expert = jnp.sum((pend[None, :] <= block_start[:, None]).astype(jnp.int32), axis=1)
    block_expert = jnp.minimum(block_expert, N_EXPERTS - 1).astype(jnp.int32)
    n_used = (pend[-1:] // MOE_BLOCK).astype(jnp.int32)
    return dest.astype(jnp.int32).T, block_expert, n_used, n_slots


def _trunk(x, p):
    batch, seq_len, _ = x.shape
    t = batch * seq_len
    xn, u, qv, lf, kk, og = _in_proj(x.reshape(t, D_MODEL), p["ln_in_g"], p["ln_in_b"], p["lb2"], p["w_in"])
    zt = _fourier_mix(u, batch, seq_len, p["fourier_norm_g"])
    per_seq = lambda a: a.reshape(a.shape[0], batch, seq_len, HEAD_DIM)
    o = _gla(per_seq(qv), per_seq(lf), per_seq(kk))
    h, hpk, route, gates, counts = _out_proj(
        zt, o.reshape(HEADS, t, HEAD_DIM), og, xn, seq_len, p["norm_g6"], p["w_out"], p["ln1_g"], p["ln1_b"],
        p["rw_hi"], p["rw_lo"], p["rb_pad"])
    dest_kt, block_expert, n_used, n_slots = _moe_plan(route, counts, t)
    xb = _dispatch(dest_kt, hpk, n_slots)
    yb = _experts(block_expert, n_used, xb, p["w_gu"], p["b_gu"], p["w_dn"], p["b_dn"])
    y = _combine(_gather_rows(yb, dest_kt), gates, h, p["ln2_g"], p["ln2_b"])
    return y.reshape(batch, seq_len, D_MODEL)


def _prepare_params(ln_in_g, ln_in_b, w_in, fourier_norm_g, lb_gamma, hgrn_norm_g, w_out, ln1_g, ln1_b,
                    router_w, router_b, w_gate_up, b_gate_up, w_down, b_down, ln2_g, ln2_b):
    fw, hw = FOURIER_WIDTH, HGRN_WIDTH
    w0 = w_in[0]
    lb_all = jnp.cumsum(jax.nn.softmax(lb_gamma.astype(F32), axis=1), axis=1)
    rw = router_w[0].astype(F32)
    rw_hi = rw.astype(BF16)
    rw_lo = (rw - rw_hi.astype(F32)).astype(BF16)
    pad = ((0, 0), (0, LANES - N_EXPERTS))
    del fw, hw
    return dict(
        ln_in_g=ln_in_g, ln_in_b=ln_in_b, w_in=w0.astype(BF16),
        fourier_norm_g=fourier_norm_g[0],
        lb2=jnp.concatenate([lb_all[0, 0], lb_all[1, 0]]).reshape(1, -1),
        norm_g6=jnp.tile(hgrn_norm_g[0].astype(F32), HEADS).reshape(1, -1),
        w_out=w_out[0].astype(BF16), ln1_g=ln1_g[0], ln1_b=ln1_b[0],
        rw_hi=jnp.pad(rw_hi, pad), rw_lo=jnp.pad(rw_lo, pad),
        rb_pad=jnp.pad(router_b[0].astype(F32).reshape(1, -1), pad),
        w_gu=w_gate_up[0].astype(BF16), b_gu=b_gate_up[0].reshape(N_EXPERTS, 1, -1),
        w_dn=w_down[0].astype(BF16), b_dn=b_down[0].reshape(N_EXPERTS, 1, -1),
        ln2_g=ln2_g[0], ln2_b=ln2_b[0],
    )


def kernel(x_prompt, x_sample, ln_in_g, ln_in_b, w_in, fourier_norm_g, lb_gamma, hgrn_norm_g, w_out,
           ln1_g, ln1_b, router_w, router_b, w_gate_up, b_gate_up, w_down, b_down, ln2_g, ln2_b):
    p = _prepare_params(ln_in_g, ln_in_b, w_in, fourier_norm_g, lb_gamma, hgrn_norm_g, w_out, ln1_g, ln1_b,
                        router_w, router_b, w_gate_up, b_gate_up, w_down, b_down, ln2_g, ln2_b)
    return (_trunk(x_prompt, p), _trunk(x_sample, p))
```

```python
import functools
import math

import numpy as np
import jax
import jax.numpy as jnp
from jax import lax
from jax.experimental import pallas as pl
from jax.experimental.pallas import tpu as pltpu
from jax.experimental.pallas import tpu_sc as plsc

D_MODEL = 1024
FOURIER_WIDTH = 256
FOURIER_GROUP_DIM = 64
HGRN_WIDTH = 768
HEAD_DIM = 128
HEADS = 6
CHUNK = 64
N_EXPERTS = 32
TOP_K = 4
D_FF = 1024
SWIGLU_LIMIT = 7.0
SWIGLU_ALPHA = 1.702
MOE_BLOCK = 512
LN_EPS = 1e-5
RMS_EPS = 1e-6
DEEPNORM_ALPHA = 2.0 ** 0.25

LANES = 128
VMEM_LIMIT_BYTES = 56 * 1024 * 1024

F32 = jnp.float32
BF16 = jnp.bfloat16


def _cparams(*sem):
    return pltpu.CompilerParams(dimension_semantics=sem, vmem_limit_bytes=VMEM_LIMIT_BYTES)


def _layer_norm_rows(x, g, b):
    mu = jnp.mean(x, axis=-1, keepdims=True)
    xc = x - mu
    var = jnp.mean(xc * xc, axis=-1, keepdims=True)
    return xc * lax.rsqrt(var + LN_EPS) * g + b


IN_TM = 512
IN_TN = 512


def _in_proj_kernel(x_ref, g_ref, b_ref, lb_ref, w_ref, xn_ref, u_ref, qv_ref, lf_ref, kk_ref, og_ref):
    xn = _layer_norm_rows(x_ref[...], g_ref[...], b_ref[...])
    xn_ref[...] = xn
    xb = xn.astype(BF16)
    hw = HGRN_WIDTH

    def chunks(col, width):
        for c0 in range(0, width, IN_TN):
            cw = min(IN_TN, width - c0)
            yield c0, cw, jnp.dot(xb, w_ref[:, col + c0:col + c0 + cw], preferred_element_type=F32)

    def store_heads(ref, first_head, c0, val):
        for j in range(val.shape[1] // HEAD_DIM):
            ref[first_head + c0 // HEAD_DIM + j] = val[:, j * HEAD_DIM:(j + 1) * HEAD_DIM]

    for c0, cw, acc in chunks(0, FOURIER_WIDTH):
        u_ref[:, c0:c0 + cw] = acc.astype(BF16)
    for c0, cw, acc in chunks(FOURIER_WIDTH, hw):
        store_heads(qv_ref, 0, c0, (acc * jax.nn.sigmoid(acc) * (HEAD_DIM ** -0.5)).astype(BF16))
    for c0, cw, acc in chunks(FOURIER_WIDTH + hw, hw):
        store_heads(qv_ref, HEADS, c0, acc.astype(BF16))
    for c0, cw, acc in chunks(FOURIER_WIDTH + 2 * hw, 2 * hw):
        lb = lb_ref[:, c0:c0 + cw]
        fg = lb + (1.0 - lb) * jax.nn.sigmoid(acc)
        store_heads(lf_ref, 0, c0, jnp.log(fg))
        store_heads(kk_ref, 0, c0, (1.0 - fg).astype(BF16))
    for c0, cw, acc in chunks(FOURIER_WIDTH + 4 * hw, hw):
        og_ref[:, c0:c0 + cw] = acc.astype(BF16)


def _in_proj(x2d, ln_g, ln_b, lb2, w_bf16):
    t = x2d.shape[0]
    tm = min(IN_TM, t)
    row = lambda i: (i, 0)
    const = lambda i: (0, 0)
    flat = lambda w, dt: (pl.BlockSpec((tm, w), row), jax.ShapeDtypeStruct((t, w), dt))
    head_major = lambda nh, dt: (pl.BlockSpec((nh, tm, HEAD_DIM), lambda i: (0, i, 0)),
                                 jax.ShapeDtypeStruct((nh, t, HEAD_DIM), dt))
    outs = [flat(D_MODEL, F32), flat(FOURIER_WIDTH, BF16), head_major(2 * HEADS, BF16),
            head_major(2 * HEADS, F32), head_major(2 * HEADS, BF16), flat(HGRN_WIDTH, BF16)]
    return pl.pallas_call(
        _in_proj_kernel,
        grid=(t // tm,),
        in_specs=[
            pl.BlockSpec((tm, D_MODEL), row),
            pl.BlockSpec((1, D_MODEL), const),
            pl.BlockSpec((1, D_MODEL), const),
            pl.BlockSpec((1, 2 * HGRN_WIDTH), const),
            pl.BlockSpec(w_bf16.shape, const),
        ],
        out_specs=[o[0] for o in outs],
        out_shape=[o[1] for o in outs],
        compiler_params=_cparams("parallel"),
        name="in_proj",
    )(x2d, ln_g.reshape(1, -1), ln_b.reshape(1, -1), lb2, w_bf16)


def _fft_split(seq_len):
    n1 = 1 << ((seq_len.bit_length() - 1 + 1) // 2)
    return n1, seq_len // n1


@functools.lru_cache(maxsize=None)
def _fft_tables(seq_len):
    n1, n2 = _fft_split(seq_len)
    k1 = np.arange(n1)
    ang1 = 2.0 * np.pi * ((k1[:, None] * k1[None, :]) % n1) / n1
    s1 = 1.0 / math.sqrt(n1)
    c1, s1m = np.cos(ang1) * s1, np.sin(ang1) * s1
    l1p = np.arange(n1)[:, None, None]
    l2p = np.arange(n2)[None, :, None]
    l2 = np.arange(n2)[None, None, :]
    ang2 = 2.0 * np.pi * ((l2 * (l1p + n1 * l2p)) % seq_len) / seq_len
    s2 = 1.0 / math.sqrt(n2)
    gc, gs = np.cos(ang2) * s2, np.sin(ang2) * s2
    kc = np.arange(FOURIER_GROUP_DIM)
    angc = 2.0 * np.pi * ((kc[:, None] * kc[None, :]) % FOURIER_GROUP_DIM) / FOURIER_GROUP_DIM
    sc = 1.0 / math.sqrt(FOURIER_GROUP_DIM)
    groups = FOURIER_WIDTH // FOURIER_GROUP_DIM
    bc = np.kron(np.eye(groups), np.cos(angc) * sc)
    bs = np.kron(np.eye(groups), np.sin(angc) * sc)
    as_bf16 = lambda a: jnp.asarray(a, dtype=F32).astype(BF16)
    return tuple(as_bf16(a) for a in (c1, s1m, gc, gs, bc, bs))


def _fft1_kernel(c_ref, s_ref, u_ref, ar_ref, ai_ref):
    u = u_ref[0]
    ar_ref[0] = jnp.dot(c_ref[...], u, preferred_element_type=F32).astype(BF16)
    ai_ref[0] = (-jnp.dot(s_ref[...], u, preferred_element_type=F32)).astype(BF16)


FFT1_TN = 4096


def _fft_stage1(u3, c1, s1):
    b, n1, width = u3.shape
    tn = min(FFT1_TN, width)
    blk = pl.BlockSpec((1, n1, tn), lambda i, j: (i, 0, j))
    mat = pl.BlockSpec((n1, n1), lambda i, j: (0, 0))
    return pl.pallas_call(
        _fft1_kernel,
        grid=(b, width // tn),
        in_specs=[mat, mat, blk],
        out_specs=[blk, blk],
        out_shape=[jax.ShapeDtypeStruct(u3.shape, BF16)] * 2,
        compiler_params=_cparams("parallel", "parallel"),
        name="fft_stage1",
    )(c1, s1, u3)


FFT2_T1 = 8


def _fft2_kernel(gc_ref, gs_ref, bc_ref, bs_ref, g_ref, ar_ref, ai_ref, z_ref):
    t1 = ar_ref.shape[1]
    for j in range(t1):
        ar, ai = ar_ref[0, j], ai_ref[0, j]
        gc, gs = gc_ref[j], gs_ref[j]
        xr = jnp.dot(gc, ar, preferred_element_type=F32) + jnp.dot(gs, ai, preferred_element_type=F32)
        xi = jnp.dot(gc, ai, preferred_element_type=F32) - jnp.dot(gs, ar, preferred_element_type=F32)
        z = jnp.dot(xr.astype(BF16), bc_ref[...], preferred_element_type=F32)
        z += jnp.dot(xi.astype(BF16), bs_ref[...], preferred_element_type=F32)
        z = z * lax.rsqrt(jnp.mean(z * z, axis=-1, keepdims=True) + RMS_EPS) * g_ref[...]
        z_ref[0, j] = z.astype(BF16)


def _fft_stage2(ar4, ai4, gc, gs, bc, bs, gain):
    b, n1, n2, w = ar4.shape
    t1 = min(FFT2_T1, n1)
    a_blk = pl.BlockSpec((1, t1, n2, w), lambda i, j: (i, j, 0, 0))
    g_blk = pl.BlockSpec((t1, n2, n2), lambda i, j: (j, 0, 0))
    c_blk = pl.BlockSpec((w, w), lambda i, j: (0, 0))
    return pl.pallas_call(
        _fft2_kernel,
        grid=(b, n1 // t1),
        in_specs=[g_blk, g_blk, c_blk, c_blk, pl.BlockSpec((1, w), lambda i, j: (0, 0)), a_blk, a_blk],
        out_specs=a_blk,
        out_shape=jax.ShapeDtypeStruct(ar4.shape, BF16),
        compiler_params=_cparams("parallel", "parallel"),
        name="fft_stage2",
    )(gc, gs, bc, bs, gain.reshape(1, -1), ar4, ai4)


def _fourier_mix(u2d, batch, seq_len, gain):
    n1, n2 = _fft_split(seq_len)
    c1, s1, gc, gs, bc, bs = _fft_tables(seq_len)
    u3 = u2d.reshape(batch, n1, n2 * FOURIER_WIDTH)
    ar, ai = _fft_stage1(u3, c1, s1)
    shape4 = (batch, n1, n2, FOURIER_WIDTH)
    zt = _fft_stage2(ar.reshape(shape4), ai.reshape(shape4), gc, gs, bc, bs, gain)
    return zt.reshape(batch, n1, n2 * FOURIER_WIDTH)


GLA_LB = 1024
CUMSUM_ROWS = 256


@functools.lru_cache(maxsize=None)
def _cumsum_matrices():
    r = np.arange(CUMSUM_ROWS)
    same_chunk = (r[:, None] // CHUNK) == (r[None, :] // CHUNK)
    prefix = same_chunk & (r[None, :] <= r[:, None])
    suffix = same_chunk & (r[None, :] >= r[:, None])
    return (jnp.asarray(prefix, dtype=F32).astype(BF16), jnp.asarray(suffix, dtype=F32).astype(BF16))


def _chunk_cumsum(x, tri):
    hi = x.astype(BF16)
    lo = (x - hi.astype(F32)).astype(BF16)
    parts = []
    for r0 in range(0, x.shape[0], CUMSUM_ROWS):
        rows = slice(r0, r0 + CUMSUM_ROWS)
        parts.append(jnp.dot(tri, hi[rows], preferred_element_type=F32)
                     + jnp.dot(tri, lo[rows], preferred_element_type=F32))
    return parts[0] if len(parts) == 1 else jnp.concatenate(parts, axis=0)


def _gla_direction(q, v, k, logf, tri, st_ref, reverse):
    n = q.shape[0]
    b = _chunk_cumsum(logf, tri)
    mid = CHUNK // 2 if reverse else CHUNK // 2 - 1
    last = 0 if reverse else CHUNK - 1
    t_idx = lax.broadcasted_iota(jnp.int32, (CHUNK, CHUNK), 0)
    s_idx = lax.broadcasted_iota(jnp.int32, (CHUNK, CHUNK), 1)
    visible = (t_idx <= s_idx) if reverse else (t_idx >= s_idx)
    nchunks = n // CHUNK
    order = range(nchunks - 1, -1, -1) if reverse else range(nchunks)
    nt = (((1,), (1,)), ((), ()))
    tn = (((0,), (0,)), ((), ()))
    rows = [slice(c * CHUNK, (c + 1) * CHUNK) for c in range(nchunks)]
    qe, ke, e_mid, e_last, e_gap = [], [], [], [], []
    for c in range(nchunks):
        bc = b[rows[c]]
        b_mid = bc[mid:mid + 1]
        b_last = bc[last:last + 1]
        qe.append((q[rows[c]].astype(F32) * jnp.exp(bc - b_mid)).astype(BF16))
        ke.append((k[rows[c]].astype(F32) * jnp.exp(b_mid - bc)).astype(BF16))
        e_mid.append(jnp.exp(b_mid))
        e_last.append(jnp.exp(b_last))
        e_gap.append(jnp.exp(b_last - b_mid))
    scores, delta_t = [], []
    for c in range(nchunks):
        s = lax.dot_general(qe[c], ke[c], nt, preferred_element_type=F32)
        scores.append(jnp.where(visible, s, 0.0).astype(BF16))
        delta_t.append(lax.dot_general(v[rows[c]], ke[c], tn, preferred_element_type=F32) * e_gap[c])
    st = st_ref[...]
    st_in = [None] * nchunks
    for c in order:
        st_in[c] = (st * e_mid[c]).astype(BF16)
        st = st * e_last[c] + delta_t[c]
    st_ref[...] = st
    outs = [jnp.dot(scores[c], v[rows[c]], preferred_element_type=F32)
            + lax.dot_general(qe[c], st_in[c], nt, preferred_element_type=F32) for c in range(nchunks)]
    return jnp.concatenate(outs, axis=0)


def _gla_kernel(trif_ref, trib_ref, qf_ref, vf_ref, kf_ref, lf_ref, qb_ref, vb_ref, kb_ref, lb_ref,
                o_ref, sf_ref, sb_ref):
    j = pl.program_id(2)
    nblk = pl.num_programs(2)
    lb_rows = qf_ref.shape[2]

    @pl.when(j == 0)
    def _():
        o_ref[...] = jnp.zeros(o_ref.shape, o_ref.dtype)
        sf_ref[...] = jnp.zeros(sf_ref.shape, sf_ref.dtype)
        sb_ref[...] = jnp.zeros(sb_ref.shape, sb_ref.dtype)

    o_f = _gla_direction(qf_ref[0, 0], vf_ref[0, 0], kf_ref[0, 0], lf_ref[0, 0], trif_ref[...], sf_ref, False)
    start_f = pl.multiple_of(j * lb_rows, lb_rows)
    o_ref[0, 0, pl.ds(start_f, lb_rows), :] += o_f
    o_b = _gla_direction(qb_ref[0, 0], vb_ref[0, 0], kb_ref[0, 0], lb_ref[0, 0], trib_ref[...], sb_ref, True)
    start_b = pl.multiple_of((nblk - 1 - j) * lb_rows, lb_rows)
    o_ref[0, 0, pl.ds(start_b, lb_rows), :] += o_b


def _gla(qv4, lf4, kk4):
    _, b, seq_len, _ = qv4.shape
    lbk = min(GLA_LB, seq_len)
    assert lbk % CUMSUM_ROWS == 0 and seq_len % lbk == 0
    nblk = seq_len // lbk
    blk = lambda head0, rev: pl.BlockSpec(
        (1, 1, lbk, HEAD_DIM),
        (lambda i, h, j: (head0 + h, i, nblk - 1 - j, 0)) if rev else (lambda i, h, j: (head0 + h, i, j, 0)))
    tri_spec = pl.BlockSpec((CUMSUM_ROWS, CUMSUM_ROWS), lambda i, h, j: (0, 0))
    tri_f, tri_b = _cumsum_matrices()
    return pl.pallas_call(
        _gla_kernel,
        grid=(b, HEADS, nblk),
        in_specs=[tri_spec, tri_spec,
                  blk(0, False), blk(HEADS, False), blk(0, False), blk(0, False),
                  blk(0, True), blk(HEADS, True), blk(HEADS, True), blk(HEADS, True)],
        out_specs=pl.BlockSpec((1, 1, seq_len, HEAD_DIM), lambda i, h, j: (h, i, 0, 0)),
        out_shape=jax.ShapeDtypeStruct((HEADS, b, seq_len, HEAD_DIM), F32),
        scratch_shapes=[pltpu.VMEM((HEAD_DIM, HEAD_DIM), F32)] * 2,
        compiler_params=_cparams("parallel", "parallel", "arbitrary"),
        name="gla",
    )(tri_f, tri_b, qv4, qv4, kk4, lf4, qv4, qv4, kk4, lf4)


HALF_D = D_MODEL // 2


def _pack_bf16_pairs(x):
    bits = lax.bitcast_convert_type(x.astype(BF16).astype(F32), jnp.uint32)
    return bits[:, :HALF_D] | (bits[:, HALF_D:] >> 16)


def _unpack_bf16_pairs(words):
    hi = lax.bitcast_convert_type(words & jnp.uint32(0xFFFF0000), F32)
    lo = lax.bitcast_convert_type(words << 16, F32)
    return hi, lo


OUT_TM = 512


def _out_proj_kernel(*refs, nz):
    z_refs = refs[:nz]
    (o_ref, g_ref, xn_ref, ng_ref, wz_ref, wh_ref, l1g_ref, l1b_ref, rwh_ref, rwl_ref, rb_ref,
     h_ref, hpk_ref, route_ref, gate_ref, cnt_ref, base_ref) = refs[nz:]
    tm = o_ref.shape[1]

    @pl.when(pl.program_id(0) == 0)
    def _():
        base_ref[...] = jnp.zeros(base_ref.shape, base_ref.dtype)

    z = jnp.concatenate([r[0] for r in z_refs], axis=0) if nz > 1 else z_refs[0][0]
    g = g_ref[...].astype(F32)
    normed = []
    for hd in range(HEADS):
        oh = o_ref[hd]
        normed.append(oh * lax.rsqrt(jnp.mean(oh * oh, axis=-1, keepdims=True) + RMS_EPS))
    hg = jnp.concatenate(normed, axis=1) * ng_ref[...] * (g * jax.nn.sigmoid(g))
    mixed = jnp.dot(z, wz_ref[...], preferred_element_type=F32)
    mixed += jnp.dot(hg.astype(BF16), wh_ref[...], preferred_element_type=F32)
    h = _layer_norm_rows(DEEPNORM_ALPHA * xn_ref[...] + mixed, l1g_ref[...], l1b_ref[...])
    h_ref[...] = h
    hpk_ref[...] = _pack_bf16_pairs(h)

    h_hi = h.astype(BF16)
    h_lo = (h - h_hi.astype(F32)).astype(BF16)
    logits = jnp.dot(h_hi, rwh_ref[...], preferred_element_type=F32)
    logits += jnp.dot(h_lo, rwh_ref[...], preferred_element_type=F32)
    logits += jnp.dot(h_hi, rwl_ref[...], preferred_element_type=F32)
    logits += rb_ref[...]

    lane = lax.broadcasted_iota(jnp.int32, (tm, LANES), 1)
    lane_f = lane.astype(F32)
    work = jnp.where(lane < N_EXPERTS, logits, -jnp.inf)
    vals, idxs, hits = [], [], []
    for _ in range(TOP_K):
        m = jnp.max(work, axis=-1, keepdims=True)
        idx = jnp.min(jnp.where(work == m, lane_f, float(LANES)), axis=-1, keepdims=True)
        hit = lane_f == idx
        work = jnp.where(hit, -jnp.inf, work)
        vals.append(m)
        idxs.append(idx)
        hits.append(hit)
    exps = [jnp.exp(v - vals[0]) for v in vals]
    denom = exps[0] + exps[1] + exps[2] + exps[3]

    member = jnp.zeros((tm, LANES), F32)
    for hit in hits:
        member = member + jnp.where(hit, 1.0, 0.0)
    t_idx = lax.broadcasted_iota(jnp.int32, (tm, tm), 0)
    s_idx = lax.broadcasted_iota(jnp.int32, (tm, tm), 1)
    earlier = jnp.where(s_idx < t_idx, 1.0, 0.0).astype(BF16)
    base = base_ref[...]
    before = jnp.dot(earlier, member.astype(BF16), preferred_element_type=F32) + base

    route = jnp.zeros((tm, LANES), F32)
    gate = jnp.zeros((tm, LANES), F32)
    for k in range(TOP_K):
        rank = jnp.sum(jnp.where(hits[k], before, 0.0), axis=-1, keepdims=True)
        route = route + jnp.where(lane == k, idxs[k], 0.0) + jnp.where(lane == TOP_K + k, rank, 0.0)
        gate = gate + jnp.where(lane == k, exps[k] / denom, 0.0)
    route_ref[...] = route.astype(jnp.int32)
    gate_ref[...] = gate
    total = base + jnp.sum(member, axis=0, keepdims=True)
    base_ref[...] = total
    cnt_ref[...] = total


def _out_proj(zt, o3, g2d, xn2d, seq_len, norm_g6, w_out_bf16, ln_g, ln_b, rw_hi, rw_lo, rb_pad):
    t = o3.shape[1]
    n1 = zt.shape[1]
    tm = max(min(OUT_TM, seq_len), n1)
    nz = tm // n1
    row = lambda i: (i, 0)
    const = lambda i: (0, 0)

    def z_spec(k):
        return pl.BlockSpec((1, n1, FOURIER_WIDTH),
                            lambda i: ((i * tm) // seq_len, 0, ((i * tm) % seq_len) // n1 + k))

    in_specs = [z_spec(k) for k in range(nz)] + [
        pl.BlockSpec((HEADS, tm, HEAD_DIM), lambda i: (0, i, 0)),
        pl.BlockSpec((tm, HGRN_WIDTH), row),
        pl.BlockSpec((tm, D_MODEL), row),
        pl.BlockSpec((1, HGRN_WIDTH), const),
        pl.BlockSpec((FOURIER_WIDTH, D_MODEL), const),
        pl.BlockSpec((HGRN_WIDTH, D_MODEL), const),
        pl.BlockSpec((1, D_MODEL), const),
        pl.BlockSpec((1, D_MODEL), const),
        pl.BlockSpec((D_MODEL, LANES), const),
        pl.BlockSpec((D_MODEL, LANES), const),
        pl.BlockSpec((1, LANES), const),
    ]
    out_specs = [
        pl.BlockSpec((tm, D_MODEL), row),
        pl.BlockSpec((tm, HALF_D), row),
        pl.BlockSpec((tm, LANES), row),
        pl.BlockSpec((tm, LANES), row),
        pl.BlockSpec((1, LANES), const),
    ]
    out_shape = [
        jax.ShapeDtypeStruct((t, D_MODEL), F32),
        jax.ShapeDtypeStruct((t, HALF_D), jnp.uint32),
        jax.ShapeDtypeStruct((t, LANES), jnp.int32),
        jax.ShapeDtypeStruct((t, LANES), F32),
        jax.ShapeDtypeStruct((1, LANES), F32),
    ]
    return pl.pallas_call(
        functools.partial(_out_proj_kernel, nz=nz),
        grid=(t // tm,),
        in_specs=in_specs,
        out_specs=out_specs,
        out_shape=out_shape,
        scratch_shapes=[pltpu.VMEM((1, LANES), F32)],
        compiler_params=_cparams("arbitrary"),
        name="out_proj",
    )(*([zt] * nz), o3, g2d, xn2d, norm_g6, w_out_bf16[:FOURIER_WIDTH], w_out_bf16[FOURIER_WIDTH:],
      ln_g.reshape(1, -1), ln_b.reshape(1, -1), rw_hi, rw_lo, rb_pad)


COMBINE_TM = 256
SC_WINDOW = 128


def _sc_mesh():
    return plsc.VectorSubcoreMesh(core_axis_name="core", subcore_axis_name="subcore")


def _dispatch(dest_kt, hpk, n_slots):
    t, d = hpk.shape
    rows = pl.BlockSpec((SC_WINDOW, d), index_map=lambda i: (i, 0), pipeline_mode=pl.Buffered(1))
    idx = pl.BlockSpec((1, SC_WINDOW), index_map=lambda i: (0, i))

    @pl.kernel(out_type=jax.ShapeDtypeStruct((n_slots, d), hpk.dtype), mesh=_sc_mesh(), name="moe_dispatch_sc")
    def scatter(x_hbm, i0_hbm, i1_hbm, i2_hbm, i3_hbm, o_hbm):
        def body(x_vmem, *idx_vmem):
            for iv in idx_vmem:
                pltpu.sync_copy(x_vmem, o_hbm.at[iv.at[0]])

        pltpu.emit_pipeline(
            body, grid=(t // SC_WINDOW,), in_specs=[rows] + [idx] * TOP_K, out_specs=[],
            core_axis_name=("core", "subcore"), dimension_semantics=(pltpu.PARALLEL,),
        )(x_hbm, i0_hbm, i1_hbm, i2_hbm, i3_hbm)

    return scatter(hpk, *[dest_kt[k].reshape(1, t) for k in range(TOP_K)])


def _gather_rows(yb, dest_kt):
    _, t = dest_kt.shape
    d = yb.shape[1]
    n = TOP_K * t

    @pl.kernel(out_type=jax.ShapeDtypeStruct((n, d), yb.dtype), mesh=_sc_mesh(), name="moe_gather_sc")
    def gather(y_hbm, i_hbm, o_hbm):
        def body(i_vmem, o_vmem):
            pltpu.sync_copy(y_hbm.at[i_vmem.at[0]], o_vmem)

        pltpu.emit_pipeline(
            body, grid=(n // SC_WINDOW,),
            in_specs=[pl.BlockSpec((1, SC_WINDOW), index_map=lambda i: (0, i))],
            out_specs=[pl.BlockSpec((SC_WINDOW, d), index_map=lambda i: (i, 0), pipeline_mode=pl.Buffered(1))],
            core_axis_name=("core", "subcore"), dimension_semantics=(pltpu.PARALLEL,),
        )(i_hbm, o_hbm)

    return gather(yb, dest_kt.reshape(1, n)).reshape(TOP_K, t, d)


def _experts_kernel(be_ref, nused_ref, xb_ref, wgu_ref, bgu_ref, wdn_ref, bdn_ref, yb_ref, wgu_bf, wdn_bf):
    i = pl.program_id(0)
    used = i < nused_ref[0]

    @pl.when(used & ((i == 0) | (be_ref[i] != be_ref[jnp.maximum(i - 1, 0)])))
    def _():
        wgu_bf[...] = wgu_ref[0].astype(BF16)
        wdn_bf[...] = wdn_ref[0].astype(BF16)

    @pl.when(used)
    def _():
        x_hi, x_lo = _unpack_bf16_pairs(xb_ref[...])
        gu = jnp.dot(x_hi.astype(BF16), wgu_bf[:HALF_D, :], preferred_element_type=F32)
        gu += jnp.dot(x_lo.astype(BF16), wgu_bf[HALF_D:, :], preferred_element_type=F32)
        gu += bgu_ref[0]
        gate = jnp.minimum(gu[:, :D_FF], SWIGLU_LIMIT)
        up = jnp.clip(gu[:, D_FF:], -SWIGLU_LIMIT, SWIGLU_LIMIT)
        act = (up + 1.0) * gate * jax.nn.sigmoid(SWIGLU_ALPHA * gate)
        y = jnp.dot(act.astype(BF16), wdn_bf[...], preferred_element_type=F32) + bdn_ref[0]
        yb_ref[...] = _pack_bf16_pairs(y)

    @pl.when(jnp.logical_not(used))
    def _():
        yb_ref[...] = jnp.zeros(yb_ref.shape, yb_ref.dtype)


def _experts(block_expert, n_used, xb, wgu, bgu, wdn, bdn):
    n_slots = xb.shape[0]
    nb = n_slots // MOE_BLOCK
    blk = pl.BlockSpec((MOE_BLOCK, HALF_D), lambda i, be, nu: (i, 0))
    per_expert = lambda shape: pl.BlockSpec((1,) + shape, lambda i, be, nu: (be[i], 0, 0))
    grid_spec = pltpu.PrefetchScalarGridSpec(
        num_scalar_prefetch=2,
        grid=(nb,),
        in_specs=[blk, per_expert((D_MODEL, 2 * D_FF)), per_expert((1, 2 * D_FF)),
                  per_expert((D_FF, D_MODEL)), per_expert((1, D_MODEL))],
        out_specs=blk,
        scratch_shapes=[pltpu.VMEM((D_MODEL, 2 * D_FF), BF16), pltpu.VMEM((D_FF, D_MODEL), BF16)],
    )
    return pl.pallas_call(
        _experts_kernel,
        grid_spec=grid_spec,
        out_shape=jax.ShapeDtypeStruct((n_slots, HALF_D), jnp.uint32),
        compiler_params=_cparams("arbitrary"),
        name="moe_experts",
    )(block_expert, n_used, xb, wgu, bgu, wdn, bdn)


def _combine_kernel(rows_ref, gate_ref, h_ref, g_ref, b_ref, out_ref):
    tm = h_ref.shape[0]
    gate = gate_ref[...]
    ff_hi = jnp.zeros((tm, HALF_D), F32)
    ff_lo = jnp.zeros((tm, HALF_D), F32)
    for k in range(TOP_K):
        hi, lo = _unpack_bf16_pairs(rows_ref[k])
        gk = gate[:, k:k + 1]
        ff_hi = ff_hi + gk * hi
        ff_lo = ff_lo + gk * lo
    ff = jnp.concatenate([ff_hi, ff_lo], axis=1)
    out_ref[...] = _layer_norm_rows(DEEPNORM_ALPHA * h_ref[...] + ff, g_ref[...], b_ref[...])


def _combine(rows_kt, gates, h2d, ln_g, ln_b):
    t = h2d.shape[0]
    tm = min(COMBINE_TM, t)
    row = lambda i: (i, 0)
    const = lambda i: (0, 0)
    return pl.pallas_call(
        _combine_kernel,
        grid=(t // tm,),
        in_specs=[
            pl.BlockSpec((TOP_K, tm, HALF_D), lambda i: (0, i, 0)),
            pl.BlockSpec((tm, LANES), row),
            pl.BlockSpec((tm, D_MODEL), row),
            pl.BlockSpec((1, D_MODEL), const),
            pl.BlockSpec((1, D_MODEL), const),
        ],
        out_specs=pl.BlockSpec((tm, D_MODEL), row),
        out_shape=jax.ShapeDtypeStruct((t, D_MODEL), F32),
        compiler_params=_cparams("parallel"),
        name="moe_combine",
    )(rows_kt, gates, h2d, ln_g.reshape(1, -1), ln_b.reshape(1, -1))


def _moe_plan(route, counts_f32, n_tokens):
    eid = route[:, :TOP_K]
    rank = route[:, TOP_K:2 * TOP_K]
    counts = counts_f32[0, :N_EXPERTS].astype(jnp.int32)
    padded = ((counts + MOE_BLOCK - 1) // MOE_BLOCK) * MOE_BLOCK
    pend = jnp.cumsum(padded)
    pstart = pend - padded
    dest = pstart[eid] + rank
    n_assign = n_tokens * TOP_K
    n_slots = ((n_assign + MOE_BLOCK - 1) // MOE_BLOCK) * MOE_BLOCK + N_EXPERTS * MOE_BLOCK
    nb = n_slots // MOE_BLOCK
    block_start = jnp.arange(nb, dtype=jnp.int32) * MOE_BLOCK
    block_expert = jnp.sum((pend[None, :] <= block_start[:, None]).astype(jnp.int32), axis=1)
    block_expert = jnp.minimum(block_expert, N_EXPERTS - 1).astype(jnp.int32)
    n_used = (pend[-1:] // MOE_BLOCK).astype(jnp.int32)
    return dest.astype(jnp.int32).T, block_expert, n_used, n_slots


def _trunk(x, p):
    batch, seq_len, _ = x.shape
    t = batch * seq_len
    xn, u, qv, lf, kk, og = _in_proj(x.reshape(t, D_MODEL), p["ln_in_g"], p["ln_in_b"], p["lb2"], p["w_in"])
    zt = _fourier_mix(u, batch, seq_len, p["fourier_norm_g"])
    per_seq = lambda a: a.reshape(a.shape[0], batch, seq_len, HEAD_DIM)
    o = _gla(per_seq(qv), per_seq(lf), per_seq(kk))
    h, hpk, route, gates, counts = _out_proj(
        zt, o.reshape(HEADS, t, HEAD_DIM), og, xn, seq_len, p["norm_g6"], p["w_out"], p["ln1_g"], p["ln1_b"],
        p["rw_hi"], p["rw_lo"], p["rb_pad"])
    dest_kt, block_expert, n_used, n_slots = _moe_plan(route, counts, t)
    xb = _dispatch(dest_kt, hpk, n_slots)
    yb = _experts(block_expert, n_used, xb, p["w_gu"], p["b_gu"], p["w_dn"], p["b_dn"])
    y = _combine(_gather_rows(yb, dest_kt), gates, h, p["ln2_g"], p["ln2_b"])
    return y.reshape(batch, seq_len, D_MODEL)


def _prepare_params(ln_in_g, ln_in_b, w_in, fourier_norm_g, lb_gamma, hgrn_norm_g, w_out, ln1_g, ln1_b,
                    router_w, router_b, w_gate_up, b_gate_up, w_down, b_down, ln2_g, ln2_b):
    fw, hw = FOURIER_WIDTH, HGRN_WIDTH
    w0 = w_in[0]
    lb_all = jnp.cumsum(jax.nn.softmax(lb_gamma.astype(F32), axis=1), axis=1)
    rw = router_w[0].astype(F32)
    rw_hi = rw.astype(BF16)
    rw_lo = (rw - rw_hi.astype(F32)).astype(BF16)
    pad = ((0, 0), (0, LANES - N_EXPERTS))
    del fw, hw
    return dict(
        ln_in_g=ln_in_g, ln_in_b=ln_in_b, w_in=w0.astype(BF16),
        fourier_norm_g=fourier_norm_g[0],
        lb2=jnp.concatenate([lb_all[0, 0], lb_all[1, 0]]).reshape(1, -1),
        norm_g6=jnp.tile(hgrn_norm_g[0].astype(F32), HEADS).reshape(1, -1),
        w_out=w_out[0].astype(BF16), ln1_g=ln1_g[0], ln1_b=ln1_b[0],
        rw_hi=jnp.pad(rw_hi, pad), rw_lo=jnp.pad(rw_lo, pad),
        rb_pad=jnp.pad(router_b[0].astype(F32).reshape(1, -1), pad),
        w_gu=w_gate_up[0], b_gu=b_gate_up[0].reshape(N_EXPERTS, 1, -1),
        w_dn=w_down[0], b_dn=b_down[0].reshape(N_EXPERTS, 1, -1),
        ln2_g=ln2_g[0], ln2_b=ln2_b[0],
    )


def kernel(x_prompt, x_sample, ln_in_g, ln_in_b, w_in, fourier_norm_g, lb_gamma, hgrn_norm_g, w_out,
           ln1_g, ln1_b, router_w, router_b, w_gate_up, b_gate_up, w_down, b_down, ln2_g, ln2_b):
    p = _prepare_params(ln_in_g, ln_in_b, w_in, fourier_norm_g, lb_gamma, hgrn_norm_g, w_out, ln1_g, ln1_b,
                        router_w, router_b, w_gate_up, b_gate_up, w_down, b_down, ln2_g, ln2_b)
    return (_trunk(x_prompt, p), _trunk(x_sample, p))
```

```python
import functools
import math

import numpy as np
import jax
import jax.numpy as jnp
from jax import lax
from jax.experimental import pallas as pl
from jax.experimental.pallas import tpu as pltpu
from jax.experimental.pallas import tpu_sc as plsc

D_MODEL = 1024
FOURIER_WIDTH = 256
FOURIER_GROUP_DIM = 64
HGRN_WIDTH = 768
HEAD_DIM = 128
HEADS = 6
CHUNK = 64
N_EXPERTS = 32
TOP_K = 4
D_FF = 1024
SWIGLU_LIMIT = 7.0
SWIGLU_ALPHA = 1.702
MOE_BLOCK = 512
LN_EPS = 1e-5
RMS_EPS = 1e-6
DEEPNORM_ALPHA = 2.0 ** 0.25

LANES = 128
VMEM_LIMIT_BYTES = 56 * 1024 * 1024

F32 = jnp.float32
BF16 = jnp.bfloat16


def _cparams(*sem):
    return pltpu.CompilerParams(dimension_semantics=sem, vmem_limit_bytes=VMEM_LIMIT_BYTES)


def _layer_norm_rows(x, g, b):
    mu = jnp.mean(x, axis=-1, keepdims=True)
    xc = x - mu
    var = jnp.mean(xc * xc, axis=-1, keepdims=True)
    return xc * lax.rsqrt(var + LN_EPS) * g + b


IN_TM = 512
IN_TN = 512


def _in_proj_kernel(x_ref, g_ref, b_ref, lb_ref, w_ref, xn_ref, u_ref, qv_ref, lf_ref, kk_ref, og_ref):
    xn = _layer_norm_rows(x_ref[...], g_ref[...], b_ref[...])
    xn_ref[...] = xn
    xb = xn.astype(BF16)
    hw = HGRN_WIDTH

    def chunks(col, width):
        for c0 in range(0, width, IN_TN):
            cw = min(IN_TN, width - c0)
            yield c0, cw, jnp.dot(xb, w_ref[:, col + c0:col + c0 + cw], preferred_element_type=F32)

    def store_heads(ref, first_head, c0, val):
        for j in range(val.shape[1] // HEAD_DIM):
            ref[first_head + c0 // HEAD_DIM + j] = val[:, j * HEAD_DIM:(j + 1) * HEAD_DIM]

    for c0, cw, acc in chunks(0, FOURIER_WIDTH):
        u_ref[:, c0:c0 + cw] = acc.astype(BF16)
    for c0, cw, acc in chunks(FOURIER_WIDTH, hw):
        store_heads(qv_ref, 0, c0, (acc * jax.nn.sigmoid(acc) * (HEAD_DIM ** -0.5)).astype(BF16))
    for c0, cw, acc in chunks(FOURIER_WIDTH + hw, hw):
        store_heads(qv_ref, HEADS, c0, acc.astype(BF16))
    for c0, cw, acc in chunks(FOURIER_WIDTH + 2 * hw, 2 * hw):
        lb = lb_ref[:, c0:c0 + cw]
        fg = lb + (1.0 - lb) * jax.nn.sigmoid(acc)
        store_heads(lf_ref, 0, c0, jnp.log(fg))
        store_heads(kk_ref, 0, c0, (1.0 - fg).astype(BF16))
    for c0, cw, acc in chunks(FOURIER_WIDTH + 4 * hw, hw):
        og_ref[:, c0:c0 + cw] = acc.astype(BF16)


def _in_proj(x2d, ln_g, ln_b, lb2, w_bf16):
    t = x2d.shape[0]
    tm = min(IN_TM, t)
    row = lambda i: (i, 0)
    const = lambda i: (0, 0)
    flat = lambda w, dt: (pl.BlockSpec((tm, w), row), jax.ShapeDtypeStruct((t, w), dt))
    head_major = lambda nh, dt: (pl.BlockSpec((nh, tm, HEAD_DIM), lambda i: (0, i, 0)),
                                 jax.ShapeDtypeStruct((nh, t, HEAD_DIM), dt))
    outs = [flat(D_MODEL, F32), flat(FOURIER_WIDTH, BF16), head_major(2 * HEADS, BF16),
            head_major(2 * HEADS, F32), head_major(2 * HEADS, BF16), flat(HGRN_WIDTH, BF16)]
    return pl.pallas_call(
        _in_proj_kernel,
        grid=(t // tm,),
        in_specs=[
            pl.BlockSpec((tm, D_MODEL), row),
            pl.BlockSpec((1, D_MODEL), const),
            pl.BlockSpec((1, D_MODEL), const),
            pl.BlockSpec((1, 2 * HGRN_WIDTH), const),
            pl.BlockSpec(w_bf16.shape, const),
        ],
        out_specs=[o[0] for o in outs],
        out_shape=[o[1] for o in outs],
        compiler_params=_cparams("parallel"),
        name="in_proj",
    )(x2d, ln_g.reshape(1, -1), ln_b.reshape(1, -1), lb2, w_bf16)


def _fft_split(seq_len):
    n1 = 1 << ((seq_len.bit_length() - 1 + 1) // 2)
    return n1, seq_len // n1


@functools.lru_cache(maxsize=None)
def _fft_tables(seq_len):
    n1, n2 = _fft_split(seq_len)
    k1 = np.arange(n1)
    ang1 = 2.0 * np.pi * ((k1[:, None] * k1[None, :]) % n1) / n1
    s1 = 1.0 / math.sqrt(n1)
    c1, s1m = np.cos(ang1) * s1, np.sin(ang1) * s1
    l1p = np.arange(n1)[:, None, None]
    l2p = np.arange(n2)[None, :, None]
    l2 = np.arange(n2)[None, None, :]
    ang2 = 2.0 * np.pi * ((l2 * (l1p + n1 * l2p)) % seq_len) / seq_len
    s2 = 1.0 / math.sqrt(n2)
    gc, gs = np.cos(ang2) * s2, np.sin(ang2) * s2
    kc = np.arange(FOURIER_GROUP_DIM)
    angc = 2.0 * np.pi * ((kc[:, None] * kc[None, :]) % FOURIER_GROUP_DIM) / FOURIER_GROUP_DIM
    sc = 1.0 / math.sqrt(FOURIER_GROUP_DIM)
    groups = FOURIER_WIDTH // FOURIER_GROUP_DIM
    bc = np.kron(np.eye(groups), np.cos(angc) * sc)
    bs = np.kron(np.eye(groups), np.sin(angc) * sc)
    as_bf16 = lambda a: jnp.asarray(a, dtype=F32).astype(BF16)
    return tuple(as_bf16(a) for a in (c1, s1m, gc, gs, bc, bs))


def _fft1_kernel(c_ref, s_ref, u_ref, ar_ref, ai_ref):
    u = u_ref[0]
    ar_ref[0] = jnp.dot(c_ref[...], u, preferred_element_type=F32).astype(BF16)
    ai_ref[0] = (-jnp.dot(s_ref[...], u, preferred_element_type=F32)).astype(BF16)


FFT1_TN = 4096


def _fft_stage1(u3, c1, s1):
    b, n1, width = u3.shape
    tn = min(FFT1_TN, width)
    blk = pl.BlockSpec((1, n1, tn), lambda i, j: (i, 0, j))
    mat = pl.BlockSpec((n1, n1), lambda i, j: (0, 0))
    return pl.pallas_call(
        _fft1_kernel,
        grid=(b, width // tn),
        in_specs=[mat, mat, blk],
        out_specs=[blk, blk],
        out_shape=[jax.ShapeDtypeStruct(u3.shape, BF16)] * 2,
        compiler_params=_cparams("parallel", "parallel"),
        name="fft_stage1",
    )(c1, s1, u3)


FFT2_T1 = 8


def _fft2_kernel(gc_ref, gs_ref, bc_ref, bs_ref, g_ref, ar_ref, ai_ref, z_ref):
    t1 = ar_ref.shape[1]
    for j in range(t1):
        ar, ai = ar_ref[0, j], ai_ref[0, j]
        gc, gs = gc_ref[j], gs_ref[j]
        xr = jnp.dot(gc, ar, preferred_element_type=F32) + jnp.dot(gs, ai, preferred_element_type=F32)
        xi = jnp.dot(gc, ai, preferred_element_type=F32) - jnp.dot(gs, ar, preferred_element_type=F32)
        z = jnp.dot(xr.astype(BF16), bc_ref[...], preferred_element_type=F32)
        z += jnp.dot(xi.astype(BF16), bs_ref[...], preferred_element_type=F32)
        z = z * lax.rsqrt(jnp.mean(z * z, axis=-1, keepdims=True) + RMS_EPS) * g_ref[...]
        z_ref[0, j] = z.astype(BF16)


def _fft_stage2(ar4, ai4, gc, gs, bc, bs, gain):
    b, n1, n2, w = ar4.shape
    t1 = min(FFT2_T1, n1)
    a_blk = pl.BlockSpec((1, t1, n2, w), lambda i, j: (i, j, 0, 0))
    g_blk = pl.BlockSpec((t1, n2, n2), lambda i, j: (j, 0, 0))
    c_blk = pl.BlockSpec((w, w), lambda i, j: (0, 0))
    return pl.pallas_call(
        _fft2_kernel,
        grid=(b, n1 // t1),
        in_specs=[g_blk, g_blk, c_blk, c_blk, pl.BlockSpec((1, w), lambda i, j: (0, 0)), a_blk, a_blk],
        out_specs=a_blk,
        out_shape=jax.ShapeDtypeStruct(ar4.shape, BF16),
        compiler_params=_cparams("parallel", "parallel"),
        name="fft_stage2",
    )(gc, gs, bc, bs, gain.reshape(1, -1), ar4, ai4)


def _fourier_mix(u2d, batch, seq_len, gain):
    n1, n2 = _fft_split(seq_len)
    c1, s1, gc, gs, bc, bs = _fft_tables(seq_len)
    u3 = u2d.reshape(batch, n1, n2 * FOURIER_WIDTH)
    ar, ai = _fft_stage1(u3, c1, s1)
    shape4 = (batch, n1, n2, FOURIER_WIDTH)
    zt = _fft_stage2(ar.reshape(shape4), ai.reshape(shape4), gc, gs, bc, bs, gain)
    return zt.reshape(batch, n1, n2 * FOURIER_WIDTH)


GLA_LB = 1024
CUMSUM_ROWS = 256


@functools.lru_cache(maxsize=None)
def _cumsum_matrices():
    r = np.arange(CUMSUM_ROWS)
    same_chunk = (r[:, None] // CHUNK) == (r[None, :] // CHUNK)
    prefix = same_chunk & (r[None, :] <= r[:, None])
    suffix = same_chunk & (r[None, :] >= r[:, None])
    return (jnp.asarray(prefix, dtype=F32).astype(BF16), jnp.asarray(suffix, dtype=F32).astype(BF16))


def _chunk_cumsum(x, tri):
    hi = x.astype(BF16)
    lo = (x - hi.astype(F32)).astype(BF16)
    width = x.shape[1]
    parts = []
    for r0 in range(0, x.shape[0], CUMSUM_ROWS):
        rows = slice(r0, r0 + CUMSUM_ROWS)
        both = jnp.dot(tri, jnp.concatenate([hi[rows], lo[rows]], axis=1), preferred_element_type=F32)
        parts.append(both[:, :width] + both[:, width:])
    return parts[0] if len(parts) == 1 else jnp.concatenate(parts, axis=0)


def _gla_direction(q, v, k, logf, tri, st_ref, reverse):
    n = q.shape[0]
    b = _chunk_cumsum(logf, tri)
    mid = CHUNK // 2 if reverse else CHUNK // 2 - 1
    last = 0 if reverse else CHUNK - 1
    t_idx = lax.broadcasted_iota(jnp.int32, (CHUNK, CHUNK), 0)
    s_idx = lax.broadcasted_iota(jnp.int32, (CHUNK, CHUNK), 1)
    visible = (t_idx <= s_idx) if reverse else (t_idx >= s_idx)
    nchunks = n // CHUNK
    order = range(nchunks - 1, -1, -1) if reverse else range(nchunks)
    nt = (((1,), (1,)), ((), ()))
    tn = (((0,), (0,)), ((), ()))
    rows = [slice(c * CHUNK, (c + 1) * CHUNK) for c in range(nchunks)]
    qe, ke, e_mid, e_last, e_gap = [], [], [], [], []
    for c in range(nchunks):
        bc = b[rows[c]]
        b_mid = bc[mid:mid + 1]
        b_last = bc[last:last + 1]
        qe.append((q[rows[c]].astype(F32) * jnp.exp(bc - b_mid)).astype(BF16))
        ke.append((k[rows[c]].astype(F32) * jnp.exp(b_mid - bc)).astype(BF16))
        e_mid.append(jnp.exp(b_mid))
        e_last.append(jnp.exp(b_last))
        e_gap.append(jnp.exp(b_last - b_mid))
    scores, delta_t = [], []
    for c in range(nchunks):
        s = lax.dot_general(qe[c], ke[c], nt, preferred_element_type=F32)
        scores.append(jnp.where(visible, s, 0.0).astype(BF16))
        delta_t.append(lax.dot_general(v[rows[c]], ke[c], tn, preferred_element_type=F32) * e_gap[c])
    st = st_ref[...]
    st_in = [None] * nchunks
    for c in order:
        st_in[c] = (st * e_mid[c]).astype(BF16)
        st = st * e_last[c] + delta_t[c]
    st_ref[...] = st
    outs = [jnp.dot(scores[c], v[rows[c]], preferred_element_type=F32)
            + lax.dot_general(qe[c], st_in[c], nt, preferred_element_type=F32) for c in range(nchunks)]
    return jnp.concatenate(outs, axis=0)


def _gla_kernel(trif_ref, trib_ref, qf_ref, vf_ref, kf_ref, lf_ref, qb_ref, vb_ref, kb_ref, lb_ref,
                o_ref, sf_ref, sb_ref):
    j = pl.program_id(2)
    nblk = pl.num_programs(2)
    lb_rows = qf_ref.shape[2]

    @pl.when(j == 0)
    def _():
        o_ref[...] = jnp.zeros(o_ref.shape, o_ref.dtype)
        sf_ref[...] = jnp.zeros(sf_ref.shape, sf_ref.dtype)
        sb_ref[...] = jnp.zeros(sb_ref.shape, sb_ref.dtype)

    o_f = _gla_direction(qf_ref[0, 0], vf_ref[0, 0], kf_ref[0, 0], lf_ref[0, 0], trif_ref[...], sf_ref, False)
    start_f = pl.multiple_of(j * lb_rows, lb_rows)
    o_ref[0, 0, pl.ds(start_f, lb_rows), :] += o_f
    o_b = _gla_direction(qb_ref[0, 0], vb_ref[0, 0], kb_ref[0, 0], lb_ref[0, 0], trib_ref[...], sb_ref, True)
    start_b = pl.multiple_of((nblk - 1 - j) * lb_rows, lb_rows)
    o_ref[0, 0, pl.ds(start_b, lb_rows), :] += o_b


def _gla(qv4, lf4, kk4):
    _, b, seq_len, _ = qv4.shape
    lbk = min(GLA_LB, seq_len)
    assert lbk % CUMSUM_ROWS == 0 and seq_len % lbk == 0
    nblk = seq_len // lbk
    blk = lambda head0, rev: pl.BlockSpec(
        (1, 1, lbk, HEAD_DIM),
        (lambda i, h, j: (head0 + h, i, nblk - 1 - j, 0)) if rev else (lambda i, h, j: (head0 + h, i, j, 0)))
    tri_spec = pl.BlockSpec((CUMSUM_ROWS, CUMSUM_ROWS), lambda i, h, j: (0, 0))
    tri_f, tri_b = _cumsum_matrices()
    return pl.pallas_call(
        _gla_kernel,
        grid=(b, HEADS, nblk),
        in_specs=[tri_spec, tri_spec,
                  blk(0, False), blk(HEADS, False), blk(0, False), blk(0, False),
                  blk(0, True), blk(HEADS, True), blk(HEADS, True), blk(HEADS, True)],
        out_specs=pl.BlockSpec((1, 1, seq_len, HEAD_DIM), lambda i, h, j: (h, i, 0, 0)),
        out_shape=jax.ShapeDtypeStruct((HEADS, b, seq_len, HEAD_DIM), F32),
        scratch_shapes=[pltpu.VMEM((HEAD_DIM, HEAD_DIM), F32)] * 2,
        compiler_params=_cparams("parallel", "parallel", "arbitrary"),
        name="gla",
    )(tri_f, tri_b, qv4, qv4, kk4, lf4, qv4, qv4, kk4, lf4)


HALF_D = D_MODEL // 2


def _pack_bf16_pairs(x):
    bits = lax.bitcast_convert_type(x.astype(BF16).astype(F32), jnp.uint32)
    return bits[:, :HALF_D] | (bits[:, HALF_D:] >> 16)


def _unpack_bf16_pairs(words):
    hi = lax.bitcast_convert_type(words & jnp.uint32(0xFFFF0000), F32)
    lo = lax.bitcast_convert_type(words << 16, F32)
    return hi, lo


OUT_TM = 512


def _out_proj_kernel(*refs, nz):
    z_refs = refs[:nz]
    (o_ref, g_ref, xn_ref, ng_ref, wz_ref, wh_ref, l1g_ref, l1b_ref, rwh_ref, rwl_ref, rb_ref,
     h_ref, hpk_ref, route_ref, gate_ref, cnt_ref, base_ref) = refs[nz:]
    tm = o_ref.shape[1]

    @pl.when(pl.program_id(0) == 0)
    def _():
        base_ref[...] = jnp.zeros(base_ref.shape, base_ref.dtype)

    z = jnp.concatenate([r[0] for r in z_refs], axis=0) if nz > 1 else z_refs[0][0]
    g = g_ref[...].astype(F32)
    normed = []
    for hd in range(HEADS):
        oh = o_ref[hd]
        normed.append(oh * lax.rsqrt(jnp.mean(oh * oh, axis=-1, keepdims=True) + RMS_EPS))
    hg = jnp.concatenate(normed, axis=1) * ng_ref[...] * (g * jax.nn.sigmoid(g))
    mixed = jnp.dot(z, wz_ref[...], preferred_element_type=F32)
    mixed += jnp.dot(hg.astype(BF16), wh_ref[...], preferred_element_type=F32)
    h = _layer_norm_rows(DEEPNORM_ALPHA * xn_ref[...] + mixed, l1g_ref[...], l1b_ref[...])
    h_ref[...] = h
    hpk_ref[...] = _pack_bf16_pairs(h)

    nt = (((1,), (1,)), ((), ()))
    h_hi = h.astype(BF16)
    h_lo = (h - h_hi.astype(F32)).astype(BF16)
    logits = lax.dot_general(rwh_ref[...], h_hi, nt, preferred_element_type=F32)
    logits += lax.dot_general(rwh_ref[...], h_lo, nt, preferred_element_type=F32)
    logits += lax.dot_general(rwl_ref[...], h_hi, nt, preferred_element_type=F32)
    logits += rb_ref[...]

    eid_f = lax.broadcasted_iota(jnp.int32, (N_EXPERTS, tm), 0).astype(F32)
    work = logits
    vals, idxs, hits = [], [], []
    for _ in range(TOP_K):
        m = jnp.max(work, axis=0, keepdims=True)
        idx = jnp.min(jnp.where(work == m, eid_f, float(N_EXPERTS)), axis=0, keepdims=True)
        hit = eid_f == idx
        work = jnp.where(hit, -jnp.inf, work)
        vals.append(m)
        idxs.append(idx)
        hits.append(hit)
    exps = [jnp.exp(v - vals[0]) for v in vals]
    denom = exps[0] + exps[1] + exps[2] + exps[3]

    member = jnp.zeros((N_EXPERTS, tm), F32)
    for hit in hits:
        member = member + jnp.where(hit, 1.0, 0.0)
    s_idx = lax.broadcasted_iota(jnp.int32, (tm, tm), 0)
    t_idx = lax.broadcasted_iota(jnp.int32, (tm, tm), 1)
    earlier = jnp.where(s_idx < t_idx, 1.0, 0.0).astype(BF16)
    base = base_ref[...]
    before = jnp.dot(member.astype(BF16), earlier, preferred_element_type=F32) + base

    row8 = lax.broadcasted_iota(jnp.int32, (2 * TOP_K, tm), 0)
    route = jnp.zeros((2 * TOP_K, tm), F32)
    row128 = lax.broadcasted_iota(jnp.int32, (LANES, tm), 0)
    gate_t = jnp.zeros((LANES, tm), F32)
    for k in range(TOP_K):
        rank = jnp.sum(jnp.where(hits[k], before, 0.0), axis=0, keepdims=True)
        route = route + jnp.where(row8 == k, idxs[k], 0.0) + jnp.where(row8 == TOP_K + k, rank, 0.0)
        gate_t = gate_t + jnp.where(row128 == k, exps[k] / denom, 0.0)
    route_ref[...] = route.astype(jnp.int32)
    gate_ref[...] = gate_t.T
    total = base + jnp.sum(member, axis=1, keepdims=True)
    base_ref[...] = total
    cnt_ref[...] = total


def _out_proj(zt, o3, g2d, xn2d, seq_len, norm_g6, w_out_bf16, ln_g, ln_b, rw_hi, rw_lo, rb_pad):
    t = o3.shape[1]
    n1 = zt.shape[1]
    tm = max(min(OUT_TM, seq_len), n1)
    nz = tm // n1
    row = lambda i: (i, 0)
    const = lambda i: (0, 0)

    def z_spec(k):
        return pl.BlockSpec((1, n1, FOURIER_WIDTH),
                            lambda i: ((i * tm) // seq_len, 0, ((i * tm) % seq_len) // n1 + k))

    in_specs = [z_spec(k) for k in range(nz)] + [
        pl.BlockSpec((HEADS, tm, HEAD_DIM), lambda i: (0, i, 0)),
        pl.BlockSpec((tm, HGRN_WIDTH), row),
        pl.BlockSpec((tm, D_MODEL), row),
        pl.BlockSpec((1, HGRN_WIDTH), const),
        pl.BlockSpec((FOURIER_WIDTH, D_MODEL), const),
        pl.BlockSpec((HGRN_WIDTH, D_MODEL), const),
        pl.BlockSpec((1, D_MODEL), const),
        pl.BlockSpec((1, D_MODEL), const),
        pl.BlockSpec((N_EXPERTS, D_MODEL), const),
        pl.BlockSpec((N_EXPERTS, D_MODEL), const),
        pl.BlockSpec((N_EXPERTS, 1), const),
    ]
    out_specs = [
        pl.BlockSpec((tm, D_MODEL), row),
        pl.BlockSpec((tm, HALF_D), row),
        pl.BlockSpec((2 * TOP_K, tm), lambda i: (0, i)),
        pl.BlockSpec((tm, LANES), row),
        pl.BlockSpec((N_EXPERTS, 1), const),
    ]
    out_shape = [
        jax.ShapeDtypeStruct((t, D_MODEL), F32),
        jax.ShapeDtypeStruct((t, HALF_D), jnp.uint32),
        jax.ShapeDtypeStruct((2 * TOP_K, t), jnp.int32),
        jax.ShapeDtypeStruct((t, LANES), F32),
        jax.ShapeDtypeStruct((N_EXPERTS, 1), F32),
    ]
    return pl.pallas_call(
        functools.partial(_out_proj_kernel, nz=nz),
        grid=(t // tm,),
        in_specs=in_specs,
        out_specs=out_specs,
        out_shape=out_shape,
        scratch_shapes=[pltpu.VMEM((N_EXPERTS, 1), F32)],
        compiler_params=_cparams("arbitrary"),
        name="out_proj",
    )(*([zt] * nz), o3, g2d, xn2d, norm_g6, w_out_bf16[:FOURIER_WIDTH], w_out_bf16[FOURIER_WIDTH:],
      ln_g.reshape(1, -1), ln_b.reshape(1, -1), rw_hi, rw_lo, rb_pad)


COMBINE_TM = 512
SC_WINDOW = 128


def _sc_mesh():
    return plsc.VectorSubcoreMesh(core_axis_name="core", subcore_axis_name="subcore")


def _dispatch(dest_kt, hpk, n_slots):
    t, d = hpk.shape
    rows = pl.BlockSpec((SC_WINDOW, d), index_map=lambda i: (i, 0), pipeline_mode=pl.Buffered(1))
    idx = pl.BlockSpec((1, SC_WINDOW), index_map=lambda i: (0, i))

    @pl.kernel(out_type=jax.ShapeDtypeStruct((n_slots, d), hpk.dtype), mesh=_sc_mesh(), name="moe_dispatch_sc")
    def scatter(x_hbm, i0_hbm, i1_hbm, i2_hbm, i3_hbm, o_hbm):
        def body(x_vmem, *idx_vmem):
            for iv in idx_vmem:
                pltpu.sync_copy(x_vmem, o_hbm.at[iv.at[0]])

        pltpu.emit_pipeline(
            body, grid=(t // SC_WINDOW,), in_specs=[rows] + [idx] * TOP_K, out_specs=[],
            core_axis_name=("core", "subcore"), dimension_semantics=(pltpu.PARALLEL,),
        )(x_hbm, i0_hbm, i1_hbm, i2_hbm, i3_hbm)

    return scatter(hpk, *[dest_kt[k].reshape(1, t) for k in range(TOP_K)])


def _gather_rows(yb, dest_kt):
    _, t = dest_kt.shape
    d = yb.shape[1]
    n = TOP_K * t

    @pl.kernel(out_type=jax.ShapeDtypeStruct((n, d), yb.dtype), mesh=_sc_mesh(), name="moe_gather_sc")
    def gather(y_hbm, i_hbm, o_hbm):
        def body(i_vmem, o_vmem):
            pltpu.sync_copy(y_hbm.at[i_vmem.at[0]], o_vmem)

        pltpu.emit_pipeline(
            body, grid=(n // SC_WINDOW,),
            in_specs=[pl.BlockSpec((1, SC_WINDOW), index_map=lambda i: (0, i))],
            out_specs=[pl.BlockSpec((SC_WINDOW, d), index_map=lambda i: (i, 0), pipeline_mode=pl.Buffered(1))],
            core_axis_name=("core", "subcore"), dimension_semantics=(pltpu.PARALLEL,),
        )(i_hbm, o_hbm)

    return gather(yb, dest_kt.reshape(1, n)).reshape(TOP_K, t, d)


def _experts_kernel(be_ref, nused_ref, xb_ref, wgu_ref, bgu_ref, wdn_ref, bdn_ref, yb_ref, wgu_bf, wdn_bf):
    i = pl.program_id(0)
    used = i < nused_ref[0]

    @pl.when(used & ((i == 0) | (be_ref[i] != be_ref[jnp.maximum(i - 1, 0)])))
    def _():
        wgu_bf[...] = wgu_ref[0].astype(BF16)
        wdn_bf[...] = wdn_ref[0].astype(BF16)

    @pl.when(used)
    def _():
        x_hi, x_lo = _unpack_bf16_pairs(xb_ref[...])
        gu = jnp.dot(x_hi.astype(BF16), wgu_bf[:HALF_D, :], preferred_element_type=F32)
        gu += jnp.dot(x_lo.astype(BF16), wgu_bf[HALF_D:, :], preferred_element_type=F32)
        gu += bgu_ref[0]
        gate = jnp.minimum(gu[:, :D_FF], SWIGLU_LIMIT)
        up = jnp.clip(gu[:, D_FF:], -SWIGLU_LIMIT, SWIGLU_LIMIT)
        act = (up + 1.0) * gate * jax.nn.sigmoid(SWIGLU_ALPHA * gate)
        y = jnp.dot(act.astype(BF16), wdn_bf[...], preferred_element_type=F32) + bdn_ref[0]
        yb_ref[...] = _pack_bf16_pairs(y)

    @pl.when(jnp.logical_not(used))
    def _():
        yb_ref[...] = jnp.zeros(yb_ref.shape, yb_ref.dtype)


def _experts(block_expert, n_used, xb, wgu, bgu, wdn, bdn):
    n_slots = xb.shape[0]
    nb = n_slots // MOE_BLOCK
    blk = pl.BlockSpec((MOE_BLOCK, HALF_D), lambda i, be, nu: (i, 0))
    per_expert = lambda shape: pl.BlockSpec((1,) + shape, lambda i, be, nu: (be[i], 0, 0))
    grid_spec = pltpu.PrefetchScalarGridSpec(
        num_scalar_prefetch=2,
        grid=(nb,),
        in_specs=[blk, per_expert((D_MODEL, 2 * D_FF)), per_expert((1, 2 * D_FF)),
                  per_expert((D_FF, D_MODEL)), per_expert((1, D_MODEL))],
        out_specs=blk,
        scratch_shapes=[pltpu.VMEM((D_MODEL, 2 * D_FF), BF16), pltpu.VMEM((D_FF, D_MODEL), BF16)],
    )
    return pl.pallas_call(
        _experts_kernel,
        grid_spec=grid_spec,
        out_shape=jax.ShapeDtypeStruct((n_slots, HALF_D), jnp.uint32),
        compiler_params=_cparams("arbitrary"),
        name="moe_experts",
    )(block_expert, n_used, xb, wgu, bgu, wdn, bdn)


def _combine_kernel(rows_ref, gate_ref, h_ref, g_ref, b_ref, out_ref):
    tm = h_ref.shape[0]
    gate = gate_ref[...]
    ff_hi = jnp.zeros((tm, HALF_D), F32)
    ff_lo = jnp.zeros((tm, HALF_D), F32)
    for k in range(TOP_K):
        hi, lo = _unpack_bf16_pairs(rows_ref[k])
        gk = gate[:, k:k + 1]
        ff_hi = ff_hi + gk * hi
        ff_lo = ff_lo + gk * lo
    ff = jnp.concatenate([ff_hi, ff_lo], axis=1)
    out_ref[...] = _layer_norm_rows(DEEPNORM_ALPHA * h_ref[...] + ff, g_ref[...], b_ref[...])


def _combine(rows_kt, gates, h2d, ln_g, ln_b):
    t = h2d.shape[0]
    tm = min(COMBINE_TM, t)
    row = lambda i: (i, 0)
    const = lambda i: (0, 0)
    return pl.pallas_call(
        _combine_kernel,
        grid=(t // tm,),
        in_specs=[
            pl.BlockSpec((TOP_K, tm, HALF_D), lambda i: (0, i, 0)),
            pl.BlockSpec((tm, LANES), row),
            pl.BlockSpec((tm, D_MODEL), row),
            pl.BlockSpec((1, D_MODEL), const),
            pl.BlockSpec((1, D_MODEL), const),
        ],
        out_specs=pl.BlockSpec((tm, D_MODEL), row),
        out_shape=jax.ShapeDtypeStruct((t, D_MODEL), F32),
        compiler_params=_cparams("parallel"),
        name="moe_combine",
    )(rows_kt, gates, h2d, ln_g.reshape(1, -1), ln_b.reshape(1, -1))


def _moe_plan(route, counts_f32, n_tokens):
    eid = route[:TOP_K]
    rank = route[TOP_K:]
    counts = counts_f32[:, 0].astype(jnp.int32)
    padded = ((counts + MOE_BLOCK - 1) // MOE_BLOCK) * MOE_BLOCK
    pend = jnp.cumsum(padded)
    pstart = pend - padded
    dest = pstart[eid] + rank
    n_assign = n_tokens * TOP_K
    n_slots = ((n_assign + MOE_BLOCK - 1) // MOE_BLOCK) * MOE_BLOCK + N_EXPERTS * MOE_BLOCK
    nb = n_slots // MOE_BLOCK
    block_start = jnp.arange(nb, dtype=jnp.int32) * MOE_BLOCK
    block_expert = jnp.sum((pend[None, :] <= block_start[:, None]).astype(jnp.int32), axis=1)
    block_expert = jnp.minimum(block_expert, N_EXPERTS - 1).astype(jnp.int32)
    n_used = (pend[-1:] // MOE_BLOCK).astype(jnp.int32)
    return dest.astype(jnp.int32), block_expert, n_used, n_slots


def _trunk(x, p):
    batch, seq_len, _ = x.shape
    t = batch * seq_len
    xn, u, qv, lf, kk, og = _in_proj(x.reshape(t, D_MODEL), p["ln_in_g"], p["ln_in_b"], p["lb2"], p["w_in"])
    zt = _fourier_mix(u, batch, seq_len, p["fourier_norm_g"])
    per_seq = lambda a: a.reshape(a.shape[0], batch, seq_len, HEAD_DIM)
    o = _gla(per_seq(qv), per_seq(lf), per_seq(kk))
    h, hpk, route, gates, counts = _out_proj(
        zt, o.reshape(HEADS, t, HEAD_DIM), og, xn, seq_len, p["norm_g6"], p["w_out"], p["ln1_g"], p["ln1_b"],
        p["rw_hi"], p["rw_lo"], p["rb_pad"])
    dest_kt, block_expert, n_used, n_slots = _moe_plan(route, counts, t)
    xb = _dispatch(dest_kt, hpk, n_slots)
    yb = _experts(block_expert, n_used, xb, p["w_gu"], p["b_gu"], p["w_dn"], p["b_dn"])
    y = _combine(_gather_rows(yb, dest_kt), gates, h, p["ln2_g"], p["ln2_b"])
    return y.reshape(batch, seq_len, D_MODEL)


def _prepare_params(ln_in_g, ln_in_b, w_in, fourier_norm_g, lb_gamma, hgrn_norm_g, w_out, ln1_g, ln1_b,
                    router_w, router_b, w_gate_up, b_gate_up, w_down, b_down, ln2_g, ln2_b):
    fw, hw = FOURIER_WIDTH, HGRN_WIDTH
    w0 = w_in[0]
    lb_all = jnp.cumsum(jax.nn.softmax(lb_gamma.astype(F32), axis=1), axis=1)
    rw = router_w[0].astype(F32).T
    rw_hi = rw.astype(BF16)
    rw_lo = (rw - rw_hi.astype(F32)).astype(BF16)
    del fw, hw
    return dict(
        ln_in_g=ln_in_g, ln_in_b=ln_in_b, w_in=w0.astype(BF16),
        fourier_norm_g=fourier_norm_g[0],
        lb2=jnp.concatenate([lb_all[0, 0], lb_all[1, 0]]).reshape(1, -1),
        norm_g6=jnp.tile(hgrn_norm_g[0].astype(F32), HEADS).reshape(1, -1),
        w_out=w_out[0].astype(BF16), ln1_g=ln1_g[0], ln1_b=ln1_b[0],
        rw_hi=rw_hi, rw_lo=rw_lo, rb_pad=router_b[0].astype(F32).reshape(-1, 1),
        w_gu=w_gate_up[0], b_gu=b_gate_up[0].reshape(N_EXPERTS, 1, -1),
        w_dn=w_down[0], b_dn=b_down[0].reshape(N_EXPERTS, 1, -1),
        ln2_g=ln2_g[0], ln2_b=ln2_b[0],
    )


def kernel(x_prompt, x_sample, ln_in_g, ln_in_b, w_in, fourier_norm_g, lb_gamma, hgrn_norm_g, w_out,
           ln1_g, ln1_b, router_w, router_b, w_gate_up, b_gate_up, w_down, b_down, ln2_g, ln2_b):
    p = _prepare_params(ln_in_g, ln_in_b, w_in, fourier_norm_g, lb_gamma, hgrn_norm_g, w_out, ln1_g, ln1_b,
                        router_w, router_b, w_gate_up, b_gate_up, w_down, b_down, ln2_g, ln2_b)
    return (_trunk(x_prompt, p), _trunk(x_sample, p))
```

```python
import functools
import math

import numpy as np
import jax
import jax.numpy as jnp
from jax import lax
from jax.experimental import pallas as pl
from jax.experimental.pallas import tpu as pltpu
from jax.experimental.pallas import tpu_sc as plsc

D_MODEL = 1024
FOURIER_WIDTH = 256
FOURIER_GROUP_DIM = 64
HGRN_WIDTH = 768
HEAD_DIM = 128
HEADS = 6
CHUNK = 64
N_EXPERTS = 32
TOP_K = 4
D_FF = 1024
SWIGLU_LIMIT = 7.0
SWIGLU_ALPHA = 1.702
MOE_BLOCK = 512
LN_EPS = 1e-5
RMS_EPS = 1e-6
DEEPNORM_ALPHA = 2.0 ** 0.25

LANES = 128
VMEM_LIMIT_BYTES = 56 * 1024 * 1024

F32 = jnp.float32
BF16 = jnp.bfloat16


def _cparams(*sem):
    return pltpu.CompilerParams(dimension_semantics=sem, vmem_limit_bytes=VMEM_LIMIT_BYTES)


def _layer_norm_rows(x, g, b):
    mu = jnp.mean(x, axis=-1, keepdims=True)
    xc = x - mu
    var = jnp.mean(xc * xc, axis=-1, keepdims=True)
    return xc * lax.rsqrt(var + LN_EPS) * g + b


IN_TM = 512
IN_TN = 512


def _in_proj_kernel(x_ref, g_ref, b_ref, lb_ref, w_ref, xn_ref, u_ref, qv_ref, lf_ref, kk_ref, og_ref):
    xn = _layer_norm_rows(x_ref[...], g_ref[...], b_ref[...])
    xn_ref[...] = xn
    xb = xn.astype(BF16)
    hw = HGRN_WIDTH

    def chunks(col, width):
        for c0 in range(0, width, IN_TN):
            cw = min(IN_TN, width - c0)
            yield c0, cw, jnp.dot(xb, w_ref[:, col + c0:col + c0 + cw], preferred_element_type=F32)

    def store_heads(ref, first_head, c0, val):
        for j in range(val.shape[1] // HEAD_DIM):
            ref[first_head + c0 // HEAD_DIM + j] = val[:, j * HEAD_DIM:(j + 1) * HEAD_DIM]

    for c0, cw, acc in chunks(0, FOURIER_WIDTH):
        u_ref[:, c0:c0 + cw] = acc.astype(BF16)
    for c0, cw, acc in chunks(FOURIER_WIDTH, hw):
        store_heads(qv_ref, 0, c0, (acc * jax.nn.sigmoid(acc) * (HEAD_DIM ** -0.5)).astype(BF16))
    for c0, cw, acc in chunks(FOURIER_WIDTH + hw, hw):
        store_heads(qv_ref, HEADS, c0, acc.astype(BF16))
    for c0, cw, acc in chunks(FOURIER_WIDTH + 2 * hw, 2 * hw):
        lb = lb_ref[:, c0:c0 + cw]
        fg = lb + (1.0 - lb) * jax.nn.sigmoid(acc)
        store_heads(lf_ref, 0, c0, jnp.log(fg))
        store_heads(kk_ref, 0, c0, (1.0 - fg).astype(BF16))
    for c0, cw, acc in chunks(FOURIER_WIDTH + 4 * hw, hw):
        og_ref[:, c0:c0 + cw] = acc.astype(BF16)


def _in_proj(x2d, ln_g, ln_b, lb2, w_bf16):
    t = x2d.shape[0]
    tm = min(IN_TM, t)
    row = lambda i: (i, 0)
    const = lambda i: (0, 0)
    flat = lambda w, dt: (pl.BlockSpec((tm, w), row), jax.ShapeDtypeStruct((t, w), dt))
    head_major = lambda nh, dt: (pl.BlockSpec((nh, tm, HEAD_DIM), lambda i: (0, i, 0)),
                                 jax.ShapeDtypeStruct((nh, t, HEAD_DIM), dt))
    outs = [flat(D_MODEL, F32), flat(FOURIER_WIDTH, BF16), head_major(2 * HEADS, BF16),
            head_major(2 * HEADS, F32), head_major(2 * HEADS, BF16), flat(HGRN_WIDTH, BF16)]
    return pl.pallas_call(
        _in_proj_kernel,
        grid=(t // tm,),
        in_specs=[
            pl.BlockSpec((tm, D_MODEL), row),
            pl.BlockSpec((1, D_MODEL), const),
            pl.BlockSpec((1, D_MODEL), const),
            pl.BlockSpec((1, 2 * HGRN_WIDTH), const),
            pl.BlockSpec(w_bf16.shape, const),
        ],
        out_specs=[o[0] for o in outs],
        out_shape=[o[1] for o in outs],
        compiler_params=_cparams("parallel"),
        name="in_proj",
    )(x2d, ln_g.reshape(1, -1), ln_b.reshape(1, -1), lb2, w_bf16)


def _fft_split(seq_len):
    n1 = 1 << ((seq_len.bit_length() - 1 + 1) // 2)
    return n1, seq_len // n1


@functools.lru_cache(maxsize=None)
def _fft_tables(seq_len):
    n1, n2 = _fft_split(seq_len)
    k1 = np.arange(n1)
    ang1 = 2.0 * np.pi * ((k1[:, None] * k1[None, :]) % n1) / n1
    s1 = 1.0 / math.sqrt(n1)
    c1, s1m = np.cos(ang1) * s1, np.sin(ang1) * s1
    l1p = np.arange(n1)[:, None, None]
    l2p = np.arange(n2)[None, :, None]
    l2 = np.arange(n2)[None, None, :]
    ang2 = 2.0 * np.pi * ((l2 * (l1p + n1 * l2p)) % seq_len) / seq_len
    s2 = 1.0 / math.sqrt(n2)
    gc, gs = np.cos(ang2) * s2, np.sin(ang2) * s2
    kc = np.arange(FOURIER_GROUP_DIM)
    angc = 2.0 * np.pi * ((kc[:, None] * kc[None, :]) % FOURIER_GROUP_DIM) / FOURIER_GROUP_DIM
    sc = 1.0 / math.sqrt(FOURIER_GROUP_DIM)
    groups = FOURIER_WIDTH // FOURIER_GROUP_DIM
    bc = np.kron(np.eye(groups), np.cos(angc) * sc)
    bs = np.kron(np.eye(groups), np.sin(angc) * sc)
    as_bf16 = lambda a: jnp.asarray(a, dtype=F32).astype(BF16)
    return tuple(as_bf16(a) for a in (c1, s1m, gc, gs, bc, bs))


def _fft1_kernel(c_ref, s_ref, u_ref, ar_ref, ai_ref):
    u = u_ref[0]
    ar_ref[0] = jnp.dot(c_ref[...], u, preferred_element_type=F32).astype(BF16)
    ai_ref[0] = (-jnp.dot(s_ref[...], u, preferred_element_type=F32)).astype(BF16)


FFT1_TN = 4096


def _fft_stage1(u3, c1, s1):
    b, n1, width = u3.shape
    tn = min(FFT1_TN, width)
    blk = pl.BlockSpec((1, n1, tn), lambda i, j: (i, 0, j))
    mat = pl.BlockSpec((n1, n1), lambda i, j: (0, 0))
    return pl.pallas_call(
        _fft1_kernel,
        grid=(b, width // tn),
        in_specs=[mat, mat, blk],
        out_specs=[blk, blk],
        out_shape=[jax.ShapeDtypeStruct(u3.shape, BF16)] * 2,
        compiler_params=_cparams("parallel", "parallel"),
        name="fft_stage1",
    )(c1, s1, u3)


FFT2_T1 = 8


def _fft2_kernel(gc_ref, gs_ref, bc_ref, bs_ref, g_ref, ar_ref, ai_ref, z_ref):
    t1 = ar_ref.shape[1]
    for j in range(t1):
        ar, ai = ar_ref[0, j], ai_ref[0, j]
        gc, gs = gc_ref[j], gs_ref[j]
        xr = jnp.dot(gc, ar, preferred_element_type=F32) + jnp.dot(gs, ai, preferred_element_type=F32)
        xi = jnp.dot(gc, ai, preferred_element_type=F32) - jnp.dot(gs, ar, preferred_element_type=F32)
        z = jnp.dot(xr.astype(BF16), bc_ref[...], preferred_element_type=F32)
        z += jnp.dot(xi.astype(BF16), bs_ref[...], preferred_element_type=F32)
        z = z * lax.rsqrt(jnp.mean(z * z, axis=-1, keepdims=True) + RMS_EPS) * g_ref[...]
        z_ref[0, j] = z.astype(BF16)


def _fft_stage2(ar4, ai4, gc, gs, bc, bs, gain):
    b, n1, n2, w = ar4.shape
    t1 = min(FFT2_T1, n1)
    a_blk = pl.BlockSpec((1, t1, n2, w), lambda i, j: (i, j, 0, 0))
    g_blk = pl.BlockSpec((t1, n2, n2), lambda i, j: (j, 0, 0))
    c_blk = pl.BlockSpec((w, w), lambda i, j: (0, 0))
    return pl.pallas_call(
        _fft2_kernel,
        grid=(b, n1 // t1),
        in_specs=[g_blk, g_blk, c_blk, c_blk, pl.BlockSpec((1, w), lambda i, j: (0, 0)), a_blk, a_blk],
        out_specs=a_blk,
        out_shape=jax.ShapeDtypeStruct(ar4.shape, BF16),
        compiler_params=_cparams("parallel", "parallel"),
        name="fft_stage2",
    )(gc, gs, bc, bs, gain.reshape(1, -1), ar4, ai4)


def _fourier_mix(u2d, batch, seq_len, gain):
    n1, n2 = _fft_split(seq_len)
    c1, s1, gc, gs, bc, bs = _fft_tables(seq_len)
    u3 = u2d.reshape(batch, n1, n2 * FOURIER_WIDTH)
    ar, ai = _fft_stage1(u3, c1, s1)
    shape4 = (batch, n1, n2, FOURIER_WIDTH)
    zt = _fft_stage2(ar.reshape(shape4), ai.reshape(shape4), gc, gs, bc, bs, gain)
    return zt.reshape(batch, n1, n2 * FOURIER_WIDTH)


GLA_LB = 1024
GLA_HEADS_PER_STEP = 2
CUMSUM_ROWS = 256


@functools.lru_cache(maxsize=None)
def _cumsum_matrices():
    r = np.arange(CUMSUM_ROWS)
    same_chunk = (r[:, None] // CHUNK) == (r[None, :] // CHUNK)
    prefix = same_chunk & (r[None, :] <= r[:, None])
    suffix = same_chunk & (r[None, :] >= r[:, None])
    return (jnp.asarray(prefix, dtype=F32).astype(BF16), jnp.asarray(suffix, dtype=F32).astype(BF16))


def _chunk_cumsum(x, tri):
    hi = x.astype(BF16)
    lo = (x - hi.astype(F32)).astype(BF16)
    width = x.shape[1]
    parts = []
    for r0 in range(0, x.shape[0], CUMSUM_ROWS):
        rows = slice(r0, r0 + CUMSUM_ROWS)
        both = jnp.dot(tri, jnp.concatenate([hi[rows], lo[rows]], axis=1), preferred_element_type=F32)
        parts.append(both[:, :width] + both[:, width:])
    return parts[0] if len(parts) == 1 else jnp.concatenate(parts, axis=0)


def _gla_direction(q, v, k, logf, tri, st_ref, reverse):
    n = q.shape[0]
    b = _chunk_cumsum(logf, tri)
    mid = CHUNK // 2 if reverse else CHUNK // 2 - 1
    last = 0 if reverse else CHUNK - 1
    t_idx = lax.broadcasted_iota(jnp.int32, (CHUNK, CHUNK), 0)
    s_idx = lax.broadcasted_iota(jnp.int32, (CHUNK, CHUNK), 1)
    visible = (t_idx <= s_idx) if reverse else (t_idx >= s_idx)
    nchunks = n // CHUNK
    order = range(nchunks - 1, -1, -1) if reverse else range(nchunks)
    nt = (((1,), (1,)), ((), ()))
    tn = (((0,), (0,)), ((), ()))
    rows = [slice(c * CHUNK, (c + 1) * CHUNK) for c in range(nchunks)]
    qe, ke, e_mid, e_last, e_gap = [], [], [], [], []
    for c in range(nchunks):
        bc = b[rows[c]]
        b_mid = bc[mid:mid + 1]
        b_last = bc[last:last + 1]
        qe.append((q[rows[c]].astype(F32) * jnp.exp(bc - b_mid)).astype(BF16))
        ke.append((k[rows[c]].astype(F32) * jnp.exp(b_mid - bc)).astype(BF16))
        e_mid.append(jnp.exp(b_mid))
        e_last.append(jnp.exp(b_last))
        e_gap.append(jnp.exp(b_last - b_mid))
    scores, delta_t = [], []
    for c in range(nchunks):
        s = lax.dot_general(qe[c], ke[c], nt, preferred_element_type=F32)
        scores.append(jnp.where(visible, s, 0.0).astype(BF16))
        delta_t.append(lax.dot_general(v[rows[c]], ke[c], tn, preferred_element_type=F32) * e_gap[c])
    st = st_ref[...]
    st_in = [None] * nchunks
    for c in order:
        st_in[c] = (st * e_mid[c]).astype(BF16)
        st = st * e_last[c] + delta_t[c]
    st_ref[...] = st
    outs = [jnp.dot(scores[c], v[rows[c]], preferred_element_type=F32)
            + lax.dot_general(qe[c], st_in[c], nt, preferred_element_type=F32) for c in range(nchunks)]
    return jnp.concatenate(outs, axis=0)


def _gla_kernel(trif_ref, trib_ref, qf_ref, vf_ref, kf_ref, lf_ref, qb_ref, vb_ref, kb_ref, lb_ref,
                o_ref, sf_ref, sb_ref):
    j = pl.program_id(2)
    nblk = pl.num_programs(2)
    lb_rows = qf_ref.shape[2]

    @pl.when(j == 0)
    def _():
        o_ref[...] = jnp.zeros(o_ref.shape, o_ref.dtype)
        sf_ref[...] = jnp.zeros(sf_ref.shape, sf_ref.dtype)
        sb_ref[...] = jnp.zeros(sb_ref.shape, sb_ref.dtype)

    start_f = pl.multiple_of(j * lb_rows, lb_rows)
    start_b = pl.multiple_of((nblk - 1 - j) * lb_rows, lb_rows)
    for hh in range(qf_ref.shape[0]):
        o_f = _gla_direction(qf_ref[hh, 0], vf_ref[hh, 0], kf_ref[hh, 0], lf_ref[hh, 0], trif_ref[...],
                             sf_ref.at[hh], False)
        o_ref[hh, 0, pl.ds(start_f, lb_rows), :] += o_f
        o_b = _gla_direction(qb_ref[hh, 0], vb_ref[hh, 0], kb_ref[hh, 0], lb_ref[hh, 0], trib_ref[...],
                             sb_ref.at[hh], True)
        o_ref[hh, 0, pl.ds(start_b, lb_rows), :] += o_b


def _gla(qv4, lf4, kk4):
    _, b, seq_len, _ = qv4.shape
    lbk = min(GLA_LB, seq_len)
    assert lbk % CUMSUM_ROWS == 0 and seq_len % lbk == 0
    nblk = seq_len // lbk
    nh = GLA_HEADS_PER_STEP
    blk = lambda head0, rev: pl.BlockSpec(
        (nh, 1, lbk, HEAD_DIM),
        (lambda i, h, j: (head0 + h, i, nblk - 1 - j, 0)) if rev else (lambda i, h, j: (head0 + h, i, j, 0)))
    tri_spec = pl.BlockSpec((CUMSUM_ROWS, CUMSUM_ROWS), lambda i, h, j: (0, 0))
    tri_f, tri_b = _cumsum_matrices()
    groups = HEADS // nh
    return pl.pallas_call(
        _gla_kernel,
        grid=(b, groups, nblk),
        in_specs=[tri_spec, tri_spec,
                  blk(0, False), blk(groups, False), blk(0, False), blk(0, False),
                  blk(0, True), blk(groups, True), blk(groups, True), blk(groups, True)],
        out_specs=pl.BlockSpec((nh, 1, seq_len, HEAD_DIM), lambda i, h, j: (h, i, 0, 0)),
        out_shape=jax.ShapeDtypeStruct((HEADS, b, seq_len, HEAD_DIM), F32),
        scratch_shapes=[pltpu.VMEM((nh, HEAD_DIM, HEAD_DIM), F32)] * 2,
        compiler_params=_cparams("parallel", "parallel", "arbitrary"),
        name="gla",
    )(tri_f, tri_b, qv4, qv4, kk4, lf4, qv4, qv4, kk4, lf4)


HALF_D = D_MODEL // 2


def _pack_bf16_pairs(x):
    bits = lax.bitcast_convert_type(x.astype(BF16).astype(F32), jnp.uint32)
    return bits[:, :HALF_D] | (bits[:, HALF_D:] >> 16)


def _unpack_bf16_pairs(words):
    hi = lax.bitcast_convert_type(words & jnp.uint32(0xFFFF0000), F32)
    lo = lax.bitcast_convert_type(words << 16, F32)
    return hi, lo


OUT_TM = 512


def _out_proj_kernel(*refs, nz):
    z_refs = refs[:nz]
    (o_ref, g_ref, xn_ref, ng_ref, wz_ref, wh_ref, l1g_ref, l1b_ref, rwh_ref, rwl_ref, rb_ref,
     h_ref, hpk_ref, route_ref, gate_ref, cnt_ref, base_ref) = refs[nz:]
    tm = o_ref.shape[1]

    @pl.when(pl.program_id(0) == 0)
    def _():
        base_ref[...] = jnp.zeros(base_ref.shape, base_ref.dtype)

    z = jnp.concatenate([r[0] for r in z_refs], axis=0) if nz > 1 else z_refs[0][0]
    g = g_ref[...].astype(F32)
    normed = []
    for hd in range(HEADS):
        oh = o_ref[hd]
        normed.append(oh * lax.rsqrt(jnp.mean(oh * oh, axis=-1, keepdims=True) + RMS_EPS))
    hg = jnp.concatenate(normed, axis=1) * ng_ref[...] * (g * jax.nn.sigmoid(g))
    mixed = jnp.dot(z, wz_ref[...], preferred_element_type=F32)
    mixed += jnp.dot(hg.astype(BF16), wh_ref[...], preferred_element_type=F32)
    h = _layer_norm_rows(DEEPNORM_ALPHA * xn_ref[...] + mixed, l1g_ref[...], l1b_ref[...])
    h_ref[...] = h
    hpk_ref[...] = _pack_bf16_pairs(h)

    nt = (((1,), (1,)), ((), ()))
    h_hi = h.astype(BF16)
    h_lo = (h - h_hi.astype(F32)).astype(BF16)
    logits = lax.dot_general(rwh_ref[...], h_hi, nt, preferred_element_type=F32)
    logits += lax.dot_general(rwh_ref[...], h_lo, nt, preferred_element_type=F32)
    logits += lax.dot_general(rwl_ref[...], h_hi, nt, preferred_element_type=F32)
    logits += rb_ref[...]

    eid_f = lax.broadcasted_iota(jnp.int32, (N_EXPERTS, tm), 0).astype(F32)
    work = logits
    vals, idxs, hits = [], [], []
    for _ in range(TOP_K):
        m = jnp.max(work, axis=0, keepdims=True)
        idx = jnp.min(jnp.where(work == m, eid_f, float(N_EXPERTS)), axis=0, keepdims=True)
        hit = eid_f == idx
        work = jnp.where(hit, -jnp.inf, work)
        vals.append(m)
        idxs.append(idx)
        hits.append(hit)
    exps = [jnp.exp(v - vals[0]) for v in vals]
    denom = exps[0] + exps[1] + exps[2] + exps[3]

    member = jnp.zeros((N_EXPERTS, tm), F32)
    for hit in hits:
        member = member + jnp.where(hit, 1.0, 0.0)
    s_idx = lax.broadcasted_iota(jnp.int32, (tm, tm), 0)
    t_idx = lax.broadcasted_iota(jnp.int32, (tm, tm), 1)
    earlier = jnp.where(s_idx < t_idx, 1.0, 0.0).astype(BF16)
    base = base_ref[...]
    before = jnp.dot(member.astype(BF16), earlier, preferred_element_type=F32) + base

    row8 = lax.broadcasted_iota(jnp.int32, (2 * TOP_K, tm), 0)
    route = jnp.zeros((2 * TOP_K, tm), F32)
    row128 = lax.broadcasted_iota(jnp.int32, (LANES, tm), 0)
    gate_t = jnp.zeros((LANES, tm), F32)
    for k in range(TOP_K):
        rank = jnp.sum(jnp.where(hits[k], before, 0.0), axis=0, keepdims=True)
        route = route + jnp.where(row8 == k, idxs[k], 0.0) + jnp.where(row8 == TOP_K + k, rank, 0.0)
        gate_t = gate_t + jnp.where(row128 == k, exps[k] / denom, 0.0)
    route_ref[...] = route.astype(jnp.int32)
    gate_ref[...] = gate_t.T
    total = base + jnp.sum(member, axis=1, keepdims=True)
    base_ref[...] = total
    cnt_ref[...] = total


def _out_proj(zt, o3, g2d, xn2d, seq_len, norm_g6, w_out_bf16, ln_g, ln_b, rw_hi, rw_lo, rb_pad):
    t = o3.shape[1]
    n1 = zt.shape[1]
    tm = max(min(OUT_TM, seq_len), n1)
    nz = tm // n1
    row = lambda i: (i, 0)
    const = lambda i: (0, 0)

    def z_spec(k):
        return pl.BlockSpec((1, n1, FOURIER_WIDTH),
                            lambda i: ((i * tm) // seq_len, 0, ((i * tm) % seq_len) // n1 + k))

    in_specs = [z_spec(k) for k in range(nz)] + [
        pl.BlockSpec((HEADS, tm, HEAD_DIM), lambda i: (0, i, 0)),
        pl.BlockSpec((tm, HGRN_WIDTH), row),
        pl.BlockSpec((tm, D_MODEL), row),
        pl.BlockSpec((1, HGRN_WIDTH), const),
        pl.BlockSpec((FOURIER_WIDTH, D_MODEL), const),
        pl.BlockSpec((HGRN_WIDTH, D_MODEL), const),
        pl.BlockSpec((1, D_MODEL), const),
        pl.BlockSpec((1, D_MODEL), const),
        pl.BlockSpec((N_EXPERTS, D_MODEL), const),
        pl.BlockSpec((N_EXPERTS, D_MODEL), const),
        pl.BlockSpec((N_EXPERTS, 1), const),
    ]
    out_specs = [
        pl.BlockSpec((tm, D_MODEL), row),
        pl.BlockSpec((tm, HALF_D), row),
        pl.BlockSpec((2 * TOP_K, tm), lambda i: (0, i)),
        pl.BlockSpec((tm, LANES), row),
        pl.BlockSpec((N_EXPERTS, 1), const),
    ]
    out_shape = [
        jax.ShapeDtypeStruct((t, D_MODEL), F32),
        jax.ShapeDtypeStruct((t, HALF_D), jnp.uint32),
        jax.ShapeDtypeStruct((2 * TOP_K, t), jnp.int32),
        jax.ShapeDtypeStruct((t, LANES), F32),
        jax.ShapeDtypeStruct((N_EXPERTS, 1), F32),
    ]
    return pl.pallas_call(
        functools.partial(_out_proj_kernel, nz=nz),
        grid=(t // tm,),
        in_specs=in_specs,
        out_specs=out_specs,
        out_shape=out_shape,
        scratch_shapes=[pltpu.VMEM((N_EXPERTS, 1), F32)],
        compiler_params=_cparams("arbitrary"),
        name="out_proj",
    )(*([zt] * nz), o3, g2d, xn2d, norm_g6, w_out_bf16[:FOURIER_WIDTH], w_out_bf16[FOURIER_WIDTH:],
      ln_g.reshape(1, -1), ln_b.reshape(1, -1), rw_hi, rw_lo, rb_pad)


COMBINE_TM = 512
SC_WINDOW = 128


def _sc_mesh():
    return plsc.VectorSubcoreMesh(core_axis_name="core", subcore_axis_name="subcore")


def _dispatch(dest_kt, hpk, n_slots):
    t, d = hpk.shape
    rows = pl.BlockSpec((SC_WINDOW, d), index_map=lambda i: (i, 0), pipeline_mode=pl.Buffered(1))
    idx = pl.BlockSpec((1, SC_WINDOW), index_map=lambda i: (0, i))

    @pl.kernel(out_type=jax.ShapeDtypeStruct((n_slots, d), hpk.dtype), mesh=_sc_mesh(), name="moe_dispatch_sc")
    def scatter(x_hbm, i0_hbm, i1_hbm, i2_hbm, i3_hbm, o_hbm):
        def body(x_vmem, *idx_vmem):
            for iv in idx_vmem:
                pltpu.sync_copy(x_vmem, o_hbm.at[iv.at[0]])

        pltpu.emit_pipeline(
            body, grid=(t // SC_WINDOW,), in_specs=[rows] + [idx] * TOP_K, out_specs=[],
            core_axis_name=("core", "subcore"), dimension_semantics=(pltpu.PARALLEL,),
        )(x_hbm, i0_hbm, i1_hbm, i2_hbm, i3_hbm)

    return scatter(hpk, *[dest_kt[k].reshape(1, t) for k in range(TOP_K)])


def _gather_rows(yb, dest_kt):
    _, t = dest_kt.shape
    d = yb.shape[1]
    n = TOP_K * t

    @pl.kernel(out_type=jax.ShapeDtypeStruct((n, d), yb.dtype), mesh=_sc_mesh(), name="moe_gather_sc")
    def gather(y_hbm, i_hbm, o_hbm):
        def body(i_vmem, o_vmem):
            pltpu.sync_copy(y_hbm.at[i_vmem.at[0]], o_vmem)

        pltpu.emit_pipeline(
            body, grid=(n // SC_WINDOW,),
            in_specs=[pl.BlockSpec((1, SC_WINDOW), index_map=lambda i: (0, i))],
            out_specs=[pl.BlockSpec((SC_WINDOW, d), index_map=lambda i: (i, 0), pipeline_mode=pl.Buffered(1))],
            core_axis_name=("core", "subcore"), dimension_semantics=(pltpu.PARALLEL,),
        )(i_hbm, o_hbm)

    return gather(yb, dest_kt.reshape(1, n)).reshape(TOP_K, t, d)


def _experts_kernel(be_ref, nused_ref, next_ref, xb_ref, wgu_hbm, bgu_ref, wdn_hbm, bdn_ref, yb_ref,
                    wgu_f32, wdn_f32, wgu_bf, wdn_bf, slot_ref, sem):
    i = pl.program_id(0)
    used = i < nused_ref[0]
    e = be_ref[i]

    def weight_copies(expert, slot):
        return (pltpu.make_async_copy(wgu_hbm.at[expert], wgu_f32.at[slot], sem.at[slot, 0]),
                pltpu.make_async_copy(wdn_hbm.at[expert], wdn_f32.at[slot], sem.at[slot, 1]))

    @pl.when(used & (i == 0))
    def _():
        slot_ref[0] = 0
        for c in weight_copies(e, 0):
            c.start()

    @pl.when(used & (i > 0) & (e != be_ref[jnp.maximum(i - 1, 0)]))
    def _():
        slot_ref[0] = 1 - slot_ref[0]

    @pl.when(used & ((i == 0) | (e != be_ref[jnp.maximum(i - 1, 0)])))
    def _():
        slot = slot_ref[0]
        for c in weight_copies(e, slot):
            c.wait()
        wgu_bf[...] = wgu_f32[slot].astype(BF16)
        wdn_bf[...] = wdn_f32[slot].astype(BF16)
        nxt = next_ref[e]

        @pl.when(nxt != e)
        def _():
            for c in weight_copies(nxt, 1 - slot):
                c.start()

    @pl.when(used)
    def _():
        x_hi, x_lo = _unpack_bf16_pairs(xb_ref[...])
        gu = jnp.dot(x_hi.astype(BF16), wgu_bf[:HALF_D, :], preferred_element_type=F32)
        gu += jnp.dot(x_lo.astype(BF16), wgu_bf[HALF_D:, :], preferred_element_type=F32)
        gu += bgu_ref[0]
        gate = jnp.minimum(gu[:, :D_FF], SWIGLU_LIMIT)
        up = jnp.clip(gu[:, D_FF:], -SWIGLU_LIMIT, SWIGLU_LIMIT)
        act = (up + 1.0) * gate * jax.nn.sigmoid(SWIGLU_ALPHA * gate)
        y = jnp.dot(act.astype(BF16), wdn_bf[...], preferred_element_type=F32) + bdn_ref[0]
        yb_ref[...] = _pack_bf16_pairs(y)

    @pl.when(jnp.logical_not(used))
    def _():
        yb_ref[...] = jnp.zeros(yb_ref.shape, yb_ref.dtype)


def _experts(block_expert, n_used, next_expert, xb, wgu, bgu, wdn, bdn):
    n_slots = xb.shape[0]
    nb = n_slots // MOE_BLOCK
    blk = pl.BlockSpec((MOE_BLOCK, HALF_D), lambda i, be, nu, nx: (i, 0))
    per_expert = lambda shape: pl.BlockSpec((1,) + shape, lambda i, be, nu, nx: (be[i], 0, 0))
    hbm = pl.BlockSpec(memory_space=pl.ANY)
    grid_spec = pltpu.PrefetchScalarGridSpec(
        num_scalar_prefetch=3,
        grid=(nb,),
        in_specs=[blk, hbm, per_expert((1, 2 * D_FF)), hbm, per_expert((1, D_MODEL))],
        out_specs=blk,
        scratch_shapes=[pltpu.VMEM((2, D_MODEL, 2 * D_FF), F32), pltpu.VMEM((2, D_FF, D_MODEL), F32),
                        pltpu.VMEM((D_MODEL, 2 * D_FF), BF16), pltpu.VMEM((D_FF, D_MODEL), BF16),
                        pltpu.SMEM((1,), jnp.int32), pltpu.SemaphoreType.DMA((2, 2))],
    )
    return pl.pallas_call(
        _experts_kernel,
        grid_spec=grid_spec,
        out_shape=jax.ShapeDtypeStruct((n_slots, HALF_D), jnp.uint32),
        compiler_params=_cparams("arbitrary"),
        name="moe_experts",
    )(block_expert, n_used, next_expert, xb, wgu, bgu, wdn, bdn)


def _combine_kernel(rows_ref, gate_ref, h_ref, g_ref, b_ref, out_ref):
    tm = h_ref.shape[0]
    gate = gate_ref[...]
    ff_hi = jnp.zeros((tm, HALF_D), F32)
    ff_lo = jnp.zeros((tm, HALF_D), F32)
    for k in range(TOP_K):
        hi, lo = _unpack_bf16_pairs(rows_ref[k])
        gk = gate[:, k:k + 1]
        ff_hi = ff_hi + gk * hi
        ff_lo = ff_lo + gk * lo
    ff = jnp.concatenate([ff_hi, ff_lo], axis=1)
    out_ref[...] = _layer_norm_rows(DEEPNORM_ALPHA * h_ref[...] + ff, g_ref[...], b_ref[...])


def _combine(rows_kt, gates, h2d, ln_g, ln_b):
    t = h2d.shape[0]
    tm = min(COMBINE_TM, t)
    row = lambda i: (i, 0)
    const = lambda i: (0, 0)
    return pl.pallas_call(
        _combine_kernel,
        grid=(t // tm,),
        in_specs=[
            pl.BlockSpec((TOP_K, tm, HALF_D), lambda i: (0, i, 0)),
            pl.BlockSpec((tm, LANES), row),
            pl.BlockSpec((tm, D_MODEL), row),
            pl.BlockSpec((1, D_MODEL), const),
            pl.BlockSpec((1, D_MODEL), const),
        ],
        out_specs=pl.BlockSpec((tm, D_MODEL), row),
        out_shape=jax.ShapeDtypeStruct((t, D_MODEL), F32),
        compiler_params=_cparams("parallel"),
        name="moe_combine",
    )(rows_kt, gates, h2d, ln_g.reshape(1, -1), ln_b.reshape(1, -1))


def _moe_plan(route, counts_f32, n_tokens):
    eid = route[:TOP_K]
    rank = route[TOP_K:]
    counts = counts_f32[:, 0].astype(jnp.int32)
    padded = ((counts + MOE_BLOCK - 1) // MOE_BLOCK) * MOE_BLOCK
    pend = jnp.cumsum(padded)
    pstart = pend - padded
    experts = jnp.arange(N_EXPERTS, dtype=jnp.int32)[:, None, None]
    dest = rank + jnp.sum(jnp.where(eid[None] == experts, pstart[:, None, None], 0), axis=0)
    n_assign = n_tokens * TOP_K
    n_slots = ((n_assign + MOE_BLOCK - 1) // MOE_BLOCK) * MOE_BLOCK + N_EXPERTS * MOE_BLOCK
    nb = n_slots // MOE_BLOCK
    block_start = jnp.arange(nb, dtype=jnp.int32) * MOE_BLOCK
    block_expert = jnp.sum((pend[None, :] <= block_start[:, None]).astype(jnp.int32), axis=1)
    block_expert = jnp.minimum(block_expert, N_EXPERTS - 1).astype(jnp.int32)
    n_used = (pend[-1:] // MOE_BLOCK).astype(jnp.int32)
    ids = jnp.arange(N_EXPERTS, dtype=jnp.int32)
    later_nonempty = (ids[None, :] > ids[:, None]) & (counts[None, :] > 0)
    next_expert = jnp.min(jnp.where(later_nonempty, ids[None, :], N_EXPERTS), axis=1)
    next_expert = jnp.where(next_expert == N_EXPERTS, ids, next_expert).astype(jnp.int32)
    return dest.astype(jnp.int32), block_expert, n_used, next_expert, n_slots


def _trunk(x, p):
    batch, seq_len, _ = x.shape
    t = batch * seq_len
    xn, u, qv, lf, kk, og = _in_proj(x.reshape(t, D_MODEL), p["ln_in_g"], p["ln_in_b"], p["lb2"], p["w_in"])
    zt = _fourier_mix(u, batch, seq_len, p["fourier_norm_g"])
    per_seq = lambda a: a.reshape(a.shape[0], batch, seq_len, HEAD_DIM)
    o = _gla(per_seq(qv), per_seq(lf), per_seq(kk))
    h, hpk, route, gates, counts = _out_proj(
        zt, o.reshape(HEADS, t, HEAD_DIM), og, xn, seq_len, p["norm_g6"], p["w_out"], p["ln1_g"], p["ln1_b"],
        p["rw_hi"], p["rw_lo"], p["rb_pad"])
    dest_kt, block_expert, n_used, next_expert, n_slots = _moe_plan(route, counts, t)
    xb = _dispatch(dest_kt, hpk, n_slots)
    yb = _experts(block_expert, n_used, next_expert, xb, p["w_gu"], p["b_gu"], p["w_dn"], p["b_dn"])
    y = _combine(_gather_rows(yb, dest_kt), gates, h, p["ln2_g"], p["ln2_b"])
    return y.reshape(batch, seq_len, D_MODEL)


def _prepare_params(ln_in_g, ln_in_b, w_in, fourier_norm_g, lb_gamma, hgrn_norm_g, w_out, ln1_g, ln1_b,
                    router_w, router_b, w_gate_up, b_gate_up, w_down, b_down, ln2_g, ln2_b):
    fw, hw = FOURIER_WIDTH, HGRN_WIDTH
    w0 = w_in[0]
    lb_all = jnp.cumsum(jax.nn.softmax(lb_gamma.astype(F32), axis=1), axis=1)
    rw = router_w[0].astype(F32).T
    rw_hi = rw.astype(BF16)
    rw_lo = (rw - rw_hi.astype(F32)).astype(BF16)
    del fw, hw
    return dict(
        ln_in_g=ln_in_g, ln_in_b=ln_in_b, w_in=w0.astype(BF16),
        fourier_norm_g=fourier_norm_g[0],
        lb2=jnp.concatenate([lb_all[0, 0], lb_all[1, 0]]).reshape(1, -1),
        norm_g6=jnp.tile(hgrn_norm_g[0].astype(F32), HEADS).reshape(1, -1),
        w_out=w_out[0].astype(BF16), ln1_g=ln1_g[0], ln1_b=ln1_b[0],
        rw_hi=rw_hi, rw_lo=rw_lo, rb_pad=router_b[0].astype(F32).reshape(-1, 1),
        w_gu=w_gate_up[0], b_gu=b_gate_up[0].reshape(N_EXPERTS, 1, -1),
        w_dn=w_down[0], b_dn=b_down[0].reshape(N_EXPERTS, 1, -1),
        ln2_g=ln2_g[0], ln2_b=ln2_b[0],
    )


def kernel(x_prompt, x_sample, ln_in_g, ln_in_b, w_in, fourier_norm_g, lb_gamma, hgrn_norm_g, w_out,
           ln1_g, ln1_b, router_w, router_b, w_gate_up, b_gate_up, w_down, b_down, ln2_g, ln2_b):
    p = _prepare_params(ln_in_g, ln_in_b, w_in, fourier_norm_g, lb_gamma, hgrn_norm_g, w_out, ln1_g, ln1_b,
                        router_w, router_b, w_gate_up, b_gate_up, w_down, b_down, ln2_g, ln2_b)
    return (_trunk(x_prompt, p), _trunk(x_sample, p))
```

```python
import functools
import math

import numpy as np
import jax
import jax.numpy as jnp
from jax import lax
from jax.experimental import pallas as pl
from jax.experimental.pallas import tpu as pltpu
from jax.experimental.pallas import tpu_sc as plsc

D_MODEL = 1024
FOURIER_WIDTH = 256
FOURIER_GROUP_DIM = 64
HGRN_WIDTH = 768
HEAD_DIM = 128
HEADS = 6
CHUNK = 64
N_EXPERTS = 32
TOP_K = 4
D_FF = 1024
SWIGLU_LIMIT = 7.0
SWIGLU_ALPHA = 1.702
MOE_BLOCK = 1024
LN_EPS = 1e-5
RMS_EPS = 1e-6
DEEPNORM_ALPHA = 2.0 ** 0.25

LANES = 128
VMEM_LIMIT_BYTES = 56 * 1024 * 1024

F32 = jnp.float32
BF16 = jnp.bfloat16


def _cparams(*sem):
    return pltpu.CompilerParams(dimension_semantics=sem, vmem_limit_bytes=VMEM_LIMIT_BYTES)


def _layer_norm_rows(x, g, b):
    mu = jnp.mean(x, axis=-1, keepdims=True)
    xc = x - mu
    var = jnp.mean(xc * xc, axis=-1, keepdims=True)
    return xc * lax.rsqrt(var + LN_EPS) * g + b


IN_TM = 512
IN_TN = 512


def _in_proj_kernel(x_ref, g_ref, b_ref, lb_ref, w_ref, xn_ref, u_ref, qv_ref, lf_ref, kk_ref, og_ref):
    xn = _layer_norm_rows(x_ref[...], g_ref[...], b_ref[...])
    xn_ref[...] = xn
    xb = xn.astype(BF16)
    hw = HGRN_WIDTH

    def chunks(col, width):
        for c0 in range(0, width, IN_TN):
            cw = min(IN_TN, width - c0)
            yield c0, cw, jnp.dot(xb, w_ref[:, col + c0:col + c0 + cw], preferred_element_type=F32)

    def store_heads(ref, first_head, c0, val):
        for j in range(val.shape[1] // HEAD_DIM):
            ref[first_head + c0 // HEAD_DIM + j] = val[:, j * HEAD_DIM:(j + 1) * HEAD_DIM]

    for c0, cw, acc in chunks(0, FOURIER_WIDTH):
        u_ref[:, c0:c0 + cw] = acc.astype(BF16)
    for c0, cw, acc in chunks(FOURIER_WIDTH, hw):
        store_heads(qv_ref, 0, c0, (acc * jax.nn.sigmoid(acc) * (HEAD_DIM ** -0.5)).astype(BF16))
    for c0, cw, acc in chunks(FOURIER_WIDTH + hw, hw):
        store_heads(qv_ref, HEADS, c0, acc.astype(BF16))
    for c0, cw, acc in chunks(FOURIER_WIDTH + 2 * hw, 2 * hw):
        lb = lb_ref[:, c0:c0 + cw]
        fg = lb + (1.0 - lb) * jax.nn.sigmoid(acc)
        store_heads(lf_ref, 0, c0, jnp.log(fg))
        store_heads(kk_ref, 0, c0, (1.0 - fg).astype(BF16))
    for c0, cw, acc in chunks(FOURIER_WIDTH + 4 * hw, hw):
        og_ref[:, c0:c0 + cw] = acc.astype(BF16)


def _in_proj(x2d, ln_g, ln_b, lb2, w_bf16):
    t = x2d.shape[0]
    tm = min(IN_TM, t)
    row = lambda i: (i, 0)
    const = lambda i: (0, 0)
    flat = lambda w, dt: (pl.BlockSpec((tm, w), row), jax.ShapeDtypeStruct((t, w), dt))
    head_major = lambda nh, dt: (pl.BlockSpec((nh, tm, HEAD_DIM), lambda i: (0, i, 0)),
                                 jax.ShapeDtypeStruct((nh, t, HEAD_DIM), dt))
    outs = [flat(D_MODEL, F32), flat(FOURIER_WIDTH, BF16), head_major(2 * HEADS, BF16),
            head_major(2 * HEADS, F32), head_major(2 * HEADS, BF16), flat(HGRN_WIDTH, BF16)]
    return pl.pallas_call(
        _in_proj_kernel,
        grid=(t // tm,),
        in_specs=[
            pl.BlockSpec((tm, D_MODEL), row),
            pl.BlockSpec((1, D_MODEL), const),
            pl.BlockSpec((1, D_MODEL), const),
            pl.BlockSpec((1, 2 * HGRN_WIDTH), const),
            pl.BlockSpec(w_bf16.shape, const),
        ],
        out_specs=[o[0] for o in outs],
        out_shape=[o[1] for o in outs],
        compiler_params=_cparams("parallel"),
        name="in_proj",
    )(x2d, ln_g.reshape(1, -1), ln_b.reshape(1, -1), lb2, w_bf16)


def _fft_split(seq_len):
    n1 = 1 << ((seq_len.bit_length() - 1 + 1) // 2)
    return n1, seq_len // n1


@functools.lru_cache(maxsize=None)
def _fft_tables(seq_len):
    n1, n2 = _fft_split(seq_len)
    k1 = np.arange(n1)
    ang1 = 2.0 * np.pi * ((k1[:, None] * k1[None, :]) % n1) / n1
    s1 = 1.0 / math.sqrt(n1)
    c1, s1m = np.cos(ang1) * s1, np.sin(ang1) * s1
    l1p = np.arange(n1)[:, None, None]
    l2p = np.arange(n2)[None, :, None]
    l2 = np.arange(n2)[None, None, :]
    ang2 = 2.0 * np.pi * ((l2 * (l1p + n1 * l2p)) % seq_len) / seq_len
    s2 = 1.0 / math.sqrt(n2)
    gc, gs = np.cos(ang2) * s2, np.sin(ang2) * s2
    kc = np.arange(FOURIER_GROUP_DIM)
    angc = 2.0 * np.pi * ((kc[:, None] * kc[None, :]) % FOURIER_GROUP_DIM) / FOURIER_GROUP_DIM
    sc = 1.0 / math.sqrt(FOURIER_GROUP_DIM)
    groups = FOURIER_WIDTH // FOURIER_GROUP_DIM
    bc = np.kron(np.eye(groups), np.cos(angc) * sc)
    bs = np.kron(np.eye(groups), np.sin(angc) * sc)
    as_bf16 = lambda a: jnp.asarray(a, dtype=F32).astype(BF16)
    return tuple(as_bf16(a) for a in (c1, s1m, gc, gs, bc, bs))


def _fft1_kernel(c_ref, s_ref, u_ref, ar_ref, ai_ref):
    u = u_ref[0]
    ar_ref[0] = jnp.dot(c_ref[...], u, preferred_element_type=F32).astype(BF16)
    ai_ref[0] = (-jnp.dot(s_ref[...], u, preferred_element_type=F32)).astype(BF16)


FFT1_TN = 4096


def _fft_stage1(u3, c1, s1):
    b, n1, width = u3.shape
    tn = min(FFT1_TN, width)
    blk = pl.BlockSpec((1, n1, tn), lambda i, j: (i, 0, j))
    mat = pl.BlockSpec((n1, n1), lambda i, j: (0, 0))
    return pl.pallas_call(
        _fft1_kernel,
        grid=(b, width // tn),
        in_specs=[mat, mat, blk],
        out_specs=[blk, blk],
        out_shape=[jax.ShapeDtypeStruct(u3.shape, BF16)] * 2,
        compiler_params=_cparams("parallel", "parallel"),
        name="fft_stage1",
    )(c1, s1, u3)


FFT2_T1 = 8


def _fft2_kernel(gc_ref, gs_ref, bc_ref, bs_ref, g_ref, ar_ref, ai_ref, z_ref):
    t1 = ar_ref.shape[1]
    for j in range(t1):
        ar, ai = ar_ref[0, j], ai_ref[0, j]
        gc, gs = gc_ref[j], gs_ref[j]
        xr = jnp.dot(gc, ar, preferred_element_type=F32) + jnp.dot(gs, ai, preferred_element_type=F32)
        xi = jnp.dot(gc, ai, preferred_element_type=F32) - jnp.dot(gs, ar, preferred_element_type=F32)
        z = jnp.dot(xr.astype(BF16), bc_ref[...], preferred_element_type=F32)
        z += jnp.dot(xi.astype(BF16), bs_ref[...], preferred_element_type=F32)
        z = z * lax.rsqrt(jnp.mean(z * z, axis=-1, keepdims=True) + RMS_EPS) * g_ref[...]
        z_ref[0, j] = z.astype(BF16)


def _fft_stage2(ar4, ai4, gc, gs, bc, bs, gain):
    b, n1, n2, w = ar4.shape
    t1 = min(FFT2_T1, n1)
    a_blk = pl.BlockSpec((1, t1, n2, w), lambda i, j: (i, j, 0, 0))
    g_blk = pl.BlockSpec((t1, n2, n2), lambda i, j: (j, 0, 0))
    c_blk = pl.BlockSpec((w, w), lambda i, j: (0, 0))
    return pl.pallas_call(
        _fft2_kernel,
        grid=(b, n1 // t1),
        in_specs=[g_blk, g_blk, c_blk, c_blk, pl.BlockSpec((1, w), lambda i, j: (0, 0)), a_blk, a_blk],
        out_specs=a_blk,
        out_shape=jax.ShapeDtypeStruct(ar4.shape, BF16),
        compiler_params=_cparams("parallel", "parallel"),
        name="fft_stage2",
    )(gc, gs, bc, bs, gain.reshape(1, -1), ar4, ai4)


def _fourier_mix(u2d, batch, seq_len, gain):
    n1, n2 = _fft_split(seq_len)
    c1, s1, gc, gs, bc, bs = _fft_tables(seq_len)
    u3 = u2d.reshape(batch, n1, n2 * FOURIER_WIDTH)
    ar, ai = _fft_stage1(u3, c1, s1)
    shape4 = (batch, n1, n2, FOURIER_WIDTH)
    zt = _fft_stage2(ar.reshape(shape4), ai.reshape(shape4), gc, gs, bc, bs, gain)
    return zt.reshape(batch, n1, n2 * FOURIER_WIDTH)


GLA_LB = 1024
GLA_HEADS_PER_STEP = 2
CUMSUM_ROWS = 256


@functools.lru_cache(maxsize=None)
def _cumsum_matrices():
    r = np.arange(CUMSUM_ROWS)
    same_chunk = (r[:, None] // CHUNK) == (r[None, :] // CHUNK)
    prefix = same_chunk & (r[None, :] <= r[:, None])
    suffix = same_chunk & (r[None, :] >= r[:, None])
    return (jnp.asarray(prefix, dtype=F32).astype(BF16), jnp.asarray(suffix, dtype=F32).astype(BF16))


def _chunk_cumsum(x, tri):
    hi = x.astype(BF16)
    lo = (x - hi.astype(F32)).astype(BF16)
    width = x.shape[1]
    parts = []
    for r0 in range(0, x.shape[0], CUMSUM_ROWS):
        rows = slice(r0, r0 + CUMSUM_ROWS)
        both = jnp.dot(tri, jnp.concatenate([hi[rows], lo[rows]], axis=1), preferred_element_type=F32)
        parts.append(both[:, :width] + both[:, width:])
    return parts[0] if len(parts) == 1 else jnp.concatenate(parts, axis=0)


def _gla_direction(q, v, k, logf, tri, st_ref, reverse):
    n = q.shape[0]
    b = _chunk_cumsum(logf, tri)
    mid = CHUNK // 2 if reverse else CHUNK // 2 - 1
    last = 0 if reverse else CHUNK - 1
    t_idx = lax.broadcasted_iota(jnp.int32, (CHUNK, CHUNK), 0)
    s_idx = lax.broadcasted_iota(jnp.int32, (CHUNK, CHUNK), 1)
    visible = (t_idx <= s_idx) if reverse else (t_idx >= s_idx)
    nchunks = n // CHUNK
    order = range(nchunks - 1, -1, -1) if reverse else range(nchunks)
    nt = (((1,), (1,)), ((), ()))
    tn = (((0,), (0,)), ((), ()))
    rows = [slice(c * CHUNK, (c + 1) * CHUNK) for c in range(nchunks)]
    qe, ke, e_mid, e_last, e_gap = [], [], [], [], []
    for c in range(nchunks):
        bc = b[rows[c]]
        b_mid = bc[mid:mid + 1]
        b_last = bc[last:last + 1]
        qe.append((q[rows[c]].astype(F32) * jnp.exp(bc - b_mid)).astype(BF16))
        ke.append((k[rows[c]].astype(F32) * jnp.exp(b_mid - bc)).astype(BF16))
        e_mid.append(jnp.exp(b_mid))
        e_last.append(jnp.exp(b_last))
        e_gap.append(jnp.exp(b_last - b_mid))
    scores, delta_t = [], []
    for c in range(nchunks):
        s = lax.dot_general(qe[c], ke[c], nt, preferred_element_type=F32)
        scores.append(jnp.where(visible, s, 0.0).astype(BF16))
        delta_t.append(lax.dot_general(v[rows[c]], ke[c], tn, preferred_element_type=F32) * e_gap[c])
    st = st_ref[...]
    st_in = [None] * nchunks
    for c in order:
        st_in[c] = (st * e_mid[c]).astype(BF16)
        st = st * e_last[c] + delta_t[c]
    st_ref[...] = st
    outs = [jnp.dot(scores[c], v[rows[c]], preferred_element_type=F32)
            + lax.dot_general(qe[c], st_in[c], nt, preferred_element_type=F32) for c in range(nchunks)]
    return jnp.concatenate(outs, axis=0)


def _gla_kernel(trif_ref, trib_ref, qf_ref, vf_ref, kf_ref, lf_ref, qb_ref, vb_ref, kb_ref, lb_ref,
                o_ref, sf_ref, sb_ref):
    j = pl.program_id(2)
    nblk = pl.num_programs(2)
    lb_rows = qf_ref.shape[2]

    @pl.when(j == 0)
    def _():
        o_ref[...] = jnp.zeros(o_ref.shape, o_ref.dtype)
        sf_ref[...] = jnp.zeros(sf_ref.shape, sf_ref.dtype)
        sb_ref[...] = jnp.zeros(sb_ref.shape, sb_ref.dtype)

    start_f = pl.multiple_of(j * lb_rows, lb_rows)
    start_b = pl.multiple_of((nblk - 1 - j) * lb_rows, lb_rows)
    for hh in range(qf_ref.shape[0]):
        o_f = _gla_direction(qf_ref[hh, 0], vf_ref[hh, 0], kf_ref[hh, 0], lf_ref[hh, 0], trif_ref[...],
                             sf_ref.at[hh], False)
        o_ref[hh, 0, pl.ds(start_f, lb_rows), :] += o_f
        o_b = _gla_direction(qb_ref[hh, 0], vb_ref[hh, 0], kb_ref[hh, 0], lb_ref[hh, 0], trib_ref[...],
                             sb_ref.at[hh], True)
        o_ref[hh, 0, pl.ds(start_b, lb_rows), :] += o_b


def _gla(qv4, lf4, kk4):
    _, b, seq_len, _ = qv4.shape
    lbk = min(GLA_LB, seq_len)
    assert lbk % CUMSUM_ROWS == 0 and seq_len % lbk == 0
    nblk = seq_len // lbk
    nh = GLA_HEADS_PER_STEP
    blk = lambda head0, rev: pl.BlockSpec(
        (nh, 1, lbk, HEAD_DIM),
        (lambda i, h, j: (head0 + h, i, nblk - 1 - j, 0)) if rev else (lambda i, h, j: (head0 + h, i, j, 0)))
    tri_spec = pl.BlockSpec((CUMSUM_ROWS, CUMSUM_ROWS), lambda i, h, j: (0, 0))
    tri_f, tri_b = _cumsum_matrices()
    groups = HEADS // nh
    return pl.pallas_call(
        _gla_kernel,
        grid=(b, groups, nblk),
        in_specs=[tri_spec, tri_spec,
                  blk(0, False), blk(groups, False), blk(0, False), blk(0, False),
                  blk(0, True), blk(groups, True), blk(groups, True), blk(groups, True)],
        out_specs=pl.BlockSpec((nh, 1, seq_len, HEAD_DIM), lambda i, h, j: (h, i, 0, 0)),
        out_shape=jax.ShapeDtypeStruct((HEADS, b, seq_len, HEAD_DIM), F32),
        scratch_shapes=[pltpu.VMEM((nh, HEAD_DIM, HEAD_DIM), F32)] * 2,
        compiler_params=_cparams("parallel", "parallel", "arbitrary"),
        name="gla",
    )(tri_f, tri_b, qv4, qv4, kk4, lf4, qv4, qv4, kk4, lf4)


HALF_D = D_MODEL // 2


def _pack_bf16_pairs(x):
    bits = lax.bitcast_convert_type(x.astype(BF16).astype(F32), jnp.uint32)
    return bits[:, :HALF_D] | (bits[:, HALF_D:] >> 16)


def _unpack_bf16_pairs(words):
    hi = lax.bitcast_convert_type(words & jnp.uint32(0xFFFF0000), F32)
    lo = lax.bitcast_convert_type(words << 16, F32)
    return hi, lo


OUT_TM = 512


def _out_proj_kernel(*refs, nz):
    z_refs = refs[:nz]
    (o_ref, g_ref, xn_ref, ng_ref, wz_ref, wh_ref, l1g_ref, l1b_ref, rwh_ref, rwl_ref, rb_ref,
     h_ref, hpk_ref, route_ref, gate_ref, cnt_ref, base_ref) = refs[nz:]
    tm = o_ref.shape[1]

    @pl.when(pl.program_id(0) == 0)
    def _():
        base_ref[...] = jnp.zeros(base_ref.shape, base_ref.dtype)

    z = jnp.concatenate([r[0] for r in z_refs], axis=0) if nz > 1 else z_refs[0][0]
    g = g_ref[...].astype(F32)
    normed = []
    for hd in range(HEADS):
        oh = o_ref[hd]
        normed.append(oh * lax.rsqrt(jnp.mean(oh * oh, axis=-1, keepdims=True) + RMS_EPS))
    hg = jnp.concatenate(normed, axis=1) * ng_ref[...] * (g * jax.nn.sigmoid(g))
    mixed = jnp.dot(z, wz_ref[...], preferred_element_type=F32)
    mixed += jnp.dot(hg.astype(BF16), wh_ref[...], preferred_element_type=F32)
    h = _layer_norm_rows(DEEPNORM_ALPHA * xn_ref[...] + mixed, l1g_ref[...], l1b_ref[...])
    h_ref[...] = h
    hpk_ref[...] = _pack_bf16_pairs(h)

    nt = (((1,), (1,)), ((), ()))
    h_hi = h.astype(BF16)
    h_lo = (h - h_hi.astype(F32)).astype(BF16)
    logits = lax.dot_general(rwh_ref[...], h_hi, nt, preferred_element_type=F32)
    logits += lax.dot_general(rwh_ref[...], h_lo, nt, preferred_element_type=F32)
    logits += lax.dot_general(rwl_ref[...], h_hi, nt, preferred_element_type=F32)
    logits += rb_ref[...]

    eid_f = lax.broadcasted_iota(jnp.int32, (N_EXPERTS, tm), 0).astype(F32)
    work = logits
    vals, idxs, hits = [], [], []
    for _ in range(TOP_K):
        m = jnp.max(work, axis=0, keepdims=True)
        idx = jnp.min(jnp.where(work == m, eid_f, float(N_EXPERTS)), axis=0, keepdims=True)
        hit = eid_f == idx
        work = jnp.where(hit, -jnp.inf, work)
        vals.append(m)
        idxs.append(idx)
        hits.append(hit)
    exps = [jnp.exp(v - vals[0]) for v in vals]
    denom = exps[0] + exps[1] + exps[2] + exps[3]

    member = jnp.zeros((N_EXPERTS, tm), F32)
    for hit in hits:
        member = member + jnp.where(hit, 1.0, 0.0)
    s_idx = lax.broadcasted_iota(jnp.int32, (tm, tm), 0)
    t_idx = lax.broadcasted_iota(jnp.int32, (tm, tm), 1)
    earlier = jnp.where(s_idx < t_idx, 1.0, 0.0).astype(BF16)
    base = base_ref[...]
    before = jnp.dot(member.astype(BF16), earlier, preferred_element_type=F32) + base

    row8 = lax.broadcasted_iota(jnp.int32, (2 * TOP_K, tm), 0)
    route = jnp.zeros((2 * TOP_K, tm), F32)
    row128 = lax.broadcasted_iota(jnp.int32, (LANES, tm), 0)
    gate_t = jnp.zeros((LANES, tm), F32)
    for k in range(TOP_K):
        rank = jnp.sum(jnp.where(hits[k], before, 0.0), axis=0, keepdims=True)
        route = route + jnp.where(row8 == k, idxs[k], 0.0) + jnp.where(row8 == TOP_K + k, rank, 0.0)
        gate_t = gate_t + jnp.where(row128 == k, exps[k] / denom, 0.0)
    route_ref[...] = route.astype(jnp.int32)
    gate_ref[...] = gate_t.T
    total = base + jnp.sum(member, axis=1, keepdims=True)
    base_ref[...] = total
    cnt_ref[...] = total


def _out_proj(zt, o3, g2d, xn2d, seq_len, norm_g6, w_out_bf16, ln_g, ln_b, rw_hi, rw_lo, rb_pad):
    t = o3.shape[1]
    n1 = zt.shape[1]
    tm = max(min(OUT_TM, seq_len), n1)
    nz = tm // n1
    row = lambda i: (i, 0)
    const = lambda i: (0, 0)

    def z_spec(k):
        return pl.BlockSpec((1, n1, FOURIER_WIDTH),
                            lambda i: ((i * tm) // seq_len, 0, ((i * tm) % seq_len) // n1 + k))

    in_specs = [z_spec(k) for k in range(nz)] + [
        pl.BlockSpec((HEADS, tm, HEAD_DIM), lambda i: (0, i, 0)),
        pl.BlockSpec((tm, HGRN_WIDTH), row),
        pl.BlockSpec((tm, D_MODEL), row),
        pl.BlockSpec((1, HGRN_WIDTH), const),
        pl.BlockSpec((FOURIER_WIDTH, D_MODEL), const),
        pl.BlockSpec((HGRN_WIDTH, D_MODEL), const),
        pl.BlockSpec((1, D_MODEL), const),
        pl.BlockSpec((1, D_MODEL), const),
        pl.BlockSpec((N_EXPERTS, D_MODEL), const),
        pl.BlockSpec((N_EXPERTS, D_MODEL), const),
        pl.BlockSpec((N_EXPERTS, 1), const),
    ]
    out_specs = [
        pl.BlockSpec((tm, D_MODEL), row),
        pl.BlockSpec((tm, HALF_D), row),
        pl.BlockSpec((2 * TOP_K, tm), lambda i: (0, i)),
        pl.BlockSpec((tm, LANES), row),
        pl.BlockSpec((N_EXPERTS, 1), const),
    ]
    out_shape = [
        jax.ShapeDtypeStruct((t, D_MODEL), F32),
        jax.ShapeDtypeStruct((t, HALF_D), jnp.uint32),
        jax.ShapeDtypeStruct((2 * TOP_K, t), jnp.int32),
        jax.ShapeDtypeStruct((t, LANES), F32),
        jax.ShapeDtypeStruct((N_EXPERTS, 1), F32),
    ]
    return pl.pallas_call(
        functools.partial(_out_proj_kernel, nz=nz),
        grid=(t // tm,),
        in_specs=in_specs,
        out_specs=out_specs,
        out_shape=out_shape,
        scratch_shapes=[pltpu.VMEM((N_EXPERTS, 1), F32)],
        compiler_params=_cparams("arbitrary"),
        name="out_proj",
    )(*([zt] * nz), o3, g2d, xn2d, norm_g6, w_out_bf16[:FOURIER_WIDTH], w_out_bf16[FOURIER_WIDTH:],
      ln_g.reshape(1, -1), ln_b.reshape(1, -1), rw_hi, rw_lo, rb_pad)


COMBINE_TM = 512
COMBINE_PARTS = 2
SC_WINDOW = 128


def _sc_mesh():
    return plsc.VectorSubcoreMesh(core_axis_name="core", subcore_axis_name="subcore")


def _dispatch(dest_kt, hpk, n_slots):
    t, d = hpk.shape
    rows = pl.BlockSpec((SC_WINDOW, d), index_map=lambda i: (i, 0), pipeline_mode=pl.Buffered(1))
    idx = pl.BlockSpec((1, SC_WINDOW), index_map=lambda i: (0, i))

    @pl.kernel(out_type=jax.ShapeDtypeStruct((n_slots, d), hpk.dtype), mesh=_sc_mesh(), name="moe_dispatch_sc")
    def scatter(x_hbm, i0_hbm, i1_hbm, i2_hbm, i3_hbm, o_hbm):
        def body(x_vmem, *idx_vmem):
            for iv in idx_vmem:
                pltpu.sync_copy(x_vmem, o_hbm.at[iv.at[0]])

        pltpu.emit_pipeline(
            body, grid=(t // SC_WINDOW,), in_specs=[rows] + [idx] * TOP_K, out_specs=[],
            core_axis_name=("core", "subcore"), dimension_semantics=(pltpu.PARALLEL,),
        )(x_hbm, i0_hbm, i1_hbm, i2_hbm, i3_hbm)

    return scatter(hpk, *[dest_kt[k].reshape(1, t) for k in range(TOP_K)])


def _gather_rows(yb, dest_kt):
    _, t = dest_kt.shape
    d = yb.shape[1]
    n = TOP_K * t

    @pl.kernel(out_type=jax.ShapeDtypeStruct((n, d), yb.dtype), mesh=_sc_mesh(), name="moe_gather_sc")
    def gather(y_hbm, i_hbm, o_hbm):
        def body(i_vmem, o_vmem):
            pltpu.sync_copy(y_hbm.at[i_vmem.at[0]], o_vmem)

        pltpu.emit_pipeline(
            body, grid=(n // SC_WINDOW,),
            in_specs=[pl.BlockSpec((1, SC_WINDOW), index_map=lambda i: (0, i))],
            out_specs=[pl.BlockSpec((SC_WINDOW, d), index_map=lambda i: (i, 0), pipeline_mode=pl.Buffered(1))],
            core_axis_name=("core", "subcore"), dimension_semantics=(pltpu.PARALLEL,),
        )(i_hbm, o_hbm)

    return gather(yb, dest_kt.reshape(1, n)).reshape(TOP_K, t, d)


def _experts_kernel(be_ref, nused_ref, next_ref, halves_ref, xb_ref, wgu_hbm, bgu_ref, wdn_hbm, bdn_ref, yb_ref,
                    wgu_f32, wdn_f32, wgu_bf, wdn_bf, slot_ref, sem):
    i = pl.program_id(0)
    used = i < nused_ref[0]
    e = be_ref[i]

    def weight_copies(expert, slot):
        return (pltpu.make_async_copy(wgu_hbm.at[expert], wgu_f32.at[slot], sem.at[slot, 0]),
                pltpu.make_async_copy(wdn_hbm.at[expert], wdn_f32.at[slot], sem.at[slot, 1]))

    @pl.when(used & (i == 0))
    def _():
        slot_ref[0] = 0
        for c in weight_copies(e, 0):
            c.start()

    @pl.when(used & (i > 0) & (e != be_ref[jnp.maximum(i - 1, 0)]))
    def _():
        slot_ref[0] = 1 - slot_ref[0]

    @pl.when(used & ((i == 0) | (e != be_ref[jnp.maximum(i - 1, 0)])))
    def _():
        slot = slot_ref[0]
        for c in weight_copies(e, slot):
            c.wait()
        wgu_bf[...] = wgu_f32[slot].astype(BF16)
        wdn_bf[...] = wdn_f32[slot].astype(BF16)
        nxt = next_ref[e]

        @pl.when(nxt != e)
        def _():
            for c in weight_copies(nxt, 1 - slot):
                c.start()

    def mlp(rows):
        x_hi, x_lo = _unpack_bf16_pairs(xb_ref[:rows])
        gu = jnp.dot(x_hi.astype(BF16), wgu_bf[:HALF_D, :], preferred_element_type=F32)
        gu += jnp.dot(x_lo.astype(BF16), wgu_bf[HALF_D:, :], preferred_element_type=F32)
        gu += bgu_ref[0]
        gate = jnp.minimum(gu[:, :D_FF], SWIGLU_LIMIT)
        up = jnp.clip(gu[:, D_FF:], -SWIGLU_LIMIT, SWIGLU_LIMIT)
        act = (up + 1.0) * gate * jax.nn.sigmoid(SWIGLU_ALPHA * gate)
        y = jnp.dot(act.astype(BF16), wdn_bf[...], preferred_element_type=F32) + bdn_ref[0]
        yb_ref[:rows] = _pack_bf16_pairs(y)

    @pl.when(used & (halves_ref[i] == 2))
    def _():
        mlp(MOE_BLOCK)

    @pl.when(used & (halves_ref[i] != 2))
    def _():
        mlp(MOE_BLOCK // 2)


def _experts(block_expert, n_used, next_expert, halves, xb, wgu, bgu, wdn, bdn):
    n_slots = xb.shape[0]
    nb = n_slots // MOE_BLOCK
    blk = pl.BlockSpec((MOE_BLOCK, HALF_D), lambda i, be, nu, nx, hv: (i, 0))
    per_expert = lambda shape: pl.BlockSpec((1,) + shape, lambda i, be, nu, nx, hv: (be[i], 0, 0))
    hbm = pl.BlockSpec(memory_space=pl.ANY)
    grid_spec = pltpu.PrefetchScalarGridSpec(
        num_scalar_prefetch=4,
        grid=(nb,),
        in_specs=[blk, hbm, per_expert((1, 2 * D_FF)), hbm, per_expert((1, D_MODEL))],
        out_specs=blk,
        scratch_shapes=[pltpu.VMEM((2, D_MODEL, 2 * D_FF), F32), pltpu.VMEM((2, D_FF, D_MODEL), F32),
                        pltpu.VMEM((D_MODEL, 2 * D_FF), BF16), pltpu.VMEM((D_FF, D_MODEL), BF16),
                        pltpu.SMEM((1,), jnp.int32), pltpu.SemaphoreType.DMA((2, 2))],
    )
    return pl.pallas_call(
        _experts_kernel,
        grid_spec=grid_spec,
        out_shape=jax.ShapeDtypeStruct((n_slots, HALF_D), jnp.uint32),
        compiler_params=_cparams("arbitrary"),
        name="moe_experts",
    )(block_expert, n_used, next_expert, halves, xb, wgu, bgu, wdn, bdn)


def _combine_kernel(rows_ref, gate_ref, h_ref, g_ref, b_ref, *rest):
    out_ref = rest[-1]
    tm = h_ref.shape[0]
    gate = gate_ref[...]
    ff_hi = jnp.zeros((tm, HALF_D), F32)
    ff_lo = jnp.zeros((tm, HALF_D), F32)
    for k in range(TOP_K):
        hi, lo = _unpack_bf16_pairs(rows_ref[k])
        gk = gate[:, k:k + 1]
        ff_hi = ff_hi + gk * hi
        ff_lo = ff_lo + gk * lo
    ff = jnp.concatenate([ff_hi, ff_lo], axis=1)
    out_ref[...] = _layer_norm_rows(DEEPNORM_ALPHA * h_ref[...] + ff, g_ref[...], b_ref[...])


def _combine(rows_kt, gates, h2d, ln_g, ln_b, token0, prev_out):
    t = h2d.shape[0]
    t_part = rows_kt.shape[1]
    tm = min(COMBINE_TM, t_part)
    first = token0 // tm
    row = lambda i: (first + i, 0)
    const = lambda i: (0, 0)
    in_specs = [
        pl.BlockSpec((TOP_K, tm, HALF_D), lambda i: (0, i, 0)),
        pl.BlockSpec((tm, LANES), row),
        pl.BlockSpec((tm, D_MODEL), row),
        pl.BlockSpec((1, D_MODEL), const),
        pl.BlockSpec((1, D_MODEL), const),
    ]
    args = [rows_kt, gates, h2d, ln_g.reshape(1, -1), ln_b.reshape(1, -1)]
    aliases = {}
    if prev_out is not None:
        in_specs.append(pl.BlockSpec(memory_space=pl.ANY))
        args.append(prev_out)
        aliases = {len(args) - 1: 0}
    return pl.pallas_call(
        _combine_kernel,
        grid=(t_part // tm,),
        in_specs=in_specs,
        out_specs=pl.BlockSpec((tm, D_MODEL), row),
        out_shape=jax.ShapeDtypeStruct((t, D_MODEL), F32),
        input_output_aliases=aliases,
        compiler_params=_cparams("parallel"),
        name="moe_combine",
    )(*args)


def _moe_plan(route, counts_f32, n_tokens):
    eid = route[:TOP_K]
    rank = route[TOP_K:]
    counts = counts_f32[:, 0].astype(jnp.int32)
    padded = ((counts + MOE_BLOCK - 1) // MOE_BLOCK) * MOE_BLOCK
    pend = jnp.cumsum(padded)
    pstart = pend - padded
    experts = jnp.arange(N_EXPERTS, dtype=jnp.int32)[:, None, None]
    dest = rank + jnp.sum(jnp.where(eid[None] == experts, pstart[:, None, None], 0), axis=0)
    n_assign = n_tokens * TOP_K
    n_slots = ((n_assign + MOE_BLOCK - 1) // MOE_BLOCK) * MOE_BLOCK + N_EXPERTS * MOE_BLOCK
    nb = n_slots // MOE_BLOCK
    block_start = jnp.arange(nb, dtype=jnp.int32) * MOE_BLOCK
    block_expert = jnp.sum((pend[None, :] <= block_start[:, None]).astype(jnp.int32), axis=1)
    block_expert = jnp.minimum(block_expert, N_EXPERTS - 1).astype(jnp.int32)
    n_used = (pend[-1:] // MOE_BLOCK).astype(jnp.int32)
    onehot = (block_expert[:, None] == jnp.arange(N_EXPERTS, dtype=jnp.int32)[None, :]).astype(jnp.int32)
    valid_rows = jnp.sum(onehot * (pstart + counts)[None, :], axis=1) - block_start
    halves = jnp.where(valid_rows > MOE_BLOCK // 2, 2, 1).astype(jnp.int32)
    ids = jnp.arange(N_EXPERTS, dtype=jnp.int32)
    later_nonempty = (ids[None, :] > ids[:, None]) & (counts[None, :] > 0)
    next_expert = jnp.min(jnp.where(later_nonempty, ids[None, :], N_EXPERTS), axis=1)
    next_expert = jnp.where(next_expert == N_EXPERTS, ids, next_expert).astype(jnp.int32)
    return dest.astype(jnp.int32), block_expert, n_used, next_expert, halves, n_slots


def _trunk(x, p):
    batch, seq_len, _ = x.shape
    t = batch * seq_len
    xn, u, qv, lf, kk, og = _in_proj(x.reshape(t, D_MODEL), p["ln_in_g"], p["ln_in_b"], p["lb2"], p["w_in"])
    zt = _fourier_mix(u, batch, seq_len, p["fourier_norm_g"])
    per_seq = lambda a: a.reshape(a.shape[0], batch, seq_len, HEAD_DIM)
    o = _gla(per_seq(qv), per_seq(lf), per_seq(kk))
    h, hpk, route, gates, counts = _out_proj(
        zt, o.reshape(HEADS, t, HEAD_DIM), og, xn, seq_len, p["norm_g6"], p["w_out"], p["ln1_g"], p["ln1_b"],
        p["rw_hi"], p["rw_lo"], p["rb_pad"])
    dest_kt, block_expert, n_used, next_expert, halves, n_slots = _moe_plan(route, counts, t)
    xb = _dispatch(dest_kt, hpk, n_slots)
    yb = _experts(block_expert, n_used, next_expert, halves, xb, p["w_gu"], p["b_gu"], p["w_dn"], p["b_dn"])
    y = None
    part = t // COMBINE_PARTS
    for j in range(COMBINE_PARTS):
        rows = _gather_rows(yb, dest_kt[:, j * part:(j + 1) * part])
        y = _combine(rows, gates, h, p["ln2_g"], p["ln2_b"], j * part, y)
    return y.reshape(batch, seq_len, D_MODEL)


def _prepare_params(ln_in_g, ln_in_b, w_in, fourier_norm_g, lb_gamma, hgrn_norm_g, w_out, ln1_g, ln1_b,
                    router_w, router_b, w_gate_up, b_gate_up, w_down, b_down, ln2_g, ln2_b):
    fw, hw = FOURIER_WIDTH, HGRN_WIDTH
    w0 = w_in[0]
    lb_all = jnp.cumsum(jax.nn.softmax(lb_gamma.astype(F32), axis=1), axis=1)
    rw = router_w[0].astype(F32).T
    rw_hi = rw.astype(BF16)
    rw_lo = (rw - rw_hi.astype(F32)).astype(BF16)
    del fw, hw
    return dict(
        ln_in_g=ln_in_g, ln_in_b=ln_in_b, w_in=w0.astype(BF16),
        fourier_norm_g=fourier_norm_g[0],
        lb2=jnp.concatenate([lb_all[0, 0], lb_all[1, 0]]).reshape(1, -1),
        norm_g6=jnp.tile(hgrn_norm_g[0].astype(F32), HEADS).reshape(1, -1),
        w_out=w_out[0].astype(BF16), ln1_g=ln1_g[0], ln1_b=ln1_b[0],
        rw_hi=rw_hi, rw_lo=rw_lo, rb_pad=router_b[0].astype(F32).reshape(-1, 1),
        w_gu=w_gate_up[0], b_gu=b_gate_up[0].reshape(N_EXPERTS, 1, -1),
        w_dn=w_down[0], b_dn=b_down[0].reshape(N_EXPERTS, 1, -1),
        ln2_g=ln2_g[0], ln2_b=ln2_b[0],
    )


def kernel(x_prompt, x_sample, ln_in_g, ln_in_b, w_in, fourier_norm_g, lb_gamma, hgrn_norm_g, w_out,
           ln1_g, ln1_b, router_w, router_b, w_gate_up, b_gate_up, w_down, b_down, ln2_g, ln2_b):
    p = _prepare_params(ln_in_g, ln_in_b, w_in, fourier_norm_g, lb_gamma, hgrn_norm_g, w_out, ln1_g, ln1_b,
                        router_w, router_b, w_gate_up, b_gate_up, w_down, b_down, ln2_g, ln2_b)
    return (_trunk(x_prompt, p), _trunk(x_sample, p))
```

```python
import functools
import math
from typing import Callable, NamedTuple

import numpy as np
import jax
import jax.numpy as jnp
from jax import lax
from jax.experimental import pallas as pl
from jax.experimental.pallas import tpu as pltpu
from jax.experimental.pallas import tpu_sc as plsc

D_MODEL = 1024
FOURIER_WIDTH = 256
FOURIER_GROUP_DIM = 64
HGRN_WIDTH = 768
HEAD_DIM = 128
HEADS = 6
CHUNK = 64
N_EXPERTS = 32
TOP_K = 4
D_FF = 1024
SWIGLU_LIMIT = 7.0
SWIGLU_ALPHA = 1.702
MOE_BLOCK = 1024
LN_EPS = 1e-5
RMS_EPS = 1e-6
DEEPNORM_ALPHA = 2.0 ** 0.25

LANES = 128
VMEM_LIMIT_BYTES = 56 * 1024 * 1024

F32 = jnp.float32
BF16 = jnp.bfloat16


def _cparams(*sem):
    return pltpu.CompilerParams(dimension_semantics=sem, vmem_limit_bytes=VMEM_LIMIT_BYTES)


def _layer_norm_rows(x, g, b):
    mu = jnp.mean(x, axis=-1, keepdims=True)
    xc = x - mu
    var = jnp.mean(xc * xc, axis=-1, keepdims=True)
    return xc * lax.rsqrt(var + LN_EPS) * g + b


IN_TM = 512
IN_TN = 512


def _in_proj_kernel(x_ref, g_ref, b_ref, lb_ref, w_ref, xn_ref, u_ref, qv_ref, lf_ref, kk_ref, og_ref):
    xn = _layer_norm_rows(x_ref[...], g_ref[...], b_ref[...])
    xn_ref[...] = xn
    xb = xn.astype(BF16)
    hw = HGRN_WIDTH
    yield

    def chunks(col, width):
        for c0 in range(0, width, IN_TN):
            cw = min(IN_TN, width - c0)
            yield c0, cw, jnp.dot(xb, w_ref[:, col + c0:col + c0 + cw], preferred_element_type=F32)

    def store_heads(ref, first_head, c0, val):
        for j in range(val.shape[1] // HEAD_DIM):
            ref[first_head + c0 // HEAD_DIM + j] = val[:, j * HEAD_DIM:(j + 1) * HEAD_DIM]

    for c0, cw, acc in chunks(0, FOURIER_WIDTH):
        u_ref[:, c0:c0 + cw] = acc.astype(BF16)
        yield
    for c0, cw, acc in chunks(FOURIER_WIDTH, hw):
        store_heads(qv_ref, 0, c0, (acc * jax.nn.sigmoid(acc) * (HEAD_DIM ** -0.5)).astype(BF16))
        yield
    for c0, cw, acc in chunks(FOURIER_WIDTH + hw, hw):
        store_heads(qv_ref, HEADS, c0, acc.astype(BF16))
        yield
    for c0, cw, acc in chunks(FOURIER_WIDTH + 2 * hw, 2 * hw):
        lb = lb_ref[:, c0:c0 + cw]
        fg = lb + (1.0 - lb) * jax.nn.sigmoid(acc)
        store_heads(lf_ref, 0, c0, jnp.log(fg))
        store_heads(kk_ref, 0, c0, (1.0 - fg).astype(BF16))
        yield
    for c0, cw, acc in chunks(FOURIER_WIDTH + 4 * hw, hw):
        og_ref[:, c0:c0 + cw] = acc.astype(BF16)
        yield


class _Stage(NamedTuple):
    body: Callable
    init: Callable | None
    grid: tuple
    in_specs: list
    out_specs: list
    out_shape: list
    scratch: list
    args: list
    sequential: bool
    name: str


_DONE = object()


def _run_stages(stages):
    assert all(s.grid == stages[0].grid for s in stages)
    n_in = [len(s.in_specs) for s in stages]
    n_out = [len(s.out_specs) for s in stages]
    n_scr = [len(s.scratch) for s in stages]

    def body(*refs):
        scr = sum(n_in) + sum(n_out)
        for s, c in zip(stages, n_scr):
            if s.init is not None:
                pl.when(pl.program_id(0) == 0)(functools.partial(s.init, *refs[scr:scr + c]))
            scr += c
        ins, outs, scr = 0, sum(n_in), sum(n_in) + sum(n_out)
        pending = []
        for s, a, b, c in zip(stages, n_in, n_out, n_scr):
            pending.append(s.body(*refs[ins:ins + a], *refs[outs:outs + b], *refs[scr:scr + c]))
            ins, outs, scr = ins + a, outs + b, scr + c
        while pending:
            for piece in list(pending):
                if next(piece, _DONE) is _DONE:
                    pending.remove(piece)

    results = pl.pallas_call(
        body,
        grid=stages[0].grid,
        in_specs=[sp for s in stages for sp in s.in_specs],
        out_specs=[sp for s in stages for sp in s.out_specs],
        out_shape=[sh for s in stages for sh in s.out_shape],
        scratch_shapes=[sc for s in stages for sc in s.scratch],
        compiler_params=_cparams("arbitrary" if any(s.sequential for s in stages) else "parallel"),
        name="_".join(s.name for s in stages),
    )(*[a for s in stages for a in s.args])
    split, pos = [], 0
    for b in n_out:
        split.append(results[pos:pos + b])
        pos += b
    return split


def _zero_ref(ref):
    ref[...] = jnp.zeros(ref.shape, ref.dtype)


def _resident(shape):
    return pl.BlockSpec(shape, lambda i: (0,) * len(shape), pipeline_mode=pl.Buffered(1))


def _in_proj_stage(x2d, ln_g, ln_b, lb2, w_bf16):
    t = x2d.shape[0]
    tm = min(IN_TM, t)
    row = lambda i: (i, 0)
    flat = lambda w, dt: (pl.BlockSpec((tm, w), row), jax.ShapeDtypeStruct((t, w), dt))
    head_major = lambda nh, dt: (pl.BlockSpec((nh, tm, HEAD_DIM), lambda i: (0, i, 0)),
                                 jax.ShapeDtypeStruct((nh, t, HEAD_DIM), dt))
    outs = [flat(D_MODEL, F32), flat(FOURIER_WIDTH, BF16), head_major(2 * HEADS, BF16),
            head_major(2 * HEADS, F32), head_major(2 * HEADS, BF16), flat(HGRN_WIDTH, BF16)]
    return _Stage(
        body=_in_proj_kernel,
        init=None,
        grid=(t // tm,),
        in_specs=[pl.BlockSpec((tm, D_MODEL), row), _resident((1, D_MODEL)), _resident((1, D_MODEL)),
                  _resident((1, 2 * HGRN_WIDTH)), _resident(w_bf16.shape)],
        out_specs=[o[0] for o in outs],
        out_shape=[o[1] for o in outs],
        scratch=[],
        args=[x2d, ln_g.reshape(1, -1), ln_b.reshape(1, -1), lb2, w_bf16],
        sequential=False,
        name="in_proj",
    )


def _fft_split(seq_len):
    n1 = 1 << ((seq_len.bit_length() - 1 + 1) // 2)
    return n1, seq_len // n1


@functools.lru_cache(maxsize=None)
def _fft_tables(seq_len):
    n1, n2 = _fft_split(seq_len)
    k1 = np.arange(n1)
    ang1 = 2.0 * np.pi * ((k1[:, None] * k1[None, :]) % n1) / n1
    s1 = 1.0 / math.sqrt(n1)
    c1, s1m = np.cos(ang1) * s1, np.sin(ang1) * s1
    l1p = np.arange(n1)[:, None, None]
    l2p = np.arange(n2)[None, :, None]
    l2 = np.arange(n2)[None, None, :]
    ang2 = 2.0 * np.pi * ((l2 * (l1p + n1 * l2p)) % seq_len) / seq_len
    s2 = 1.0 / math.sqrt(n2)
    gc, gs = np.cos(ang2) * s2, np.sin(ang2) * s2
    kc = np.arange(FOURIER_GROUP_DIM)
    angc = 2.0 * np.pi * ((kc[:, None] * kc[None, :]) % FOURIER_GROUP_DIM) / FOURIER_GROUP_DIM
    sc = 1.0 / math.sqrt(FOURIER_GROUP_DIM)
    groups = FOURIER_WIDTH // FOURIER_GROUP_DIM
    bc = np.kron(np.eye(groups), np.cos(angc) * sc)
    bs = np.kron(np.eye(groups), np.sin(angc) * sc)
    as_bf16 = lambda a: jnp.asarray(a, dtype=F32).astype(BF16)
    return tuple(as_bf16(a) for a in (c1, s1m, gc, gs, bc, bs))


def _fft1_kernel(c_ref, s_ref, u_ref, ar_ref, ai_ref):
    u = u_ref[0]
    ar_ref[0] = jnp.dot(c_ref[...], u, preferred_element_type=F32).astype(BF16)
    ai_ref[0] = (-jnp.dot(s_ref[...], u, preferred_element_type=F32)).astype(BF16)


FFT1_TN = 4096


def _fft_stage1(u3, c1, s1):
    b, n1, width = u3.shape
    tn = min(FFT1_TN, width)
    blk = pl.BlockSpec((1, n1, tn), lambda i, j: (i, 0, j))
    mat = pl.BlockSpec((n1, n1), lambda i, j: (0, 0))
    return pl.pallas_call(
        _fft1_kernel,
        grid=(b, width // tn),
        in_specs=[mat, mat, blk],
        out_specs=[blk, blk],
        out_shape=[jax.ShapeDtypeStruct(u3.shape, BF16)] * 2,
        compiler_params=_cparams("parallel", "parallel"),
        name="fft_stage1",
    )(c1, s1, u3)


FFT2_T1 = 8


def _fft2_kernel(gc_ref, gs_ref, bc_ref, bs_ref, g_ref, ar_ref, ai_ref, z_ref):
    t1 = ar_ref.shape[1]
    for j in range(t1):
        ar, ai = ar_ref[0, j], ai_ref[0, j]
        gc, gs = gc_ref[j], gs_ref[j]
        xr = jnp.dot(gc, ar, preferred_element_type=F32) + jnp.dot(gs, ai, preferred_element_type=F32)
        xi = jnp.dot(gc, ai, preferred_element_type=F32) - jnp.dot(gs, ar, preferred_element_type=F32)
        z = jnp.dot(xr.astype(BF16), bc_ref[...], preferred_element_type=F32)
        z += jnp.dot(xi.astype(BF16), bs_ref[...], preferred_element_type=F32)
        z = z * lax.rsqrt(jnp.mean(z * z, axis=-1, keepdims=True) + RMS_EPS) * g_ref[...]
        z_ref[0, j] = z.astype(BF16)


def _fft_stage2(ar4, ai4, gc, gs, bc, bs, gain):
    b, n1, n2, w = ar4.shape
    t1 = min(FFT2_T1, n1)
    a_blk = pl.BlockSpec((1, t1, n2, w), lambda i, j: (i, j, 0, 0))
    g_blk = pl.BlockSpec((t1, n2, n2), lambda i, j: (j, 0, 0))
    c_blk = pl.BlockSpec((w, w), lambda i, j: (0, 0))
    return pl.pallas_call(
        _fft2_kernel,
        grid=(b, n1 // t1),
        in_specs=[g_blk, g_blk, c_blk, c_blk, pl.BlockSpec((1, w), lambda i, j: (0, 0)), a_blk, a_blk],
        out_specs=a_blk,
        out_shape=jax.ShapeDtypeStruct(ar4.shape, BF16),
        compiler_params=_cparams("parallel", "parallel"),
        name="fft_stage2",
    )(gc, gs, bc, bs, gain.reshape(1, -1), ar4, ai4)


def _fourier_mix(u2d, batch, seq_len, gain):
    n1, n2 = _fft_split(seq_len)
    c1, s1, gc, gs, bc, bs = _fft_tables(seq_len)
    u3 = u2d.reshape(batch, n1, n2 * FOURIER_WIDTH)
    ar, ai = _fft_stage1(u3, c1, s1)
    shape4 = (batch, n1, n2, FOURIER_WIDTH)
    zt = _fft_stage2(ar.reshape(shape4), ai.reshape(shape4), gc, gs, bc, bs, gain)
    return zt.reshape(batch, n1, n2 * FOURIER_WIDTH)


GLA_LB = 1024
GLA_HEADS_PER_STEP = 2
CUMSUM_ROWS = 256


@functools.lru_cache(maxsize=None)
def _cumsum_matrices():
    r = np.arange(CUMSUM_ROWS)
    same_chunk = (r[:, None] // CHUNK) == (r[None, :] // CHUNK)
    prefix = same_chunk & (r[None, :] <= r[:, None])
    suffix = same_chunk & (r[None, :] >= r[:, None])
    return (jnp.asarray(prefix, dtype=F32).astype(BF16), jnp.asarray(suffix, dtype=F32).astype(BF16))


def _chunk_cumsum(x, tri):
    hi = x.astype(BF16)
    lo = (x - hi.astype(F32)).astype(BF16)
    width = x.shape[1]
    parts = []
    for r0 in range(0, x.shape[0], CUMSUM_ROWS):
        rows = slice(r0, r0 + CUMSUM_ROWS)
        both = jnp.dot(tri, jnp.concatenate([hi[rows], lo[rows]], axis=1), preferred_element_type=F32)
        parts.append(both[:, :width] + both[:, width:])
    return parts[0] if len(parts) == 1 else jnp.concatenate(parts, axis=0)


def _gla_direction(q, v, k, logf, tri, st_ref, reverse):
    n = q.shape[0]
    b = _chunk_cumsum(logf, tri)
    mid = CHUNK // 2 if reverse else CHUNK // 2 - 1
    last = 0 if reverse else CHUNK - 1
    t_idx = lax.broadcasted_iota(jnp.int32, (CHUNK, CHUNK), 0)
    s_idx = lax.broadcasted_iota(jnp.int32, (CHUNK, CHUNK), 1)
    visible = (t_idx <= s_idx) if reverse else (t_idx >= s_idx)
    nchunks = n // CHUNK
    order = range(nchunks - 1, -1, -1) if reverse else range(nchunks)
    nt = (((1,), (1,)), ((), ()))
    tn = (((0,), (0,)), ((), ()))
    rows = [slice(c * CHUNK, (c + 1) * CHUNK) for c in range(nchunks)]
    qe, ke, e_mid, e_last, e_gap = [], [], [], [], []
    for c in range(nchunks):
        bc = b[rows[c]]
        b_mid = bc[mid:mid + 1]
        b_last = bc[last:last + 1]
        qe.append((q[rows[c]].astype(F32) * jnp.exp(bc - b_mid)).astype(BF16))
        ke.append((k[rows[c]].astype(F32) * jnp.exp(b_mid - bc)).astype(BF16))
        e_mid.append(jnp.exp(b_mid))
        e_last.append(jnp.exp(b_last))
        e_gap.append(jnp.exp(b_last - b_mid))
    scores, delta_t = [], []
    for c in range(nchunks):
        s = lax.dot_general(qe[c], ke[c], nt, preferred_element_type=F32)
        scores.append(jnp.where(visible, s, 0.0).astype(BF16))
        delta_t.append(lax.dot_general(v[rows[c]], ke[c], tn, preferred_element_type=F32) * e_gap[c])
    st = st_ref[...]
    st_in = [None] * nchunks
    for c in order:
        st_in[c] = (st * e_mid[c]).astype(BF16)
        st = st * e_last[c] + delta_t[c]
    st_ref[...] = st
    outs = [jnp.dot(scores[c], v[rows[c]], preferred_element_type=F32)
            + lax.dot_general(qe[c], st_in[c], nt, preferred_element_type=F32) for c in range(nchunks)]
    return jnp.concatenate(outs, axis=0)


def _gla_kernel(trif_ref, trib_ref, qf_ref, vf_ref, kf_ref, lf_ref, qb_ref, vb_ref, kb_ref, lb_ref,
                o_ref, sf_ref, sb_ref):
    j = pl.program_id(2)
    nblk = pl.num_programs(2)
    lb_rows = qf_ref.shape[2]

    @pl.when(j == 0)
    def _():
        o_ref[...] = jnp.zeros(o_ref.shape, o_ref.dtype)
        sf_ref[...] = jnp.zeros(sf_ref.shape, sf_ref.dtype)
        sb_ref[...] = jnp.zeros(sb_ref.shape, sb_ref.dtype)

    start_f = pl.multiple_of(j * lb_rows, lb_rows)
    start_b = pl.multiple_of((nblk - 1 - j) * lb_rows, lb_rows)
    for hh in range(qf_ref.shape[0]):
        o_f = _gla_direction(qf_ref[hh, 0], vf_ref[hh, 0], kf_ref[hh, 0], lf_ref[hh, 0], trif_ref[...],
                             sf_ref.at[hh], False)
        o_ref[hh, 0, pl.ds(start_f, lb_rows), :] += o_f
        o_b = _gla_direction(qb_ref[hh, 0], vb_ref[hh, 0], kb_ref[hh, 0], lb_ref[hh, 0], trib_ref[...],
                             sb_ref.at[hh], True)
        o_ref[hh, 0, pl.ds(start_b, lb_rows), :] += o_b


def _gla(qv4, lf4, kk4):
    _, b, seq_len, _ = qv4.shape
    lbk = min(GLA_LB, seq_len)
    assert lbk % CUMSUM_ROWS == 0 and seq_len % lbk == 0
    nblk = seq_len // lbk
    nh = GLA_HEADS_PER_STEP
    blk = lambda head0, rev: pl.BlockSpec(
        (nh, 1, lbk, HEAD_DIM),
        (lambda i, h, j: (head0 + h, i, nblk - 1 - j, 0)) if rev else (lambda i, h, j: (head0 + h, i, j, 0)))
    tri_spec = pl.BlockSpec((CUMSUM_ROWS, CUMSUM_ROWS), lambda i, h, j: (0, 0))
    tri_f, tri_b = _cumsum_matrices()
    groups = HEADS // nh
    return pl.pallas_call(
        _gla_kernel,
        grid=(b, groups, nblk),
        in_specs=[tri_spec, tri_spec,
                  blk(0, False), blk(groups, False), blk(0, False), blk(0, False),
                  blk(0, True), blk(groups, True), blk(groups, True), blk(groups, True)],
        out_specs=pl.BlockSpec((nh, 1, seq_len, HEAD_DIM), lambda i, h, j: (h, i, 0, 0)),
        out_shape=jax.ShapeDtypeStruct((HEADS, b, seq_len, HEAD_DIM), F32),
        scratch_shapes=[pltpu.VMEM((nh, HEAD_DIM, HEAD_DIM), F32)] * 2,
        compiler_params=_cparams("parallel", "parallel", "arbitrary"),
        name="gla",
    )(tri_f, tri_b, qv4, qv4, kk4, lf4, qv4, qv4, kk4, lf4)


HALF_D = D_MODEL // 2


def _pack_bf16_pairs(x):
    bits = lax.bitcast_convert_type(x.astype(BF16).astype(F32), jnp.uint32)
    return bits[:, :HALF_D] | (bits[:, HALF_D:] >> 16)


def _unpack_bf16_pairs(words):
    hi = lax.bitcast_convert_type(words & jnp.uint32(0xFFFF0000), F32)
    lo = lax.bitcast_convert_type(words << 16, F32)
    return hi, lo


OUT_TM = 512


def _out_proj_kernel(*refs, nz):
    z_refs = refs[:nz]
    (o_ref, g_ref, xn_ref, ng_ref, wz_ref, wh_ref, l1g_ref, l1b_ref, rwh_ref, rwl_ref, rb_ref,
     h_ref, hpk_ref, route_ref, gate_ref, cnt_ref, base_ref) = refs[nz:]
    tm = o_ref.shape[1]

    z = jnp.concatenate([r[0] for r in z_refs], axis=0) if nz > 1 else z_refs[0][0]
    g = g_ref[...].astype(F32)
    normed = []
    for hd in range(HEADS):
        oh = o_ref[hd]
        normed.append(oh * lax.rsqrt(jnp.mean(oh * oh, axis=-1, keepdims=True) + RMS_EPS))
        yield
    hg = jnp.concatenate(normed, axis=1) * ng_ref[...] * (g * jax.nn.sigmoid(g))
    yield
    mixed = jnp.dot(z, wz_ref[...], preferred_element_type=F32)
    mixed += jnp.dot(hg.astype(BF16), wh_ref[...], preferred_element_type=F32)
    yield
    h = _layer_norm_rows(DEEPNORM_ALPHA * xn_ref[...] + mixed, l1g_ref[...], l1b_ref[...])
    h_ref[...] = h
    yield
    hpk_ref[...] = _pack_bf16_pairs(h)
    yield

    nt = (((1,), (1,)), ((), ()))
    h_hi = h.astype(BF16)
    h_lo = (h - h_hi.astype(F32)).astype(BF16)
    logits = lax.dot_general(rwh_ref[...], h_hi, nt, preferred_element_type=F32)
    logits += lax.dot_general(rwh_ref[...], h_lo, nt, preferred_element_type=F32)
    logits += lax.dot_general(rwl_ref[...], h_hi, nt, preferred_element_type=F32)
    logits += rb_ref[...]
    yield

    eid_f = lax.broadcasted_iota(jnp.int32, (N_EXPERTS, tm), 0).astype(F32)
    work = logits
    vals, idxs, hits = [], [], []
    for _ in range(TOP_K):
        m = jnp.max(work, axis=0, keepdims=True)
        idx = jnp.min(jnp.where(work == m, eid_f, float(N_EXPERTS)), axis=0, keepdims=True)
        hit = eid_f == idx
        work = jnp.where(hit, -jnp.inf, work)
        vals.append(m)
        idxs.append(idx)
        hits.append(hit)
        yield
    exps = [jnp.exp(v - vals[0]) for v in vals]
    denom = exps[0] + exps[1] + exps[2] + exps[3]

    member = jnp.zeros((N_EXPERTS, tm), F32)
    for hit in hits:
        member = member + jnp.where(hit, 1.0, 0.0)
    s_idx = lax.broadcasted_iota(jnp.int32, (tm, tm), 0)
    t_idx = lax.broadcasted_iota(jnp.int32, (tm, tm), 1)
    earlier = jnp.where(s_idx < t_idx, 1.0, 0.0).astype(BF16)
    base = base_ref[...]
    before = jnp.dot(member.astype(BF16), earlier, preferred_element_type=F32) + base
    yield

    row8 = lax.broadcasted_iota(jnp.int32, (2 * TOP_K, tm), 0)
    route = jnp.zeros((2 * TOP_K, tm), F32)
    row128 = lax.broadcasted_iota(jnp.int32, (LANES, tm), 0)
    gate_t = jnp.zeros((LANES, tm), F32)
    for k in range(TOP_K):
        rank = jnp.sum(jnp.where(hits[k], before, 0.0), axis=0, keepdims=True)
        route = route + jnp.where(row8 == k, idxs[k], 0.0) + jnp.where(row8 == TOP_K + k, rank, 0.0)
        gate_t = gate_t + jnp.where(row128 == k, exps[k] / denom, 0.0)
    route_ref[...] = route.astype(jnp.int32)
    gate_ref[...] = gate_t.T
    total = base + jnp.sum(member, axis=1, keepdims=True)
    base_ref[...] = total
    cnt_ref[...] = total


def _out_proj_stage(zt, o3, g2d, xn2d, seq_len, norm_g6, w_out_bf16, ln_g, ln_b, rw_hi, rw_lo, rb_pad):
    t = o3.shape[1]
    n1 = zt.shape[1]
    tm = max(min(OUT_TM, seq_len), n1)
    nz = tm // n1
    row = lambda i: (i, 0)
    const = lambda i: (0, 0)

    def z_spec(k):
        return pl.BlockSpec((1, n1, FOURIER_WIDTH),
                            lambda i: ((i * tm) // seq_len, 0, ((i * tm) % seq_len) // n1 + k))

    in_specs = [z_spec(k) for k in range(nz)] + [
        pl.BlockSpec((HEADS, tm, HEAD_DIM), lambda i: (0, i, 0)),
        pl.BlockSpec((tm, HGRN_WIDTH), row),
        pl.BlockSpec((tm, D_MODEL), row),
        _resident((1, HGRN_WIDTH)),
        _resident((FOURIER_WIDTH, D_MODEL)),
        _resident((HGRN_WIDTH, D_MODEL)),
        _resident((1, D_MODEL)),
        _resident((1, D_MODEL)),
        _resident((N_EXPERTS, D_MODEL)),
        _resident((N_EXPERTS, D_MODEL)),
        _resident((N_EXPERTS, 1)),
    ]
    out_specs = [
        pl.BlockSpec((tm, D_MODEL), row),
        pl.BlockSpec((tm, HALF_D), row),
        pl.BlockSpec((2 * TOP_K, tm), lambda i: (0, i)),
        pl.BlockSpec((tm, LANES), row),
        pl.BlockSpec((N_EXPERTS, 1), const),
    ]
    out_shape = [
        jax.ShapeDtypeStruct((t, D_MODEL), F32),
        jax.ShapeDtypeStruct((t, HALF_D), jnp.uint32),
        jax.ShapeDtypeStruct((2 * TOP_K, t), jnp.int32),
        jax.ShapeDtypeStruct((t, LANES), F32),
        jax.ShapeDtypeStruct((N_EXPERTS, 1), F32),
    ]
    return _Stage(
        body=functools.partial(_out_proj_kernel, nz=nz),
        init=_zero_ref,
        grid=(t // tm,),
        in_specs=in_specs,
        out_specs=out_specs,
        out_shape=out_shape,
        scratch=[pltpu.VMEM((N_EXPERTS, 1), F32)],
        args=[zt] * nz + [o3, g2d, xn2d, norm_g6, w_out_bf16[:FOURIER_WIDTH], w_out_bf16[FOURIER_WIDTH:],
                          ln_g.reshape(1, -1), ln_b.reshape(1, -1), rw_hi, rw_lo, rb_pad],
        sequential=True,
        name="out_proj",
    )


COMBINE_TM = 512
COMBINE_PARTS = 2
SC_WINDOW = 128


def _sc_mesh():
    return plsc.VectorSubcoreMesh(core_axis_name="core", subcore_axis_name="subcore")


def _dispatch(dest_kt, hpk, n_slots):
    t, d = hpk.shape
    rows = pl.BlockSpec((SC_WINDOW, d), index_map=lambda i: (i, 0), pipeline_mode=pl.Buffered(1))
    idx = pl.BlockSpec((1, SC_WINDOW), index_map=lambda i: (0, i))

    @pl.kernel(out_type=jax.ShapeDtypeStruct((n_slots, d), hpk.dtype), mesh=_sc_mesh(), name="moe_dispatch_sc")
    def scatter(x_hbm, i0_hbm, i1_hbm, i2_hbm, i3_hbm, o_hbm):
        def body(x_vmem, *idx_vmem):
            for iv in idx_vmem:
                pltpu.sync_copy(x_vmem, o_hbm.at[iv.at[0]])

        pltpu.emit_pipeline(
            body, grid=(t // SC_WINDOW,), in_specs=[rows] + [idx] * TOP_K, out_specs=[],
            core_axis_name=("core", "subcore"), dimension_semantics=(pltpu.PARALLEL,),
        )(x_hbm, i0_hbm, i1_hbm, i2_hbm, i3_hbm)

    return scatter(hpk, *[dest_kt[k].reshape(1, t) for k in range(TOP_K)])


def _gather_rows(yb, dest_kt):
    _, t = dest_kt.shape
    d = yb.shape[1]
    n = TOP_K * t

    @pl.kernel(out_type=jax.ShapeDtypeStruct((n, d), yb.dtype), mesh=_sc_mesh(), name="moe_gather_sc")
    def gather(y_hbm, i_hbm, o_hbm):
        def body(i_vmem, o_vmem):
            pltpu.sync_copy(y_hbm.at[i_vmem.at[0]], o_vmem)

        pltpu.emit_pipeline(
            body, grid=(n // SC_WINDOW,),
            in_specs=[pl.BlockSpec((1, SC_WINDOW), index_map=lambda i: (0, i))],
            out_specs=[pl.BlockSpec((SC_WINDOW, d), index_map=lambda i: (i, 0), pipeline_mode=pl.Buffered(1))],
            core_axis_name=("core", "subcore"), dimension_semantics=(pltpu.PARALLEL,),
        )(i_hbm, o_hbm)

    return gather(yb, dest_kt.reshape(1, n)).reshape(TOP_K, t, d)


def _experts_kernel(be_ref, nused_ref, next_ref, halves_ref, xb_ref, wgu_hbm, bgu_ref, wdn_hbm, bdn_ref, yb_ref,
                    wgu_f32, wdn_f32, wgu_bf, wdn_bf, slot_ref, sem):
    i = pl.program_id(0)
    used = i < nused_ref[0]
    e = be_ref[i]

    def weight_copies(expert, slot):
        return (pltpu.make_async_copy(wgu_hbm.at[expert], wgu_f32.at[slot], sem.at[slot, 0]),
                pltpu.make_async_copy(wdn_hbm.at[expert], wdn_f32.at[slot], sem.at[slot, 1]))

    @pl.when(used & (i == 0))
    def _():
        slot_ref[0] = 0
        for c in weight_copies(e, 0):
            c.start()

    @pl.when(used & (i > 0) & (e != be_ref[jnp.maximum(i - 1, 0)]))
    def _():
        slot_ref[0] = 1 - slot_ref[0]

    @pl.when(used & ((i == 0) | (e != be_ref[jnp.maximum(i - 1, 0)])))
    def _():
        slot = slot_ref[0]
        for c in weight_copies(e, slot):
            c.wait()
        wgu_bf[...] = wgu_f32[slot].astype(BF16)
        wdn_bf[...] = wdn_f32[slot].astype(BF16)
        nxt = next_ref[e]

        @pl.when(nxt != e)
        def _():
            for c in weight_copies(nxt, 1 - slot):
                c.start()

    def mlp(rows):
        x_hi, x_lo = _unpack_bf16_pairs(xb_ref[:rows])
        x = jnp.concatenate([x_hi.astype(BF16), x_lo.astype(BF16)], axis=1)
        gu = jnp.dot(x, wgu_bf[...], preferred_element_type=F32) + bgu_ref[0]
        gate = jnp.minimum(gu[:, :D_FF], SWIGLU_LIMIT)
        up = jnp.clip(gu[:, D_FF:], -SWIGLU_LIMIT, SWIGLU_LIMIT)
        act = (up + 1.0) * gate * jax.nn.sigmoid(SWIGLU_ALPHA * gate)
        y = jnp.dot(act.astype(BF16), wdn_bf[...], preferred_element_type=F32) + bdn_ref[0]
        yb_ref[:rows] = _pack_bf16_pairs(y)

    @pl.when(used & (halves_ref[i] == 2))
    def _():
        mlp(MOE_BLOCK)

    @pl.when(used & (halves_ref[i] != 2))
    def _():
        mlp(MOE_BLOCK // 2)


def _experts(block_expert, n_used, next_expert, halves, xb, wgu, bgu, wdn, bdn):
    n_slots = xb.shape[0]
    nb = n_slots // MOE_BLOCK
    blk = pl.BlockSpec((MOE_BLOCK, HALF_D), lambda i, be, nu, nx, hv: (i, 0))
    per_expert = lambda shape: pl.BlockSpec((1,) + shape, lambda i, be, nu, nx, hv: (be[i], 0, 0))
    hbm = pl.BlockSpec(memory_space=pl.ANY)
    grid_spec = pltpu.PrefetchScalarGridSpec(
        num_scalar_prefetch=4,
        grid=(nb,),
        in_specs=[blk, hbm, per_expert((1, 2 * D_FF)), hbm, per_expert((1, D_MODEL))],
        out_specs=blk,
        scratch_shapes=[pltpu.VMEM((2, D_MODEL, 2 * D_FF), F32), pltpu.VMEM((2, D_FF, D_MODEL), F32),
                        pltpu.VMEM((D_MODEL, 2 * D_FF), BF16), pltpu.VMEM((D_FF, D_MODEL), BF16),
                        pltpu.SMEM((1,), jnp.int32), pltpu.SemaphoreType.DMA((2, 2))],
    )
    return pl.pallas_call(
        _experts_kernel,
        grid_spec=grid_spec,
        out_shape=jax.ShapeDtypeStruct((n_slots, HALF_D), jnp.uint32),
        compiler_params=_cparams("arbitrary"),
        name="moe_experts",
    )(block_expert, n_used, next_expert, halves, xb, wgu, bgu, wdn, bdn)


def _combine_kernel(rows_ref, gate_ref, h_ref, g_ref, b_ref, *rest):
    out_ref = rest[-1]
    tm = h_ref.shape[0]
    gate = gate_ref[...]
    ff_hi = jnp.zeros((tm, HALF_D), F32)
    ff_lo = jnp.zeros((tm, HALF_D), F32)
    for k in range(TOP_K):
        hi, lo = _unpack_bf16_pairs(rows_ref[k])
        gk = gate[:, k:k + 1]
        ff_hi = ff_hi + gk * hi
        ff_lo = ff_lo + gk * lo
    ff = jnp.concatenate([ff_hi, ff_lo], axis=1)
    out_ref[...] = _layer_norm_rows(DEEPNORM_ALPHA * h_ref[...] + ff, g_ref[...], b_ref[...])


def _combine(rows_kt, gates, h2d, ln_g, ln_b, token0, prev_out):
    t = h2d.shape[0]
    t_part = rows_kt.shape[1]
    tm = min(COMBINE_TM, t_part)
    first = token0 // tm
    row = lambda i: (first + i, 0)
    const = lambda i: (0, 0)
    in_specs = [
        pl.BlockSpec((TOP_K, tm, HALF_D), lambda i: (0, i, 0)),
        pl.BlockSpec((tm, LANES), row),
        pl.BlockSpec((tm, D_MODEL), row),
        pl.BlockSpec((1, D_MODEL), const),
        pl.BlockSpec((1, D_MODEL), const),
    ]
    args = [rows_kt, gates, h2d, ln_g.reshape(1, -1), ln_b.reshape(1, -1)]
    aliases = {}
    if prev_out is not None:
        in_specs.append(pl.BlockSpec(memory_space=pl.ANY))
        args.append(prev_out)
        aliases = {len(args) - 1: 0}
    return pl.pallas_call(
        _combine_kernel,
        grid=(t_part // tm,),
        in_specs=in_specs,
        out_specs=pl.BlockSpec((tm, D_MODEL), row),
        out_shape=jax.ShapeDtypeStruct((t, D_MODEL), F32),
        input_output_aliases=aliases,
        compiler_params=_cparams("parallel"),
        name="moe_combine",
    )(*args)


def _moe_plan(route, counts_f32, n_tokens):
    eid = route[:TOP_K]
    rank = route[TOP_K:]
    counts = counts_f32[:, 0].astype(jnp.int32)
    padded = ((counts + MOE_BLOCK - 1) // MOE_BLOCK) * MOE_BLOCK
    pend = jnp.cumsum(padded)
    pstart = pend - padded
    experts = jnp.arange(N_EXPERTS, dtype=jnp.int32)[:, None, None]
    dest = rank + jnp.sum(jnp.where(eid[None] == experts, pstart[:, None, None], 0), axis=0)
    n_assign = n_tokens * TOP_K
    n_slots = ((n_assign + MOE_BLOCK - 1) // MOE_BLOCK) * MOE_BLOCK + N_EXPERTS * MOE_BLOCK
    nb = n_slots // MOE_BLOCK
    block_start = jnp.arange(nb, dtype=jnp.int32) * MOE_BLOCK
    block_expert = jnp.sum((pend[None, :] <= block_start[:, None]).astype(jnp.int32), axis=1)
    block_expert = jnp.minimum(block_expert, N_EXPERTS - 1).astype(jnp.int32)
    n_used = (pend[-1:] // MOE_BLOCK).astype(jnp.int32)
    onehot = (block_expert[:, None] == jnp.arange(N_EXPERTS, dtype=jnp.int32)[None, :]).astype(jnp.int32)
    valid_rows = jnp.sum(onehot * (pstart + counts)[None, :], axis=1) - block_start
    halves = jnp.where(valid_rows > MOE_BLOCK // 2, 2, 1).astype(jnp.int32)
    ids = jnp.arange(N_EXPERTS, dtype=jnp.int32)
    later_nonempty = (ids[None, :] > ids[:, None]) & (counts[None, :] > 0)
    next_expert = jnp.min(jnp.where(later_nonempty, ids[None, :], N_EXPERTS), axis=1)
    next_expert = jnp.where(next_expert == N_EXPERTS, ids, next_expert).astype(jnp.int32)
    return dest.astype(jnp.int32), block_expert, n_used, next_expert, halves, n_slots


def _in_stage(x, p):
    t = x.shape[0] * x.shape[1]
    return _in_proj_stage(x.reshape(t, D_MODEL), p["ln_in_g"], p["ln_in_b"], p["lb2"], p["w_in"])


def _mix_and_out_stage(x, in_outs, p):
    batch, seq_len, _ = x.shape
    t = batch * seq_len
    xn, u, qv, lf, kk, og = in_outs
    zt = _fourier_mix(u, batch, seq_len, p["fourier_norm_g"])
    per_seq = lambda a: a.reshape(a.shape[0], batch, seq_len, HEAD_DIM)
    o = _gla(per_seq(qv), per_seq(lf), per_seq(kk))
    return _out_proj_stage(
        zt, o.reshape(HEADS, t, HEAD_DIM), og, xn, seq_len, p["norm_g6"], p["w_out"], p["ln1_g"], p["ln1_b"],
        p["rw_hi"], p["rw_lo"], p["rb_pad"])


def _moe(x, out_outs, p):
    batch, seq_len, _ = x.shape
    t = batch * seq_len
    h, hpk, route, gates, counts = out_outs
    dest_kt, block_expert, n_used, next_expert, halves, n_slots = _moe_plan(route, counts, t)
    xb = _dispatch(dest_kt, hpk, n_slots)
    yb = _experts(block_expert, n_used, next_expert, halves, xb, p["w_gu"], p["b_gu"], p["w_dn"], p["b_dn"])
    y = None
    part = t // COMBINE_PARTS
    for j in range(COMBINE_PARTS):
        rows = _gather_rows(yb, dest_kt[:, j * part:(j + 1) * part])
        y = _combine(rows, gates, h, p["ln2_g"], p["ln2_b"], j * part, y)
    return y.reshape(batch, seq_len, D_MODEL)


def _prepare_params(ln_in_g, ln_in_b, w_in, fourier_norm_g, lb_gamma, hgrn_norm_g, w_out, ln1_g, ln1_b,
                    router_w, router_b, w_gate_up, b_gate_up, w_down, b_down, ln2_g, ln2_b):
    fw, hw = FOURIER_WIDTH, HGRN_WIDTH
    w0 = w_in[0]
    lb_all = jnp.cumsum(jax.nn.softmax(lb_gamma.astype(F32), axis=1), axis=1)
    rw = router_w[0].astype(F32).T
    rw_hi = rw.astype(BF16)
    rw_lo = (rw - rw_hi.astype(F32)).astype(BF16)
    del fw, hw
    return dict(
        ln_in_g=ln_in_g, ln_in_b=ln_in_b, w_in=w0.astype(BF16),
        fourier_norm_g=fourier_norm_g[0],
        lb2=jnp.concatenate([lb_all[0, 0], lb_all[1, 0]]).reshape(1, -1),
        norm_g6=jnp.tile(hgrn_norm_g[0].astype(F32), HEADS).reshape(1, -1),
        w_out=w_out[0].astype(BF16), ln1_g=ln1_g[0], ln1_b=ln1_b[0],
        rw_hi=rw_hi, rw_lo=rw_lo, rb_pad=router_b[0].astype(F32).reshape(-1, 1),
        w_gu=w_gate_up[0], b_gu=b_gate_up[0].reshape(N_EXPERTS, 1, -1),
        w_dn=w_down[0], b_dn=b_down[0].reshape(N_EXPERTS, 1, -1),
        ln2_g=ln2_g[0], ln2_b=ln2_b[0],
    )


def kernel(x_prompt, x_sample, ln_in_g, ln_in_b, w_in, fourier_norm_g, lb_gamma, hgrn_norm_g, w_out,
           ln1_g, ln1_b, router_w, router_b, w_gate_up, b_gate_up, w_down, b_down, ln2_g, ln2_b):
    p = _prepare_params(ln_in_g, ln_in_b, w_in, fourier_norm_g, lb_gamma, hgrn_norm_g, w_out, ln1_g, ln1_b,
                        router_w, router_b, w_gate_up, b_gate_up, w_down, b_down, ln2_g, ln2_b)
    first, second = x_sample, x_prompt
    (in_first,) = _run_stages([_in_stage(first, p)])
    out_stage_first = _mix_and_out_stage(first, in_first, p)
    in_stage_second = _in_stage(second, p)
    if in_stage_second.grid == out_stage_first.grid:
        in_second, out_first = _run_stages([in_stage_second, out_stage_first])
    else:
        (in_second,) = _run_stages([in_stage_second])
        (out_first,) = _run_stages([out_stage_first])
    (out_second,) = _run_stages([_mix_and_out_stage(second, in_second, p)])
    y_first = _moe(first, out_first, p)
    y_second = _moe(second, out_second, p)
    return (y_second, y_first)
```

```python
import functools
import math
from typing import Callable, NamedTuple

import numpy as np
import jax
import jax.numpy as jnp
from jax import lax
from jax.experimental import pallas as pl
from jax.experimental.pallas import tpu as pltpu
from jax.experimental.pallas import tpu_sc as plsc

D_MODEL = 1024
FOURIER_WIDTH = 256
FOURIER_GROUP_DIM = 64
HGRN_WIDTH = 768
HEAD_DIM = 128
HEADS = 6
CHUNK = 64
N_EXPERTS = 32
TOP_K = 4
D_FF = 1024
SWIGLU_LIMIT = 7.0
SWIGLU_ALPHA = 1.702
MOE_BLOCK = 1024
LN_EPS = 1e-5
RMS_EPS = 1e-6
DEEPNORM_ALPHA = 2.0 ** 0.25

LANES = 128
VMEM_LIMIT_BYTES = 56 * 1024 * 1024

F32 = jnp.float32
BF16 = jnp.bfloat16


def _cparams(*sem):
    return pltpu.CompilerParams(dimension_semantics=sem, vmem_limit_bytes=VMEM_LIMIT_BYTES)


def _layer_norm_rows(x, g, b):
    mu = jnp.mean(x, axis=-1, keepdims=True)
    xc = x - mu
    var = jnp.mean(xc * xc, axis=-1, keepdims=True)
    return xc * lax.rsqrt(var + LN_EPS) * g + b


IN_TM = 512
IN_TN = 512


def _in_proj_kernel(x_ref, g_ref, b_ref, lb_ref, w_ref, xn_ref, u_ref, qv_ref, lf_ref, kk_ref, og_ref):
    xn = _layer_norm_rows(x_ref[...], g_ref[...], b_ref[...])
    xn_ref[...] = xn
    xb = xn.astype(BF16)
    hw = HGRN_WIDTH
    yield

    def chunks(col, width):
        for c0 in range(0, width, IN_TN):
            cw = min(IN_TN, width - c0)
            yield c0, cw, jnp.dot(xb, w_ref[:, col + c0:col + c0 + cw], preferred_element_type=F32)

    def store_heads(ref, first_head, c0, val):
        for j in range(val.shape[1] // HEAD_DIM):
            ref[first_head + c0 // HEAD_DIM + j] = val[:, j * HEAD_DIM:(j + 1) * HEAD_DIM]

    for c0, cw, acc in chunks(0, FOURIER_WIDTH):
        u_ref[:, c0:c0 + cw] = acc.astype(BF16)
        yield
    for c0, cw, acc in chunks(FOURIER_WIDTH, hw):
        store_heads(qv_ref, 0, c0, (acc * jax.nn.sigmoid(acc) * (HEAD_DIM ** -0.5)).astype(BF16))
        yield
    for c0, cw, acc in chunks(FOURIER_WIDTH + hw, hw):
        store_heads(qv_ref, HEADS, c0, acc.astype(BF16))
        yield
    for c0, cw, acc in chunks(FOURIER_WIDTH + 2 * hw, 2 * hw):
        lb = lb_ref[:, c0:c0 + cw]
        fg = lb + (1.0 - lb) * jax.nn.sigmoid(acc)
        store_heads(lf_ref, 0, c0, jnp.log(fg))
        store_heads(kk_ref, 0, c0, (1.0 - fg).astype(BF16))
        yield
    for c0, cw, acc in chunks(FOURIER_WIDTH + 4 * hw, hw):
        og_ref[:, c0:c0 + cw] = acc.astype(BF16)
        yield


class _Stage(NamedTuple):
    body: Callable
    init: Callable | None
    grid: tuple
    in_specs: list
    out_specs: list
    out_shape: list
    scratch: list
    args: list
    sequential: bool
    name: str


_DONE = object()


def _interleave(pieces, stagger):
    done = [False] * len(pieces)
    t = 0
    while not all(done):
        for i, g in enumerate(pieces):
            if not done[i] and t >= i * stagger:
                done[i] = next(g, _DONE) is _DONE
        t += 1


def _run_stages(stages):
    assert all(s.grid == stages[0].grid for s in stages)
    n_in = [len(s.in_specs) for s in stages]
    n_out = [len(s.out_specs) for s in stages]
    n_scr = [len(s.scratch) for s in stages]

    def body(*refs):
        scr = sum(n_in) + sum(n_out)
        for s, c in zip(stages, n_scr):
            if s.init is not None:
                pl.when(pl.program_id(0) == 0)(functools.partial(s.init, *refs[scr:scr + c]))
            scr += c
        ins, outs, scr = 0, sum(n_in), sum(n_in) + sum(n_out)
        bodies = []
        for s, a, b, c in zip(stages, n_in, n_out, n_scr):
            bodies.append(s.body(*refs[ins:ins + a], *refs[outs:outs + b], *refs[scr:scr + c]))
            ins, outs, scr = ins + a, outs + b, scr + c
        _interleave(bodies, 0)

    results = pl.pallas_call(
        body,
        grid=stages[0].grid,
        in_specs=[sp for s in stages for sp in s.in_specs],
        out_specs=[sp for s in stages for sp in s.out_specs],
        out_shape=[sh for s in stages for sh in s.out_shape],
        scratch_shapes=[sc for s in stages for sc in s.scratch],
        compiler_params=_cparams("arbitrary" if any(s.sequential for s in stages) else "parallel"),
        name="_".join(s.name for s in stages),
    )(*[a for s in stages for a in s.args])
    split, pos = [], 0
    for b in n_out:
        split.append(results[pos:pos + b])
        pos += b
    return split


def _zero_ref(ref):
    ref[...] = jnp.zeros(ref.shape, ref.dtype)


def _resident(shape):
    return pl.BlockSpec(shape, lambda i: (0,) * len(shape), pipeline_mode=pl.Buffered(1))


def _in_proj_stage(x2d, ln_g, ln_b, lb2, w_bf16):
    t = x2d.shape[0]
    tm = min(IN_TM, t)
    row = lambda i: (i, 0)
    flat = lambda w, dt: (pl.BlockSpec((tm, w), row), jax.ShapeDtypeStruct((t, w), dt))
    head_major = lambda nh, dt: (pl.BlockSpec((nh, tm, HEAD_DIM), lambda i: (0, i, 0)),
                                 jax.ShapeDtypeStruct((nh, t, HEAD_DIM), dt))
    outs = [flat(D_MODEL, F32), flat(FOURIER_WIDTH, BF16), head_major(2 * HEADS, BF16),
            head_major(2 * HEADS, F32), head_major(2 * HEADS, BF16), flat(HGRN_WIDTH, BF16)]
    return _Stage(
        body=_in_proj_kernel,
        init=None,
        grid=(t // tm,),
        in_specs=[pl.BlockSpec((tm, D_MODEL), row), _resident((1, D_MODEL)), _resident((1, D_MODEL)),
                  _resident((1, 2 * HGRN_WIDTH)), _resident(w_bf16.shape)],
        out_specs=[o[0] for o in outs],
        out_shape=[o[1] for o in outs],
        scratch=[],
        args=[x2d, ln_g.reshape(1, -1), ln_b.reshape(1, -1), lb2, w_bf16],
        sequential=False,
        name="in_proj",
    )


def _fft_split(seq_len):
    n1 = 1 << ((seq_len.bit_length() - 1 + 1) // 2)
    return n1, seq_len // n1


@functools.lru_cache(maxsize=None)
def _fft_tables(seq_len):
    n1, n2 = _fft_split(seq_len)
    k1 = np.arange(n1)
    ang1 = 2.0 * np.pi * ((k1[:, None] * k1[None, :]) % n1) / n1
    s1 = 1.0 / math.sqrt(n1)
    c1, s1m = np.cos(ang1) * s1, np.sin(ang1) * s1
    l1p = np.arange(n1)[:, None, None]
    l2p = np.arange(n2)[None, :, None]
    l2 = np.arange(n2)[None, None, :]
    ang2 = 2.0 * np.pi * ((l2 * (l1p + n1 * l2p)) % seq_len) / seq_len
    s2 = 1.0 / math.sqrt(n2)
    gc, gs = np.cos(ang2) * s2, np.sin(ang2) * s2
    kc = np.arange(FOURIER_GROUP_DIM)
    angc = 2.0 * np.pi * ((kc[:, None] * kc[None, :]) % FOURIER_GROUP_DIM) / FOURIER_GROUP_DIM
    sc = 1.0 / math.sqrt(FOURIER_GROUP_DIM)
    groups = FOURIER_WIDTH // FOURIER_GROUP_DIM
    bc = np.kron(np.eye(groups), np.cos(angc) * sc)
    bs = np.kron(np.eye(groups), np.sin(angc) * sc)
    as_bf16 = lambda a: jnp.asarray(a, dtype=F32).astype(BF16)
    return tuple(as_bf16(a) for a in (c1, s1m, gc, gs, bc, bs))


def _fft1_kernel(c_ref, s_ref, u_ref, ar_ref, ai_ref):
    u = u_ref[0]
    ar_ref[0] = jnp.dot(c_ref[...], u, preferred_element_type=F32).astype(BF16)
    ai_ref[0] = (-jnp.dot(s_ref[...], u, preferred_element_type=F32)).astype(BF16)


FFT1_TN = 4096


def _fft_stage1(u3, c1, s1):
    b, n1, width = u3.shape
    tn = min(FFT1_TN, width)
    blk = pl.BlockSpec((1, n1, tn), lambda i, j: (i, 0, j))
    mat = pl.BlockSpec((n1, n1), lambda i, j: (0, 0))
    return pl.pallas_call(
        _fft1_kernel,
        grid=(b, width // tn),
        in_specs=[mat, mat, blk],
        out_specs=[blk, blk],
        out_shape=[jax.ShapeDtypeStruct(u3.shape, BF16)] * 2,
        compiler_params=_cparams("parallel", "parallel"),
        name="fft_stage1",
    )(c1, s1, u3)


FFT2_ROWS = 1024


def _fft2_kernel(gc_ref, gs_ref, bc_ref, bs_ref, g_ref, ar_ref, ai_ref, z_ref):
    t1, n2 = ar_ref.shape[1], ar_ref.shape[2]
    xr, xi = [], []
    for j in range(t1):
        ar, ai = ar_ref[0, j], ai_ref[0, j]
        gc, gs = gc_ref[j], gs_ref[j]
        xr.append((jnp.dot(gc, ar, preferred_element_type=F32)
                   + jnp.dot(gs, ai, preferred_element_type=F32)).astype(BF16))
        xi.append((jnp.dot(gc, ai, preferred_element_type=F32)
                   - jnp.dot(gs, ar, preferred_element_type=F32)).astype(BF16))
    z = jnp.dot(jnp.concatenate(xr, axis=0), bc_ref[...], preferred_element_type=F32)
    z += jnp.dot(jnp.concatenate(xi, axis=0), bs_ref[...], preferred_element_type=F32)
    z = (z * lax.rsqrt(jnp.mean(z * z, axis=-1, keepdims=True) + RMS_EPS) * g_ref[...]).astype(BF16)
    for j in range(t1):
        z_ref[0, j] = z[j * n2:(j + 1) * n2]


def _fft_stage2(ar4, ai4, gc, gs, bc, bs, gain):
    b, n1, n2, w = ar4.shape
    t1 = min(max(FFT2_ROWS // n2, 1), n1)
    a_blk = pl.BlockSpec((1, t1, n2, w), lambda i, j: (i, j, 0, 0))
    g_blk = pl.BlockSpec((t1, n2, n2), lambda i, j: (j, 0, 0))
    c_blk = pl.BlockSpec((w, w), lambda i, j: (0, 0))
    return pl.pallas_call(
        _fft2_kernel,
        grid=(b, n1 // t1),
        in_specs=[g_blk, g_blk, c_blk, c_blk, pl.BlockSpec((1, w), lambda i, j: (0, 0)), a_blk, a_blk],
        out_specs=a_blk,
        out_shape=jax.ShapeDtypeStruct(ar4.shape, BF16),
        compiler_params=_cparams("parallel", "parallel"),
        name="fft_stage2",
    )(gc, gs, bc, bs, gain.reshape(1, -1), ar4, ai4)


def _fourier_mix(u2d, batch, seq_len, gain):
    n1, n2 = _fft_split(seq_len)
    c1, s1, gc, gs, bc, bs = _fft_tables(seq_len)
    u3 = u2d.reshape(batch, n1, n2 * FOURIER_WIDTH)
    ar, ai = _fft_stage1(u3, c1, s1)
    shape4 = (batch, n1, n2, FOURIER_WIDTH)
    zt = _fft_stage2(ar.reshape(shape4), ai.reshape(shape4), gc, gs, bc, bs, gain)
    return zt.reshape(batch, n1, n2 * FOURIER_WIDTH)


GLA_LB = 1024
GLA_HEADS_PER_STEP = 2
GLA_STAGGER = 6
CUMSUM_ROWS = 256


@functools.lru_cache(maxsize=None)
def _cumsum_matrices():
    r = np.arange(CUMSUM_ROWS)
    same_chunk = (r[:, None] // CHUNK) == (r[None, :] // CHUNK)
    prefix = same_chunk & (r[None, :] <= r[:, None])
    suffix = same_chunk & (r[None, :] >= r[:, None])
    return (jnp.asarray(prefix, dtype=F32).astype(BF16), jnp.asarray(suffix, dtype=F32).astype(BF16))


def _chunk_cumsum(x, tri):
    hi = x.astype(BF16)
    lo = (x - hi.astype(F32)).astype(BF16)
    width = x.shape[1]
    parts = []
    for r0 in range(0, x.shape[0], CUMSUM_ROWS):
        rows = slice(r0, r0 + CUMSUM_ROWS)
        both = jnp.dot(tri, jnp.concatenate([hi[rows], lo[rows]], axis=1), preferred_element_type=F32)
        parts.append(both[:, :width] + both[:, width:])
    return parts[0] if len(parts) == 1 else jnp.concatenate(parts, axis=0)


def _gla_direction(q_ref, v_ref, k_ref, lf_ref, tri, st_ref, o_ref, start, reverse):
    n = q_ref.shape[0]
    b = _chunk_cumsum(lf_ref[...], tri)
    yield
    mid = CHUNK // 2 if reverse else CHUNK // 2 - 1
    last = 0 if reverse else CHUNK - 1
    t_idx = lax.broadcasted_iota(jnp.int32, (CHUNK, CHUNK), 0)
    s_idx = lax.broadcasted_iota(jnp.int32, (CHUNK, CHUNK), 1)
    visible = (t_idx <= s_idx) if reverse else (t_idx >= s_idx)
    nchunks = n // CHUNK
    order = range(nchunks - 1, -1, -1) if reverse else range(nchunks)
    nt = (((1,), (1,)), ((), ()))
    tn = (((0,), (0,)), ((), ()))
    rows = [slice(c * CHUNK, (c + 1) * CHUNK) for c in range(nchunks)]
    qe, ke, e_mid, e_last, e_gap = [], [], [], [], []
    for c in range(nchunks):
        bc = b[rows[c]]
        b_mid = bc[mid:mid + 1]
        b_last = bc[last:last + 1]
        qe.append((q_ref[rows[c]].astype(F32) * jnp.exp(bc - b_mid)).astype(BF16))
        ke.append((k_ref[rows[c]].astype(F32) * jnp.exp(b_mid - bc)).astype(BF16))
        e_mid.append(jnp.exp(b_mid))
        e_last.append(jnp.exp(b_last))
        e_gap.append(jnp.exp(b_last - b_mid))
        yield
    scores, delta_t = [], []
    for c in range(nchunks):
        s = lax.dot_general(qe[c], ke[c], nt, preferred_element_type=F32)
        scores.append(jnp.where(visible, s, 0.0).astype(BF16))
        delta_t.append(lax.dot_general(v_ref[rows[c]], ke[c], tn, preferred_element_type=F32) * e_gap[c])
        yield
    st = st_ref[...]
    st_in = [None] * nchunks
    for c in order:
        st_in[c] = (st * e_mid[c]).astype(BF16)
        st = st * e_last[c] + delta_t[c]
    st_ref[...] = st
    yield
    for c in range(nchunks):
        o = jnp.dot(scores[c], v_ref[rows[c]], preferred_element_type=F32)
        o += lax.dot_general(qe[c], st_in[c], nt, preferred_element_type=F32)
        o_ref[pl.ds(start + c * CHUNK, CHUNK), :] += o
        yield


def _gla_kernel(trif_ref, trib_ref, qf_ref, vf_ref, kf_ref, lf_ref, qb_ref, vb_ref, kb_ref, lb_ref,
                o_ref, sf_ref, sb_ref):
    j = pl.program_id(2)
    nblk = pl.num_programs(2)
    lb_rows = qf_ref.shape[2]

    @pl.when(j == 0)
    def _():
        o_ref[...] = jnp.zeros(o_ref.shape, o_ref.dtype)
        sf_ref[...] = jnp.zeros(sf_ref.shape, sf_ref.dtype)
        sb_ref[...] = jnp.zeros(sb_ref.shape, sb_ref.dtype)

    start_f = pl.multiple_of(j * lb_rows, lb_rows)
    start_b = pl.multiple_of((nblk - 1 - j) * lb_rows, lb_rows)
    streams = []
    for hh in range(qf_ref.shape[0]):
        streams.append(_gla_direction(qf_ref.at[hh, 0], vf_ref.at[hh, 0], kf_ref.at[hh, 0], lf_ref.at[hh, 0],
                                      trif_ref[...], sf_ref.at[hh], o_ref.at[hh, 0], start_f, False))
        streams.append(_gla_direction(qb_ref.at[hh, 0], vb_ref.at[hh, 0], kb_ref.at[hh, 0], lb_ref.at[hh, 0],
                                      trib_ref[...], sb_ref.at[hh], o_ref.at[hh, 0], start_b, True))
    _interleave(streams, GLA_STAGGER)


def _gla(qv4, lf4, kk4):
    _, b, seq_len, _ = qv4.shape
    lbk = min(GLA_LB, seq_len)
    assert lbk % CUMSUM_ROWS == 0 and seq_len % lbk == 0
    nblk = seq_len // lbk
    nh = GLA_HEADS_PER_STEP
    blk = lambda head0, rev: pl.BlockSpec(
        (nh, 1, lbk, HEAD_DIM),
        (lambda i, h, j: (head0 + h, i, nblk - 1 - j, 0)) if rev else (lambda i, h, j: (head0 + h, i, j, 0)))
    tri_spec = pl.BlockSpec((CUMSUM_ROWS, CUMSUM_ROWS), lambda i, h, j: (0, 0))
    tri_f, tri_b = _cumsum_matrices()
    groups = HEADS // nh
    return pl.pallas_call(
        _gla_kernel,
        grid=(b, groups, nblk),
        in_specs=[tri_spec, tri_spec,
                  blk(0, False), blk(groups, False), blk(0, False), blk(0, False),
                  blk(0, True), blk(groups, True), blk(groups, True), blk(groups, True)],
        out_specs=pl.BlockSpec((nh, 1, seq_len, HEAD_DIM), lambda i, h, j: (h, i, 0, 0)),
        out_shape=jax.ShapeDtypeStruct((HEADS, b, seq_len, HEAD_DIM), F32),
        scratch_shapes=[pltpu.VMEM((nh, HEAD_DIM, HEAD_DIM), F32)] * 2,
        compiler_params=_cparams("parallel", "parallel", "arbitrary"),
        name="gla",
    )(tri_f, tri_b, qv4, qv4, kk4, lf4, qv4, qv4, kk4, lf4)


HALF_D = D_MODEL // 2


def _pack_bf16_pairs(x):
    bits = lax.bitcast_convert_type(x.astype(BF16).astype(F32), jnp.uint32)
    return bits[:, :HALF_D] | (bits[:, HALF_D:] >> 16)


def _unpack_bf16_pairs(words):
    hi = lax.bitcast_convert_type(words & jnp.uint32(0xFFFF0000), F32)
    lo = lax.bitcast_convert_type(words << 16, F32)
    return hi, lo


OUT_TM = 512


def _out_proj_kernel(*refs, nz):
    z_refs = refs[:nz]
    (o_ref, g_ref, xn_ref, ng_ref, wz_ref, wh_ref, l1g_ref, l1b_ref, rwh_ref, rwl_ref, rb_ref,
     h_ref, hpk_ref, route_ref, gate_ref, cnt_ref, base_ref) = refs[nz:]
    tm = o_ref.shape[1]

    z = jnp.concatenate([r[0] for r in z_refs], axis=0) if nz > 1 else z_refs[0][0]
    g = g_ref[...].astype(F32)
    normed = []
    for hd in range(HEADS):
        oh = o_ref[hd]
        normed.append(oh * lax.rsqrt(jnp.mean(oh * oh, axis=-1, keepdims=True) + RMS_EPS))
        yield
    hg = jnp.concatenate(normed, axis=1) * ng_ref[...] * (g * jax.nn.sigmoid(g))
    yield
    mixed = jnp.dot(z, wz_ref[...], preferred_element_type=F32)
    mixed += jnp.dot(hg.astype(BF16), wh_ref[...], preferred_element_type=F32)
    yield
    h = _layer_norm_rows(DEEPNORM_ALPHA * xn_ref[...] + mixed, l1g_ref[...], l1b_ref[...])
    h_ref[...] = h
    yield
    hpk_ref[...] = _pack_bf16_pairs(h)
    yield

    nt = (((1,), (1,)), ((), ()))
    h_hi = h.astype(BF16)
    h_lo = (h - h_hi.astype(F32)).astype(BF16)
    logits = lax.dot_general(rwh_ref[...], h_hi, nt, preferred_element_type=F32)
    logits += lax.dot_general(rwh_ref[...], h_lo, nt, preferred_element_type=F32)
    logits += lax.dot_general(rwl_ref[...], h_hi, nt, preferred_element_type=F32)
    logits += rb_ref[...]
    yield

    eid_f = lax.broadcasted_iota(jnp.int32, (N_EXPERTS, tm), 0).astype(F32)
    work = logits
    vals, idxs, hits = [], [], []
    for _ in range(TOP_K):
        m = jnp.max(work, axis=0, keepdims=True)
        idx = jnp.min(jnp.where(work == m, eid_f, float(N_EXPERTS)), axis=0, keepdims=True)
        hit = eid_f == idx
        work = jnp.where(hit, -jnp.inf, work)
        vals.append(m)
        idxs.append(idx)
        hits.append(hit)
        yield
    exps = [jnp.exp(v - vals[0]) for v in vals]
    denom = exps[0] + exps[1] + exps[2] + exps[3]

    member = jnp.zeros((N_EXPERTS, tm), F32)
    for hit in hits:
        member = member + jnp.where(hit, 1.0, 0.0)
    s_idx = lax.broadcasted_iota(jnp.int32, (tm, tm), 0)
    t_idx = lax.broadcasted_iota(jnp.int32, (tm, tm), 1)
    earlier = jnp.where(s_idx < t_idx, 1.0, 0.0).astype(BF16)
    base = base_ref[...]
    before = jnp.dot(member.astype(BF16), earlier, preferred_element_type=F32) + base
    yield

    row8 = lax.broadcasted_iota(jnp.int32, (2 * TOP_K, tm), 0)
    route = jnp.zeros((2 * TOP_K, tm), F32)
    row128 = lax.broadcasted_iota(jnp.int32, (LANES, tm), 0)
    gate_t = jnp.zeros((LANES, tm), F32)
    for k in range(TOP_K):
        rank = jnp.sum(jnp.where(hits[k], before, 0.0), axis=0, keepdims=True)
        route = route + jnp.where(row8 == k, idxs[k], 0.0) + jnp.where(row8 == TOP_K + k, rank, 0.0)
        gate_t = gate_t + jnp.where(row128 == k, exps[k] / denom, 0.0)
    route_ref[...] = route.astype(jnp.int32)
    gate_ref[...] = gate_t.T
    total = base + jnp.sum(member, axis=1, keepdims=True)
    base_ref[...] = total
    cnt_ref[...] = total


def _out_proj_stage(zt, o3, g2d, xn2d, seq_len, norm_g6, w_out_bf16, ln_g, ln_b, rw_hi, rw_lo, rb_pad):
    t = o3.shape[1]
    n1 = zt.shape[1]
    tm = max(min(OUT_TM, seq_len), n1)
    nz = tm // n1
    row = lambda i: (i, 0)
    const = lambda i: (0, 0)

    def z_spec(k):
        return pl.BlockSpec((1, n1, FOURIER_WIDTH),
                            lambda i: ((i * tm) // seq_len, 0, ((i * tm) % seq_len) // n1 + k))

    in_specs = [z_spec(k) for k in range(nz)] + [
        pl.BlockSpec((HEADS, tm, HEAD_DIM), lambda i: (0, i, 0)),
        pl.BlockSpec((tm, HGRN_WIDTH), row),
        pl.BlockSpec((tm, D_MODEL), row),
        _resident((1, HGRN_WIDTH)),
        _resident((FOURIER_WIDTH, D_MODEL)),
        _resident((HGRN_WIDTH, D_MODEL)),
        _resident((1, D_MODEL)),
        _resident((1, D_MODEL)),
        _resident((N_EXPERTS, D_MODEL)),
        _resident((N_EXPERTS, D_MODEL)),
        _resident((N_EXPERTS, 1)),
    ]
    out_specs = [
        pl.BlockSpec((tm, D_MODEL), row),
        pl.BlockSpec((tm, HALF_D), row),
        pl.BlockSpec((2 * TOP_K, tm), lambda i: (0, i)),
        pl.BlockSpec((tm, LANES), row),
        pl.BlockSpec((N_EXPERTS, 1), const),
    ]
    out_shape = [
        jax.ShapeDtypeStruct((t, D_MODEL), F32),
        jax.ShapeDtypeStruct((t, HALF_D), jnp.uint32),
        jax.ShapeDtypeStruct((2 * TOP_K, t), jnp.int32),
        jax.ShapeDtypeStruct((t, LANES), F32),
        jax.ShapeDtypeStruct((N_EXPERTS, 1), F32),
    ]
    return _Stage(
        body=functools.partial(_out_proj_kernel, nz=nz),
        init=_zero_ref,
        grid=(t // tm,),
        in_specs=in_specs,
        out_specs=out_specs,
        out_shape=out_shape,
        scratch=[pltpu.VMEM((N_EXPERTS, 1), F32)],
        args=[zt] * nz + [o3, g2d, xn2d, norm_g6, w_out_bf16[:FOURIER_WIDTH], w_out_bf16[FOURIER_WIDTH:],
                          ln_g.reshape(1, -1), ln_b.reshape(1, -1), rw_hi, rw_lo, rb_pad],
        sequential=True,
        name="out_proj",
    )


COMBINE_TM = 512
COMBINE_PARTS = 2
SC_WINDOW = 128


def _sc_mesh():
    return plsc.VectorSubcoreMesh(core_axis_name="core", subcore_axis_name="subcore")


def _dispatch(dest_kt, hpk, n_slots):
    t, d = hpk.shape
    rows = pl.BlockSpec((SC_WINDOW, d), index_map=lambda i: (i, 0), pipeline_mode=pl.Buffered(1))
    idx = pl.BlockSpec((1, SC_WINDOW), index_map=lambda i: (0, i))

    @pl.kernel(out_type=jax.ShapeDtypeStruct((n_slots, d), hpk.dtype), mesh=_sc_mesh(), name="moe_dispatch_sc")
    def scatter(x_hbm, i0_hbm, i1_hbm, i2_hbm, i3_hbm, o_hbm):
        def body(x_vmem, *idx_vmem):
            for iv in idx_vmem:
                pltpu.sync_copy(x_vmem, o_hbm.at[iv.at[0]])

        pltpu.emit_pipeline(
            body, grid=(t // SC_WINDOW,), in_specs=[rows] + [idx] * TOP_K, out_specs=[],
            core_axis_name=("core", "subcore"), dimension_semantics=(pltpu.PARALLEL,),
        )(x_hbm, i0_hbm, i1_hbm, i2_hbm, i3_hbm)

    return scatter(hpk, *[dest_kt[k].reshape(1, t) for k in range(TOP_K)])


def _gather_rows(yb, dest_kt):
    _, t = dest_kt.shape
    d = yb.shape[1]
    n = TOP_K * t

    @pl.kernel(out_type=jax.ShapeDtypeStruct((n, d), yb.dtype), mesh=_sc_mesh(), name="moe_gather_sc")
    def gather(y_hbm, i_hbm, o_hbm):
        def body(i_vmem, o_vmem):
            pltpu.sync_copy(y_hbm.at[i_vmem.at[0]], o_vmem)

        pltpu.emit_pipeline(
            body, grid=(n // SC_WINDOW,),
            in_specs=[pl.BlockSpec((1, SC_WINDOW), index_map=lambda i: (0, i))],
            out_specs=[pl.BlockSpec((SC_WINDOW, d), index_map=lambda i: (i, 0), pipeline_mode=pl.Buffered(1))],
            core_axis_name=("core", "subcore"), dimension_semantics=(pltpu.PARALLEL,),
        )(i_hbm, o_hbm)

    return gather(yb, dest_kt.reshape(1, n)).reshape(TOP_K, t, d)


def _experts_kernel(be_ref, nused_ref, next_ref, halves_ref, xb_ref, wgu_hbm, bgu_ref, wdn_hbm, bdn_ref, yb_ref,
                    wgu_f32, wdn_f32, wgu_bf, wdn_bf, slot_ref, sem):
    i = pl.program_id(0)
    used = i < nused_ref[0]
    e = be_ref[i]

    def weight_copies(expert, slot):
        return (pltpu.make_async_copy(wgu_hbm.at[expert], wgu_f32.at[slot], sem.at[slot, 0]),
                pltpu.make_async_copy(wdn_hbm.at[expert], wdn_f32.at[slot], sem.at[slot, 1]))

    @pl.when(used & (i == 0))
    def _():
        slot_ref[0] = 0
        for c in weight_copies(e, 0):
            c.start()

    @pl.when(used & (i > 0) & (e != be_ref[jnp.maximum(i - 1, 0)]))
    def _():
        slot_ref[0] = 1 - slot_ref[0]

    @pl.when(used & ((i == 0) | (e != be_ref[jnp.maximum(i - 1, 0)])))
    def _():
        slot = slot_ref[0]
        for c in weight_copies(e, slot):
            c.wait()
        wgu_bf[...] = wgu_f32[slot].astype(BF16)
        wdn_bf[...] = wdn_f32[slot].astype(BF16)
        nxt = next_ref[e]

        @pl.when(nxt != e)
        def _():
            for c in weight_copies(nxt, 1 - slot):
                c.start()

    def mlp(rows):
        x_hi, x_lo = _unpack_bf16_pairs(xb_ref[:rows])
        x = jnp.concatenate([x_hi.astype(BF16), x_lo.astype(BF16)], axis=1)
        gu = jnp.dot(x, wgu_bf[...], preferred_element_type=F32) + bgu_ref[0]
        gate = jnp.minimum(gu[:, :D_FF], SWIGLU_LIMIT)
        up = jnp.clip(gu[:, D_FF:], -SWIGLU_LIMIT, SWIGLU_LIMIT)
        act = (up + 1.0) * gate * jax.nn.sigmoid(SWIGLU_ALPHA * gate)
        y = jnp.dot(act.astype(BF16), wdn_bf[...], preferred_element_type=F32) + bdn_ref[0]
        yb_ref[:rows] = _pack_bf16_pairs(y)

    @pl.when(used & (halves_ref[i] == 2))
    def _():
        mlp(MOE_BLOCK)

    @pl.when(used & (halves_ref[i] != 2))
    def _():
        mlp(MOE_BLOCK // 2)


def _experts(block_expert, n_used, next_expert, halves, xb, wgu, bgu, wdn, bdn):
    n_slots = xb.shape[0]
    nb = n_slots // MOE_BLOCK
    blk = pl.BlockSpec((MOE_BLOCK, HALF_D), lambda i, be, nu, nx, hv: (i, 0))
    per_expert = lambda shape: pl.BlockSpec((1,) + shape, lambda i, be, nu, nx, hv: (be[i], 0, 0))
    hbm = pl.BlockSpec(memory_space=pl.ANY)
    grid_spec = pltpu.PrefetchScalarGridSpec(
        num_scalar_prefetch=4,
        grid=(nb,),
        in_specs=[blk, hbm, per_expert((1, 2 * D_FF)), hbm, per_expert((1, D_MODEL))],
        out_specs=blk,
        scratch_shapes=[pltpu.VMEM((2, D_MODEL, 2 * D_FF), F32), pltpu.VMEM((2, D_FF, D_MODEL), F32),
                        pltpu.VMEM((D_MODEL, 2 * D_FF), BF16), pltpu.VMEM((D_FF, D_MODEL), BF16),
                        pltpu.SMEM((1,), jnp.int32), pltpu.SemaphoreType.DMA((2, 2))],
    )
    return pl.pallas_call(
        _experts_kernel,
        grid_spec=grid_spec,
        out_shape=jax.ShapeDtypeStruct((n_slots, HALF_D), jnp.uint32),
        compiler_params=_cparams("arbitrary"),
        name="moe_experts",
    )(block_expert, n_used, next_expert, halves, xb, wgu, bgu, wdn, bdn)


def _combine_kernel(rows_ref, gate_ref, h_ref, g_ref, b_ref, *rest):
    out_ref = rest[-1]
    tm = h_ref.shape[0]
    gate = gate_ref[...]
    ff_hi = jnp.zeros((tm, HALF_D), F32)
    ff_lo = jnp.zeros((tm, HALF_D), F32)
    for k in range(TOP_K):
        hi, lo = _unpack_bf16_pairs(rows_ref[k])
        gk = gate[:, k:k + 1]
        ff_hi = ff_hi + gk * hi
        ff_lo = ff_lo + gk * lo
    ff = jnp.concatenate([ff_hi, ff_lo], axis=1)
    out_ref[...] = _layer_norm_rows(DEEPNORM_ALPHA * h_ref[...] + ff, g_ref[...], b_ref[...])


def _combine(rows_kt, gates, h2d, ln_g, ln_b, token0, prev_out):
    t = h2d.shape[0]
    t_part = rows_kt.shape[1]
    tm = min(COMBINE_TM, t_part)
    first = token0 // tm
    row = lambda i: (first + i, 0)
    const = lambda i: (0, 0)
    in_specs = [
        pl.BlockSpec((TOP_K, tm, HALF_D), lambda i: (0, i, 0)),
        pl.BlockSpec((tm, LANES), row),
        pl.BlockSpec((tm, D_MODEL), row),
        pl.BlockSpec((1, D_MODEL), const),
        pl.BlockSpec((1, D_MODEL), const),
    ]
    args = [rows_kt, gates, h2d, ln_g.reshape(1, -1), ln_b.reshape(1, -1)]
    aliases = {}
    if prev_out is not None:
        in_specs.append(pl.BlockSpec(memory_space=pl.ANY))
        args.append(prev_out)
        aliases = {len(args) - 1: 0}
    return pl.pallas_call(
        _combine_kernel,
        grid=(t_part // tm,),
        in_specs=in_specs,
        out_specs=pl.BlockSpec((tm, D_MODEL), row),
        out_shape=jax.ShapeDtypeStruct((t, D_MODEL), F32),
        input_output_aliases=aliases,
        compiler_params=_cparams("parallel"),
        name="moe_combine",
    )(*args)


def _moe_plan(route, counts_f32, n_tokens):
    eid = route[:TOP_K]
    rank = route[TOP_K:]
    counts = counts_f32[:, 0].astype(jnp.int32)
    padded = ((counts + MOE_BLOCK - 1) // MOE_BLOCK) * MOE_BLOCK
    pend = jnp.cumsum(padded)
    pstart = pend - padded
    experts = jnp.arange(N_EXPERTS, dtype=jnp.int32)[:, None, None]
    dest = rank + jnp.sum(jnp.where(eid[None] == experts, pstart[:, None, None], 0), axis=0)
    n_assign = n_tokens * TOP_K
    n_slots = ((n_assign + MOE_BLOCK - 1) // MOE_BLOCK) * MOE_BLOCK + N_EXPERTS * MOE_BLOCK
    nb = n_slots // MOE_BLOCK
    block_start = jnp.arange(nb, dtype=jnp.int32) * MOE_BLOCK
    block_expert = jnp.sum((pend[None, :] <= block_start[:, None]).astype(jnp.int32), axis=1)
    block_expert = jnp.minimum(block_expert, N_EXPERTS - 1).astype(jnp.int32)
    n_used = (pend[-1:] // MOE_BLOCK).astype(jnp.int32)
    onehot = (block_expert[:, None] == jnp.arange(N_EXPERTS, dtype=jnp.int32)[None, :]).astype(jnp.int32)
    valid_rows = jnp.sum(onehot * (pstart + counts)[None, :], axis=1) - block_start
    halves = jnp.where(valid_rows > MOE_BLOCK // 2, 2, 1).astype(jnp.int32)
    ids = jnp.arange(N_EXPERTS, dtype=jnp.int32)
    later_nonempty = (ids[None, :] > ids[:, None]) & (counts[None, :] > 0)
    next_expert = jnp.min(jnp.where(later_nonempty, ids[None, :], N_EXPERTS), axis=1)
    next_expert = jnp.where(next_expert == N_EXPERTS, ids, next_expert).astype(jnp.int32)
    return dest.astype(jnp.int32), block_expert, n_used, next_expert, halves, n_slots


def _in_stage(x, p):
    t = x.shape[0] * x.shape[1]
    return _in_proj_stage(x.reshape(t, D_MODEL), p["ln_in_g"], p["ln_in_b"], p["lb2"], p["w_in"])


def _mix_and_out_stage(x, in_outs, p):
    batch, seq_len, _ = x.shape
    t = batch * seq_len
    xn, u, qv, lf, kk, og = in_outs
    zt = _fourier_mix(u, batch, seq_len, p["fourier_norm_g"])
    per_seq = lambda a: a.reshape(a.shape[0], batch, seq_len, HEAD_DIM)
    o = _gla(per_seq(qv), per_seq(lf), per_seq(kk))
    return _out_proj_stage(
        zt, o.reshape(HEADS, t, HEAD_DIM), og, xn, seq_len, p["norm_g6"], p["w_out"], p["ln1_g"], p["ln1_b"],
        p["rw_hi"], p["rw_lo"], p["rb_pad"])


def _moe(x, out_outs, p):
    batch, seq_len, _ = x.shape
    t = batch * seq_len
    h, hpk, route, gates, counts = out_outs
    dest_kt, block_expert, n_used, next_expert, halves, n_slots = _moe_plan(route, counts, t)
    xb = _dispatch(dest_kt, hpk, n_slots)
    yb = _experts(block_expert, n_used, next_expert, halves, xb, p["w_gu"], p["b_gu"], p["w_dn"], p["b_dn"])
    y = None
    part = t // COMBINE_PARTS
    for j in range(COMBINE_PARTS):
        rows = _gather_rows(yb, dest_kt[:, j * part:(j + 1) * part])
        y = _combine(rows, gates, h, p["ln2_g"], p["ln2_b"], j * part, y)
    return y.reshape(batch, seq_len, D_MODEL)


def _prepare_params(ln_in_g, ln_in_b, w_in, fourier_norm_g, lb_gamma, hgrn_norm_g, w_out, ln1_g, ln1_b,
                    router_w, router_b, w_gate_up, b_gate_up, w_down, b_down, ln2_g, ln2_b):
    fw, hw = FOURIER_WIDTH, HGRN_WIDTH
    w0 = w_in[0]
    lb_all = jnp.cumsum(jax.nn.softmax(lb_gamma.astype(F32), axis=1), axis=1)
    rw = router_w[0].astype(F32).T
    rw_hi = rw.astype(BF16)
    rw_lo = (rw - rw_hi.astype(F32)).astype(BF16)
    del fw, hw
    return dict(
        ln_in_g=ln_in_g, ln_in_b=ln_in_b, w_in=w0.astype(BF16),
        fourier_norm_g=fourier_norm_g[0],
        lb2=jnp.concatenate([lb_all[0, 0], lb_all[1, 0]]).reshape(1, -1),
        norm_g6=jnp.tile(hgrn_norm_g[0].astype(F32), HEADS).reshape(1, -1),
        w_out=w_out[0].astype(BF16), ln1_g=ln1_g[0], ln1_b=ln1_b[0],
        rw_hi=rw_hi, rw_lo=rw_lo, rb_pad=router_b[0].astype(F32).reshape(-1, 1),
        w_gu=w_gate_up[0], b_gu=b_gate_up[0].reshape(N_EXPERTS, 1, -1),
        w_dn=w_down[0], b_dn=b_down[0].reshape(N_EXPERTS, 1, -1),
        ln2_g=ln2_g[0], ln2_b=ln2_b[0],
    )


def kernel(x_prompt, x_sample, ln_in_g, ln_in_b, w_in, fourier_norm_g, lb_gamma, hgrn_norm_g, w_out,
           ln1_g, ln1_b, router_w, router_b, w_gate_up, b_gate_up, w_down, b_down, ln2_g, ln2_b):
    p = _prepare_params(ln_in_g, ln_in_b, w_in, fourier_norm_g, lb_gamma, hgrn_norm_g, w_out, ln1_g, ln1_b,
                        router_w, router_b, w_gate_up, b_gate_up, w_down, b_down, ln2_g, ln2_b)
    first, second = x_sample, x_prompt
    (in_first,) = _run_stages([_in_stage(first, p)])
    out_stage_first = _mix_and_out_stage(first, in_first, p)
    in_stage_second = _in_stage(second, p)
    if in_stage_second.grid == out_stage_first.grid:
        in_second, out_first = _run_stages([in_stage_second, out_stage_first])
    else:
        (in_second,) = _run_stages([in_stage_second])
        (out_first,) = _run_stages([out_stage_first])
    (out_second,) = _run_stages([_mix_and_out_stage(second, in_second, p)])
    y_first = _moe(first, out_first, p)
    y_second = _moe(second, out_second, p)
    return (y_second, y_first)
```

```python
import functools
import math
from typing import Callable, NamedTuple

import numpy as np
import jax
import jax.numpy as jnp
from jax import lax
from jax.experimental import pallas as pl
from jax.experimental.pallas import tpu as pltpu
from jax.experimental.pallas import tpu_sc as plsc

D_MODEL = 1024
FOURIER_WIDTH = 256
FOURIER_GROUP_DIM = 64
HGRN_WIDTH = 768
HEAD_DIM = 128
HEADS = 6
CHUNK = 64
N_EXPERTS = 32
TOP_K = 4
D_FF = 1024
SWIGLU_LIMIT = 7.0
SWIGLU_ALPHA = 1.702
MOE_BLOCK = 1024
LN_EPS = 1e-5
RMS_EPS = 1e-6
DEEPNORM_ALPHA = 2.0 ** 0.25

LANES = 128
VMEM_LIMIT_BYTES = 56 * 1024 * 1024

F32 = jnp.float32
BF16 = jnp.bfloat16


def _cparams(*sem):
    return pltpu.CompilerParams(dimension_semantics=sem, vmem_limit_bytes=VMEM_LIMIT_BYTES)


def _layer_norm_rows(x, g, b):
    mu = jnp.mean(x, axis=-1, keepdims=True)
    xc = x - mu
    var = jnp.mean(xc * xc, axis=-1, keepdims=True)
    return xc * lax.rsqrt(var + LN_EPS) * g + b


IN_TM = 512
IN_TN = 512


def _in_proj_kernel(x_ref, g_ref, b_ref, lb_ref, w_ref, xn_ref, u_ref, qv_ref, lf_ref, kk_ref, og_ref):
    xn = _layer_norm_rows(x_ref[...], g_ref[...], b_ref[...])
    xn_ref[...] = xn
    xb = xn.astype(BF16)
    hw = HGRN_WIDTH
    yield

    def store_heads(ref, first_head, c0, val):
        for j in range(val.shape[1] // HEAD_DIM):
            ref[first_head + c0 // HEAD_DIM + j] = val[:, j * HEAD_DIM:(j + 1) * HEAD_DIM]

    def chunks(col, width):
        for c0 in range(0, width, IN_TN):
            cw = min(IN_TN, width - c0)
            yield c0, cw, jnp.dot(xb, w_ref[:, col + c0:col + c0 + cw], preferred_element_type=F32)

    for c0, cw, acc in chunks(0, FOURIER_WIDTH):
        u_ref[:, c0:c0 + cw] = acc.astype(BF16)
        yield
    for c0, cw, acc in chunks(FOURIER_WIDTH, hw):
        store_heads(qv_ref, 0, c0, (acc * jax.nn.sigmoid(acc) * (HEAD_DIM ** -0.5)).astype(BF16))
        yield
    for c0, cw, acc in chunks(FOURIER_WIDTH + hw, hw):
        store_heads(qv_ref, HEADS, c0, acc.astype(BF16))
        yield
    for c0, cw, acc in chunks(FOURIER_WIDTH + 2 * hw, 2 * hw):
        lb = lb_ref[:, c0:c0 + cw]
        fg = lb + (1.0 - lb) * jax.nn.sigmoid(acc)
        store_heads(lf_ref, 0, c0, jnp.log(fg))
        store_heads(kk_ref, 0, c0, (1.0 - fg).astype(BF16))
        yield
    for c0, cw, acc in chunks(FOURIER_WIDTH + 4 * hw, hw):
        og_ref[:, c0:c0 + cw] = acc.astype(BF16)
        yield


class _Stage(NamedTuple):
    body: Callable
    init: Callable | None
    grid: tuple
    in_specs: list
    out_specs: list
    out_shape: list
    scratch: list
    args: list
    sequential: bool
    name: str


_DONE = object()


def _interleave(pieces, stagger):
    done = [False] * len(pieces)
    t = 0
    while not all(done):
        for i, g in enumerate(pieces):
            if not done[i] and t >= i * stagger:
                done[i] = next(g, _DONE) is _DONE
        t += 1


def _run_stages(stages):
    assert all(s.grid == stages[0].grid for s in stages)
    n_in = [len(s.in_specs) for s in stages]
    n_out = [len(s.out_specs) for s in stages]
    n_scr = [len(s.scratch) for s in stages]

    def body(*refs):
        scr = sum(n_in) + sum(n_out)
        for s, c in zip(stages, n_scr):
            if s.init is not None:
                pl.when(pl.program_id(0) == 0)(functools.partial(s.init, *refs[scr:scr + c]))
            scr += c
        ins, outs, scr = 0, sum(n_in), sum(n_in) + sum(n_out)
        bodies = []
        for s, a, b, c in zip(stages, n_in, n_out, n_scr):
            bodies.append(s.body(*refs[ins:ins + a], *refs[outs:outs + b], *refs[scr:scr + c]))
            ins, outs, scr = ins + a, outs + b, scr + c
        _interleave(bodies, 0)

    results = pl.pallas_call(
        body,
        grid=stages[0].grid,
        in_specs=[sp for s in stages for sp in s.in_specs],
        out_specs=[sp for s in stages for sp in s.out_specs],
        out_shape=[sh for s in stages for sh in s.out_shape],
        scratch_shapes=[sc for s in stages for sc in s.scratch],
        compiler_params=_cparams("arbitrary" if any(s.sequential for s in stages) else "parallel"),
        name="_".join(s.name for s in stages),
    )(*[a for s in stages for a in s.args])
    split, pos = [], 0
    for b in n_out:
        split.append(results[pos:pos + b])
        pos += b
    return split


def _zero_ref(ref):
    ref[...] = jnp.zeros(ref.shape, ref.dtype)


def _resident(shape):
    return pl.BlockSpec(shape, lambda i: (0,) * len(shape), pipeline_mode=pl.Buffered(1))


def _in_proj_stage(x2d, ln_g, ln_b, lb2, w_bf16):
    t = x2d.shape[0]
    tm = min(IN_TM, t)
    row = lambda i: (i, 0)
    flat = lambda w, dt: (pl.BlockSpec((tm, w), row), jax.ShapeDtypeStruct((t, w), dt))
    head_major = lambda nh, dt: (pl.BlockSpec((nh, tm, HEAD_DIM), lambda i: (0, i, 0)),
                                 jax.ShapeDtypeStruct((nh, t, HEAD_DIM), dt))
    outs = [flat(D_MODEL, F32), flat(FOURIER_WIDTH, BF16), head_major(2 * HEADS, BF16),
            head_major(2 * HEADS, F32), head_major(2 * HEADS, BF16), flat(HGRN_WIDTH, BF16)]
    return _Stage(
        body=_in_proj_kernel,
        init=None,
        grid=(t // tm,),
        in_specs=[pl.BlockSpec((tm, D_MODEL), row), _resident((1, D_MODEL)), _resident((1, D_MODEL)),
                  _resident((1, 2 * HGRN_WIDTH)), _resident(w_bf16.shape)],
        out_specs=[o[0] for o in outs],
        out_shape=[o[1] for o in outs],
        scratch=[],
        args=[x2d, ln_g.reshape(1, -1), ln_b.reshape(1, -1), lb2, w_bf16],
        sequential=False,
        name="in_proj",
    )


def _fft_split(seq_len):
    n1 = 1 << ((seq_len.bit_length() - 1 + 1) // 2)
    return n1, seq_len // n1


@functools.lru_cache(maxsize=None)
def _fft_tables(seq_len):
    n1, n2 = _fft_split(seq_len)
    k1 = np.arange(n1)
    ang1 = 2.0 * np.pi * ((k1[:, None] * k1[None, :]) % n1) / n1
    s1 = 1.0 / math.sqrt(n1)
    c1, s1m = np.cos(ang1) * s1, np.sin(ang1) * s1
    l1p = np.arange(n1)[:, None, None]
    l2p = np.arange(n2)[None, :, None]
    l2 = np.arange(n2)[None, None, :]
    ang2 = 2.0 * np.pi * ((l2 * (l1p + n1 * l2p)) % seq_len) / seq_len
    s2 = 1.0 / math.sqrt(n2)
    gc, gs = np.cos(ang2) * s2, np.sin(ang2) * s2
    kc = np.arange(FOURIER_GROUP_DIM)
    angc = 2.0 * np.pi * ((kc[:, None] * kc[None, :]) % FOURIER_GROUP_DIM) / FOURIER_GROUP_DIM
    sc = 1.0 / math.sqrt(FOURIER_GROUP_DIM)
    groups = FOURIER_WIDTH // FOURIER_GROUP_DIM
    bc = np.kron(np.eye(groups), np.cos(angc) * sc)
    bs = np.kron(np.eye(groups), np.sin(angc) * sc)
    as_bf16 = lambda a: jnp.asarray(a, dtype=F32).astype(BF16)
    return tuple(as_bf16(a) for a in (c1, s1m, gc, gs, bc, bs))


def _fft1_kernel(c_ref, s_ref, u_ref, ar_ref, ai_ref):
    u = u_ref[0]
    ar_ref[0] = jnp.dot(c_ref[...], u, preferred_element_type=F32).astype(BF16)
    ai_ref[0] = (-jnp.dot(s_ref[...], u, preferred_element_type=F32)).astype(BF16)


FFT1_TN = 4096


def _fft_stage1(u3, c1, s1):
    b, n1, width = u3.shape
    tn = min(FFT1_TN, width)
    blk = pl.BlockSpec((1, n1, tn), lambda i, j: (i, 0, j))
    mat = pl.BlockSpec((n1, n1), lambda i, j: (0, 0))
    return pl.pallas_call(
        _fft1_kernel,
        grid=(b, width // tn),
        in_specs=[mat, mat, blk],
        out_specs=[blk, blk],
        out_shape=[jax.ShapeDtypeStruct(u3.shape, BF16)] * 2,
        compiler_params=_cparams("parallel", "parallel"),
        name="fft_stage1",
    )(c1, s1, u3)


FFT2_ROWS = 1024


def _fft2_kernel(gc_ref, gs_ref, bc_ref, bs_ref, g_ref, ar_ref, ai_ref, z_ref):
    t1, n2 = ar_ref.shape[1], ar_ref.shape[2]
    xr, xi = [], []
    for j in range(t1):
        ar, ai = ar_ref[0, j], ai_ref[0, j]
        gc, gs = gc_ref[j], gs_ref[j]
        xr.append((jnp.dot(gc, ar, preferred_element_type=F32)
                   + jnp.dot(gs, ai, preferred_element_type=F32)).astype(BF16))
        xi.append((jnp.dot(gc, ai, preferred_element_type=F32)
                   - jnp.dot(gs, ar, preferred_element_type=F32)).astype(BF16))
    z = jnp.dot(jnp.concatenate(xr, axis=0), bc_ref[...], preferred_element_type=F32)
    z += jnp.dot(jnp.concatenate(xi, axis=0), bs_ref[...], preferred_element_type=F32)
    z = (z * lax.rsqrt(jnp.mean(z * z, axis=-1, keepdims=True) + RMS_EPS) * g_ref[...]).astype(BF16)
    for j in range(t1):
        z_ref[0, j] = z[j * n2:(j + 1) * n2]


def _fft_stage2(ar4, ai4, gc, gs, bc, bs, gain):
    b, n1, n2, w = ar4.shape
    t1 = min(max(FFT2_ROWS // n2, 1), n1)
    a_blk = pl.BlockSpec((1, t1, n2, w), lambda i, j: (i, j, 0, 0))
    g_blk = pl.BlockSpec((t1, n2, n2), lambda i, j: (j, 0, 0))
    c_blk = pl.BlockSpec((w, w), lambda i, j: (0, 0))
    return pl.pallas_call(
        _fft2_kernel,
        grid=(b, n1 // t1),
        in_specs=[g_blk, g_blk, c_blk, c_blk, pl.BlockSpec((1, w), lambda i, j: (0, 0)), a_blk, a_blk],
        out_specs=a_blk,
        out_shape=jax.ShapeDtypeStruct(ar4.shape, BF16),
        compiler_params=_cparams("parallel", "parallel"),
        name="fft_stage2",
    )(gc, gs, bc, bs, gain.reshape(1, -1), ar4, ai4)


def _fourier_mix(u2d, batch, seq_len, gain):
    n1, n2 = _fft_split(seq_len)
    c1, s1, gc, gs, bc, bs = _fft_tables(seq_len)
    u3 = u2d.reshape(batch, n1, n2 * FOURIER_WIDTH)
    ar, ai = _fft_stage1(u3, c1, s1)
    shape4 = (batch, n1, n2, FOURIER_WIDTH)
    zt = _fft_stage2(ar.reshape(shape4), ai.reshape(shape4), gc, gs, bc, bs, gain)
    return zt.reshape(batch, n1, n2 * FOURIER_WIDTH)


GLA_LB = 1024
GLA_OUT_BLOCK_BYTES = 32 * 1024 * 1024
GLA_STAGGER = 6
CUMSUM_ROWS = 256


@functools.lru_cache(maxsize=None)
def _cumsum_matrices():
    r = np.arange(CUMSUM_ROWS)
    same_chunk = (r[:, None] // CHUNK) == (r[None, :] // CHUNK)
    prefix = same_chunk & (r[None, :] <= r[:, None])
    suffix = same_chunk & (r[None, :] >= r[:, None])
    return (jnp.asarray(prefix, dtype=F32).astype(BF16), jnp.asarray(suffix, dtype=F32).astype(BF16))


def _chunk_cumsum(x, tri):
    hi = x.astype(BF16)
    lo = (x - hi.astype(F32)).astype(BF16)
    width = x.shape[1]
    parts = []
    for r0 in range(0, x.shape[0], CUMSUM_ROWS):
        rows = slice(r0, r0 + CUMSUM_ROWS)
        both = jnp.dot(tri, jnp.concatenate([hi[rows], lo[rows]], axis=1), preferred_element_type=F32)
        parts.append(both[:, :width] + both[:, width:])
    return parts[0] if len(parts) == 1 else jnp.concatenate(parts, axis=0)


def _gla_direction(q_ref, v_ref, k_ref, lf_ref, tri, st_ref, o_ref, start, reverse):
    n = q_ref.shape[0]
    b = _chunk_cumsum(lf_ref[...], tri)
    yield
    mid = CHUNK // 2 if reverse else CHUNK // 2 - 1
    last = 0 if reverse else CHUNK - 1
    t_idx = lax.broadcasted_iota(jnp.int32, (CHUNK, CHUNK), 0)
    s_idx = lax.broadcasted_iota(jnp.int32, (CHUNK, CHUNK), 1)
    visible = (t_idx <= s_idx) if reverse else (t_idx >= s_idx)
    nchunks = n // CHUNK
    order = range(nchunks - 1, -1, -1) if reverse else range(nchunks)
    nt = (((1,), (1,)), ((), ()))
    tn = (((0,), (0,)), ((), ()))
    rows = [slice(c * CHUNK, (c + 1) * CHUNK) for c in range(nchunks)]
    qe, ke, e_mid, e_last, e_gap = [], [], [], [], []
    for c in range(nchunks):
        bc = b[rows[c]]
        b_mid = bc[mid:mid + 1]
        b_last = bc[last:last + 1]
        qe.append((q_ref[rows[c]].astype(F32) * jnp.exp(bc - b_mid)).astype(BF16))
        ke.append((k_ref[rows[c]].astype(F32) * jnp.exp(b_mid - bc)).astype(BF16))
        e_mid.append(jnp.exp(b_mid))
        e_last.append(jnp.exp(b_last))
        e_gap.append(jnp.exp(b_last - b_mid))
        yield
    scores, delta_t = [], []
    for c in range(nchunks):
        s = lax.dot_general(qe[c], ke[c], nt, preferred_element_type=F32)
        scores.append(jnp.where(visible, s, 0.0).astype(BF16))
        delta_t.append(lax.dot_general(v_ref[rows[c]], ke[c], tn, preferred_element_type=F32) * e_gap[c])
        yield
    st = st_ref[...]
    st_in = [None] * nchunks
    for c in order:
        st_in[c] = (st * e_mid[c]).astype(BF16)
        st = st * e_last[c] + delta_t[c]
    st_ref[...] = st
    yield
    for c in range(nchunks):
        o = jnp.dot(scores[c], v_ref[rows[c]], preferred_element_type=F32)
        o += lax.dot_general(qe[c], st_in[c], nt, preferred_element_type=F32)
        o_ref[pl.ds(start + c * CHUNK, CHUNK), :] += o
        yield


def _gla_kernel(trif_ref, trib_ref, qf_ref, vf_ref, kf_ref, lf_ref, qb_ref, vb_ref, kb_ref, lb_ref,
                o_ref, sf_ref, sb_ref):
    j = pl.program_id(2)
    nblk = pl.num_programs(2)
    lb_rows = qf_ref.shape[2]

    @pl.when(j == 0)
    def _():
        o_ref[...] = jnp.zeros(o_ref.shape, o_ref.dtype)
        sf_ref[...] = jnp.zeros(sf_ref.shape, sf_ref.dtype)
        sb_ref[...] = jnp.zeros(sb_ref.shape, sb_ref.dtype)

    start_f = pl.multiple_of(j * lb_rows, lb_rows)
    start_b = pl.multiple_of((nblk - 1 - j) * lb_rows, lb_rows)
    streams = []
    for hh in range(qf_ref.shape[0]):
        streams.append(_gla_direction(qf_ref.at[hh, 0], vf_ref.at[hh, 0], kf_ref.at[hh, 0], lf_ref.at[hh, 0],
                                      trif_ref[...], sf_ref.at[hh], o_ref.at[hh, 0], start_f, False))
        streams.append(_gla_direction(qb_ref.at[hh, 0], vb_ref.at[hh, 0], kb_ref.at[hh, 0], lb_ref.at[hh, 0],
                                      trib_ref[...], sb_ref.at[hh], o_ref.at[hh, 0], start_b, True))
    _interleave(streams, GLA_STAGGER)


def _gla(qv4, lf4, kk4):
    _, b, seq_len, _ = qv4.shape
    lbk = min(GLA_LB, seq_len)
    assert lbk % CUMSUM_ROWS == 0 and seq_len % lbk == 0
    nblk = seq_len // lbk
    nh = max(h for h in (1, 2, 3) if 2 * h * seq_len * HEAD_DIM * 4 <= GLA_OUT_BLOCK_BYTES or h == 1)
    blk = lambda head0, rev: pl.BlockSpec(
        (nh, 1, lbk, HEAD_DIM),
        (lambda i, h, j: (head0 + h, i, nblk - 1 - j, 0)) if rev else (lambda i, h, j: (head0 + h, i, j, 0)))
    tri_spec = pl.BlockSpec((CUMSUM_ROWS, CUMSUM_ROWS), lambda i, h, j: (0, 0))
    tri_f, tri_b = _cumsum_matrices()
    groups = HEADS // nh
    return pl.pallas_call(
        _gla_kernel,
        grid=(b, groups, nblk),
        in_specs=[tri_spec, tri_spec,
                  blk(0, False), blk(groups, False), blk(0, False), blk(0, False),
                  blk(0, True), blk(groups, True), blk(groups, True), blk(groups, True)],
        out_specs=pl.BlockSpec((nh, 1, seq_len, HEAD_DIM), lambda i, h, j: (h, i, 0, 0)),
        out_shape=jax.ShapeDtypeStruct((HEADS, b, seq_len, HEAD_DIM), F32),
        scratch_shapes=[pltpu.VMEM((nh, HEAD_DIM, HEAD_DIM), F32)] * 2,
        compiler_params=_cparams("parallel", "parallel", "arbitrary"),
        name="gla",
    )(tri_f, tri_b, qv4, qv4, kk4, lf4, qv4, qv4, kk4, lf4)


HALF_D = D_MODEL // 2


def _pack_bf16_pairs(x):
    bits = lax.bitcast_convert_type(x.astype(BF16).astype(F32), jnp.uint32)
    return bits[:, :HALF_D] | (bits[:, HALF_D:] >> 16)


def _unpack_bf16_pairs(words):
    hi = lax.bitcast_convert_type(words & jnp.uint32(0xFFFF0000), F32)
    lo = lax.bitcast_convert_type(words << 16, F32)
    return hi, lo


OUT_TM = 512


def _out_proj_kernel(*refs, nz):
    z_refs = refs[:nz]
    (o_ref, g_ref, xn_ref, ng_ref, wz_ref, wh_ref, l1g_ref, l1b_ref, rwh_ref, rwl_ref, rb_ref,
     h_ref, hpk_ref, route_ref, gate_ref, cnt_ref, base_ref) = refs[nz:]
    tm = o_ref.shape[1]

    z = jnp.concatenate([r[0] for r in z_refs], axis=0) if nz > 1 else z_refs[0][0]
    g = g_ref[...].astype(F32)
    normed = []
    for hd in range(HEADS):
        oh = o_ref[hd]
        normed.append(oh * lax.rsqrt(jnp.mean(oh * oh, axis=-1, keepdims=True) + RMS_EPS))
        yield
    hg = jnp.concatenate(normed, axis=1) * ng_ref[...] * (g * jax.nn.sigmoid(g))
    yield
    mixed = jnp.dot(z, wz_ref[...], preferred_element_type=F32)
    mixed += jnp.dot(hg.astype(BF16), wh_ref[...], preferred_element_type=F32)
    yield
    h = _layer_norm_rows(DEEPNORM_ALPHA * xn_ref[...] + mixed, l1g_ref[...], l1b_ref[...])
    h_ref[...] = h
    yield
    hpk_ref[...] = _pack_bf16_pairs(h)
    yield

    nt = (((1,), (1,)), ((), ()))
    h_hi = h.astype(BF16)
    h_lo = (h - h_hi.astype(F32)).astype(BF16)
    logits = lax.dot_general(rwh_ref[...], h_hi, nt, preferred_element_type=F32)
    logits += lax.dot_general(rwh_ref[...], h_lo, nt, preferred_element_type=F32)
    logits += lax.dot_general(rwl_ref[...], h_hi, nt, preferred_element_type=F32)
    logits += rb_ref[...]
    yield

    eid_f = lax.broadcasted_iota(jnp.int32, (N_EXPERTS, tm), 0).astype(F32)
    work = logits
    vals, idxs, hits = [], [], []
    for _ in range(TOP_K):
        m = jnp.max(work, axis=0, keepdims=True)
        idx = jnp.min(jnp.where(work == m, eid_f, float(N_EXPERTS)), axis=0, keepdims=True)
        hit = eid_f == idx
        work = jnp.where(hit, -jnp.inf, work)
        vals.append(m)
        idxs.append(idx)
        hits.append(hit)
        yield
    exps = [jnp.exp(v - vals[0]) for v in vals]
    denom = exps[0] + exps[1] + exps[2] + exps[3]

    member = jnp.zeros((N_EXPERTS, tm), F32)
    for hit in hits:
        member = member + jnp.where(hit, 1.0, 0.0)
    s_idx = lax.broadcasted_iota(jnp.int32, (tm, tm), 0)
    t_idx = lax.broadcasted_iota(jnp.int32, (tm, tm), 1)
    earlier = jnp.where(s_idx < t_idx, 1.0, 0.0).astype(BF16)
    base = base_ref[...]
    before = jnp.dot(member.astype(BF16), earlier, preferred_element_type=F32) + base
    yield

    row8 = lax.broadcasted_iota(jnp.int32, (2 * TOP_K, tm), 0)
    route = jnp.zeros((2 * TOP_K, tm), F32)
    row128 = lax.broadcasted_iota(jnp.int32, (LANES, tm), 0)
    gate_t = jnp.zeros((LANES, tm), F32)
    for k in range(TOP_K):
        rank = jnp.sum(jnp.where(hits[k], before, 0.0), axis=0, keepdims=True)
        route = route + jnp.where(row8 == k, idxs[k], 0.0) + jnp.where(row8 == TOP_K + k, rank, 0.0)
        gate_t = gate_t + jnp.where(row128 == k, exps[k] / denom, 0.0)
    route_ref[...] = route.astype(jnp.int32)
    gate_ref[...] = gate_t.T
    total = base + jnp.sum(member, axis=1, keepdims=True)
    base_ref[...] = total
    cnt_ref[...] = total


def _out_proj_stage(zt, o3, g2d, xn2d, seq_len, norm_g6, w_out_bf16, ln_g, ln_b, rw_hi, rw_lo, rb_pad):
    t = o3.shape[1]
    n1 = zt.shape[1]
    tm = max(min(OUT_TM, seq_len), n1)
    nz = tm // n1
    row = lambda i: (i, 0)
    const = lambda i: (0, 0)

    def z_spec(k):
        return pl.BlockSpec((1, n1, FOURIER_WIDTH),
                            lambda i: ((i * tm) // seq_len, 0, ((i * tm) % seq_len) // n1 + k))

    in_specs = [z_spec(k) for k in range(nz)] + [
        pl.BlockSpec((HEADS, tm, HEAD_DIM), lambda i: (0, i, 0)),
        pl.BlockSpec((tm, HGRN_WIDTH), row),
        pl.BlockSpec((tm, D_MODEL), row),
        _resident((1, HGRN_WIDTH)),
        _resident((FOURIER_WIDTH, D_MODEL)),
        _resident((HGRN_WIDTH, D_MODEL)),
        _resident((1, D_MODEL)),
        _resident((1, D_MODEL)),
        _resident((N_EXPERTS, D_MODEL)),
        _resident((N_EXPERTS, D_MODEL)),
        _resident((N_EXPERTS, 1)),
    ]
    out_specs = [
        pl.BlockSpec((tm, D_MODEL), row),
        pl.BlockSpec((tm, HALF_D), row),
        pl.BlockSpec((2 * TOP_K, tm), lambda i: (0, i)),
        pl.BlockSpec((tm, LANES), row),
        pl.BlockSpec((N_EXPERTS, 1), const),
    ]
    out_shape = [
        jax.ShapeDtypeStruct((t, D_MODEL), F32),
        jax.ShapeDtypeStruct((t, HALF_D), jnp.uint32),
        jax.ShapeDtypeStruct((2 * TOP_K, t), jnp.int32),
        jax.ShapeDtypeStruct((t, LANES), F32),
        jax.ShapeDtypeStruct((N_EXPERTS, 1), F32),
    ]
    return _Stage(
        body=functools.partial(_out_proj_kernel, nz=nz),
        init=_zero_ref,
        grid=(t // tm,),
        in_specs=in_specs,
        out_specs=out_specs,
        out_shape=out_shape,
        scratch=[pltpu.VMEM((N_EXPERTS, 1), F32)],
        args=[zt] * nz + [o3, g2d, xn2d, norm_g6, w_out_bf16[:FOURIER_WIDTH], w_out_bf16[FOURIER_WIDTH:],
                          ln_g.reshape(1, -1), ln_b.reshape(1, -1), rw_hi, rw_lo, rb_pad],
        sequential=True,
        name="out_proj",
    )


COMBINE_TM = 512
COMBINE_PARTS = 2
SC_WINDOW = 128


def _sc_mesh():
    return plsc.VectorSubcoreMesh(core_axis_name="core", subcore_axis_name="subcore")


def _dispatch(dest_kt, hpk, n_slots):
    t, d = hpk.shape
    rows = pl.BlockSpec((SC_WINDOW, d), index_map=lambda i: (i, 0), pipeline_mode=pl.Buffered(1))
    idx = pl.BlockSpec((1, SC_WINDOW), index_map=lambda i: (0, i))

    @pl.kernel(out_type=jax.ShapeDtypeStruct((n_slots, d), hpk.dtype), mesh=_sc_mesh(), name="moe_dispatch_sc")
    def scatter(x_hbm, i0_hbm, i1_hbm, i2_hbm, i3_hbm, o_hbm):
        def body(x_vmem, *idx_vmem):
            for iv in idx_vmem:
                pltpu.sync_copy(x_vmem, o_hbm.at[iv.at[0]])

        pltpu.emit_pipeline(
            body, grid=(t // SC_WINDOW,), in_specs=[rows] + [idx] * TOP_K, out_specs=[],
            core_axis_name=("core", "subcore"), dimension_semantics=(pltpu.PARALLEL,),
        )(x_hbm, i0_hbm, i1_hbm, i2_hbm, i3_hbm)

    return scatter(hpk, *[dest_kt[k].reshape(1, t) for k in range(TOP_K)])


def _gather_rows(yb, dest_kt):
    _, t = dest_kt.shape
    d = yb.shape[1]
    n = TOP_K * t

    @pl.kernel(out_type=jax.ShapeDtypeStruct((n, d), yb.dtype), mesh=_sc_mesh(), name="moe_gather_sc")
    def gather(y_hbm, i_hbm, o_hbm):
        def body(i_vmem, o_vmem):
            pltpu.sync_copy(y_hbm.at[i_vmem.at[0]], o_vmem)

        pltpu.emit_pipeline(
            body, grid=(n // SC_WINDOW,),
            in_specs=[pl.BlockSpec((1, SC_WINDOW), index_map=lambda i: (0, i))],
            out_specs=[pl.BlockSpec((SC_WINDOW, d), index_map=lambda i: (i, 0), pipeline_mode=pl.Buffered(1))],
            core_axis_name=("core", "subcore"), dimension_semantics=(pltpu.PARALLEL,),
        )(i_hbm, o_hbm)

    return gather(yb, dest_kt.reshape(1, n)).reshape(TOP_K, t, d)


def _experts_kernel(be_ref, nused_ref, next_ref, halves_ref, xb_ref, wgu_hbm, bgu_ref, wdn_hbm, bdn_ref, yb_ref,
                    wgu_f32, wdn_f32, wgu_bf, wdn_bf, slot_ref, sem):
    i = pl.program_id(0)
    used = i < nused_ref[0]
    e = be_ref[i]

    def weight_copies(expert, slot):
        return (pltpu.make_async_copy(wgu_hbm.at[expert], wgu_f32.at[slot], sem.at[slot, 0]),
                pltpu.make_async_copy(wdn_hbm.at[expert], wdn_f32.at[slot], sem.at[slot, 1]))

    @pl.when(used & (i == 0))
    def _():
        slot_ref[0] = 0
        for c in weight_copies(e, 0):
            c.start()

    @pl.when(used & (i > 0) & (e != be_ref[jnp.maximum(i - 1, 0)]))
    def _():
        slot_ref[0] = 1 - slot_ref[0]

    @pl.when(used & ((i == 0) | (e != be_ref[jnp.maximum(i - 1, 0)])))
    def _():
        slot = slot_ref[0]
        for c in weight_copies(e, slot):
            c.wait()
        wgu_bf[...] = wgu_f32[slot].astype(BF16)
        wdn_bf[...] = wdn_f32[slot].astype(BF16)
        nxt = next_ref[e]

        @pl.when(nxt != e)
        def _():
            for c in weight_copies(nxt, 1 - slot):
                c.start()

    def mlp(rows):
        x_hi, x_lo = _unpack_bf16_pairs(xb_ref[:rows])
        x = jnp.concatenate([x_hi.astype(BF16), x_lo.astype(BF16)], axis=1)
        gu = jnp.dot(x, wgu_bf[...], preferred_element_type=F32) + bgu_ref[0]
        gate = jnp.minimum(gu[:, :D_FF], SWIGLU_LIMIT)
        up = jnp.clip(gu[:, D_FF:], -SWIGLU_LIMIT, SWIGLU_LIMIT)
        act = (up + 1.0) * gate * jax.nn.sigmoid(SWIGLU_ALPHA * gate)
        y = jnp.dot(act.astype(BF16), wdn_bf[...], preferred_element_type=F32) + bdn_ref[0]
        yb_ref[:rows] = _pack_bf16_pairs(y)

    @pl.when(used & (halves_ref[i] == 2))
    def _():
        mlp(MOE_BLOCK)

    @pl.when(used & (halves_ref[i] != 2))
    def _():
        mlp(MOE_BLOCK // 2)


def _experts(block_expert, n_used, next_expert, halves, xb, wgu, bgu, wdn, bdn):
    n_slots = xb.shape[0]
    nb = n_slots // MOE_BLOCK
    blk = pl.BlockSpec((MOE_BLOCK, HALF_D),
                       lambda i, be, nu, nx, hv: (jnp.minimum(i, jnp.maximum(nu[0] - 1, 0)), 0))
    per_expert = lambda shape: pl.BlockSpec((1,) + shape, lambda i, be, nu, nx, hv: (be[i], 0, 0))
    hbm = pl.BlockSpec(memory_space=pl.ANY)
    grid_spec = pltpu.PrefetchScalarGridSpec(
        num_scalar_prefetch=4,
        grid=(nb,),
        in_specs=[blk, hbm, per_expert((1, 2 * D_FF)), hbm, per_expert((1, D_MODEL))],
        out_specs=blk,
        scratch_shapes=[pltpu.VMEM((2, D_MODEL, 2 * D_FF), F32), pltpu.VMEM((2, D_FF, D_MODEL), F32),
                        pltpu.VMEM((D_MODEL, 2 * D_FF), BF16), pltpu.VMEM((D_FF, D_MODEL), BF16),
                        pltpu.SMEM((1,), jnp.int32), pltpu.SemaphoreType.DMA((2, 2))],
    )
    return pl.pallas_call(
        _experts_kernel,
        grid_spec=grid_spec,
        out_shape=jax.ShapeDtypeStruct((n_slots, HALF_D), jnp.uint32),
        compiler_params=_cparams("arbitrary"),
        name="moe_experts",
    )(block_expert, n_used, next_expert, halves, xb, wgu, bgu, wdn, bdn)


def _combine_kernel(rows_ref, gate_ref, h_ref, g_ref, b_ref, *rest):
    out_ref = rest[-1]
    tm = h_ref.shape[0]
    gate = gate_ref[...]
    ff_hi = jnp.zeros((tm, HALF_D), F32)
    ff_lo = jnp.zeros((tm, HALF_D), F32)
    for k in range(TOP_K):
        hi, lo = _unpack_bf16_pairs(rows_ref[k])
        gk = gate[:, k:k + 1]
        ff_hi = ff_hi + gk * hi
        ff_lo = ff_lo + gk * lo
    ff = jnp.concatenate([ff_hi, ff_lo], axis=1)
    out_ref[...] = _layer_norm_rows(DEEPNORM_ALPHA * h_ref[...] + ff, g_ref[...], b_ref[...])


def _combine(rows_kt, gates, h2d, ln_g, ln_b, token0, prev_out):
    t = h2d.shape[0]
    t_part = rows_kt.shape[1]
    tm = min(COMBINE_TM, t_part)
    first = token0 // tm
    row = lambda i: (first + i, 0)
    const = lambda i: (0, 0)
    in_specs = [
        pl.BlockSpec((TOP_K, tm, HALF_D), lambda i: (0, i, 0)),
        pl.BlockSpec((tm, LANES), row),
        pl.BlockSpec((tm, D_MODEL), row),
        pl.BlockSpec((1, D_MODEL), const),
        pl.BlockSpec((1, D_MODEL), const),
    ]
    args = [rows_kt, gates, h2d, ln_g.reshape(1, -1), ln_b.reshape(1, -1)]
    aliases = {}
    if prev_out is not None:
        in_specs.append(pl.BlockSpec(memory_space=pl.ANY))
        args.append(prev_out)
        aliases = {len(args) - 1: 0}
    return pl.pallas_call(
        _combine_kernel,
        grid=(t_part // tm,),
        in_specs=in_specs,
        out_specs=pl.BlockSpec((tm, D_MODEL), row),
        out_shape=jax.ShapeDtypeStruct((t, D_MODEL), F32),
        input_output_aliases=aliases,
        compiler_params=_cparams("parallel"),
        name="moe_combine",
    )(*args)


def _moe_plan(route, counts_f32, n_tokens):
    eid = route[:TOP_K]
    rank = route[TOP_K:]
    counts = counts_f32[:, 0].astype(jnp.int32)
    padded = ((counts + MOE_BLOCK - 1) // MOE_BLOCK) * MOE_BLOCK
    pend = jnp.cumsum(padded)
    pstart = pend - padded
    experts = jnp.arange(N_EXPERTS, dtype=jnp.int32)[:, None, None]
    dest = rank + jnp.sum(jnp.where(eid[None] == experts, pstart[:, None, None], 0), axis=0)
    n_assign = n_tokens * TOP_K
    n_slots = ((n_assign + MOE_BLOCK - 1) // MOE_BLOCK) * MOE_BLOCK + N_EXPERTS * MOE_BLOCK
    nb = n_slots // MOE_BLOCK
    block_start = jnp.arange(nb, dtype=jnp.int32) * MOE_BLOCK
    block_expert = jnp.sum((pend[None, :] <= block_start[:, None]).astype(jnp.int32), axis=1)
    block_expert = jnp.minimum(block_expert, N_EXPERTS - 1).astype(jnp.int32)
    n_used = (pend[-1:] // MOE_BLOCK).astype(jnp.int32)
    onehot = (block_expert[:, None] == jnp.arange(N_EXPERTS, dtype=jnp.int32)[None, :]).astype(jnp.int32)
    valid_rows = jnp.sum(onehot * (pstart + counts)[None, :], axis=1) - block_start
    halves = jnp.where(valid_rows > MOE_BLOCK // 2, 2, 1).astype(jnp.int32)
    ids = jnp.arange(N_EXPERTS, dtype=jnp.int32)
    later_nonempty = (ids[None, :] > ids[:, None]) & (counts[None, :] > 0)
    next_expert = jnp.min(jnp.where(later_nonempty, ids[None, :], N_EXPERTS), axis=1)
    next_expert = jnp.where(next_expert == N_EXPERTS, ids, next_expert).astype(jnp.int32)
    return dest.astype(jnp.int32), block_expert, n_used, next_expert, halves, n_slots


def _in_stage(x, p):
    t = x.shape[0] * x.shape[1]
    return _in_proj_stage(x.reshape(t, D_MODEL), p["ln_in_g"], p["ln_in_b"], p["lb2"], p["w_in"])


def _mix_and_out_stage(x, in_outs, p):
    batch, seq_len, _ = x.shape
    t = batch * seq_len
    xn, u, qv, lf, kk, og = in_outs
    zt = _fourier_mix(u, batch, seq_len, p["fourier_norm_g"])
    per_seq = lambda a: a.reshape(a.shape[0], batch, seq_len, HEAD_DIM)
    o = _gla(per_seq(qv), per_seq(lf), per_seq(kk))
    return _out_proj_stage(
        zt, o.reshape(HEADS, t, HEAD_DIM), og, xn, seq_len, p["norm_g6"], p["w_out"], p["ln1_g"], p["ln1_b"],
        p["rw_hi"], p["rw_lo"], p["rb_pad"])


def _moe(x, out_outs, p):
    batch, seq_len, _ = x.shape
    t = batch * seq_len
    h, hpk, route, gates, counts = out_outs
    dest_kt, block_expert, n_used, next_expert, halves, n_slots = _moe_plan(route, counts, t)
    xb = _dispatch(dest_kt, hpk, n_slots)
    yb = _experts(block_expert, n_used, next_expert, halves, xb, p["w_gu"], p["b_gu"], p["w_dn"], p["b_dn"])
    y = None
    part = t // COMBINE_PARTS
    for j in range(COMBINE_PARTS):
        rows = _gather_rows(yb, dest_kt[:, j * part:(j + 1) * part])
        y = _combine(rows, gates, h, p["ln2_g"], p["ln2_b"], j * part, y)
    return y.reshape(batch, seq_len, D_MODEL)


def _prepare_params(ln_in_g, ln_in_b, w_in, fourier_norm_g, lb_gamma, hgrn_norm_g, w_out, ln1_g, ln1_b,
                    router_w, router_b, w_gate_up, b_gate_up, w_down, b_down, ln2_g, ln2_b):
    fw, hw = FOURIER_WIDTH, HGRN_WIDTH
    w0 = w_in[0]
    lb_all = jnp.cumsum(jax.nn.softmax(lb_gamma.astype(F32), axis=1), axis=1)
    rw = router_w[0].astype(F32).T
    rw_hi = rw.astype(BF16)
    rw_lo = (rw - rw_hi.astype(F32)).astype(BF16)
    del fw, hw
    return dict(
        ln_in_g=ln_in_g, ln_in_b=ln_in_b, w_in=w0.astype(BF16),
        fourier_norm_g=fourier_norm_g[0],
        lb2=jnp.concatenate([lb_all[0, 0], lb_all[1, 0]]).reshape(1, -1),
        norm_g6=jnp.tile(hgrn_norm_g[0].astype(F32), HEADS).reshape(1, -1),
        w_out=w_out[0].astype(BF16), ln1_g=ln1_g[0], ln1_b=ln1_b[0],
        rw_hi=rw_hi, rw_lo=rw_lo, rb_pad=router_b[0].astype(F32).reshape(-1, 1),
        w_gu=w_gate_up[0], b_gu=b_gate_up[0].reshape(N_EXPERTS, 1, -1),
        w_dn=w_down[0], b_dn=b_down[0].reshape(N_EXPERTS, 1, -1),
        ln2_g=ln2_g[0], ln2_b=ln2_b[0],
    )


def kernel(x_prompt, x_sample, ln_in_g, ln_in_b, w_in, fourier_norm_g, lb_gamma, hgrn_norm_g, w_out,
           ln1_g, ln1_b, router_w, router_b, w_gate_up, b_gate_up, w_down, b_down, ln2_g, ln2_b):
    p = _prepare_params(ln_in_g, ln_in_b, w_in, fourier_norm_g, lb_gamma, hgrn_norm_g, w_out, ln1_g, ln1_b,
                        router_w, router_b, w_gate_up, b_gate_up, w_down, b_down, ln2_g, ln2_b)
    first, second = x_sample, x_prompt
    (in_first,) = _run_stages([_in_stage(first, p)])
    out_stage_first = _mix_and_out_stage(first, in_first, p)
    in_stage_second = _in_stage(second, p)
    if in_stage_second.grid == out_stage_first.grid:
        in_second, out_first = _run_stages([in_stage_second, out_stage_first])
    else:
        (in_second,) = _run_stages([in_stage_second])
        (out_first,) = _run_stages([out_stage_first])
    (out_second,) = _run_stages([_mix_and_out_stage(second, in_second, p)])
    y_first = _moe(first, out_first, p)
    y_second = _moe(second, out_second, p)
    return (y_second, y_first)
```

```python
import functools
import math
from typing import Callable, NamedTuple

import numpy as np
import jax
import jax.numpy as jnp
from jax import lax
from jax.experimental import pallas as pl
from jax.experimental.pallas import tpu as pltpu
from jax.experimental.pallas import tpu_sc as plsc

D_MODEL = 1024
FOURIER_WIDTH = 256
FOURIER_GROUP_DIM = 64
HGRN_WIDTH = 768
HEAD_DIM = 128
HEADS = 6
CHUNK = 64
N_EXPERTS = 32
TOP_K = 4
D_FF = 1024
SWIGLU_LIMIT = 7.0
SWIGLU_ALPHA = 1.702
MOE_BLOCK = 1024
MOE_PARTS = 4
LN_EPS = 1e-5
RMS_EPS = 1e-6
DEEPNORM_ALPHA = 2.0 ** 0.25

LANES = 128
VMEM_LIMIT_BYTES = 56 * 1024 * 1024

F32 = jnp.float32
BF16 = jnp.bfloat16


def _cparams(*sem):
    return pltpu.CompilerParams(dimension_semantics=sem, vmem_limit_bytes=VMEM_LIMIT_BYTES)


def _layer_norm_rows(x, g, b):
    mu = jnp.mean(x, axis=-1, keepdims=True)
    xc = x - mu
    var = jnp.mean(xc * xc, axis=-1, keepdims=True)
    return xc * lax.rsqrt(var + LN_EPS) * g + b


IN_TM = 512
IN_TN = 512


def _in_proj_kernel(x_ref, g_ref, b_ref, lb_ref, w_ref, xn_ref, u_ref, qv_ref, lf_ref, kk_ref, og_ref):
    xn = _layer_norm_rows(x_ref[...], g_ref[...], b_ref[...])
    xn_ref[...] = xn
    xb = xn.astype(BF16)
    hw = HGRN_WIDTH
    yield

    def store_heads(ref, first_head, c0, val):
        for j in range(val.shape[1] // HEAD_DIM):
            ref[first_head + c0 // HEAD_DIM + j] = val[:, j * HEAD_DIM:(j + 1) * HEAD_DIM]

    def chunks(col, width):
        for c0 in range(0, width, IN_TN):
            cw = min(IN_TN, width - c0)
            yield c0, cw, jnp.dot(xb, w_ref[:, col + c0:col + c0 + cw], preferred_element_type=F32)

    for c0, cw, acc in chunks(0, FOURIER_WIDTH):
        u_ref[:, c0:c0 + cw] = acc.astype(BF16)
        yield
    for c0, cw, acc in chunks(FOURIER_WIDTH, hw):
        store_heads(qv_ref, 0, c0, (acc * jax.nn.sigmoid(acc) * (HEAD_DIM ** -0.5)).astype(BF16))
        yield
    for c0, cw, acc in chunks(FOURIER_WIDTH + hw, hw):
        store_heads(qv_ref, HEADS, c0, acc.astype(BF16))
        yield
    for c0, cw, acc in chunks(FOURIER_WIDTH + 2 * hw, 2 * hw):
        lb = lb_ref[:, c0:c0 + cw]
        fg = lb + (1.0 - lb) * jax.nn.sigmoid(acc)
        store_heads(lf_ref, 0, c0, jnp.log(fg))
        store_heads(kk_ref, 0, c0, (1.0 - fg).astype(BF16))
        yield
    for c0, cw, acc in chunks(FOURIER_WIDTH + 4 * hw, hw):
        og_ref[:, c0:c0 + cw] = acc.astype(BF16)
        yield


class _Stage(NamedTuple):
    body: Callable
    init: Callable | None
    grid: tuple
    in_specs: list
    out_specs: list
    out_shape: list
    scratch: list
    args: list
    sequential: bool
    name: str


_DONE = object()


def _interleave(pieces, stagger):
    done = [False] * len(pieces)
    t = 0
    while not all(done):
        for i, g in enumerate(pieces):
            if not done[i] and t >= i * stagger:
                done[i] = next(g, _DONE) is _DONE
        t += 1


def _run_stages(stages):
    assert all(s.grid == stages[0].grid for s in stages)
    n_in = [len(s.in_specs) for s in stages]
    n_out = [len(s.out_specs) for s in stages]
    n_scr = [len(s.scratch) for s in stages]

    def body(*refs):
        scr = sum(n_in) + sum(n_out)
        for s, c in zip(stages, n_scr):
            if s.init is not None:
                pl.when(pl.program_id(0) == 0)(functools.partial(s.init, *refs[scr:scr + c]))
            scr += c
        ins, outs, scr = 0, sum(n_in), sum(n_in) + sum(n_out)
        bodies = []
        for s, a, b, c in zip(stages, n_in, n_out, n_scr):
            bodies.append(s.body(*refs[ins:ins + a], *refs[outs:outs + b], *refs[scr:scr + c]))
            ins, outs, scr = ins + a, outs + b, scr + c
        _interleave(bodies, 0)

    results = pl.pallas_call(
        body,
        grid=stages[0].grid,
        in_specs=[sp for s in stages for sp in s.in_specs],
        out_specs=[sp for s in stages for sp in s.out_specs],
        out_shape=[sh for s in stages for sh in s.out_shape],
        scratch_shapes=[sc for s in stages for sc in s.scratch],
        compiler_params=_cparams("arbitrary" if any(s.sequential for s in stages) else "parallel"),
        name="_".join(s.name for s in stages),
    )(*[a for s in stages for a in s.args])
    split, pos = [], 0
    for b in n_out:
        split.append(results[pos:pos + b])
        pos += b
    return split


def _zero_ref(ref):
    ref[...] = jnp.zeros(ref.shape, ref.dtype)


def _resident(shape):
    return pl.BlockSpec(shape, lambda i: (0,) * len(shape), pipeline_mode=pl.Buffered(1))


def _in_proj_stage(x2d, ln_g, ln_b, lb2, w_bf16):
    t = x2d.shape[0]
    tm = min(IN_TM, t)
    row = lambda i: (i, 0)
    flat = lambda w, dt: (pl.BlockSpec((tm, w), row), jax.ShapeDtypeStruct((t, w), dt))
    head_major = lambda nh, dt: (pl.BlockSpec((nh, tm, HEAD_DIM), lambda i: (0, i, 0)),
                                 jax.ShapeDtypeStruct((nh, t, HEAD_DIM), dt))
    outs = [flat(D_MODEL, F32), flat(FOURIER_WIDTH, BF16), head_major(2 * HEADS, BF16),
            head_major(2 * HEADS, F32), head_major(2 * HEADS, BF16), flat(HGRN_WIDTH, BF16)]
    return _Stage(
        body=_in_proj_kernel,
        init=None,
        grid=(t // tm,),
        in_specs=[pl.BlockSpec((tm, D_MODEL), row), _resident((1, D_MODEL)), _resident((1, D_MODEL)),
                  _resident((1, 2 * HGRN_WIDTH)), _resident(w_bf16.shape)],
        out_specs=[o[0] for o in outs],
        out_shape=[o[1] for o in outs],
        scratch=[],
        args=[x2d, ln_g.reshape(1, -1), ln_b.reshape(1, -1), lb2, w_bf16],
        sequential=False,
        name="in_proj",
    )


def _fft_split(seq_len):
    n1 = 1 << ((seq_len.bit_length() - 1 + 1) // 2)
    return n1, seq_len // n1


@functools.lru_cache(maxsize=None)
def _fft_tables(seq_len):
    n1, n2 = _fft_split(seq_len)
    k1 = np.arange(n1)
    ang1 = 2.0 * np.pi * ((k1[:, None] * k1[None, :]) % n1) / n1
    s1 = 1.0 / math.sqrt(n1)
    c1, s1m = np.cos(ang1) * s1, np.sin(ang1) * s1
    l1p = np.arange(n1)[:, None, None]
    l2p = np.arange(n2)[None, :, None]
    l2 = np.arange(n2)[None, None, :]
    ang2 = 2.0 * np.pi * ((l2 * (l1p + n1 * l2p)) % seq_len) / seq_len
    s2 = 1.0 / math.sqrt(n2)
    gc, gs = np.cos(ang2) * s2, np.sin(ang2) * s2
    kc = np.arange(FOURIER_GROUP_DIM)
    angc = 2.0 * np.pi * ((kc[:, None] * kc[None, :]) % FOURIER_GROUP_DIM) / FOURIER_GROUP_DIM
    sc = 1.0 / math.sqrt(FOURIER_GROUP_DIM)
    groups = FOURIER_WIDTH // FOURIER_GROUP_DIM
    bc = np.kron(np.eye(groups), np.cos(angc) * sc)
    bs = np.kron(np.eye(groups), np.sin(angc) * sc)
    as_bf16 = lambda a: jnp.asarray(a, dtype=F32).astype(BF16)
    return tuple(as_bf16(a) for a in (c1, s1m, gc, gs, bc, bs))


def _fft1_kernel(c_ref, s_ref, u_ref, ar_ref, ai_ref):
    u = u_ref[0]
    ar_ref[0] = jnp.dot(c_ref[...], u, preferred_element_type=F32).astype(BF16)
    ai_ref[0] = (-jnp.dot(s_ref[...], u, preferred_element_type=F32)).astype(BF16)


FFT1_TN = 4096


def _fft_stage1(u3, c1, s1):
    b, n1, width = u3.shape
    tn = min(FFT1_TN, width)
    blk = pl.BlockSpec((1, n1, tn), lambda i, j: (i, 0, j))
    mat = pl.BlockSpec((n1, n1), lambda i, j: (0, 0))
    return pl.pallas_call(
        _fft1_kernel,
        grid=(b, width // tn),
        in_specs=[mat, mat, blk],
        out_specs=[blk, blk],
        out_shape=[jax.ShapeDtypeStruct(u3.shape, BF16)] * 2,
        compiler_params=_cparams("parallel", "parallel"),
        name="fft_stage1",
    )(c1, s1, u3)


FFT2_ROWS = 1024


def _fft2_kernel(gc_ref, gs_ref, bc_ref, bs_ref, g_ref, ar_ref, ai_ref, z_ref):
    t1, n2 = ar_ref.shape[1], ar_ref.shape[2]
    xr, xi = [], []
    for j in range(t1):
        ar, ai = ar_ref[0, j], ai_ref[0, j]
        gc, gs = gc_ref[j], gs_ref[j]
        xr.append((jnp.dot(gc, ar, preferred_element_type=F32)
                   + jnp.dot(gs, ai, preferred_element_type=F32)).astype(BF16))
        xi.append((jnp.dot(gc, ai, preferred_element_type=F32)
                   - jnp.dot(gs, ar, preferred_element_type=F32)).astype(BF16))
    z = jnp.dot(jnp.concatenate(xr, axis=0), bc_ref[...], preferred_element_type=F32)
    z += jnp.dot(jnp.concatenate(xi, axis=0), bs_ref[...], preferred_element_type=F32)
    z = (z * lax.rsqrt(jnp.mean(z * z, axis=-1, keepdims=True) + RMS_EPS) * g_ref[...]).astype(BF16)
    for j in range(t1):
        z_ref[0, j] = z[j * n2:(j + 1) * n2]


def _fft_stage2(ar4, ai4, gc, gs, bc, bs, gain):
    b, n1, n2, w = ar4.shape
    t1 = min(max(FFT2_ROWS // n2, 1), n1)
    a_blk = pl.BlockSpec((1, t1, n2, w), lambda i, j: (i, j, 0, 0))
    g_blk = pl.BlockSpec((t1, n2, n2), lambda i, j: (j, 0, 0))
    c_blk = pl.BlockSpec((w, w), lambda i, j: (0, 0))
    return pl.pallas_call(
        _fft2_kernel,
        grid=(b, n1 // t1),
        in_specs=[g_blk, g_blk, c_blk, c_blk, pl.BlockSpec((1, w), lambda i, j: (0, 0)), a_blk, a_blk],
        out_specs=a_blk,
        out_shape=jax.ShapeDtypeStruct(ar4.shape, BF16),
        compiler_params=_cparams("parallel", "parallel"),
        name="fft_stage2",
    )(gc, gs, bc, bs, gain.reshape(1, -1), ar4, ai4)


def _fourier_mix(u2d, batch, seq_len, gain):
    n1, n2 = _fft_split(seq_len)
    c1, s1, gc, gs, bc, bs = _fft_tables(seq_len)
    u3 = u2d.reshape(batch, n1, n2 * FOURIER_WIDTH)
    ar, ai = _fft_stage1(u3, c1, s1)
    shape4 = (batch, n1, n2, FOURIER_WIDTH)
    zt = _fft_stage2(ar.reshape(shape4), ai.reshape(shape4), gc, gs, bc, bs, gain)
    return zt.reshape(batch, n1, n2 * FOURIER_WIDTH)


GLA_LB = 1024
GLA_OUT_BLOCK_BYTES = 32 * 1024 * 1024
GLA_STAGGER = 6
CUMSUM_ROWS = 256


@functools.lru_cache(maxsize=None)
def _cumsum_matrices():
    r = np.arange(CUMSUM_ROWS)
    same_chunk = (r[:, None] // CHUNK) == (r[None, :] // CHUNK)
    prefix = same_chunk & (r[None, :] <= r[:, None])
    suffix = same_chunk & (r[None, :] >= r[:, None])
    return (jnp.asarray(prefix, dtype=F32).astype(BF16), jnp.asarray(suffix, dtype=F32).astype(BF16))


def _chunk_cumsum(x, tri):
    hi = x.astype(BF16)
    lo = (x - hi.astype(F32)).astype(BF16)
    width = x.shape[1]
    parts = []
    for r0 in range(0, x.shape[0], CUMSUM_ROWS):
        rows = slice(r0, r0 + CUMSUM_ROWS)
        both = jnp.dot(tri, jnp.concatenate([hi[rows], lo[rows]], axis=1), preferred_element_type=F32)
        parts.append(both[:, :width] + both[:, width:])
    return parts[0] if len(parts) == 1 else jnp.concatenate(parts, axis=0)


def _gla_direction(q_ref, v_ref, k_ref, lf_ref, tri, st_ref, o_ref, start, reverse):
    n = q_ref.shape[0]
    b = _chunk_cumsum(lf_ref[...], tri)
    yield
    mid = CHUNK // 2 if reverse else CHUNK // 2 - 1
    last = 0 if reverse else CHUNK - 1
    t_idx = lax.broadcasted_iota(jnp.int32, (CHUNK, CHUNK), 0)
    s_idx = lax.broadcasted_iota(jnp.int32, (CHUNK, CHUNK), 1)
    visible = (t_idx <= s_idx) if reverse else (t_idx >= s_idx)
    nchunks = n // CHUNK
    order = range(nchunks - 1, -1, -1) if reverse else range(nchunks)
    nt = (((1,), (1,)), ((), ()))
    tn = (((0,), (0,)), ((), ()))
    rows = [slice(c * CHUNK, (c + 1) * CHUNK) for c in range(nchunks)]
    qe, ke, e_mid, e_last, e_gap = [], [], [], [], []
    for c in range(nchunks):
        bc = b[rows[c]]
        b_mid = bc[mid:mid + 1]
        b_last = bc[last:last + 1]
        qe.append((q_ref[rows[c]].astype(F32) * jnp.exp(bc - b_mid)).astype(BF16))
        ke.append((k_ref[rows[c]].astype(F32) * jnp.exp(b_mid - bc)).astype(BF16))
        e_mid.append(jnp.exp(b_mid))
        e_last.append(jnp.exp(b_last))
        e_gap.append(jnp.exp(b_last - b_mid))
        yield
    scores, delta_t = [], []
    for c in range(nchunks):
        s = lax.dot_general(qe[c], ke[c], nt, preferred_element_type=F32)
        scores.append(jnp.where(visible, s, 0.0).astype(BF16))
        delta_t.append(lax.dot_general(v_ref[rows[c]], ke[c], tn, preferred_element_type=F32) * e_gap[c])
        yield
    st = st_ref[...]
    st_in = [None] * nchunks
    for c in order:
        st_in[c] = (st * e_mid[c]).astype(BF16)
        st = st * e_last[c] + delta_t[c]
    st_ref[...] = st
    yield
    for c in range(nchunks):
        o = jnp.dot(scores[c], v_ref[rows[c]], preferred_element_type=F32)
        o += lax.dot_general(qe[c], st_in[c], nt, preferred_element_type=F32)
        o_ref[pl.ds(start + c * CHUNK, CHUNK), :] += o
        yield


def _gla_kernel(trif_ref, trib_ref, qf_ref, vf_ref, kf_ref, lf_ref, qb_ref, vb_ref, kb_ref, lb_ref,
                o_ref, sf_ref, sb_ref):
    j = pl.program_id(2)
    nblk = pl.num_programs(2)
    lb_rows = qf_ref.shape[2]

    @pl.when(j == 0)
    def _():
        o_ref[...] = jnp.zeros(o_ref.shape, o_ref.dtype)
        sf_ref[...] = jnp.zeros(sf_ref.shape, sf_ref.dtype)
        sb_ref[...] = jnp.zeros(sb_ref.shape, sb_ref.dtype)

    start_f = pl.multiple_of(j * lb_rows, lb_rows)
    start_b = pl.multiple_of((nblk - 1 - j) * lb_rows, lb_rows)
    streams = []
    for hh in range(qf_ref.shape[0]):
        streams.append(_gla_direction(qf_ref.at[hh, 0], vf_ref.at[hh, 0], kf_ref.at[hh, 0], lf_ref.at[hh, 0],
                                      trif_ref[...], sf_ref.at[hh], o_ref.at[hh, 0], start_f, False))
        streams.append(_gla_direction(qb_ref.at[hh, 0], vb_ref.at[hh, 0], kb_ref.at[hh, 0], lb_ref.at[hh, 0],
                                      trib_ref[...], sb_ref.at[hh], o_ref.at[hh, 0], start_b, True))
    _interleave(streams, GLA_STAGGER)


def _gla(qv4, lf4, kk4):
    _, b, seq_len, _ = qv4.shape
    lbk = min(GLA_LB, seq_len)
    assert lbk % CUMSUM_ROWS == 0 and seq_len % lbk == 0
    nblk = seq_len // lbk
    nh = max(h for h in (1, 2) if 2 * h * seq_len * HEAD_DIM * 4 <= GLA_OUT_BLOCK_BYTES or h == 1)
    blk = lambda head0, rev: pl.BlockSpec(
        (nh, 1, lbk, HEAD_DIM),
        (lambda i, h, j: (head0 + h, i, nblk - 1 - j, 0)) if rev else (lambda i, h, j: (head0 + h, i, j, 0)))
    tri_spec = pl.BlockSpec((CUMSUM_ROWS, CUMSUM_ROWS), lambda i, h, j: (0, 0))
    tri_f, tri_b = _cumsum_matrices()
    groups = HEADS // nh
    return pl.pallas_call(
        _gla_kernel,
        grid=(b, groups, nblk),
        in_specs=[tri_spec, tri_spec,
                  blk(0, False), blk(groups, False), blk(0, False), blk(0, False),
                  blk(0, True), blk(groups, True), blk(groups, True), blk(groups, True)],
        out_specs=pl.BlockSpec((nh, 1, seq_len, HEAD_DIM), lambda i, h, j: (h, i, 0, 0)),
        out_shape=jax.ShapeDtypeStruct((HEADS, b, seq_len, HEAD_DIM), F32),
        scratch_shapes=[pltpu.VMEM((nh, HEAD_DIM, HEAD_DIM), F32)] * 2,
        compiler_params=_cparams("parallel", "parallel", "arbitrary"),
        name="gla",
    )(tri_f, tri_b, qv4, qv4, kk4, lf4, qv4, qv4, kk4, lf4)


HALF_D = D_MODEL // 2


def _pack_bf16_pairs(x):
    bits = lax.bitcast_convert_type(x.astype(BF16).astype(F32), jnp.uint32)
    return bits[:, :HALF_D] | (bits[:, HALF_D:] >> 16)


def _unpack_bf16_pairs(words):
    hi = lax.bitcast_convert_type(words & jnp.uint32(0xFFFF0000), F32)
    lo = lax.bitcast_convert_type(words << 16, F32)
    return hi, lo


OUT_TM = 512


def _out_proj_kernel(*refs, nz):
    z_refs = refs[:nz]
    (o_ref, g_ref, xn_ref, ng_ref, wz_ref, wh_ref, l1g_ref, l1b_ref, rwh_ref, rwl_ref, rb_ref,
     h_ref, hpk_ref, route_ref, gate_ref, cnt_ref, base_ref) = refs[nz:]
    tm = o_ref.shape[1]

    z = jnp.concatenate([r[0] for r in z_refs], axis=0) if nz > 1 else z_refs[0][0]
    g = g_ref[...].astype(F32)
    normed = []
    for hd in range(HEADS):
        oh = o_ref[hd]
        normed.append(oh * lax.rsqrt(jnp.mean(oh * oh, axis=-1, keepdims=True) + RMS_EPS))
        yield
    hg = jnp.concatenate(normed, axis=1) * ng_ref[...] * (g * jax.nn.sigmoid(g))
    yield
    mixed = jnp.dot(z, wz_ref[...], preferred_element_type=F32)
    mixed += jnp.dot(hg.astype(BF16), wh_ref[...], preferred_element_type=F32)
    yield
    h = _layer_norm_rows(DEEPNORM_ALPHA * xn_ref[...] + mixed, l1g_ref[...], l1b_ref[...])
    h_ref[...] = h
    yield
    hpk_ref[...] = _pack_bf16_pairs(h)
    yield

    nt = (((1,), (1,)), ((), ()))
    h_hi = h.astype(BF16)
    h_lo = (h - h_hi.astype(F32)).astype(BF16)
    logits = lax.dot_general(rwh_ref[...], h_hi, nt, preferred_element_type=F32)
    logits += lax.dot_general(rwh_ref[...], h_lo, nt, preferred_element_type=F32)
    logits += lax.dot_general(rwl_ref[...], h_hi, nt, preferred_element_type=F32)
    logits += rb_ref[...]
    yield

    eid_f = lax.broadcasted_iota(jnp.int32, (N_EXPERTS, tm), 0).astype(F32)
    work = logits
    vals, idxs, hits = [], [], []
    for _ in range(TOP_K):
        m = jnp.max(work, axis=0, keepdims=True)
        idx = jnp.min(jnp.where(work == m, eid_f, float(N_EXPERTS)), axis=0, keepdims=True)
        hit = eid_f == idx
        work = jnp.where(hit, -jnp.inf, work)
        vals.append(m)
        idxs.append(idx)
        hits.append(hit)
        yield
    exps = [jnp.exp(v - vals[0]) for v in vals]
    denom = exps[0] + exps[1] + exps[2] + exps[3]

    member = jnp.zeros((N_EXPERTS, tm), F32)
    for hit in hits:
        member = member + jnp.where(hit, 1.0, 0.0)
    s_idx = lax.broadcasted_iota(jnp.int32, (tm, tm), 0)
    t_idx = lax.broadcasted_iota(jnp.int32, (tm, tm), 1)
    earlier = jnp.where(s_idx < t_idx, 1.0, 0.0).astype(BF16)
    base = base_ref[...]
    before = jnp.dot(member.astype(BF16), earlier, preferred_element_type=F32) + base
    yield

    row8 = lax.broadcasted_iota(jnp.int32, (2 * TOP_K, tm), 0)
    route = jnp.zeros((2 * TOP_K, tm), F32)
    row128 = lax.broadcasted_iota(jnp.int32, (LANES, tm), 0)
    gate_t = jnp.zeros((LANES, tm), F32)
    for k in range(TOP_K):
        rank = jnp.sum(jnp.where(hits[k], before, 0.0), axis=0, keepdims=True)
        route = route + jnp.where(row8 == k, idxs[k], 0.0) + jnp.where(row8 == TOP_K + k, rank, 0.0)
        gate_t = gate_t + jnp.where(row128 == k, exps[k] / denom, 0.0)
    route_ref[...] = route.astype(jnp.int32)
    gate_ref[...] = gate_t.T
    total = base + jnp.sum(member, axis=1, keepdims=True)
    base_ref[...] = total
    cnt_ref[...] = total


def _out_proj_stage(zt, o3, g2d, xn2d, seq_len, norm_g6, w_out_bf16, ln_g, ln_b, rw_hi, rw_lo, rb_pad):
    t = o3.shape[1]
    n1 = zt.shape[1]
    tm = max(min(OUT_TM, seq_len), n1)
    nz = tm // n1
    row = lambda i: (i, 0)
    const = lambda i: (0, 0)

    def z_spec(k):
        return pl.BlockSpec((1, n1, FOURIER_WIDTH),
                            lambda i: ((i * tm) // seq_len, 0, ((i * tm) % seq_len) // n1 + k))

    in_specs = [z_spec(k) for k in range(nz)] + [
        pl.BlockSpec((HEADS, tm, HEAD_DIM), lambda i: (0, i, 0)),
        pl.BlockSpec((tm, HGRN_WIDTH), row),
        pl.BlockSpec((tm, D_MODEL), row),
        _resident((1, HGRN_WIDTH)),
        _resident((FOURIER_WIDTH, D_MODEL)),
        _resident((HGRN_WIDTH, D_MODEL)),
        _resident((1, D_MODEL)),
        _resident((1, D_MODEL)),
        _resident((N_EXPERTS, D_MODEL)),
        _resident((N_EXPERTS, D_MODEL)),
        _resident((N_EXPERTS, 1)),
    ]
    out_specs = [
        pl.BlockSpec((tm, D_MODEL), row),
        pl.BlockSpec((tm, HALF_D), row),
        pl.BlockSpec((2 * TOP_K, tm), lambda i: (0, i)),
        pl.BlockSpec((tm, LANES), row),
        pl.BlockSpec((N_EXPERTS, 1), const),
    ]
    out_shape = [
        jax.ShapeDtypeStruct((t, D_MODEL), F32),
        jax.ShapeDtypeStruct((t, HALF_D), jnp.uint32),
        jax.ShapeDtypeStruct((2 * TOP_K, t), jnp.int32),
        jax.ShapeDtypeStruct((t, LANES), F32),
        jax.ShapeDtypeStruct((N_EXPERTS, 1), F32),
    ]
    return _Stage(
        body=functools.partial(_out_proj_kernel, nz=nz),
        init=_zero_ref,
        grid=(t // tm,),
        in_specs=in_specs,
        out_specs=out_specs,
        out_shape=out_shape,
        scratch=[pltpu.VMEM((N_EXPERTS, 1), F32)],
        args=[zt] * nz + [o3, g2d, xn2d, norm_g6, w_out_bf16[:FOURIER_WIDTH], w_out_bf16[FOURIER_WIDTH:],
                          ln_g.reshape(1, -1), ln_b.reshape(1, -1), rw_hi, rw_lo, rb_pad],
        sequential=True,
        name="out_proj",
    )


COMBINE_TM = 512
COMBINE_PARTS = 2
SC_WINDOW = 128


def _sc_mesh():
    return plsc.VectorSubcoreMesh(core_axis_name="core", subcore_axis_name="subcore")


def _dispatch(dest_kt, hpk, n_slots):
    t, d = hpk.shape
    rows = pl.BlockSpec((SC_WINDOW, d), index_map=lambda i: (i, 0), pipeline_mode=pl.Buffered(1))
    idx = pl.BlockSpec((1, SC_WINDOW), index_map=lambda i: (0, i))

    @pl.kernel(out_type=jax.ShapeDtypeStruct((n_slots, d), hpk.dtype), mesh=_sc_mesh(), name="moe_dispatch_sc")
    def scatter(x_hbm, i0_hbm, i1_hbm, i2_hbm, i3_hbm, o_hbm):
        def body(x_vmem, *idx_vmem):
            for iv in idx_vmem:
                pltpu.sync_copy(x_vmem, o_hbm.at[iv.at[0]])

        pltpu.emit_pipeline(
            body, grid=(t // SC_WINDOW,), in_specs=[rows] + [idx] * TOP_K, out_specs=[],
            core_axis_name=("core", "subcore"), dimension_semantics=(pltpu.PARALLEL,),
        )(x_hbm, i0_hbm, i1_hbm, i2_hbm, i3_hbm)

    return scatter(hpk, *[dest_kt[k].reshape(1, t) for k in range(TOP_K)])


def _gather_rows(yb, dest_kt):
    _, t = dest_kt.shape
    d = yb.shape[1]
    n = TOP_K * t

    @pl.kernel(out_type=jax.ShapeDtypeStruct((n, d), yb.dtype), mesh=_sc_mesh(), name="moe_gather_sc")
    def gather(y_hbm, i_hbm, o_hbm):
        def body(i_vmem, o_vmem):
            pltpu.sync_copy(y_hbm.at[i_vmem.at[0]], o_vmem)

        pltpu.emit_pipeline(
            body, grid=(n // SC_WINDOW,),
            in_specs=[pl.BlockSpec((1, SC_WINDOW), index_map=lambda i: (0, i))],
            out_specs=[pl.BlockSpec((SC_WINDOW, d), index_map=lambda i: (i, 0), pipeline_mode=pl.Buffered(1))],
            core_axis_name=("core", "subcore"), dimension_semantics=(pltpu.PARALLEL,),
        )(i_hbm, o_hbm)

    return gather(yb, dest_kt.reshape(1, n)).reshape(TOP_K, t, d)


def _experts_kernel(be_ref, nused_ref, next_ref, halves_ref, xb_ref, wgu_hbm, bgu_ref, wdn_hbm, bdn_ref, yb_ref,
                    wgu_f32, wdn_f32, wgu_bf, wdn_bf, slot_ref, sem):
    i = pl.program_id(0)
    used = i < nused_ref[0]
    e = be_ref[i]

    def weight_copies(expert, slot):
        return (pltpu.make_async_copy(wgu_hbm.at[expert], wgu_f32.at[slot], sem.at[slot, 0]),
                pltpu.make_async_copy(wdn_hbm.at[expert], wdn_f32.at[slot], sem.at[slot, 1]))

    @pl.when(used & (i == 0))
    def _():
        slot_ref[0] = 0
        for c in weight_copies(e, 0):
            c.start()

    @pl.when(used & (i > 0) & (e != be_ref[jnp.maximum(i - 1, 0)]))
    def _():
        slot_ref[0] = 1 - slot_ref[0]

    @pl.when(used & ((i == 0) | (e != be_ref[jnp.maximum(i - 1, 0)])))
    def _():
        slot = slot_ref[0]
        for c in weight_copies(e, slot):
            c.wait()
        wgu_bf[...] = wgu_f32[slot].astype(BF16)
        wdn_bf[...] = wdn_f32[slot].astype(BF16)
        nxt = next_ref[e]

        @pl.when(nxt != e)
        def _():
            for c in weight_copies(nxt, 1 - slot):
                c.start()

    def mlp(rows):
        x_hi, x_lo = _unpack_bf16_pairs(xb_ref[:rows])
        x = jnp.concatenate([x_hi.astype(BF16), x_lo.astype(BF16)], axis=1)
        gu = jnp.dot(x, wgu_bf[...], preferred_element_type=F32) + bgu_ref[0]
        gate = jnp.minimum(gu[:, :D_FF], SWIGLU_LIMIT)
        up = jnp.clip(gu[:, D_FF:], -SWIGLU_LIMIT, SWIGLU_LIMIT)
        act = (up + 1.0) * gate * jax.nn.sigmoid(SWIGLU_ALPHA * gate)
        y = jnp.dot(act.astype(BF16), wdn_bf[...], preferred_element_type=F32) + bdn_ref[0]
        yb_ref[:rows] = _pack_bf16_pairs(y)

    for parts in range(1, MOE_PARTS + 1):
        pl.when(used & (halves_ref[i] == parts))(functools.partial(mlp, parts * (MOE_BLOCK // MOE_PARTS)))


def _experts(block_expert, n_used, next_expert, halves, xb, wgu, bgu, wdn, bdn):
    n_slots = xb.shape[0]
    nb = n_slots // MOE_BLOCK
    blk = pl.BlockSpec((MOE_BLOCK, HALF_D),
                       lambda i, be, nu, nx, hv: (jnp.minimum(i, jnp.maximum(nu[0] - 1, 0)), 0))
    per_expert = lambda shape: pl.BlockSpec((1,) + shape, lambda i, be, nu, nx, hv: (be[i], 0, 0))
    hbm = pl.BlockSpec(memory_space=pl.ANY)
    grid_spec = pltpu.PrefetchScalarGridSpec(
        num_scalar_prefetch=4,
        grid=(nb,),
        in_specs=[blk, hbm, per_expert((1, 2 * D_FF)), hbm, per_expert((1, D_MODEL))],
        out_specs=blk,
        scratch_shapes=[pltpu.VMEM((2, D_MODEL, 2 * D_FF), F32), pltpu.VMEM((2, D_FF, D_MODEL), F32),
                        pltpu.VMEM((D_MODEL, 2 * D_FF), BF16), pltpu.VMEM((D_FF, D_MODEL), BF16),
                        pltpu.SMEM((1,), jnp.int32), pltpu.SemaphoreType.DMA((2, 2))],
    )
    return pl.pallas_call(
        _experts_kernel,
        grid_spec=grid_spec,
        out_shape=jax.ShapeDtypeStruct((n_slots, HALF_D), jnp.uint32),
        compiler_params=_cparams("arbitrary"),
        name="moe_experts",
    )(block_expert, n_used, next_expert, halves, xb, wgu, bgu, wdn, bdn)


def _combine_kernel(rows_ref, gate_ref, h_ref, g_ref, b_ref, *rest):
    out_ref = rest[-1]
    tm = h_ref.shape[0]
    gate = gate_ref[...]
    ff_hi = jnp.zeros((tm, HALF_D), F32)
    ff_lo = jnp.zeros((tm, HALF_D), F32)
    for k in range(TOP_K):
        hi, lo = _unpack_bf16_pairs(rows_ref[k])
        gk = gate[:, k:k + 1]
        ff_hi = ff_hi + gk * hi
        ff_lo = ff_lo + gk * lo
    ff = jnp.concatenate([ff_hi, ff_lo], axis=1)
    out_ref[...] = _layer_norm_rows(DEEPNORM_ALPHA * h_ref[...] + ff, g_ref[...], b_ref[...])


def _combine(rows_kt, gates, h2d, ln_g, ln_b, token0, prev_out):
    t = h2d.shape[0]
    t_part = rows_kt.shape[1]
    tm = min(COMBINE_TM, t_part)
    first = token0 // tm
    row = lambda i: (first + i, 0)
    const = lambda i: (0, 0)
    in_specs = [
        pl.BlockSpec((TOP_K, tm, HALF_D), lambda i: (0, i, 0)),
        pl.BlockSpec((tm, LANES), row),
        pl.BlockSpec((tm, D_MODEL), row),
        pl.BlockSpec((1, D_MODEL), const),
        pl.BlockSpec((1, D_MODEL), const),
    ]
    args = [rows_kt, gates, h2d, ln_g.reshape(1, -1), ln_b.reshape(1, -1)]
    aliases = {}
    if prev_out is not None:
        in_specs.append(pl.BlockSpec(memory_space=pl.ANY))
        args.append(prev_out)
        aliases = {len(args) - 1: 0}
    return pl.pallas_call(
        _combine_kernel,
        grid=(t_part // tm,),
        in_specs=in_specs,
        out_specs=pl.BlockSpec((tm, D_MODEL), row),
        out_shape=jax.ShapeDtypeStruct((t, D_MODEL), F32),
        input_output_aliases=aliases,
        compiler_params=_cparams("parallel"),
        name="moe_combine",
    )(*args)


def _moe_plan(route, counts_f32, n_tokens):
    eid = route[:TOP_K]
    rank = route[TOP_K:]
    counts = counts_f32[:, 0].astype(jnp.int32)
    padded = ((counts + MOE_BLOCK - 1) // MOE_BLOCK) * MOE_BLOCK
    pend = jnp.cumsum(padded)
    pstart = pend - padded
    experts = jnp.arange(N_EXPERTS, dtype=jnp.int32)[:, None, None]
    dest = rank + jnp.sum(jnp.where(eid[None] == experts, pstart[:, None, None], 0), axis=0)
    n_assign = n_tokens * TOP_K
    n_slots = ((n_assign + MOE_BLOCK - 1) // MOE_BLOCK) * MOE_BLOCK + N_EXPERTS * MOE_BLOCK
    nb = n_slots // MOE_BLOCK
    block_start = jnp.arange(nb, dtype=jnp.int32) * MOE_BLOCK
    block_expert = jnp.sum((pend[None, :] <= block_start[:, None]).astype(jnp.int32), axis=1)
    block_expert = jnp.minimum(block_expert, N_EXPERTS - 1).astype(jnp.int32)
    n_used = (pend[-1:] // MOE_BLOCK).astype(jnp.int32)
    onehot = (block_expert[:, None] == jnp.arange(N_EXPERTS, dtype=jnp.int32)[None, :]).astype(jnp.int32)
    valid_rows = jnp.sum(onehot * (pstart + counts)[None, :], axis=1) - block_start
    part_rows = MOE_BLOCK // MOE_PARTS
    halves = jnp.clip((valid_rows + part_rows - 1) // part_rows, 1, MOE_PARTS).astype(jnp.int32)
    ids = jnp.arange(N_EXPERTS, dtype=jnp.int32)
    later_nonempty = (ids[None, :] > ids[:, None]) & (counts[None, :] > 0)
    next_expert = jnp.min(jnp.where(later_nonempty, ids[None, :], N_EXPERTS), axis=1)
    next_expert = jnp.where(next_expert == N_EXPERTS, ids, next_expert).astype(jnp.int32)
    return dest.astype(jnp.int32), block_expert, n_used, next_expert, halves, n_slots


def _in_stage(x, p):
    t = x.shape[0] * x.shape[1]
    return _in_proj_stage(x.reshape(t, D_MODEL), p["ln_in_g"], p["ln_in_b"], p["lb2"], p["w_in"])


def _mix_and_out_stage(x, in_outs, p):
    batch, seq_len, _ = x.shape
    t = batch * seq_len
    xn, u, qv, lf, kk, og = in_outs
    zt = _fourier_mix(u, batch, seq_len, p["fourier_norm_g"])
    per_seq = lambda a: a.reshape(a.shape[0], batch, seq_len, HEAD_DIM)
    o = _gla(per_seq(qv), per_seq(lf), per_seq(kk))
    return _out_proj_stage(
        zt, o.reshape(HEADS, t, HEAD_DIM), og, xn, seq_len, p["norm_g6"], p["w_out"], p["ln1_g"], p["ln1_b"],
        p["rw_hi"], p["rw_lo"], p["rb_pad"])


def _moe(x, out_outs, p):
    batch, seq_len, _ = x.shape
    t = batch * seq_len
    h, hpk, route, gates, counts = out_outs
    dest_kt, block_expert, n_used, next_expert, halves, n_slots = _moe_plan(route, counts, t)
    xb = _dispatch(dest_kt, hpk, n_slots)
    yb = _experts(block_expert, n_used, next_expert, halves, xb, p["w_gu"], p["b_gu"], p["w_dn"], p["b_dn"])
    y = None
    part = t // COMBINE_PARTS
    for j in range(COMBINE_PARTS):
        rows = _gather_rows(yb, dest_kt[:, j * part:(j + 1) * part])
        y = _combine(rows, gates, h, p["ln2_g"], p["ln2_b"], j * part, y)
    return y.reshape(batch, seq_len, D_MODEL)


def _prepare_params(ln_in_g, ln_in_b, w_in, fourier_norm_g, lb_gamma, hgrn_norm_g, w_out, ln1_g, ln1_b,
                    router_w, router_b, w_gate_up, b_gate_up, w_down, b_down, ln2_g, ln2_b):
    fw, hw = FOURIER_WIDTH, HGRN_WIDTH
    w0 = w_in[0]
    lb_all = jnp.cumsum(jax.nn.softmax(lb_gamma.astype(F32), axis=1), axis=1)
    rw = router_w[0].astype(F32).T
    rw_hi = rw.astype(BF16)
    rw_lo = (rw - rw_hi.astype(F32)).astype(BF16)
    del fw, hw
    return dict(
        ln_in_g=ln_in_g, ln_in_b=ln_in_b, w_in=w0.astype(BF16),
        fourier_norm_g=fourier_norm_g[0],
        lb2=jnp.concatenate([lb_all[0, 0], lb_all[1, 0]]).reshape(1, -1),
        norm_g6=jnp.tile(hgrn_norm_g[0].astype(F32), HEADS).reshape(1, -1),
        w_out=w_out[0].astype(BF16), ln1_g=ln1_g[0], ln1_b=ln1_b[0],
        rw_hi=rw_hi, rw_lo=rw_lo, rb_pad=router_b[0].astype(F32).reshape(-1, 1),
        w_gu=w_gate_up[0], b_gu=b_gate_up[0].reshape(N_EXPERTS, 1, -1),
        w_dn=w_down[0], b_dn=b_down[0].reshape(N_EXPERTS, 1, -1),
        ln2_g=ln2_g[0], ln2_b=ln2_b[0],
    )


def kernel(x_prompt, x_sample, ln_in_g, ln_in_b, w_in, fourier_norm_g, lb_gamma, hgrn_norm_g, w_out,
           ln1_g, ln1_b, router_w, router_b, w_gate_up, b_gate_up, w_down, b_down, ln2_g, ln2_b):
    p = _prepare_params(ln_in_g, ln_in_b, w_in, fourier_norm_g, lb_gamma, hgrn_norm_g, w_out, ln1_g, ln1_b,
                        router_w, router_b, w_gate_up, b_gate_up, w_down, b_down, ln2_g, ln2_b)
    first, second = x_sample, x_prompt
    (in_first,) = _run_stages([_in_stage(first, p)])
    out_stage_first = _mix_and_out_stage(first, in_first, p)
    in_stage_second = _in_stage(second, p)
    if in_stage_second.grid == out_stage_first.grid:
        in_second, out_first = _run_stages([in_stage_second, out_stage_first])
    else:
        (in_second,) = _run_stages([in_stage_second])
        (out_first,) = _run_stages([out_stage_first])
    (out_second,) = _run_stages([_mix_and_out_stage(second, in_second, p)])
    y_first = _moe(first, out_first, p)
    y_second = _moe(second, out_second, p)
    return (y_second, y_first)
```

```python
import functools
import math
from typing import Callable, NamedTuple

import numpy as np
import jax
import jax.numpy as jnp
from jax import lax
from jax.experimental import pallas as pl
from jax.experimental.pallas import tpu as pltpu
from jax.experimental.pallas import tpu_sc as plsc

D_MODEL = 1024
FOURIER_WIDTH = 256
FOURIER_GROUP_DIM = 64
HGRN_WIDTH = 768
HEAD_DIM = 128
HEADS = 6
CHUNK = 64
N_EXPERTS = 32
TOP_K = 4
D_FF = 1024
SWIGLU_LIMIT = 7.0
SWIGLU_ALPHA = 1.702
LN_EPS = 1e-5
RMS_EPS = 1e-6
DEEPNORM_ALPHA = 2.0 ** 0.25

LANES = 128
VMEM_LIMIT_BYTES = 56 * 1024 * 1024

F32 = jnp.float32
BF16 = jnp.bfloat16


def _cparams(*sem):
    return pltpu.CompilerParams(dimension_semantics=sem, vmem_limit_bytes=VMEM_LIMIT_BYTES)


def _layer_norm_rows(x, g, b):
    mu = jnp.mean(x, axis=-1, keepdims=True)
    xc = x - mu
    var = jnp.mean(xc * xc, axis=-1, keepdims=True)
    return xc * lax.rsqrt(var + LN_EPS) * g + b


IN_TM = 512
IN_TN = 512


def _in_proj_kernel(x_ref, g_ref, b_ref, lb_ref, w_ref, xn_ref, u_ref, qv_ref, lf_ref, kk_ref, og_ref):
    xn = _layer_norm_rows(x_ref[...], g_ref[...], b_ref[...])
    xn_ref[...] = xn
    xb = xn.astype(BF16)
    hw = HGRN_WIDTH
    yield

    def store_heads(ref, first_head, c0, val):
        for j in range(val.shape[1] // HEAD_DIM):
            ref[first_head + c0 // HEAD_DIM + j] = val[:, j * HEAD_DIM:(j + 1) * HEAD_DIM]

    def chunks(col, width):
        for c0 in range(0, width, IN_TN):
            cw = min(IN_TN, width - c0)
            yield c0, cw, jnp.dot(xb, w_ref[:, col + c0:col + c0 + cw], preferred_element_type=F32)

    for c0, cw, acc in chunks(0, FOURIER_WIDTH):
        u_ref[:, c0:c0 + cw] = acc.astype(BF16)
        yield
    for c0, cw, acc in chunks(FOURIER_WIDTH, hw):
        store_heads(qv_ref, 0, c0, (acc * jax.nn.sigmoid(acc) * (HEAD_DIM ** -0.5)).astype(BF16))
        yield
    for c0, cw, acc in chunks(FOURIER_WIDTH + hw, hw):
        store_heads(qv_ref, HEADS, c0, acc.astype(BF16))
        yield
    for c0, cw, acc in chunks(FOURIER_WIDTH + 2 * hw, 2 * hw):
        lb = lb_ref[:, c0:c0 + cw]
        fg = lb + (1.0 - lb) * jax.nn.sigmoid(acc)
        store_heads(lf_ref, 0, c0, jnp.log(fg))
        store_heads(kk_ref, 0, c0, (1.0 - fg).astype(BF16))
        yield
    for c0, cw, acc in chunks(FOURIER_WIDTH + 4 * hw, hw):
        og_ref[:, c0:c0 + cw] = acc.astype(BF16)
        yield


class _Stage(NamedTuple):
    body: Callable
    init: Callable | None
    grid: tuple
    in_specs: list
    out_specs: list
    out_shape: list
    scratch: list
    args: list
    sequential: bool
    name: str


_DONE = object()


def _interleave(pieces, stagger):
    done = [False] * len(pieces)
    t = 0
    while not all(done):
        for i, g in enumerate(pieces):
            if not done[i] and t >= i * stagger:
                done[i] = next(g, _DONE) is _DONE
        t += 1


def _run_stages(stages):
    assert all(s.grid == stages[0].grid for s in stages)
    n_in = [len(s.in_specs) for s in stages]
    n_out = [len(s.out_specs) for s in stages]
    n_scr = [len(s.scratch) for s in stages]

    def body(*refs):
        scr = sum(n_in) + sum(n_out)
        for s, c in zip(stages, n_scr):
            if s.init is not None:
                pl.when(pl.program_id(0) == 0)(functools.partial(s.init, *refs[scr:scr + c]))
            scr += c
        ins, outs, scr = 0, sum(n_in), sum(n_in) + sum(n_out)
        bodies = []
        for s, a, b, c in zip(stages, n_in, n_out, n_scr):
            bodies.append(s.body(*refs[ins:ins + a], *refs[outs:outs + b], *refs[scr:scr + c]))
            ins, outs, scr = ins + a, outs + b, scr + c
        _interleave(bodies, 0)

    results = pl.pallas_call(
        body,
        grid=stages[0].grid,
        in_specs=[sp for s in stages for sp in s.in_specs],
        out_specs=[sp for s in stages for sp in s.out_specs],
        out_shape=[sh for s in stages for sh in s.out_shape],
        scratch_shapes=[sc for s in stages for sc in s.scratch],
        compiler_params=_cparams("arbitrary" if any(s.sequential for s in stages) else "parallel"),
        name="_".join(s.name for s in stages),
    )(*[a for s in stages for a in s.args])
    split, pos = [], 0
    for b in n_out:
        split.append(results[pos:pos + b])
        pos += b
    return split


def _zero_ref(ref):
    ref[...] = jnp.zeros(ref.shape, ref.dtype)


def _resident(shape):
    return pl.BlockSpec(shape, lambda i: (0,) * len(shape), pipeline_mode=pl.Buffered(1))


def _in_proj_stage(x2d, ln_g, ln_b, lb2, w_bf16):
    t = x2d.shape[0]
    tm = min(IN_TM, t)
    row = lambda i: (i, 0)
    flat = lambda w, dt: (pl.BlockSpec((tm, w), row), jax.ShapeDtypeStruct((t, w), dt))
    head_major = lambda nh, dt: (pl.BlockSpec((nh, tm, HEAD_DIM), lambda i: (0, i, 0)),
                                 jax.ShapeDtypeStruct((nh, t, HEAD_DIM), dt))
    outs = [flat(D_MODEL, F32), flat(FOURIER_WIDTH, BF16), head_major(2 * HEADS, BF16),
            head_major(2 * HEADS, F32), head_major(2 * HEADS, BF16), flat(HGRN_WIDTH, BF16)]
    return _Stage(
        body=_in_proj_kernel,
        init=None,
        grid=(t // tm,),
        in_specs=[pl.BlockSpec((tm, D_MODEL), row), _resident((1, D_MODEL)), _resident((1, D_MODEL)),
                  _resident((1, 2 * HGRN_WIDTH)), _resident(w_bf16.shape)],
        out_specs=[o[0] for o in outs],
        out_shape=[o[1] for o in outs],
        scratch=[],
        args=[x2d, ln_g.reshape(1, -1), ln_b.reshape(1, -1), lb2, w_bf16],
        sequential=False,
        name="in_proj",
    )


def _fft_split(seq_len):
    n1 = 1 << ((seq_len.bit_length() - 1 + 1) // 2)
    return n1, seq_len // n1


@functools.lru_cache(maxsize=None)
def _fft_tables(seq_len):
    n1, n2 = _fft_split(seq_len)
    k1 = np.arange(n1)
    ang1 = 2.0 * np.pi * ((k1[:, None] * k1[None, :]) % n1) / n1
    s1 = 1.0 / math.sqrt(n1)
    c1, s1m = np.cos(ang1) * s1, np.sin(ang1) * s1
    l1p = np.arange(n1)[:, None, None]
    l2p = np.arange(n2)[None, :, None]
    l2 = np.arange(n2)[None, None, :]
    ang2 = 2.0 * np.pi * ((l2 * (l1p + n1 * l2p)) % seq_len) / seq_len
    s2 = 1.0 / math.sqrt(n2)
    gc, gs = np.cos(ang2) * s2, np.sin(ang2) * s2
    kc = np.arange(FOURIER_GROUP_DIM)
    angc = 2.0 * np.pi * ((kc[:, None] * kc[None, :]) % FOURIER_GROUP_DIM) / FOURIER_GROUP_DIM
    sc = 1.0 / math.sqrt(FOURIER_GROUP_DIM)
    groups = FOURIER_WIDTH // FOURIER_GROUP_DIM
    bc = np.kron(np.eye(groups), np.cos(angc) * sc)
    bs = np.kron(np.eye(groups), np.sin(angc) * sc)
    as_bf16 = lambda a: jnp.asarray(a, dtype=F32).astype(BF16)
    return tuple(as_bf16(a) for a in (c1, s1m, gc, gs, bc, bs))


def _fft1_kernel(c_ref, s_ref, u_ref, ar_ref, ai_ref):
    u = u_ref[0]
    ar_ref[0] = jnp.dot(c_ref[...], u, preferred_element_type=F32).astype(BF16)
    ai_ref[0] = (-jnp.dot(s_ref[...], u, preferred_element_type=F32)).astype(BF16)


FFT1_TN = 4096


def _fft_stage1(u3, c1, s1):
    b, n1, width = u3.shape
    tn = min(FFT1_TN, width)
    blk = pl.BlockSpec((1, n1, tn), lambda i, j: (i, 0, j))
    mat = pl.BlockSpec((n1, n1), lambda i, j: (0, 0))
    return pl.pallas_call(
        _fft1_kernel,
        grid=(b, width // tn),
        in_specs=[mat, mat, blk],
        out_specs=[blk, blk],
        out_shape=[jax.ShapeDtypeStruct(u3.shape, BF16)] * 2,
        compiler_params=_cparams("parallel", "parallel"),
        name="fft_stage1",
    )(c1, s1, u3)


FFT2_ROWS = 1024


def _fft2_kernel(gc_ref, gs_ref, bc_ref, bs_ref, g_ref, ar_ref, ai_ref, z_ref):
    t1, n2 = ar_ref.shape[1], ar_ref.shape[2]
    xr, xi = [], []
    for j in range(t1):
        ar, ai = ar_ref[0, j], ai_ref[0, j]
        gc, gs = gc_ref[j], gs_ref[j]
        xr.append((jnp.dot(gc, ar, preferred_element_type=F32)
                   + jnp.dot(gs, ai, preferred_element_type=F32)).astype(BF16))
        xi.append((jnp.dot(gc, ai, preferred_element_type=F32)
                   - jnp.dot(gs, ar, preferred_element_type=F32)).astype(BF16))
    z = jnp.dot(jnp.concatenate(xr, axis=0), bc_ref[...], preferred_element_type=F32)
    z += jnp.dot(jnp.concatenate(xi, axis=0), bs_ref[...], preferred_element_type=F32)
    z = (z * lax.rsqrt(jnp.mean(z * z, axis=-1, keepdims=True) + RMS_EPS) * g_ref[...]).astype(BF16)
    for j in range(t1):
        z_ref[0, j] = z[j * n2:(j + 1) * n2]


def _fft_stage2(ar4, ai4, gc, gs, bc, bs, gain):
    b, n1, n2, w = ar4.shape
    t1 = min(max(FFT2_ROWS // n2, 1), n1)
    a_blk = pl.BlockSpec((1, t1, n2, w), lambda i, j: (i, j, 0, 0))
    g_blk = pl.BlockSpec((t1, n2, n2), lambda i, j: (j, 0, 0))
    c_blk = pl.BlockSpec((w, w), lambda i, j: (0, 0))
    return pl.pallas_call(
        _fft2_kernel,
        grid=(b, n1 // t1),
        in_specs=[g_blk, g_blk, c_blk, c_blk, pl.BlockSpec((1, w), lambda i, j: (0, 0)), a_blk, a_blk],
        out_specs=a_blk,
        out_shape=jax.ShapeDtypeStruct(ar4.shape, BF16),
        compiler_params=_cparams("parallel", "parallel"),
        name="fft_stage2",
    )(gc, gs, bc, bs, gain.reshape(1, -1), ar4, ai4)


def _fourier_mix(u2d, batch, seq_len, gain):
    n1, n2 = _fft_split(seq_len)
    c1, s1, gc, gs, bc, bs = _fft_tables(seq_len)
    u3 = u2d.reshape(batch, n1, n2 * FOURIER_WIDTH)
    ar, ai = _fft_stage1(u3, c1, s1)
    shape4 = (batch, n1, n2, FOURIER_WIDTH)
    zt = _fft_stage2(ar.reshape(shape4), ai.reshape(shape4), gc, gs, bc, bs, gain)
    return zt.reshape(batch, n1, n2 * FOURIER_WIDTH)


GLA_LB = 1024
GLA_OUT_BLOCK_BYTES = 32 * 1024 * 1024
GLA_STAGGER = 6
CUMSUM_ROWS = 256


@functools.lru_cache(maxsize=None)
def _cumsum_matrices():
    r = np.arange(CUMSUM_ROWS)
    same_chunk = (r[:, None] // CHUNK) == (r[None, :] // CHUNK)
    prefix = same_chunk & (r[None, :] <= r[:, None])
    suffix = same_chunk & (r[None, :] >= r[:, None])
    return (jnp.asarray(prefix, dtype=F32).astype(BF16), jnp.asarray(suffix, dtype=F32).astype(BF16))


def _chunk_cumsum(x, tri):
    hi = x.astype(BF16)
    lo = (x - hi.astype(F32)).astype(BF16)
    width = x.shape[1]
    parts = []
    for r0 in range(0, x.shape[0], CUMSUM_ROWS):
        rows = slice(r0, r0 + CUMSUM_ROWS)
        both = jnp.dot(tri, jnp.concatenate([hi[rows], lo[rows]], axis=1), preferred_element_type=F32)
        parts.append(both[:, :width] + both[:, width:])
    return parts[0] if len(parts) == 1 else jnp.concatenate(parts, axis=0)


def _gla_direction(q_ref, v_ref, k_ref, lf_ref, tri, st_ref, o_ref, start, reverse):
    n = q_ref.shape[0]
    b = _chunk_cumsum(lf_ref[...], tri)
    yield
    mid = CHUNK // 2 if reverse else CHUNK // 2 - 1
    last = 0 if reverse else CHUNK - 1
    t_idx = lax.broadcasted_iota(jnp.int32, (CHUNK, CHUNK), 0)
    s_idx = lax.broadcasted_iota(jnp.int32, (CHUNK, CHUNK), 1)
    visible = (t_idx <= s_idx) if reverse else (t_idx >= s_idx)
    nchunks = n // CHUNK
    order = range(nchunks - 1, -1, -1) if reverse else range(nchunks)
    nt = (((1,), (1,)), ((), ()))
    tn = (((0,), (0,)), ((), ()))
    rows = [slice(c * CHUNK, (c + 1) * CHUNK) for c in range(nchunks)]
    qe, ke, e_mid, e_last, e_gap = [], [], [], [], []
    for c in range(nchunks):
        bc = b[rows[c]]
        b_mid = bc[mid:mid + 1]
        b_last = bc[last:last + 1]
        qe.append((q_ref[rows[c]].astype(F32) * jnp.exp(bc - b_mid)).astype(BF16))
        ke.append((k_ref[rows[c]].astype(F32) * jnp.exp(b_mid - bc)).astype(BF16))
        e_mid.append(jnp.exp(b_mid))
        e_last.append(jnp.exp(b_last))
        e_gap.append(jnp.exp(b_last - b_mid))
        yield
    scores, delta_t = [], []
    for c in range(nchunks):
        s = lax.dot_general(qe[c], ke[c], nt, preferred_element_type=F32)
        scores.append(jnp.where(visible, s, 0.0).astype(BF16))
        delta_t.append(lax.dot_general(v_ref[rows[c]], ke[c], tn, preferred_element_type=F32) * e_gap[c])
        yield
    st = st_ref[...]
    st_in = [None] * nchunks
    for c in order:
        st_in[c] = (st * e_mid[c]).astype(BF16)
        st = st * e_last[c] + delta_t[c]
    st_ref[...] = st
    yield
    for c in range(nchunks):
        o = jnp.dot(scores[c], v_ref[rows[c]], preferred_element_type=F32)
        o += lax.dot_general(qe[c], st_in[c], nt, preferred_element_type=F32)
        o_ref[pl.ds(start + c * CHUNK, CHUNK), :] += o
        yield


def _gla_kernel(trif_ref, trib_ref, qf_ref, vf_ref, kf_ref, lf_ref, qb_ref, vb_ref, kb_ref, lb_ref,
                o_ref, sf_ref, sb_ref):
    j = pl.program_id(2)
    nblk = pl.num_programs(2)
    lb_rows = qf_ref.shape[2]

    @pl.when(j == 0)
    def _():
        o_ref[...] = jnp.zeros(o_ref.shape, o_ref.dtype)
        sf_ref[...] = jnp.zeros(sf_ref.shape, sf_ref.dtype)
        sb_ref[...] = jnp.zeros(sb_ref.shape, sb_ref.dtype)

    start_f = pl.multiple_of(j * lb_rows, lb_rows)
    start_b = pl.multiple_of((nblk - 1 - j) * lb_rows, lb_rows)
    streams = []
    for hh in range(qf_ref.shape[0]):
        streams.append(_gla_direction(qf_ref.at[hh, 0], vf_ref.at[hh, 0], kf_ref.at[hh, 0], lf_ref.at[hh, 0],
                                      trif_ref[...], sf_ref.at[hh], o_ref.at[hh, 0], start_f, False))
        streams.append(_gla_direction(qb_ref.at[hh, 0], vb_ref.at[hh, 0], kb_ref.at[hh, 0], lb_ref.at[hh, 0],
                                      trib_ref[...], sb_ref.at[hh], o_ref.at[hh, 0], start_b, True))
    _interleave(streams, GLA_STAGGER)


def _gla(qv4, lf4, kk4):
    _, b, seq_len, _ = qv4.shape
    lbk = min(GLA_LB, seq_len)
    assert lbk % CUMSUM_ROWS == 0 and seq_len % lbk == 0
    nblk = seq_len // lbk
    nh = max(h for h in (1, 2) if 2 * h * seq_len * HEAD_DIM * 4 <= GLA_OUT_BLOCK_BYTES or h == 1)
    blk = lambda head0, rev: pl.BlockSpec(
        (nh, 1, lbk, HEAD_DIM),
        (lambda i, h, j: (head0 + h, i, nblk - 1 - j, 0)) if rev else (lambda i, h, j: (head0 + h, i, j, 0)))
    tri_spec = pl.BlockSpec((CUMSUM_ROWS, CUMSUM_ROWS), lambda i, h, j: (0, 0))
    tri_f, tri_b = _cumsum_matrices()
    groups = HEADS // nh
    return pl.pallas_call(
        _gla_kernel,
        grid=(b, groups, nblk),
        in_specs=[tri_spec, tri_spec,
                  blk(0, False), blk(groups, False), blk(0, False), blk(0, False),
                  blk(0, True), blk(groups, True), blk(groups, True), blk(groups, True)],
        out_specs=pl.BlockSpec((nh, 1, seq_len, HEAD_DIM), lambda i, h, j: (h, i, 0, 0)),
        out_shape=jax.ShapeDtypeStruct((HEADS, b, seq_len, HEAD_DIM), F32),
        scratch_shapes=[pltpu.VMEM((nh, HEAD_DIM, HEAD_DIM), F32)] * 2,
        compiler_params=_cparams("parallel", "parallel", "arbitrary"),
        name="gla",
    )(tri_f, tri_b, qv4, qv4, kk4, lf4, qv4, qv4, kk4, lf4)


HALF_D = D_MODEL // 2


def _pack_bf16_pairs(x):
    bits = lax.bitcast_convert_type(x.astype(BF16).astype(F32), jnp.uint32)
    return bits[:, :HALF_D] | (bits[:, HALF_D:] >> 16)


def _unpack_bf16_pairs(words):
    hi = lax.bitcast_convert_type(words & jnp.uint32(0xFFFF0000), F32)
    lo = lax.bitcast_convert_type(words << 16, F32)
    return hi, lo


OUT_TM = 512


def _out_proj_kernel(*refs, nz):
    z_refs = refs[:nz]
    (o_ref, g_ref, xn_ref, ng_ref, wz_ref, wh_ref, l1g_ref, l1b_ref, rwh_ref, rwl_ref, rb_ref,
     h_ref, hpk_ref, route_ref, gate_ref, cnt_ref, base_ref) = refs[nz:]
    tm = o_ref.shape[1]

    z = jnp.concatenate([r[0] for r in z_refs], axis=0) if nz > 1 else z_refs[0][0]
    g = g_ref[...].astype(F32)
    normed = []
    for hd in range(HEADS):
        oh = o_ref[hd]
        normed.append(oh * lax.rsqrt(jnp.mean(oh * oh, axis=-1, keepdims=True) + RMS_EPS))
        yield
    hg = jnp.concatenate(normed, axis=1) * ng_ref[...] * (g * jax.nn.sigmoid(g))
    yield
    mixed = jnp.dot(z, wz_ref[...], preferred_element_type=F32)
    mixed += jnp.dot(hg.astype(BF16), wh_ref[...], preferred_element_type=F32)
    yield
    h = _layer_norm_rows(DEEPNORM_ALPHA * xn_ref[...] + mixed, l1g_ref[...], l1b_ref[...])
    h_ref[...] = h
    yield
    hpk_ref[...] = _pack_bf16_pairs(h)
    yield

    nt = (((1,), (1,)), ((), ()))
    h_hi = h.astype(BF16)
    h_lo = (h - h_hi.astype(F32)).astype(BF16)
    logits = lax.dot_general(rwh_ref[...], h_hi, nt, preferred_element_type=F32)
    logits += lax.dot_general(rwh_ref[...], h_lo, nt, preferred_element_type=F32)
    logits += lax.dot_general(rwl_ref[...], h_hi, nt, preferred_element_type=F32)
    logits += rb_ref[...]
    yield

    eid_f = lax.broadcasted_iota(jnp.int32, (N_EXPERTS, tm), 0).astype(F32)
    work = logits
    vals, idxs, hits = [], [], []
    for _ in range(TOP_K):
        m = jnp.max(work, axis=0, keepdims=True)
        idx = jnp.min(jnp.where(work == m, eid_f, float(N_EXPERTS)), axis=0, keepdims=True)
        hit = eid_f == idx
        work = jnp.where(hit, -jnp.inf, work)
        vals.append(m)
        idxs.append(idx)
        hits.append(hit)
        yield
    exps = [jnp.exp(v - vals[0]) for v in vals]
    denom = exps[0] + exps[1] + exps[2] + exps[3]

    member = jnp.zeros((N_EXPERTS, tm), F32)
    for hit in hits:
        member = member + jnp.where(hit, 1.0, 0.0)
    s_idx = lax.broadcasted_iota(jnp.int32, (tm, tm), 0)
    t_idx = lax.broadcasted_iota(jnp.int32, (tm, tm), 1)
    earlier = jnp.where(s_idx < t_idx, 1.0, 0.0).astype(BF16)
    base = base_ref[...]
    before = jnp.dot(member.astype(BF16), earlier, preferred_element_type=F32) + base
    yield

    row8 = lax.broadcasted_iota(jnp.int32, (2 * TOP_K, tm), 0)
    route = jnp.zeros((2 * TOP_K, tm), F32)
    row128 = lax.broadcasted_iota(jnp.int32, (LANES, tm), 0)
    gate_t = jnp.zeros((LANES, tm), F32)
    for k in range(TOP_K):
        rank = jnp.sum(jnp.where(hits[k], before, 0.0), axis=0, keepdims=True)
        route = route + jnp.where(row8 == k, idxs[k], 0.0) + jnp.where(row8 == TOP_K + k, rank, 0.0)
        gate_t = gate_t + jnp.where(row128 == k, exps[k] / denom, 0.0)
    route_ref[...] = route.astype(jnp.int32)
    gate_ref[...] = gate_t.T
    total = base + jnp.sum(member, axis=1, keepdims=True)
    base_ref[...] = total
    cnt_ref[...] = total


def _out_proj_stage(zt, o3, g2d, xn2d, seq_len, norm_g6, w_out_bf16, ln_g, ln_b, rw_hi, rw_lo, rb_col):
    t = o3.shape[1]
    n1 = zt.shape[1]
    tm = max(min(OUT_TM, seq_len), n1)
    nz = tm // n1
    row = lambda i: (i, 0)
    const = lambda i: (0, 0)

    def z_spec(k):
        return pl.BlockSpec((1, n1, FOURIER_WIDTH),
                            lambda i: ((i * tm) // seq_len, 0, ((i * tm) % seq_len) // n1 + k))

    in_specs = [z_spec(k) for k in range(nz)] + [
        pl.BlockSpec((HEADS, tm, HEAD_DIM), lambda i: (0, i, 0)),
        pl.BlockSpec((tm, HGRN_WIDTH), row),
        pl.BlockSpec((tm, D_MODEL), row),
        _resident((1, HGRN_WIDTH)),
        _resident((FOURIER_WIDTH, D_MODEL)),
        _resident((HGRN_WIDTH, D_MODEL)),
        _resident((1, D_MODEL)),
        _resident((1, D_MODEL)),
        _resident((N_EXPERTS, D_MODEL)),
        _resident((N_EXPERTS, D_MODEL)),
        _resident((N_EXPERTS, 1)),
    ]
    out_specs = [
        pl.BlockSpec((tm, D_MODEL), row),
        pl.BlockSpec((tm, HALF_D), row),
        pl.BlockSpec((2 * TOP_K, tm), lambda i: (0, i)),
        pl.BlockSpec((tm, LANES), row),
        pl.BlockSpec((N_EXPERTS, 1), const),
    ]
    out_shape = [
        jax.ShapeDtypeStruct((t, D_MODEL), F32),
        jax.ShapeDtypeStruct((t, HALF_D), jnp.uint32),
        jax.ShapeDtypeStruct((2 * TOP_K, t), jnp.int32),
        jax.ShapeDtypeStruct((t, LANES), F32),
        jax.ShapeDtypeStruct((N_EXPERTS, 1), F32),
    ]
    return _Stage(
        body=functools.partial(_out_proj_kernel, nz=nz),
        init=_zero_ref,
        grid=(t // tm,),
        in_specs=in_specs,
        out_specs=out_specs,
        out_shape=out_shape,
        scratch=[pltpu.VMEM((N_EXPERTS, 1), F32)],
        args=[zt] * nz + [o3, g2d, xn2d, norm_g6, w_out_bf16[:FOURIER_WIDTH], w_out_bf16[FOURIER_WIDTH:],
                          ln_g.reshape(1, -1), ln_b.reshape(1, -1), rw_hi, rw_lo, rb_col],
        sequential=True,
        name="out_proj",
    )


MOE_BLOCK = 1024
MOE_PARTS = 4
COMBINE_TM = 512
COMBINE_PARTS = 2
SC_WINDOW = 128


def _sc_mesh():
    return plsc.VectorSubcoreMesh(core_axis_name="core", subcore_axis_name="subcore")


def _dispatch(dest_kt, hpk, n_slots):
    t, d = hpk.shape
    rows = pl.BlockSpec((SC_WINDOW, d), index_map=lambda i: (i, 0), pipeline_mode=pl.Buffered(1))
    idx = pl.BlockSpec((1, SC_WINDOW), index_map=lambda i: (0, i))

    @pl.kernel(out_type=jax.ShapeDtypeStruct((n_slots, d), hpk.dtype), mesh=_sc_mesh(), name="moe_dispatch_sc")
    def scatter(x_hbm, i0_hbm, i1_hbm, i2_hbm, i3_hbm, o_hbm):
        def body(x_vmem, *idx_vmem):
            for iv in idx_vmem:
                pltpu.sync_copy(x_vmem, o_hbm.at[iv.at[0]])

        pltpu.emit_pipeline(
            body, grid=(t // SC_WINDOW,), in_specs=[rows] + [idx] * TOP_K, out_specs=[],
            core_axis_name=("core", "subcore"), dimension_semantics=(pltpu.PARALLEL,),
        )(x_hbm, i0_hbm, i1_hbm, i2_hbm, i3_hbm)

    return scatter(hpk, *[dest_kt[k].reshape(1, t) for k in range(TOP_K)])


def _gather_rows(yb, dest_kt):
    _, t = dest_kt.shape
    d = yb.shape[1]
    n = TOP_K * t

    @pl.kernel(out_type=jax.ShapeDtypeStruct((n, d), yb.dtype), mesh=_sc_mesh(), name="moe_gather_sc")
    def gather(y_hbm, i_hbm, o_hbm):
        def body(i_vmem, o_vmem):
            pltpu.sync_copy(y_hbm.at[i_vmem.at[0]], o_vmem)

        pltpu.emit_pipeline(
            body, grid=(n // SC_WINDOW,),
            in_specs=[pl.BlockSpec((1, SC_WINDOW), index_map=lambda i: (0, i))],
            out_specs=[pl.BlockSpec((SC_WINDOW, d), index_map=lambda i: (i, 0), pipeline_mode=pl.Buffered(1))],
            core_axis_name=("core", "subcore"), dimension_semantics=(pltpu.PARALLEL,),
        )(i_hbm, o_hbm)

    return gather(yb, dest_kt.reshape(1, n)).reshape(TOP_K, t, d)


def _experts_kernel(be_ref, nused_ref, next_ref, parts_ref, xb_ref, wgu_hbm, bgu_ref, wdn_hbm, bdn_ref, yb_ref,
                    wgu_f32, wdn_f32, wgu_bf, wdn_bf, slot_ref, sem):
    i = pl.program_id(0)
    used = i < nused_ref[0]
    e = be_ref[i]

    def weight_copies(expert, slot):
        return (pltpu.make_async_copy(wgu_hbm.at[expert], wgu_f32.at[slot], sem.at[slot, 0]),
                pltpu.make_async_copy(wdn_hbm.at[expert], wdn_f32.at[slot], sem.at[slot, 1]))

    @pl.when(used & (i == 0))
    def _():
        slot_ref[0] = 0
        for c in weight_copies(e, 0):
            c.start()

    @pl.when(used & (i > 0) & (e != be_ref[jnp.maximum(i - 1, 0)]))
    def _():
        slot_ref[0] = 1 - slot_ref[0]

    @pl.when(used & ((i == 0) | (e != be_ref[jnp.maximum(i - 1, 0)])))
    def _():
        slot = slot_ref[0]
        for c in weight_copies(e, slot):
            c.wait()
        wgu_bf[...] = wgu_f32[slot].astype(BF16)
        wdn_bf[...] = wdn_f32[slot].astype(BF16)
        nxt = next_ref[e]

        @pl.when(nxt != e)
        def _():
            for c in weight_copies(nxt, 1 - slot):
                c.start()

    def mlp(rows):
        x_hi, x_lo = _unpack_bf16_pairs(xb_ref[:rows])
        x = jnp.concatenate([x_hi.astype(BF16), x_lo.astype(BF16)], axis=1)
        gu = jnp.dot(x, wgu_bf[...], preferred_element_type=F32) + bgu_ref[0]
        gate = jnp.minimum(gu[:, :D_FF], SWIGLU_LIMIT)
        up = jnp.clip(gu[:, D_FF:], -SWIGLU_LIMIT, SWIGLU_LIMIT)
        act = (up + 1.0) * gate * jax.nn.sigmoid(SWIGLU_ALPHA * gate)
        y = jnp.dot(act.astype(BF16), wdn_bf[...], preferred_element_type=F32) + bdn_ref[0]
        yb_ref[:rows] = _pack_bf16_pairs(y)

    for parts in range(1, MOE_PARTS + 1):
        pl.when(used & (parts_ref[i] == parts))(functools.partial(mlp, parts * (MOE_BLOCK // MOE_PARTS)))


def _experts(block_expert, n_used, next_expert, parts, xb, wgu, bgu, wdn, bdn):
    n_slots = xb.shape[0]
    nb = n_slots // MOE_BLOCK
    blk = pl.BlockSpec((MOE_BLOCK, HALF_D),
                       lambda i, be, nu, nx, hv: (jnp.minimum(i, jnp.maximum(nu[0] - 1, 0)), 0))
    per_expert = lambda shape: pl.BlockSpec((1,) + shape, lambda i, be, nu, nx, hv: (be[i], 0, 0))
    hbm = pl.BlockSpec(memory_space=pl.ANY)
    grid_spec = pltpu.PrefetchScalarGridSpec(
        num_scalar_prefetch=4,
        grid=(nb,),
        in_specs=[blk, hbm, per_expert((1, 2 * D_FF)), hbm, per_expert((1, D_MODEL))],
        out_specs=blk,
        scratch_shapes=[pltpu.VMEM((2, D_MODEL, 2 * D_FF), F32), pltpu.VMEM((2, D_FF, D_MODEL), F32),
                        pltpu.VMEM((D_MODEL, 2 * D_FF), BF16), pltpu.VMEM((D_FF, D_MODEL), BF16),
                        pltpu.SMEM((1,), jnp.int32), pltpu.SemaphoreType.DMA((2, 2))],
    )
    return pl.pallas_call(
        _experts_kernel,
        grid_spec=grid_spec,
        out_shape=jax.ShapeDtypeStruct((n_slots, HALF_D), jnp.uint32),
        compiler_params=_cparams("arbitrary"),
        name="moe_experts",
    )(block_expert, n_used, next_expert, parts, xb, wgu, bgu, wdn, bdn)


def _combine_kernel(rows_ref, gate_ref, h_ref, g_ref, b_ref, *rest):
    out_ref = rest[-1]
    tm = h_ref.shape[0]
    gate = gate_ref[...]
    ff_hi = jnp.zeros((tm, HALF_D), F32)
    ff_lo = jnp.zeros((tm, HALF_D), F32)
    for k in range(TOP_K):
        hi, lo = _unpack_bf16_pairs(rows_ref[k])
        gk = gate[:, k:k + 1]
        ff_hi = ff_hi + gk * hi
        ff_lo = ff_lo + gk * lo
    ff = jnp.concatenate([ff_hi, ff_lo], axis=1)
    out_ref[...] = _layer_norm_rows(DEEPNORM_ALPHA * h_ref[...] + ff, g_ref[...], b_ref[...])


def _combine(rows_kt, gates, h2d, ln_g, ln_b, token0, prev_out):
    t = h2d.shape[0]
    t_part = rows_kt.shape[1]
    tm = min(COMBINE_TM, t_part)
    first = token0 // tm
    row = lambda i: (first + i, 0)
    const = lambda i: (0, 0)
    in_specs = [
        pl.BlockSpec((TOP_K, tm, HALF_D), lambda i: (0, i, 0)),
        pl.BlockSpec((tm, LANES), row),
        pl.BlockSpec((tm, D_MODEL), row),
        pl.BlockSpec((1, D_MODEL), const),
        pl.BlockSpec((1, D_MODEL), const),
    ]
    args = [rows_kt, gates, h2d, ln_g.reshape(1, -1), ln_b.reshape(1, -1)]
    aliases = {}
    if prev_out is not None:
        in_specs.append(pl.BlockSpec(memory_space=pl.ANY))
        args.append(prev_out)
        aliases = {len(args) - 1: 0}
    return pl.pallas_call(
        _combine_kernel,
        grid=(t_part // tm,),
        in_specs=in_specs,
        out_specs=pl.BlockSpec((tm, D_MODEL), row),
        out_shape=jax.ShapeDtypeStruct((t, D_MODEL), F32),
        input_output_aliases=aliases,
        compiler_params=_cparams("parallel"),
        name="moe_combine",
    )(*args)


def _moe_plan(route, counts_f32, n_tokens):
    eid = route[:TOP_K]
    rank = route[TOP_K:]
    counts = counts_f32[:, 0].astype(jnp.int32)
    padded = ((counts + MOE_BLOCK - 1) // MOE_BLOCK) * MOE_BLOCK
    pend = jnp.cumsum(padded)
    pstart = pend - padded
    experts = jnp.arange(N_EXPERTS, dtype=jnp.int32)[:, None, None]
    dest = rank + jnp.sum(jnp.where(eid[None] == experts, pstart[:, None, None], 0), axis=0)
    n_assign = n_tokens * TOP_K
    n_slots = ((n_assign + MOE_BLOCK - 1) // MOE_BLOCK) * MOE_BLOCK + N_EXPERTS * MOE_BLOCK
    nb = n_slots // MOE_BLOCK
    block_start = jnp.arange(nb, dtype=jnp.int32) * MOE_BLOCK
    block_expert = jnp.sum((pend[None, :] <= block_start[:, None]).astype(jnp.int32), axis=1)
    block_expert = jnp.minimum(block_expert, N_EXPERTS - 1).astype(jnp.int32)
    n_used = (pend[-1:] // MOE_BLOCK).astype(jnp.int32)
    onehot = (block_expert[:, None] == jnp.arange(N_EXPERTS, dtype=jnp.int32)[None, :]).astype(jnp.int32)
    valid_rows = jnp.sum(onehot * (pstart + counts)[None, :], axis=1) - block_start
    part_rows = MOE_BLOCK // MOE_PARTS
    parts = jnp.clip((valid_rows + part_rows - 1) // part_rows, 1, MOE_PARTS).astype(jnp.int32)
    ids = jnp.arange(N_EXPERTS, dtype=jnp.int32)
    later_nonempty = (ids[None, :] > ids[:, None]) & (counts[None, :] > 0)
    next_expert = jnp.min(jnp.where(later_nonempty, ids[None, :], N_EXPERTS), axis=1)
    next_expert = jnp.where(next_expert == N_EXPERTS, ids, next_expert).astype(jnp.int32)
    return dest.astype(jnp.int32), block_expert, n_used, next_expert, parts, n_slots


def _in_stage(x, p):
    t = x.shape[0] * x.shape[1]
    return _in_proj_stage(x.reshape(t, D_MODEL), p["ln_in_g"], p["ln_in_b"], p["lb2"], p["w_in"])


def _mix_and_out_stage(x, in_outs, p):
    batch, seq_len, _ = x.shape
    t = batch * seq_len
    xn, u, qv, lf, kk, og = in_outs
    zt = _fourier_mix(u, batch, seq_len, p["fourier_norm_g"])
    per_seq = lambda a: a.reshape(a.shape[0], batch, seq_len, HEAD_DIM)
    o = _gla(per_seq(qv), per_seq(lf), per_seq(kk))
    return _out_proj_stage(
        zt, o.reshape(HEADS, t, HEAD_DIM), og, xn, seq_len, p["norm_g6"], p["w_out"], p["ln1_g"], p["ln1_b"],
        p["rw_hi"], p["rw_lo"], p["rb_col"])


def _moe(x, out_outs, p):
    batch, seq_len, _ = x.shape
    t = batch * seq_len
    h, hpk, route, gates, counts = out_outs
    dest_kt, block_expert, n_used, next_expert, parts, n_slots = _moe_plan(route, counts, t)
    xb = _dispatch(dest_kt, hpk, n_slots)
    yb = _experts(block_expert, n_used, next_expert, parts, xb, p["w_gu"], p["b_gu"], p["w_dn"], p["b_dn"])
    y = None
    part = t // COMBINE_PARTS
    for j in range(COMBINE_PARTS):
        rows = _gather_rows(yb, dest_kt[:, j * part:(j + 1) * part])
        y = _combine(rows, gates, h, p["ln2_g"], p["ln2_b"], j * part, y)
    return y.reshape(batch, seq_len, D_MODEL)


def _prepare_params(ln_in_g, ln_in_b, w_in, fourier_norm_g, lb_gamma, hgrn_norm_g, w_out, ln1_g, ln1_b,
                    router_w, router_b, w_gate_up, b_gate_up, w_down, b_down, ln2_g, ln2_b):
    lb_all = jnp.cumsum(jax.nn.softmax(lb_gamma.astype(F32), axis=1), axis=1)
    rw = router_w[0].astype(F32).T
    rw_hi = rw.astype(BF16)
    rw_lo = (rw - rw_hi.astype(F32)).astype(BF16)
    return dict(
        ln_in_g=ln_in_g, ln_in_b=ln_in_b, w_in=w_in[0].astype(BF16),
        fourier_norm_g=fourier_norm_g[0],
        lb2=jnp.concatenate([lb_all[0, 0], lb_all[1, 0]]).reshape(1, -1),
        norm_g6=jnp.tile(hgrn_norm_g[0].astype(F32), HEADS).reshape(1, -1),
        w_out=w_out[0].astype(BF16), ln1_g=ln1_g[0], ln1_b=ln1_b[0],
        rw_hi=rw_hi, rw_lo=rw_lo, rb_col=router_b[0].astype(F32).reshape(-1, 1),
        w_gu=w_gate_up[0], b_gu=b_gate_up[0].reshape(N_EXPERTS, 1, -1),
        w_dn=w_down[0], b_dn=b_down[0].reshape(N_EXPERTS, 1, -1),
        ln2_g=ln2_g[0], ln2_b=ln2_b[0],
    )


def kernel(x_prompt, x_sample, ln_in_g, ln_in_b, w_in, fourier_norm_g, lb_gamma, hgrn_norm_g, w_out,
           ln1_g, ln1_b, router_w, router_b, w_gate_up, b_gate_up, w_down, b_down, ln2_g, ln2_b):
    p = _prepare_params(ln_in_g, ln_in_b, w_in, fourier_norm_g, lb_gamma, hgrn_norm_g, w_out, ln1_g, ln1_b,
                        router_w, router_b, w_gate_up, b_gate_up, w_down, b_down, ln2_g, ln2_b)
    first, second = x_sample, x_prompt
    (in_first,) = _run_stages([_in_stage(first, p)])
    out_stage_first = _mix_and_out_stage(first, in_first, p)
    in_stage_second = _in_stage(second, p)
    if in_stage_second.grid == out_stage_first.grid:
        in_second, out_first = _run_stages([in_stage_second, out_stage_first])
    else:
        (in_second,) = _run_stages([in_stage_second])
        (out_first,) = _run_stages([out_stage_first])
    (out_second,) = _run_stages([_mix_and_out_stage(second, in_second, p)])
    y_first = _moe(first, out_first, p)
    y_second = _moe(second, out_second, p)
    return (y_second, y_first)
```

```python
import functools
import math
from typing import Callable, NamedTuple

import numpy as np
import jax
import jax.numpy as jnp
from jax import lax
from jax.experimental import pallas as pl
from jax.experimental.pallas import tpu as pltpu
from jax.experimental.pallas import tpu_sc as plsc

D_MODEL = 1024
FOURIER_WIDTH = 256
FOURIER_GROUP_DIM = 64
HGRN_WIDTH = 768
HEAD_DIM = 128
HEADS = 6
CHUNK = 64
N_EXPERTS = 32
TOP_K = 4
D_FF = 1024
SWIGLU_LIMIT = 7.0
SWIGLU_ALPHA = 1.702
LN_EPS = 1e-5
RMS_EPS = 1e-6
DEEPNORM_ALPHA = 2.0 ** 0.25

LANES = 128
VMEM_LIMIT_BYTES = 56 * 1024 * 1024

F32 = jnp.float32
BF16 = jnp.bfloat16


def _cparams(*sem):
    return pltpu.CompilerParams(dimension_semantics=sem, vmem_limit_bytes=VMEM_LIMIT_BYTES)


def _layer_norm_rows(x, g, b):
    mu = jnp.mean(x, axis=-1, keepdims=True)
    xc = x - mu
    var = jnp.mean(xc * xc, axis=-1, keepdims=True)
    return xc * lax.rsqrt(var + LN_EPS) * g + b


IN_TM = 512
IN_TN = 512


def _in_proj_kernel(x_ref, g_ref, b_ref, lb_ref, w_ref, xn_ref, u_ref, qv_ref, lf_ref, kk_ref, og_ref):
    xn = _layer_norm_rows(x_ref[...], g_ref[...], b_ref[...])
    xn_ref[...] = xn
    xb = xn.astype(BF16)
    hw = HGRN_WIDTH
    yield

    def store_heads(ref, first_head, c0, val):
        for j in range(val.shape[1] // HEAD_DIM):
            ref[first_head + c0 // HEAD_DIM + j] = val[:, j * HEAD_DIM:(j + 1) * HEAD_DIM]

    def chunks(col, width):
        for c0 in range(0, width, IN_TN):
            cw = min(IN_TN, width - c0)
            yield c0, cw, jnp.dot(xb, w_ref[:, col + c0:col + c0 + cw], preferred_element_type=F32)

    for c0, cw, acc in chunks(0, FOURIER_WIDTH):
        u_ref[:, c0:c0 + cw] = acc.astype(BF16)
        yield
    for c0, cw, acc in chunks(FOURIER_WIDTH, hw):
        store_heads(qv_ref, 0, c0, (acc * jax.nn.sigmoid(acc) * (HEAD_DIM ** -0.5)).astype(BF16))
        yield
    for c0, cw, acc in chunks(FOURIER_WIDTH + hw, hw):
        store_heads(qv_ref, HEADS, c0, acc.astype(BF16))
        yield
    for c0, cw, acc in chunks(FOURIER_WIDTH + 2 * hw, 2 * hw):
        lb = lb_ref[:, c0:c0 + cw]
        fg = lb + (1.0 - lb) * jax.nn.sigmoid(acc)
        store_heads(lf_ref, 0, c0, jnp.log(fg))
        store_heads(kk_ref, 0, c0, (1.0 - fg).astype(BF16))
        yield
    for c0, cw, acc in chunks(FOURIER_WIDTH + 4 * hw, hw):
        og_ref[:, c0:c0 + cw] = acc.astype(BF16)
        yield


class _Stage(NamedTuple):
    body: Callable
    init: Callable | None
    grid: tuple
    in_specs: list
    out_specs: list
    out_shape: list
    scratch: list
    args: list
    sequential: bool
    name: str


_DONE = object()


def _interleave(pieces, stagger):
    done = [False] * len(pieces)
    t = 0
    while not all(done):
        for i, g in enumerate(pieces):
            if not done[i] and t >= i * stagger:
                done[i] = next(g, _DONE) is _DONE
        t += 1


def _run_stages(stages):
    assert all(s.grid == stages[0].grid for s in stages)
    n_in = [len(s.in_specs) for s in stages]
    n_out = [len(s.out_specs) for s in stages]
    n_scr = [len(s.scratch) for s in stages]

    def body(*refs):
        scr = sum(n_in) + sum(n_out)
        for s, c in zip(stages, n_scr):
            if s.init is not None:
                pl.when(pl.program_id(0) == 0)(functools.partial(s.init, *refs[scr:scr + c]))
            scr += c
        ins, outs, scr = 0, sum(n_in), sum(n_in) + sum(n_out)
        bodies = []
        for s, a, b, c in zip(stages, n_in, n_out, n_scr):
            bodies.append(s.body(*refs[ins:ins + a], *refs[outs:outs + b], *refs[scr:scr + c]))
            ins, outs, scr = ins + a, outs + b, scr + c
        _interleave(bodies, 0)

    results = pl.pallas_call(
        body,
        grid=stages[0].grid,
        in_specs=[sp for s in stages for sp in s.in_specs],
        out_specs=[sp for s in stages for sp in s.out_specs],
        out_shape=[sh for s in stages for sh in s.out_shape],
        scratch_shapes=[sc for s in stages for sc in s.scratch],
        compiler_params=_cparams("arbitrary" if any(s.sequential for s in stages) else "parallel"),
        name="_".join(s.name for s in stages),
    )(*[a for s in stages for a in s.args])
    split, pos = [], 0
    for b in n_out:
        split.append(results[pos:pos + b])
        pos += b
    return split


def _zero_ref(ref):
    ref[...] = jnp.zeros(ref.shape, ref.dtype)


def _resident(shape):
    return pl.BlockSpec(shape, lambda i: (0,) * len(shape), pipeline_mode=pl.Buffered(1))


def _in_proj_stage(x2d, ln_g, ln_b, lb2, w_bf16):
    t = x2d.shape[0]
    tm = min(IN_TM, t)
    row = lambda i: (i, 0)
    flat = lambda w, dt: (pl.BlockSpec((tm, w), row), jax.ShapeDtypeStruct((t, w), dt))
    head_major = lambda nh, dt: (pl.BlockSpec((nh, tm, HEAD_DIM), lambda i: (0, i, 0)),
                                 jax.ShapeDtypeStruct((nh, t, HEAD_DIM), dt))
    outs = [flat(D_MODEL, F32), flat(FOURIER_WIDTH, BF16), head_major(2 * HEADS, BF16),
            head_major(2 * HEADS, F32), head_major(2 * HEADS, BF16), flat(HGRN_WIDTH, BF16)]
    return _Stage(
        body=_in_proj_kernel,
        init=None,
        grid=(t // tm,),
        in_specs=[pl.BlockSpec((tm, D_MODEL), row), _resident((1, D_MODEL)), _resident((1, D_MODEL)),
                  _resident((1, 2 * HGRN_WIDTH)), _resident(w_bf16.shape)],
        out_specs=[o[0] for o in outs],
        out_shape=[o[1] for o in outs],
        scratch=[],
        args=[x2d, ln_g.reshape(1, -1), ln_b.reshape(1, -1), lb2, w_bf16],
        sequential=False,
        name="in_proj",
    )


def _fft_split(seq_len):
    n1 = 1 << ((seq_len.bit_length() - 1 + 1) // 2)
    return n1, seq_len // n1


@functools.lru_cache(maxsize=None)
def _fft_tables(seq_len):
    n1, n2 = _fft_split(seq_len)
    k1 = np.arange(n1)
    ang1 = 2.0 * np.pi * ((k1[:, None] * k1[None, :]) % n1) / n1
    s1 = 1.0 / math.sqrt(n1)
    c1, s1m = np.cos(ang1) * s1, np.sin(ang1) * s1
    l1p = np.arange(n1)[:, None, None]
    l2p = np.arange(n2)[None, :, None]
    l2 = np.arange(n2)[None, None, :]
    ang2 = 2.0 * np.pi * ((l2 * (l1p + n1 * l2p)) % seq_len) / seq_len
    s2 = 1.0 / math.sqrt(n2)
    gc, gs = np.cos(ang2) * s2, np.sin(ang2) * s2
    kc = np.arange(FOURIER_GROUP_DIM)
    angc = 2.0 * np.pi * ((kc[:, None] * kc[None, :]) % FOURIER_GROUP_DIM) / FOURIER_GROUP_DIM
    sc = 1.0 / math.sqrt(FOURIER_GROUP_DIM)
    groups = FOURIER_WIDTH // FOURIER_GROUP_DIM
    bc = np.kron(np.eye(groups), np.cos(angc) * sc)
    bs = np.kron(np.eye(groups), np.sin(angc) * sc)
    as_bf16 = lambda a: jnp.asarray(a, dtype=F32).astype(BF16)
    return tuple(as_bf16(a) for a in (c1, s1m, gc, gs, bc, bs))


def _fft1_kernel(c_ref, s_ref, u_ref, ar_ref, ai_ref):
    u = u_ref[0]
    ar_ref[0] = jnp.dot(c_ref[...], u, preferred_element_type=F32).astype(BF16)
    ai_ref[0] = (-jnp.dot(s_ref[...], u, preferred_element_type=F32)).astype(BF16)


FFT1_TN = 4096


def _fft_stage1(u3, c1, s1):
    b, n1, width = u3.shape
    tn = min(FFT1_TN, width)
    blk = pl.BlockSpec((1, n1, tn), lambda i, j: (i, 0, j))
    mat = pl.BlockSpec((n1, n1), lambda i, j: (0, 0))
    return pl.pallas_call(
        _fft1_kernel,
        grid=(b, width // tn),
        in_specs=[mat, mat, blk],
        out_specs=[blk, blk],
        out_shape=[jax.ShapeDtypeStruct(u3.shape, BF16)] * 2,
        compiler_params=_cparams("parallel", "parallel"),
        name="fft_stage1",
    )(c1, s1, u3)


FFT2_ROWS = 1024


def _fft2_kernel(gc_ref, gs_ref, bc_ref, bs_ref, g_ref, ar_ref, ai_ref, z_ref):
    t1, n2 = ar_ref.shape[1], ar_ref.shape[2]
    xr, xi = [], []
    for j in range(t1):
        ar, ai = ar_ref[0, j], ai_ref[0, j]
        gc, gs = gc_ref[j], gs_ref[j]
        xr.append((jnp.dot(gc, ar, preferred_element_type=F32)
                   + jnp.dot(gs, ai, preferred_element_type=F32)).astype(BF16))
        xi.append((jnp.dot(gc, ai, preferred_element_type=F32)
                   - jnp.dot(gs, ar, preferred_element_type=F32)).astype(BF16))
    z = jnp.dot(jnp.concatenate(xr, axis=0), bc_ref[...], preferred_element_type=F32)
    z += jnp.dot(jnp.concatenate(xi, axis=0), bs_ref[...], preferred_element_type=F32)
    z = (z * lax.rsqrt(jnp.mean(z * z, axis=-1, keepdims=True) + RMS_EPS) * g_ref[...]).astype(BF16)
    for j in range(t1):
        z_ref[0, j] = z[j * n2:(j + 1) * n2]


def _fft_stage2(ar4, ai4, gc, gs, bc, bs, gain):
    b, n1, n2, w = ar4.shape
    t1 = min(max(FFT2_ROWS // n2, 1), n1)
    a_blk = pl.BlockSpec((1, t1, n2, w), lambda i, j: (i, j, 0, 0))
    g_blk = pl.BlockSpec((t1, n2, n2), lambda i, j: (j, 0, 0))
    c_blk = pl.BlockSpec((w, w), lambda i, j: (0, 0))
    return pl.pallas_call(
        _fft2_kernel,
        grid=(b, n1 // t1),
        in_specs=[g_blk, g_blk, c_blk, c_blk, pl.BlockSpec((1, w), lambda i, j: (0, 0)), a_blk, a_blk],
        out_specs=a_blk,
        out_shape=jax.ShapeDtypeStruct(ar4.shape, BF16),
        compiler_params=_cparams("parallel", "parallel"),
        name="fft_stage2",
    )(gc, gs, bc, bs, gain.reshape(1, -1), ar4, ai4)


def _fourier_mix(u2d, batch, seq_len, gain):
    n1, n2 = _fft_split(seq_len)
    c1, s1, gc, gs, bc, bs = _fft_tables(seq_len)
    u3 = u2d.reshape(batch, n1, n2 * FOURIER_WIDTH)
    ar, ai = _fft_stage1(u3, c1, s1)
    shape4 = (batch, n1, n2, FOURIER_WIDTH)
    zt = _fft_stage2(ar.reshape(shape4), ai.reshape(shape4), gc, gs, bc, bs, gain)
    return zt.reshape(batch, n1, n2 * FOURIER_WIDTH)


GLA_LB = 1024
GLA_OUT_BLOCK_BYTES = 32 * 1024 * 1024
GLA_STAGGER = 6
CUMSUM_ROWS = 256


@functools.lru_cache(maxsize=None)
def _cumsum_matrices():
    r = np.arange(CUMSUM_ROWS)
    same_chunk = (r[:, None] // CHUNK) == (r[None, :] // CHUNK)
    prefix = same_chunk & (r[None, :] <= r[:, None])
    suffix = same_chunk & (r[None, :] >= r[:, None])
    return (jnp.asarray(prefix, dtype=F32).astype(BF16), jnp.asarray(suffix, dtype=F32).astype(BF16))


def _chunk_cumsum(x, tri):
    hi = x.astype(BF16)
    lo = (x - hi.astype(F32)).astype(BF16)
    width = x.shape[1]
    parts = []
    for r0 in range(0, x.shape[0], CUMSUM_ROWS):
        rows = slice(r0, r0 + CUMSUM_ROWS)
        both = jnp.dot(tri, jnp.concatenate([hi[rows], lo[rows]], axis=1), preferred_element_type=F32)
        parts.append(both[:, :width] + both[:, width:])
    return parts[0] if len(parts) == 1 else jnp.concatenate(parts, axis=0)


def _gla_direction(q_ref, v_ref, k_ref, lf_ref, tri, st_ref, o_ref, start, reverse):
    n = q_ref.shape[0]
    b = _chunk_cumsum(lf_ref[...], tri)
    yield
    mid = CHUNK // 2 if reverse else CHUNK // 2 - 1
    last = 0 if reverse else CHUNK - 1
    t_idx = lax.broadcasted_iota(jnp.int32, (CHUNK, CHUNK), 0)
    s_idx = lax.broadcasted_iota(jnp.int32, (CHUNK, CHUNK), 1)
    visible = (t_idx <= s_idx) if reverse else (t_idx >= s_idx)
    nchunks = n // CHUNK
    order = range(nchunks - 1, -1, -1) if reverse else range(nchunks)
    nt = (((1,), (1,)), ((), ()))
    tn = (((0,), (0,)), ((), ()))
    rows = [slice(c * CHUNK, (c + 1) * CHUNK) for c in range(nchunks)]
    qe, ke, e_mid, e_last, e_gap = [], [], [], [], []
    for c in range(nchunks):
        bc = b[rows[c]]
        b_mid = bc[mid:mid + 1]
        b_last = bc[last:last + 1]
        qe.append((q_ref[rows[c]].astype(F32) * jnp.exp(bc - b_mid)).astype(BF16))
        ke.append((k_ref[rows[c]].astype(F32) * jnp.exp(b_mid - bc)).astype(BF16))
        e_mid.append(jnp.exp(b_mid))
        e_last.append(jnp.exp(b_last))
        e_gap.append(jnp.exp(b_last - b_mid))
        yield
    scores, delta_t = [], []
    for c in range(nchunks):
        s = lax.dot_general(qe[c], ke[c], nt, preferred_element_type=F32)
        scores.append(jnp.where(visible, s, 0.0).astype(BF16))
        delta_t.append(lax.dot_general(v_ref[rows[c]], ke[c], tn, preferred_element_type=F32) * e_gap[c])
        yield
    st = st_ref[...]
    st_in = [None] * nchunks
    for c in order:
        st_in[c] = (st * e_mid[c]).astype(BF16)
        st = st * e_last[c] + delta_t[c]
    st_ref[...] = st
    yield
    for c in range(nchunks):
        o = jnp.dot(scores[c], v_ref[rows[c]], preferred_element_type=F32)
        o += lax.dot_general(qe[c], st_in[c], nt, preferred_element_type=F32)
        o_ref[pl.ds(start + c * CHUNK, CHUNK), :] += o
        yield


def _gla_kernel(trif_ref, trib_ref, qf_ref, vf_ref, kf_ref, lf_ref, qb_ref, vb_ref, kb_ref, lb_ref,
                o_ref, sf_ref, sb_ref):
    j = pl.program_id(2)
    nblk = pl.num_programs(2)
    lb_rows = qf_ref.shape[2]

    @pl.when(j == 0)
    def _():
        o_ref[...] = jnp.zeros(o_ref.shape, o_ref.dtype)
        sf_ref[...] = jnp.zeros(sf_ref.shape, sf_ref.dtype)
        sb_ref[...] = jnp.zeros(sb_ref.shape, sb_ref.dtype)

    start_f = pl.multiple_of(j * lb_rows, lb_rows)
    start_b = pl.multiple_of((nblk - 1 - j) * lb_rows, lb_rows)
    streams = []
    for hh in range(qf_ref.shape[0]):
        streams.append(_gla_direction(qf_ref.at[hh, 0], vf_ref.at[hh, 0], kf_ref.at[hh, 0], lf_ref.at[hh, 0],
                                      trif_ref[...], sf_ref.at[hh], o_ref.at[hh, 0], start_f, False))
        streams.append(_gla_direction(qb_ref.at[hh, 0], vb_ref.at[hh, 0], kb_ref.at[hh, 0], lb_ref.at[hh, 0],
                                      trib_ref[...], sb_ref.at[hh], o_ref.at[hh, 0], start_b, True))
    _interleave(streams, GLA_STAGGER)


def _gla(qv4, lf4, kk4):
    _, b, seq_len, _ = qv4.shape
    lbk = min(GLA_LB, seq_len)
    assert lbk % CUMSUM_ROWS == 0 and seq_len % lbk == 0
    nblk = seq_len // lbk
    nh = max(h for h in (1, 2) if 2 * h * seq_len * HEAD_DIM * 4 <= GLA_OUT_BLOCK_BYTES or h == 1)
    blk = lambda head0, rev: pl.BlockSpec(
        (nh, 1, lbk, HEAD_DIM),
        (lambda i, h, j: (head0 + h, i, nblk - 1 - j, 0)) if rev else (lambda i, h, j: (head0 + h, i, j, 0)))
    tri_spec = pl.BlockSpec((CUMSUM_ROWS, CUMSUM_ROWS), lambda i, h, j: (0, 0))
    tri_f, tri_b = _cumsum_matrices()
    groups = HEADS // nh
    return pl.pallas_call(
        _gla_kernel,
        grid=(b, groups, nblk),
        in_specs=[tri_spec, tri_spec,
                  blk(0, False), blk(groups, False), blk(0, False), blk(0, False),
                  blk(0, True), blk(groups, True), blk(groups, True), blk(groups, True)],
        out_specs=pl.BlockSpec((nh, 1, seq_len, HEAD_DIM), lambda i, h, j: (h, i, 0, 0)),
        out_shape=jax.ShapeDtypeStruct((HEADS, b, seq_len, HEAD_DIM), F32),
        scratch_shapes=[pltpu.VMEM((nh, HEAD_DIM, HEAD_DIM), F32)] * 2,
        compiler_params=_cparams("parallel", "parallel", "arbitrary"),
        name="gla",
    )(tri_f, tri_b, qv4, qv4, kk4, lf4, qv4, qv4, kk4, lf4)


HALF_D = D_MODEL // 2


def _pack_bf16_pairs(x):
    bits = lax.bitcast_convert_type(x.astype(BF16).astype(F32), jnp.uint32)
    return bits[:, :HALF_D] | (bits[:, HALF_D:] >> 16)


def _unpack_bf16_pairs(words):
    hi = lax.bitcast_convert_type(words & jnp.uint32(0xFFFF0000), F32)
    lo = lax.bitcast_convert_type(words << 16, F32)
    return hi, lo


OUT_TM = 512


def _out_proj_kernel(*refs, nz):
    z_refs = refs[:nz]
    (o_ref, g_ref, xn_ref, ng_ref, wz_ref, wh_ref, l1g_ref, l1b_ref, rwh_ref, rwl_ref, rb_ref,
     h_ref, hpk_ref, route_ref, gate_ref, cnt_ref, base_ref) = refs[nz:]
    tm = o_ref.shape[1]

    z = jnp.concatenate([r[0] for r in z_refs], axis=0) if nz > 1 else z_refs[0][0]
    g = g_ref[...].astype(F32)
    normed = []
    for hd in range(HEADS):
        oh = o_ref[hd]
        normed.append(oh * lax.rsqrt(jnp.mean(oh * oh, axis=-1, keepdims=True) + RMS_EPS))
        yield
    hg = jnp.concatenate(normed, axis=1) * ng_ref[...] * (g * jax.nn.sigmoid(g))
    yield
    mixed = jnp.dot(z, wz_ref[...], preferred_element_type=F32)
    mixed += jnp.dot(hg.astype(BF16), wh_ref[...], preferred_element_type=F32)
    yield
    h = _layer_norm_rows(DEEPNORM_ALPHA * xn_ref[...] + mixed, l1g_ref[...], l1b_ref[...])
    h_ref[...] = h
    yield
    hpk_ref[...] = _pack_bf16_pairs(h)
    yield

    nt = (((1,), (1,)), ((), ()))
    h_hi = h.astype(BF16)
    h_lo = (h - h_hi.astype(F32)).astype(BF16)
    logits = lax.dot_general(rwh_ref[...], h_hi, nt, preferred_element_type=F32)
    logits += lax.dot_general(rwh_ref[...], h_lo, nt, preferred_element_type=F32)
    logits += lax.dot_general(rwl_ref[...], h_hi, nt, preferred_element_type=F32)
    logits += rb_ref[...]
    yield

    eid_f = lax.broadcasted_iota(jnp.int32, (N_EXPERTS, tm), 0).astype(F32)
    work = logits
    vals, idxs, hits = [], [], []
    for _ in range(TOP_K):
        m = jnp.max(work, axis=0, keepdims=True)
        idx = jnp.min(jnp.where(work == m, eid_f, float(N_EXPERTS)), axis=0, keepdims=True)
        hit = eid_f == idx
        work = jnp.where(hit, -jnp.inf, work)
        vals.append(m)
        idxs.append(idx)
        hits.append(hit)
        yield
    exps = [jnp.exp(v - vals[0]) for v in vals]
    denom = exps[0] + exps[1] + exps[2] + exps[3]

    member = jnp.zeros((N_EXPERTS, tm), F32)
    for hit in hits:
        member = member + jnp.where(hit, 1.0, 0.0)
    s_idx = lax.broadcasted_iota(jnp.int32, (tm, tm), 0)
    t_idx = lax.broadcasted_iota(jnp.int32, (tm, tm), 1)
    earlier = jnp.where(s_idx < t_idx, 1.0, 0.0).astype(BF16)
    base = base_ref[...]
    before = jnp.dot(member.astype(BF16), earlier, preferred_element_type=F32) + base
    yield

    row8 = lax.broadcasted_iota(jnp.int32, (2 * TOP_K, tm), 0)
    route = jnp.zeros((2 * TOP_K, tm), F32)
    row128 = lax.broadcasted_iota(jnp.int32, (LANES, tm), 0)
    gate_t = jnp.zeros((LANES, tm), F32)
    for k in range(TOP_K):
        rank = jnp.sum(jnp.where(hits[k], before, 0.0), axis=0, keepdims=True)
        route = route + jnp.where(row8 == k, idxs[k], 0.0) + jnp.where(row8 == TOP_K + k, rank, 0.0)
        gate_t = gate_t + jnp.where(row128 == k, exps[k] / denom, 0.0)
    route_ref[...] = route.astype(jnp.int32)
    gate_ref[...] = gate_t.T
    total = base + jnp.sum(member, axis=1, keepdims=True)
    base_ref[...] = total
    cnt_ref[...] = total


def _out_proj_stage(zt, o3, g2d, xn2d, seq_len, norm_g6, w_out_bf16, ln_g, ln_b, rw_hi, rw_lo, rb_col):
    t = o3.shape[1]
    n1 = zt.shape[1]
    tm = max(min(OUT_TM, seq_len), n1)
    nz = tm // n1
    row = lambda i: (i, 0)
    const = lambda i: (0, 0)

    def z_spec(k):
        return pl.BlockSpec((1, n1, FOURIER_WIDTH),
                            lambda i: ((i * tm) // seq_len, 0, ((i * tm) % seq_len) // n1 + k))

    in_specs = [z_spec(k) for k in range(nz)] + [
        pl.BlockSpec((HEADS, tm, HEAD_DIM), lambda i: (0, i, 0)),
        pl.BlockSpec((tm, HGRN_WIDTH), row),
        pl.BlockSpec((tm, D_MODEL), row),
        _resident((1, HGRN_WIDTH)),
        _resident((FOURIER_WIDTH, D_MODEL)),
        _resident((HGRN_WIDTH, D_MODEL)),
        _resident((1, D_MODEL)),
        _resident((1, D_MODEL)),
        _resident((N_EXPERTS, D_MODEL)),
        _resident((N_EXPERTS, D_MODEL)),
        _resident((N_EXPERTS, 1)),
    ]
    out_specs = [
        pl.BlockSpec((tm, D_MODEL), row),
        pl.BlockSpec((tm, HALF_D), row),
        pl.BlockSpec((2 * TOP_K, tm), lambda i: (0, i)),
        pl.BlockSpec((tm, LANES), row),
        pl.BlockSpec((N_EXPERTS, 1), const),
    ]
    out_shape = [
        jax.ShapeDtypeStruct((t, D_MODEL), F32),
        jax.ShapeDtypeStruct((t, HALF_D), jnp.uint32),
        jax.ShapeDtypeStruct((2 * TOP_K, t), jnp.int32),
        jax.ShapeDtypeStruct((t, LANES), F32),
        jax.ShapeDtypeStruct((N_EXPERTS, 1), F32),
    ]
    return _Stage(
        body=functools.partial(_out_proj_kernel, nz=nz),
        init=_zero_ref,
        grid=(t // tm,),
        in_specs=in_specs,
        out_specs=out_specs,
        out_shape=out_shape,
        scratch=[pltpu.VMEM((N_EXPERTS, 1), F32)],
        args=[zt] * nz + [o3, g2d, xn2d, norm_g6, w_out_bf16[:FOURIER_WIDTH], w_out_bf16[FOURIER_WIDTH:],
                          ln_g.reshape(1, -1), ln_b.reshape(1, -1), rw_hi, rw_lo, rb_col],
        sequential=True,
        name="out_proj",
    )


MOE_BLOCK = 1024
MOE_PARTS = 8
COMBINE_TM = 512
COMBINE_PARTS = 2
SC_WINDOW = 128


def _sc_mesh():
    return plsc.VectorSubcoreMesh(core_axis_name="core", subcore_axis_name="subcore")


def _dispatch(dest_kt, hpk, n_slots):
    t, d = hpk.shape
    rows = pl.BlockSpec((SC_WINDOW, d), index_map=lambda i: (i, 0), pipeline_mode=pl.Buffered(1))
    idx = pl.BlockSpec((1, SC_WINDOW), index_map=lambda i: (0, i))

    @pl.kernel(out_type=jax.ShapeDtypeStruct((n_slots, d), hpk.dtype), mesh=_sc_mesh(), name="moe_dispatch_sc")
    def scatter(x_hbm, i0_hbm, i1_hbm, i2_hbm, i3_hbm, o_hbm):
        def body(x_vmem, *idx_vmem):
            for iv in idx_vmem:
                pltpu.sync_copy(x_vmem, o_hbm.at[iv.at[0]])

        pltpu.emit_pipeline(
            body, grid=(t // SC_WINDOW,), in_specs=[rows] + [idx] * TOP_K, out_specs=[],
            core_axis_name=("core", "subcore"), dimension_semantics=(pltpu.PARALLEL,),
        )(x_hbm, i0_hbm, i1_hbm, i2_hbm, i3_hbm)

    return scatter(hpk, *[dest_kt[k].reshape(1, t) for k in range(TOP_K)])


def _gather_rows(yb, dest_kt):
    _, t = dest_kt.shape
    d = yb.shape[1]
    n = TOP_K * t

    @pl.kernel(out_type=jax.ShapeDtypeStruct((n, d), yb.dtype), mesh=_sc_mesh(), name="moe_gather_sc")
    def gather(y_hbm, i_hbm, o_hbm):
        def body(i_vmem, o_vmem):
            pltpu.sync_copy(y_hbm.at[i_vmem.at[0]], o_vmem)

        pltpu.emit_pipeline(
            body, grid=(n // SC_WINDOW,),
            in_specs=[pl.BlockSpec((1, SC_WINDOW), index_map=lambda i: (0, i))],
            out_specs=[pl.BlockSpec((SC_WINDOW, d), index_map=lambda i: (i, 0), pipeline_mode=pl.Buffered(1))],
            core_axis_name=("core", "subcore"), dimension_semantics=(pltpu.PARALLEL,),
        )(i_hbm, o_hbm)

    return gather(yb, dest_kt.reshape(1, n)).reshape(TOP_K, t, d)


def _experts_kernel(be_ref, nused_ref, next_ref, parts_ref, xb_ref, wgu_hbm, bgu_ref, wdn_hbm, bdn_ref, yb_ref,
                    wgu_f32, wdn_f32, wgu_bf, wdn_bf, slot_ref, sem):
    i = pl.program_id(0)
    used = i < nused_ref[0]
    e = be_ref[i]

    def weight_copies(expert, slot):
        return (pltpu.make_async_copy(wgu_hbm.at[expert], wgu_f32.at[slot], sem.at[slot, 0]),
                pltpu.make_async_copy(wdn_hbm.at[expert], wdn_f32.at[slot], sem.at[slot, 1]))

    @pl.when(used & (i == 0))
    def _():
        slot_ref[0] = 0
        for c in weight_copies(e, 0):
            c.start()

    @pl.when(used & (i > 0) & (e != be_ref[jnp.maximum(i - 1, 0)]))
    def _():
        slot_ref[0] = 1 - slot_ref[0]

    @pl.when(used & ((i == 0) | (e != be_ref[jnp.maximum(i - 1, 0)])))
    def _():
        slot = slot_ref[0]
        for c in weight_copies(e, slot):
            c.wait()
        wgu_bf[...] = wgu_f32[slot].astype(BF16)
        wdn_bf[...] = wdn_f32[slot].astype(BF16)
        nxt = next_ref[e]

        @pl.when(nxt != e)
        def _():
            for c in weight_copies(nxt, 1 - slot):
                c.start()

    def mlp(rows):
        x_hi, x_lo = _unpack_bf16_pairs(xb_ref[:rows])
        x = jnp.concatenate([x_hi.astype(BF16), x_lo.astype(BF16)], axis=1)
        gu = jnp.dot(x, wgu_bf[...], preferred_element_type=F32) + bgu_ref[0]
        gate = jnp.minimum(gu[:, :D_FF], SWIGLU_LIMIT)
        up = jnp.clip(gu[:, D_FF:], -SWIGLU_LIMIT, SWIGLU_LIMIT)
        act = (up + 1.0) * gate * jax.nn.sigmoid(SWIGLU_ALPHA * gate)
        y = jnp.dot(act.astype(BF16), wdn_bf[...], preferred_element_type=F32) + bdn_ref[0]
        yb_ref[:rows] = _pack_bf16_pairs(y)

    for parts in range(1, MOE_PARTS + 1):
        pl.when(used & (parts_ref[i] == parts))(functools.partial(mlp, parts * (MOE_BLOCK // MOE_PARTS)))


def _experts(block_expert, n_used, next_expert, parts, xb, wgu, bgu, wdn, bdn):
    n_slots = xb.shape[0]
    nb = n_slots // MOE_BLOCK
    blk = pl.BlockSpec((MOE_BLOCK, HALF_D),
                       lambda i, be, nu, nx, hv: (jnp.minimum(i, jnp.maximum(nu[0] - 1, 0)), 0))
    per_expert = lambda shape: pl.BlockSpec((1,) + shape, lambda i, be, nu, nx, hv: (be[i], 0, 0))
    hbm = pl.BlockSpec(memory_space=pl.ANY)
    grid_spec = pltpu.PrefetchScalarGridSpec(
        num_scalar_prefetch=4,
        grid=(nb,),
        in_specs=[blk, hbm, per_expert((1, 2 * D_FF)), hbm, per_expert((1, D_MODEL))],
        out_specs=blk,
        scratch_shapes=[pltpu.VMEM((2, D_MODEL, 2 * D_FF), F32), pltpu.VMEM((2, D_FF, D_MODEL), F32),
                        pltpu.VMEM((D_MODEL, 2 * D_FF), BF16), pltpu.VMEM((D_FF, D_MODEL), BF16),
                        pltpu.SMEM((1,), jnp.int32), pltpu.SemaphoreType.DMA((2, 2))],
    )
    return pl.pallas_call(
        _experts_kernel,
        grid_spec=grid_spec,
        out_shape=jax.ShapeDtypeStruct((n_slots, HALF_D), jnp.uint32),
        compiler_params=_cparams("arbitrary"),
        name="moe_experts",
    )(block_expert, n_used, next_expert, parts, xb, wgu, bgu, wdn, bdn)


def _combine_kernel(rows_ref, gate_ref, h_ref, g_ref, b_ref, *rest):
    out_ref = rest[-1]
    tm = h_ref.shape[0]
    gate = gate_ref[...]
    ff_hi = jnp.zeros((tm, HALF_D), F32)
    ff_lo = jnp.zeros((tm, HALF_D), F32)
    for k in range(TOP_K):
        hi, lo = _unpack_bf16_pairs(rows_ref[k])
        gk = gate[:, k:k + 1]
        ff_hi = ff_hi + gk * hi
        ff_lo = ff_lo + gk * lo
    ff = jnp.concatenate([ff_hi, ff_lo], axis=1)
    out_ref[...] = _layer_norm_rows(DEEPNORM_ALPHA * h_ref[...] + ff, g_ref[...], b_ref[...])


def _combine(rows_kt, gates, h2d, ln_g, ln_b, token0, prev_out):
    t = h2d.shape[0]
    t_part = rows_kt.shape[1]
    tm = min(COMBINE_TM, t_part)
    first = token0 // tm
    row = lambda i: (first + i, 0)
    const = lambda i: (0, 0)
    in_specs = [
        pl.BlockSpec((TOP_K, tm, HALF_D), lambda i: (0, i, 0)),
        pl.BlockSpec((tm, LANES), row),
        pl.BlockSpec((tm, D_MODEL), row),
        pl.BlockSpec((1, D_MODEL), const),
        pl.BlockSpec((1, D_MODEL), const),
    ]
    args = [rows_kt, gates, h2d, ln_g.reshape(1, -1), ln_b.reshape(1, -1)]
    aliases = {}
    if prev_out is not None:
        in_specs.append(pl.BlockSpec(memory_space=pl.ANY))
        args.append(prev_out)
        aliases = {len(args) - 1: 0}
    return pl.pallas_call(
        _combine_kernel,
        grid=(t_part // tm,),
        in_specs=in_specs,
        out_specs=pl.BlockSpec((tm, D_MODEL), row),
        out_shape=jax.ShapeDtypeStruct((t, D_MODEL), F32),
        input_output_aliases=aliases,
        compiler_params=_cparams("parallel"),
        name="moe_combine",
    )(*args)


def _moe_plan(route, counts_f32, n_tokens):
    eid = route[:TOP_K]
    rank = route[TOP_K:]
    counts = counts_f32[:, 0].astype(jnp.int32)
    padded = ((counts + MOE_BLOCK - 1) // MOE_BLOCK) * MOE_BLOCK
    pend = jnp.cumsum(padded)
    pstart = pend - padded
    experts = jnp.arange(N_EXPERTS, dtype=jnp.int32)[:, None, None]
    dest = rank + jnp.sum(jnp.where(eid[None] == experts, pstart[:, None, None], 0), axis=0)
    n_assign = n_tokens * TOP_K
    n_slots = ((n_assign + MOE_BLOCK - 1) // MOE_BLOCK) * MOE_BLOCK + N_EXPERTS * MOE_BLOCK
    nb = n_slots // MOE_BLOCK
    block_start = jnp.arange(nb, dtype=jnp.int32) * MOE_BLOCK
    block_expert = jnp.sum((pend[None, :] <= block_start[:, None]).astype(jnp.int32), axis=1)
    block_expert = jnp.minimum(block_expert, N_EXPERTS - 1).astype(jnp.int32)
    n_used = (pend[-1:] // MOE_BLOCK).astype(jnp.int32)
    onehot = (block_expert[:, None] == jnp.arange(N_EXPERTS, dtype=jnp.int32)[None, :]).astype(jnp.int32)
    valid_rows = jnp.sum(onehot * (pstart + counts)[None, :], axis=1) - block_start
    part_rows = MOE_BLOCK // MOE_PARTS
    parts = jnp.clip((valid_rows + part_rows - 1) // part_rows, 1, MOE_PARTS).astype(jnp.int32)
    ids = jnp.arange(N_EXPERTS, dtype=jnp.int32)
    later_nonempty = (ids[None, :] > ids[:, None]) & (counts[None, :] > 0)
    next_expert = jnp.min(jnp.where(later_nonempty, ids[None, :], N_EXPERTS), axis=1)
    next_expert = jnp.where(next_expert == N_EXPERTS, ids, next_expert).astype(jnp.int32)
    return dest.astype(jnp.int32), block_expert, n_used, next_expert, parts, n_slots


def _in_stage(x, p):
    t = x.shape[0] * x.shape[1]
    return _in_proj_stage(x.reshape(t, D_MODEL), p["ln_in_g"], p["ln_in_b"], p["lb2"], p["w_in"])


def _mix_and_out_stage(x, in_outs, p):
    batch, seq_len, _ = x.shape
    t = batch * seq_len
    xn, u, qv, lf, kk, og = in_outs
    zt = _fourier_mix(u, batch, seq_len, p["fourier_norm_g"])
    per_seq = lambda a: a.reshape(a.shape[0], batch, seq_len, HEAD_DIM)
    o = _gla(per_seq(qv), per_seq(lf), per_seq(kk))
    return _out_proj_stage(
        zt, o.reshape(HEADS, t, HEAD_DIM), og, xn, seq_len, p["norm_g6"], p["w_out"], p["ln1_g"], p["ln1_b"],
        p["rw_hi"], p["rw_lo"], p["rb_col"])


def _moe(x, out_outs, p):
    batch, seq_len, _ = x.shape
    t = batch * seq_len
    h, hpk, route, gates, counts = out_outs
    dest_kt, block_expert, n_used, next_expert, parts, n_slots = _moe_plan(route, counts, t)
    xb = _dispatch(dest_kt, hpk, n_slots)
    yb = _experts(block_expert, n_used, next_expert, parts, xb, p["w_gu"], p["b_gu"], p["w_dn"], p["b_dn"])
    y = None
    part = t // COMBINE_PARTS
    for j in range(COMBINE_PARTS):
        rows = _gather_rows(yb, dest_kt[:, j * part:(j + 1) * part])
        y = _combine(rows, gates, h, p["ln2_g"], p["ln2_b"], j * part, y)
    return y.reshape(batch, seq_len, D_MODEL)


def _prepare_params(ln_in_g, ln_in_b, w_in, fourier_norm_g, lb_gamma, hgrn_norm_g, w_out, ln1_g, ln1_b,
                    router_w, router_b, w_gate_up, b_gate_up, w_down, b_down, ln2_g, ln2_b):
    lb_all = jnp.cumsum(jax.nn.softmax(lb_gamma.astype(F32), axis=1), axis=1)
    rw = router_w[0].astype(F32).T
    rw_hi = rw.astype(BF16)
    rw_lo = (rw - rw_hi.astype(F32)).astype(BF16)
    return dict(
        ln_in_g=ln_in_g, ln_in_b=ln_in_b, w_in=w_in[0].astype(BF16),
        fourier_norm_g=fourier_norm_g[0],
        lb2=jnp.concatenate([lb_all[0, 0], lb_all[1, 0]]).reshape(1, -1),
        norm_g6=jnp.tile(hgrn_norm_g[0].astype(F32), HEADS).reshape(1, -1),
        w_out=w_out[0].astype(BF16), ln1_g=ln1_g[0], ln1_b=ln1_b[0],
        rw_hi=rw_hi, rw_lo=rw_lo, rb_col=router_b[0].astype(F32).reshape(-1, 1),
        w_gu=w_gate_up[0], b_gu=b_gate_up[0].reshape(N_EXPERTS, 1, -1),
        w_dn=w_down[0], b_dn=b_down[0].reshape(N_EXPERTS, 1, -1),
        ln2_g=ln2_g[0], ln2_b=ln2_b[0],
    )


def kernel(x_prompt, x_sample, ln_in_g, ln_in_b, w_in, fourier_norm_g, lb_gamma, hgrn_norm_g, w_out,
           ln1_g, ln1_b, router_w, router_b, w_gate_up, b_gate_up, w_down, b_down, ln2_g, ln2_b):
    p = _prepare_params(ln_in_g, ln_in_b, w_in, fourier_norm_g, lb_gamma, hgrn_norm_g, w_out, ln1_g, ln1_b,
                        router_w, router_b, w_gate_up, b_gate_up, w_down, b_down, ln2_g, ln2_b)
    first, second = x_sample, x_prompt
    (in_first,) = _run_stages([_in_stage(first, p)])
    out_stage_first = _mix_and_out_stage(first, in_first, p)
    in_stage_second = _in_stage(second, p)
    if in_stage_second.grid == out_stage_first.grid:
        in_second, out_first = _run_stages([in_stage_second, out_stage_first])
    else:
        (in_second,) = _run_stages([in_stage_second])
        (out_first,) = _run_stages([out_stage_first])
    (out_second,) = _run_stages([_mix_and_out_stage(second, in_second, p)])
    y_first = _moe(first, out_first, p)
    y_second = _moe(second, out_second, p)
    return (y_second, y_first)
```

```python
import functools
import math
from typing import Callable, NamedTuple

import numpy as np
import jax
import jax.numpy as jnp
from jax import lax
from jax.experimental import pallas as pl
from jax.experimental.pallas import tpu as pltpu
from jax.experimental.pallas import tpu_sc as plsc

D_MODEL = 1024
FOURIER_WIDTH = 256
FOURIER_GROUP_DIM = 64
HGRN_WIDTH = 768
HEAD_DIM = 128
HEADS = 6
CHUNK = 64
N_EXPERTS = 32
TOP_K = 4
D_FF = 1024
SWIGLU_LIMIT = 7.0
SWIGLU_ALPHA = 1.702
LN_EPS = 1e-5
RMS_EPS = 1e-6
DEEPNORM_ALPHA = 2.0 ** 0.25

LANES = 128
VMEM_LIMIT_BYTES = 56 * 1024 * 1024

F32 = jnp.float32
BF16 = jnp.bfloat16


def _cparams(*sem):
    return pltpu.CompilerParams(dimension_semantics=sem, vmem_limit_bytes=VMEM_LIMIT_BYTES)


def _layer_norm_rows(x, g, b):
    mu = jnp.mean(x, axis=-1, keepdims=True)
    xc = x - mu
    var = jnp.mean(xc * xc, axis=-1, keepdims=True)
    return xc * lax.rsqrt(var + LN_EPS) * g + b


IN_TM = 512
IN_TN = 512


def _in_proj_kernel(x_ref, g_ref, b_ref, lb_ref, w_ref, u_ref, qv_ref, lf_ref, kk_ref, og_ref):
    xb = _layer_norm_rows(x_ref[...], g_ref[...], b_ref[...]).astype(BF16)
    hw = HGRN_WIDTH
    yield

    def store_heads(ref, first_head, c0, val):
        for j in range(val.shape[1] // HEAD_DIM):
            ref[first_head + c0 // HEAD_DIM + j] = val[:, j * HEAD_DIM:(j + 1) * HEAD_DIM]

    def chunks(col, width):
        for c0 in range(0, width, IN_TN):
            cw = min(IN_TN, width - c0)
            yield c0, cw, jnp.dot(xb, w_ref[:, col + c0:col + c0 + cw], preferred_element_type=F32)

    for c0, cw, acc in chunks(0, FOURIER_WIDTH):
        u_ref[:, c0:c0 + cw] = acc.astype(BF16)
        yield
    for c0, cw, acc in chunks(FOURIER_WIDTH, hw):
        store_heads(qv_ref, 0, c0, (acc * jax.nn.sigmoid(acc) * (HEAD_DIM ** -0.5)).astype(BF16))
        yield
    for c0, cw, acc in chunks(FOURIER_WIDTH + hw, hw):
        store_heads(qv_ref, HEADS, c0, acc.astype(BF16))
        yield
    for c0, cw, acc in chunks(FOURIER_WIDTH + 2 * hw, 2 * hw):
        lb = lb_ref[:, c0:c0 + cw]
        fg = lb + (1.0 - lb) * jax.nn.sigmoid(acc)
        store_heads(lf_ref, 0, c0, jnp.log(fg))
        store_heads(kk_ref, 0, c0, (1.0 - fg).astype(BF16))
        yield
    for c0, cw, acc in chunks(FOURIER_WIDTH + 4 * hw, hw):
        og_ref[:, c0:c0 + cw] = acc.astype(BF16)
        yield


class _Stage(NamedTuple):
    body: Callable
    init: Callable | None
    grid: tuple
    in_specs: list
    out_specs: list
    out_shape: list
    scratch: list
    args: list
    sequential: bool
    name: str


_DONE = object()


def _interleave(pieces, stagger):
    done = [False] * len(pieces)
    t = 0
    while not all(done):
        for i, g in enumerate(pieces):
            if not done[i] and t >= i * stagger:
                done[i] = next(g, _DONE) is _DONE
        t += 1


def _run_stages(stages):
    assert all(s.grid == stages[0].grid for s in stages)
    n_in = [len(s.in_specs) for s in stages]
    n_out = [len(s.out_specs) for s in stages]
    n_scr = [len(s.scratch) for s in stages]

    def body(*refs):
        scr = sum(n_in) + sum(n_out)
        for s, c in zip(stages, n_scr):
            if s.init is not None:
                pl.when(pl.program_id(0) == 0)(functools.partial(s.init, *refs[scr:scr + c]))
            scr += c
        ins, outs, scr = 0, sum(n_in), sum(n_in) + sum(n_out)
        bodies = []
        for s, a, b, c in zip(stages, n_in, n_out, n_scr):
            bodies.append(s.body(*refs[ins:ins + a], *refs[outs:outs + b], *refs[scr:scr + c]))
            ins, outs, scr = ins + a, outs + b, scr + c
        _interleave(bodies, 0)

    results = pl.pallas_call(
        body,
        grid=stages[0].grid,
        in_specs=[sp for s in stages for sp in s.in_specs],
        out_specs=[sp for s in stages for sp in s.out_specs],
        out_shape=[sh for s in stages for sh in s.out_shape],
        scratch_shapes=[sc for s in stages for sc in s.scratch],
        compiler_params=_cparams("arbitrary" if any(s.sequential for s in stages) else "parallel"),
        name="_".join(s.name for s in stages),
    )(*[a for s in stages for a in s.args])
    split, pos = [], 0
    for b in n_out:
        split.append(results[pos:pos + b])
        pos += b
    return split


def _zero_ref(ref):
    ref[...] = jnp.zeros(ref.shape, ref.dtype)


def _resident(shape):
    return pl.BlockSpec(shape, lambda i: (0,) * len(shape), pipeline_mode=pl.Buffered(1))


def _in_proj_stage(x2d, ln_g, ln_b, lb2, w_bf16):
    t = x2d.shape[0]
    tm = min(IN_TM, t)
    row = lambda i: (i, 0)
    flat = lambda w, dt: (pl.BlockSpec((tm, w), row), jax.ShapeDtypeStruct((t, w), dt))
    head_major = lambda nh, dt: (pl.BlockSpec((nh, tm, HEAD_DIM), lambda i: (0, i, 0)),
                                 jax.ShapeDtypeStruct((nh, t, HEAD_DIM), dt))
    outs = [flat(FOURIER_WIDTH, BF16), head_major(2 * HEADS, BF16),
            head_major(2 * HEADS, F32), head_major(2 * HEADS, BF16), flat(HGRN_WIDTH, BF16)]
    return _Stage(
        body=_in_proj_kernel,
        init=None,
        grid=(t // tm,),
        in_specs=[pl.BlockSpec((tm, D_MODEL), row), _resident((1, D_MODEL)), _resident((1, D_MODEL)),
                  _resident((1, 2 * HGRN_WIDTH)), _resident(w_bf16.shape)],
        out_specs=[o[0] for o in outs],
        out_shape=[o[1] for o in outs],
        scratch=[],
        args=[x2d, ln_g.reshape(1, -1), ln_b.reshape(1, -1), lb2, w_bf16],
        sequential=False,
        name="in_proj",
    )


def _fft_split(seq_len):
    n1 = 1 << ((seq_len.bit_length() - 1 + 1) // 2)
    return n1, seq_len // n1


@functools.lru_cache(maxsize=None)
def _fft_tables(seq_len):
    n1, n2 = _fft_split(seq_len)
    k1 = np.arange(n1)
    ang1 = 2.0 * np.pi * ((k1[:, None] * k1[None, :]) % n1) / n1
    s1 = 1.0 / math.sqrt(n1)
    c1, s1m = np.cos(ang1) * s1, np.sin(ang1) * s1
    l1p = np.arange(n1)[:, None, None]
    l2p = np.arange(n2)[None, :, None]
    l2 = np.arange(n2)[None, None, :]
    ang2 = 2.0 * np.pi * ((l2 * (l1p + n1 * l2p)) % seq_len) / seq_len
    s2 = 1.0 / math.sqrt(n2)
    gc, gs = np.cos(ang2) * s2, np.sin(ang2) * s2
    kc = np.arange(FOURIER_GROUP_DIM)
    angc = 2.0 * np.pi * ((kc[:, None] * kc[None, :]) % FOURIER_GROUP_DIM) / FOURIER_GROUP_DIM
    sc = 1.0 / math.sqrt(FOURIER_GROUP_DIM)
    groups = FOURIER_WIDTH // FOURIER_GROUP_DIM
    bc = np.kron(np.eye(groups), np.cos(angc) * sc)
    bs = np.kron(np.eye(groups), np.sin(angc) * sc)
    as_bf16 = lambda a: jnp.asarray(a, dtype=F32).astype(BF16)
    return tuple(as_bf16(a) for a in (c1, s1m, gc, gs, bc, bs))


def _fft1_kernel(c_ref, s_ref, u_ref, ar_ref, ai_ref):
    u = u_ref[0]
    ar_ref[0] = jnp.dot(c_ref[...], u, preferred_element_type=F32).astype(BF16)
    ai_ref[0] = (-jnp.dot(s_ref[...], u, preferred_element_type=F32)).astype(BF16)


FFT1_TN = 4096


def _fft_stage1(u3, c1, s1):
    b, n1, width = u3.shape
    tn = min(FFT1_TN, width)
    blk = pl.BlockSpec((1, n1, tn), lambda i, j: (i, 0, j))
    mat = pl.BlockSpec((n1, n1), lambda i, j: (0, 0))
    return pl.pallas_call(
        _fft1_kernel,
        grid=(b, width // tn),
        in_specs=[mat, mat, blk],
        out_specs=[blk, blk],
        out_shape=[jax.ShapeDtypeStruct(u3.shape, BF16)] * 2,
        compiler_params=_cparams("parallel", "parallel"),
        name="fft_stage1",
    )(c1, s1, u3)


FFT2_ROWS = 1024


def _fft2_kernel(gc_ref, gs_ref, bc_ref, bs_ref, g_ref, ar_ref, ai_ref, z_ref):
    t1, n2 = ar_ref.shape[1], ar_ref.shape[2]
    xr, xi = [], []
    for j in range(t1):
        ar, ai = ar_ref[0, j], ai_ref[0, j]
        gc, gs = gc_ref[j], gs_ref[j]
        xr.append((jnp.dot(gc, ar, preferred_element_type=F32)
                   + jnp.dot(gs, ai, preferred_element_type=F32)).astype(BF16))
        xi.append((jnp.dot(gc, ai, preferred_element_type=F32)
                   - jnp.dot(gs, ar, preferred_element_type=F32)).astype(BF16))
    z = jnp.dot(jnp.concatenate(xr, axis=0), bc_ref[...], preferred_element_type=F32)
    z += jnp.dot(jnp.concatenate(xi, axis=0), bs_ref[...], preferred_element_type=F32)
    z = (z * lax.rsqrt(jnp.mean(z * z, axis=-1, keepdims=True) + RMS_EPS) * g_ref[...]).astype(BF16)
    for j in range(t1):
        z_ref[0, j] = z[j * n2:(j + 1) * n2]


def _fft_stage2(ar4, ai4, gc, gs, bc, bs, gain):
    b, n1, n2, w = ar4.shape
    t1 = min(max(FFT2_ROWS // n2, 1), n1)
    a_blk = pl.BlockSpec((1, t1, n2, w), lambda i, j: (i, j, 0, 0))
    g_blk = pl.BlockSpec((t1, n2, n2), lambda i, j: (j, 0, 0))
    c_blk = pl.BlockSpec((w, w), lambda i, j: (0, 0))
    return pl.pallas_call(
        _fft2_kernel,
        grid=(b, n1 // t1),
        in_specs=[g_blk, g_blk, c_blk, c_blk, pl.BlockSpec((1, w), lambda i, j: (0, 0)), a_blk, a_blk],
        out_specs=a_blk,
        out_shape=jax.ShapeDtypeStruct(ar4.shape, BF16),
        compiler_params=_cparams("parallel", "parallel"),
        name="fft_stage2",
    )(gc, gs, bc, bs, gain.reshape(1, -1), ar4, ai4)


def _fourier_mix(u2d, batch, seq_len, gain):
    n1, n2 = _fft_split(seq_len)
    c1, s1, gc, gs, bc, bs = _fft_tables(seq_len)
    u3 = u2d.reshape(batch, n1, n2 * FOURIER_WIDTH)
    ar, ai = _fft_stage1(u3, c1, s1)
    shape4 = (batch, n1, n2, FOURIER_WIDTH)
    zt = _fft_stage2(ar.reshape(shape4), ai.reshape(shape4), gc, gs, bc, bs, gain)
    return zt.reshape(batch, n1, n2 * FOURIER_WIDTH)


GLA_LB = 1024
GLA_OUT_BLOCK_BYTES = 32 * 1024 * 1024
GLA_STAGGER = 6
CUMSUM_ROWS = 256


@functools.lru_cache(maxsize=None)
def _cumsum_matrices():
    r = np.arange(CUMSUM_ROWS)
    same_chunk = (r[:, None] // CHUNK) == (r[None, :] // CHUNK)
    prefix = same_chunk & (r[None, :] <= r[:, None])
    suffix = same_chunk & (r[None, :] >= r[:, None])
    return (jnp.asarray(prefix, dtype=F32).astype(BF16), jnp.asarray(suffix, dtype=F32).astype(BF16))


def _chunk_cumsum(x, tri):
    hi = x.astype(BF16)
    lo = (x - hi.astype(F32)).astype(BF16)
    width = x.shape[1]
    parts = []
    for r0 in range(0, x.shape[0], CUMSUM_ROWS):
        rows = slice(r0, r0 + CUMSUM_ROWS)
        both = jnp.dot(tri, jnp.concatenate([hi[rows], lo[rows]], axis=1), preferred_element_type=F32)
        parts.append(both[:, :width] + both[:, width:])
    return parts[0] if len(parts) == 1 else jnp.concatenate(parts, axis=0)


def _gla_direction(q_ref, v_ref, k_ref, lf_ref, tri, st_ref, o_ref, start, reverse):
    n = q_ref.shape[0]
    b = _chunk_cumsum(lf_ref[...], tri)
    yield
    mid = CHUNK // 2 if reverse else CHUNK // 2 - 1
    last = 0 if reverse else CHUNK - 1
    t_idx = lax.broadcasted_iota(jnp.int32, (CHUNK, CHUNK), 0)
    s_idx = lax.broadcasted_iota(jnp.int32, (CHUNK, CHUNK), 1)
    visible = (t_idx <= s_idx) if reverse else (t_idx >= s_idx)
    nchunks = n // CHUNK
    order = range(nchunks - 1, -1, -1) if reverse else range(nchunks)
    nt = (((1,), (1,)), ((), ()))
    tn = (((0,), (0,)), ((), ()))
    rows = [slice(c * CHUNK, (c + 1) * CHUNK) for c in range(nchunks)]
    qe, ke, e_mid, e_last, e_gap = [], [], [], [], []
    for c in range(nchunks):
        bc = b[rows[c]]
        b_mid = bc[mid:mid + 1]
        b_last = bc[last:last + 1]
        qe.append((q_ref[rows[c]].astype(F32) * jnp.exp(bc - b_mid)).astype(BF16))
        ke.append((k_ref[rows[c]].astype(F32) * jnp.exp(b_mid - bc)).astype(BF16))
        e_mid.append(jnp.exp(b_mid))
        e_last.append(jnp.exp(b_last))
        e_gap.append(jnp.exp(b_last - b_mid))
        yield
    scores, delta_t = [], []
    for c in range(nchunks):
        s = lax.dot_general(qe[c], ke[c], nt, preferred_element_type=F32)
        scores.append(jnp.where(visible, s, 0.0).astype(BF16))
        delta_t.append(lax.dot_general(v_ref[rows[c]], ke[c], tn, preferred_element_type=F32) * e_gap[c])
        yield
    st = st_ref[...]
    st_in = [None] * nchunks
    for c in order:
        st_in[c] = (st * e_mid[c]).astype(BF16)
        st = st * e_last[c] + delta_t[c]
    st_ref[...] = st
    yield
    for c in range(nchunks):
        o = jnp.dot(scores[c], v_ref[rows[c]], preferred_element_type=F32)
        o += lax.dot_general(qe[c], st_in[c], nt, preferred_element_type=F32)
        o_ref[pl.ds(start + c * CHUNK, CHUNK), :] += o
        yield


def _gla_kernel(trif_ref, trib_ref, qf_ref, vf_ref, kf_ref, lf_ref, qb_ref, vb_ref, kb_ref, lb_ref,
                o_ref, sf_ref, sb_ref):
    j = pl.program_id(2)
    nblk = pl.num_programs(2)
    lb_rows = qf_ref.shape[2]

    @pl.when(j == 0)
    def _():
        o_ref[...] = jnp.zeros(o_ref.shape, o_ref.dtype)
        sf_ref[...] = jnp.zeros(sf_ref.shape, sf_ref.dtype)
        sb_ref[...] = jnp.zeros(sb_ref.shape, sb_ref.dtype)

    start_f = pl.multiple_of(j * lb_rows, lb_rows)
    start_b = pl.multiple_of((nblk - 1 - j) * lb_rows, lb_rows)
    streams = []
    for hh in range(qf_ref.shape[0]):
        streams.append(_gla_direction(qf_ref.at[hh, 0], vf_ref.at[hh, 0], kf_ref.at[hh, 0], lf_ref.at[hh, 0],
                                      trif_ref[...], sf_ref.at[hh], o_ref.at[hh, 0], start_f, False))
        streams.append(_gla_direction(qb_ref.at[hh, 0], vb_ref.at[hh, 0], kb_ref.at[hh, 0], lb_ref.at[hh, 0],
                                      trib_ref[...], sb_ref.at[hh], o_ref.at[hh, 0], start_b, True))
    _interleave(streams, GLA_STAGGER)


def _gla(qv4, lf4, kk4):
    _, b, seq_len, _ = qv4.shape
    lbk = min(GLA_LB, seq_len)
    assert lbk % CUMSUM_ROWS == 0 and seq_len % lbk == 0
    nblk = seq_len // lbk
    nh = max(h for h in (1, 2) if 2 * h * seq_len * HEAD_DIM * 4 <= GLA_OUT_BLOCK_BYTES or h == 1)
    blk = lambda head0, rev: pl.BlockSpec(
        (nh, 1, lbk, HEAD_DIM),
        (lambda i, h, j: (head0 + h, i, nblk - 1 - j, 0)) if rev else (lambda i, h, j: (head0 + h, i, j, 0)))
    tri_spec = pl.BlockSpec((CUMSUM_ROWS, CUMSUM_ROWS), lambda i, h, j: (0, 0))
    tri_f, tri_b = _cumsum_matrices()
    groups = HEADS // nh
    return pl.pallas_call(
        _gla_kernel,
        grid=(b, groups, nblk),
        in_specs=[tri_spec, tri_spec,
                  blk(0, False), blk(groups, False), blk(0, False), blk(0, False),
                  blk(0, True), blk(groups, True), blk(groups, True), blk(groups, True)],
        out_specs=pl.BlockSpec((nh, 1, seq_len, HEAD_DIM), lambda i, h, j: (h, i, 0, 0)),
        out_shape=jax.ShapeDtypeStruct((HEADS, b, seq_len, HEAD_DIM), F32),
        scratch_shapes=[pltpu.VMEM((nh, HEAD_DIM, HEAD_DIM), F32)] * 2,
        compiler_params=_cparams("parallel", "parallel", "arbitrary"),
        name="gla",
    )(tri_f, tri_b, qv4, qv4, kk4, lf4, qv4, qv4, kk4, lf4)


HALF_D = D_MODEL // 2


def _pack_bf16_pairs(x):
    bits = lax.bitcast_convert_type(x.astype(BF16).astype(F32), jnp.uint32)
    return bits[:, :HALF_D] | (bits[:, HALF_D:] >> 16)


def _unpack_bf16_pairs(words):
    hi = lax.bitcast_convert_type(words & jnp.uint32(0xFFFF0000), F32)
    lo = lax.bitcast_convert_type(words << 16, F32)
    return hi, lo


OUT_TM = 512


def _out_proj_kernel(*refs, nz):
    z_refs = refs[:nz]
    (o_ref, g_ref, x_ref, ling_ref, linb_ref, ng_ref, wz_ref, wh_ref, l1g_ref, l1b_ref, rwh_ref, rwl_ref, rb_ref,
     h_ref, hpk_ref, route_ref, gate_ref, cnt_ref, base_ref) = refs[nz:]
    tm = o_ref.shape[1]

    z = jnp.concatenate([r[0] for r in z_refs], axis=0) if nz > 1 else z_refs[0][0]
    g = g_ref[...].astype(F32)
    normed = []
    for hd in range(HEADS):
        oh = o_ref[hd]
        normed.append(oh * lax.rsqrt(jnp.mean(oh * oh, axis=-1, keepdims=True) + RMS_EPS))
        yield
    hg = jnp.concatenate(normed, axis=1) * ng_ref[...] * (g * jax.nn.sigmoid(g))
    yield
    mixed = jnp.dot(z, wz_ref[...], preferred_element_type=F32)
    mixed += jnp.dot(hg.astype(BF16), wh_ref[...], preferred_element_type=F32)
    yield
    xn = _layer_norm_rows(x_ref[...], ling_ref[...], linb_ref[...])
    h = _layer_norm_rows(DEEPNORM_ALPHA * xn + mixed, l1g_ref[...], l1b_ref[...])
    h_ref[...] = h
    yield
    hpk_ref[...] = _pack_bf16_pairs(h)
    yield

    nt = (((1,), (1,)), ((), ()))
    h_hi = h.astype(BF16)
    h_lo = (h - h_hi.astype(F32)).astype(BF16)
    logits = lax.dot_general(rwh_ref[...], h_hi, nt, preferred_element_type=F32)
    logits += lax.dot_general(rwh_ref[...], h_lo, nt, preferred_element_type=F32)
    logits += lax.dot_general(rwl_ref[...], h_hi, nt, preferred_element_type=F32)
    logits += rb_ref[...]
    yield

    eid_f = lax.broadcasted_iota(jnp.int32, (N_EXPERTS, tm), 0).astype(F32)
    work = logits
    vals, idxs, hits = [], [], []
    for _ in range(TOP_K):
        m = jnp.max(work, axis=0, keepdims=True)
        idx = jnp.min(jnp.where(work == m, eid_f, float(N_EXPERTS)), axis=0, keepdims=True)
        hit = eid_f == idx
        work = jnp.where(hit, -jnp.inf, work)
        vals.append(m)
        idxs.append(idx)
        hits.append(hit)
        yield
    exps = [jnp.exp(v - vals[0]) for v in vals]
    denom = exps[0] + exps[1] + exps[2] + exps[3]

    member = jnp.zeros((N_EXPERTS, tm), F32)
    for hit in hits:
        member = member + jnp.where(hit, 1.0, 0.0)
    s_idx = lax.broadcasted_iota(jnp.int32, (tm, tm), 0)
    t_idx = lax.broadcasted_iota(jnp.int32, (tm, tm), 1)
    earlier = jnp.where(s_idx < t_idx, 1.0, 0.0).astype(BF16)
    base = base_ref[...]
    before = jnp.dot(member.astype(BF16), earlier, preferred_element_type=F32) + base
    yield

    row8 = lax.broadcasted_iota(jnp.int32, (2 * TOP_K, tm), 0)
    route = jnp.zeros((2 * TOP_K, tm), F32)
    row128 = lax.broadcasted_iota(jnp.int32, (LANES, tm), 0)
    gate_t = jnp.zeros((LANES, tm), F32)
    for k in range(TOP_K):
        rank = jnp.sum(jnp.where(hits[k], before, 0.0), axis=0, keepdims=True)
        route = route + jnp.where(row8 == k, idxs[k], 0.0) + jnp.where(row8 == TOP_K + k, rank, 0.0)
        gate_t = gate_t + jnp.where(row128 == k, exps[k] / denom, 0.0)
    route_ref[...] = route.astype(jnp.int32)
    gate_ref[...] = gate_t.T
    total = base + jnp.sum(member, axis=1, keepdims=True)
    base_ref[...] = total
    cnt_ref[...] = total


def _out_proj_stage(zt, o3, g2d, x2d, ln_in_g, ln_in_b, seq_len, norm_g6, w_out_bf16, ln_g, ln_b,
                    rw_hi, rw_lo, rb_col):
    t = o3.shape[1]
    n1 = zt.shape[1]
    tm = max(min(OUT_TM, seq_len), n1)
    nz = tm // n1
    row = lambda i: (i, 0)
    const = lambda i: (0, 0)

    def z_spec(k):
        return pl.BlockSpec((1, n1, FOURIER_WIDTH),
                            lambda i: ((i * tm) // seq_len, 0, ((i * tm) % seq_len) // n1 + k))

    in_specs = [z_spec(k) for k in range(nz)] + [
        pl.BlockSpec((HEADS, tm, HEAD_DIM), lambda i: (0, i, 0)),
        pl.BlockSpec((tm, HGRN_WIDTH), row),
        pl.BlockSpec((tm, D_MODEL), row),
        _resident((1, D_MODEL)),
        _resident((1, D_MODEL)),
        _resident((1, HGRN_WIDTH)),
        _resident((FOURIER_WIDTH, D_MODEL)),
        _resident((HGRN_WIDTH, D_MODEL)),
        _resident((1, D_MODEL)),
        _resident((1, D_MODEL)),
        _resident((N_EXPERTS, D_MODEL)),
        _resident((N_EXPERTS, D_MODEL)),
        _resident((N_EXPERTS, 1)),
    ]
    out_specs = [
        pl.BlockSpec((tm, D_MODEL), row),
        pl.BlockSpec((tm, HALF_D), row),
        pl.BlockSpec((2 * TOP_K, tm), lambda i: (0, i)),
        pl.BlockSpec((tm, LANES), row),
        pl.BlockSpec((N_EXPERTS, 1), const),
    ]
    out_shape = [
        jax.ShapeDtypeStruct((t, D_MODEL), F32),
        jax.ShapeDtypeStruct((t, HALF_D), jnp.uint32),
        jax.ShapeDtypeStruct((2 * TOP_K, t), jnp.int32),
        jax.ShapeDtypeStruct((t, LANES), F32),
        jax.ShapeDtypeStruct((N_EXPERTS, 1), F32),
    ]
    return _Stage(
        body=functools.partial(_out_proj_kernel, nz=nz),
        init=_zero_ref,
        grid=(t // tm,),
        in_specs=in_specs,
        out_specs=out_specs,
        out_shape=out_shape,
        scratch=[pltpu.VMEM((N_EXPERTS, 1), F32)],
        args=[zt] * nz + [o3, g2d, x2d, ln_in_g.reshape(1, -1), ln_in_b.reshape(1, -1), norm_g6, w_out_bf16[:FOURIER_WIDTH], w_out_bf16[FOURIER_WIDTH:],
                          ln_g.reshape(1, -1), ln_b.reshape(1, -1), rw_hi, rw_lo, rb_col],
        sequential=True,
        name="out_proj",
    )


MOE_BLOCK = 1024
MOE_PARTS = 4
COMBINE_TM = 512
COMBINE_PARTS = 2
SC_WINDOW = 128


def _sc_mesh():
    return plsc.VectorSubcoreMesh(core_axis_name="core", subcore_axis_name="subcore")


def _dispatch(dest_kt, hpk, n_slots):
    t, d = hpk.shape
    rows = pl.BlockSpec((SC_WINDOW, d), index_map=lambda i: (i, 0), pipeline_mode=pl.Buffered(1))
    idx = pl.BlockSpec((1, SC_WINDOW), index_map=lambda i: (0, i))

    @pl.kernel(out_type=jax.ShapeDtypeStruct((n_slots, d), hpk.dtype), mesh=_sc_mesh(), name="moe_dispatch_sc")
    def scatter(x_hbm, i0_hbm, i1_hbm, i2_hbm, i3_hbm, o_hbm):
        def body(x_vmem, *idx_vmem):
            for iv in idx_vmem:
                pltpu.sync_copy(x_vmem, o_hbm.at[iv.at[0]])

        pltpu.emit_pipeline(
            body, grid=(t // SC_WINDOW,), in_specs=[rows] + [idx] * TOP_K, out_specs=[],
            core_axis_name=("core", "subcore"), dimension_semantics=(pltpu.PARALLEL,),
        )(x_hbm, i0_hbm, i1_hbm, i2_hbm, i3_hbm)

    return scatter(hpk, *[dest_kt[k].reshape(1, t) for k in range(TOP_K)])


def _gather_rows(yb, dest_kt):
    _, t = dest_kt.shape
    d = yb.shape[1]
    n = TOP_K * t

    @pl.kernel(out_type=jax.ShapeDtypeStruct((n, d), yb.dtype), mesh=_sc_mesh(), name="moe_gather_sc")
    def gather(y_hbm, i_hbm, o_hbm):
        def body(i_vmem, o_vmem):
            pltpu.sync_copy(y_hbm.at[i_vmem.at[0]], o_vmem)

        pltpu.emit_pipeline(
            body, grid=(n // SC_WINDOW,),
            in_specs=[pl.BlockSpec((1, SC_WINDOW), index_map=lambda i: (0, i))],
            out_specs=[pl.BlockSpec((SC_WINDOW, d), index_map=lambda i: (i, 0), pipeline_mode=pl.Buffered(1))],
            core_axis_name=("core", "subcore"), dimension_semantics=(pltpu.PARALLEL,),
        )(i_hbm, o_hbm)

    return gather(yb, dest_kt.reshape(1, n)).reshape(TOP_K, t, d)


def _experts_kernel(be_ref, nused_ref, next_ref, parts_ref, xb_ref, wgu_hbm, bgu_ref, wdn_hbm, bdn_ref, yb_ref,
                    wgu_f32, wdn_f32, wgu_bf, wdn_bf, slot_ref, sem):
    i = pl.program_id(0)
    used = i < nused_ref[0]
    e = be_ref[i]

    def weight_copies(expert, slot):
        return (pltpu.make_async_copy(wgu_hbm.at[expert], wgu_f32.at[slot], sem.at[slot, 0]),
                pltpu.make_async_copy(wdn_hbm.at[expert], wdn_f32.at[slot], sem.at[slot, 1]))

    @pl.when(used & (i == 0))
    def _():
        slot_ref[0] = 0
        for c in weight_copies(e, 0):
            c.start()

    @pl.when(used & (i > 0) & (e != be_ref[jnp.maximum(i - 1, 0)]))
    def _():
        slot_ref[0] = 1 - slot_ref[0]

    @pl.when(used & ((i == 0) | (e != be_ref[jnp.maximum(i - 1, 0)])))
    def _():
        slot = slot_ref[0]
        for c in weight_copies(e, slot):
            c.wait()
        wgu_bf[...] = wgu_f32[slot].astype(BF16)
        wdn_bf[...] = wdn_f32[slot].astype(BF16)
        nxt = next_ref[e]

        @pl.when(nxt != e)
        def _():
            for c in weight_copies(nxt, 1 - slot):
                c.start()

    def mlp(rows):
        x_hi, x_lo = _unpack_bf16_pairs(xb_ref[:rows])
        x = jnp.concatenate([x_hi.astype(BF16), x_lo.astype(BF16)], axis=1)
        gu = jnp.dot(x, wgu_bf[...], preferred_element_type=F32) + bgu_ref[0]
        gate = jnp.minimum(gu[:, :D_FF], SWIGLU_LIMIT)
        up = jnp.clip(gu[:, D_FF:], -SWIGLU_LIMIT, SWIGLU_LIMIT)
        act = (up + 1.0) * gate * jax.nn.sigmoid(SWIGLU_ALPHA * gate)
        y = jnp.dot(act.astype(BF16), wdn_bf[...], preferred_element_type=F32) + bdn_ref[0]
        yb_ref[:rows] = _pack_bf16_pairs(y)

    for parts in range(1, MOE_PARTS + 1):
        pl.when(used & (parts_ref[i] == parts))(functools.partial(mlp, parts * (MOE_BLOCK // MOE_PARTS)))


def _experts(block_expert, n_used, next_expert, parts, xb, wgu, bgu, wdn, bdn):
    n_slots = xb.shape[0]
    nb = n_slots // MOE_BLOCK
    blk = pl.BlockSpec((MOE_BLOCK, HALF_D),
                       lambda i, be, nu, nx, hv: (jnp.minimum(i, jnp.maximum(nu[0] - 1, 0)), 0))
    per_expert = lambda shape: pl.BlockSpec((1,) + shape, lambda i, be, nu, nx, hv: (be[i], 0, 0))
    hbm = pl.BlockSpec(memory_space=pl.ANY)
    grid_spec = pltpu.PrefetchScalarGridSpec(
        num_scalar_prefetch=4,
        grid=(nb,),
        in_specs=[blk, hbm, per_expert((1, 2 * D_FF)), hbm, per_expert((1, D_MODEL))],
        out_specs=blk,
        scratch_shapes=[pltpu.VMEM((2, D_MODEL, 2 * D_FF), F32), pltpu.VMEM((2, D_FF, D_MODEL), F32),
                        pltpu.VMEM((D_MODEL, 2 * D_FF), BF16), pltpu.VMEM((D_FF, D_MODEL), BF16),
                        pltpu.SMEM((1,), jnp.int32), pltpu.SemaphoreType.DMA((2, 2))],
    )
    return pl.pallas_call(
        _experts_kernel,
        grid_spec=grid_spec,
        out_shape=jax.ShapeDtypeStruct((n_slots, HALF_D), jnp.uint32),
        compiler_params=_cparams("arbitrary"),
        name="moe_experts",
    )(block_expert, n_used, next_expert, parts, xb, wgu, bgu, wdn, bdn)


def _combine_kernel(rows_ref, gate_ref, h_ref, g_ref, b_ref, *rest):
    out_ref = rest[-1]
    tm = h_ref.shape[0]
    gate = gate_ref[...]
    ff_hi = jnp.zeros((tm, HALF_D), F32)
    ff_lo = jnp.zeros((tm, HALF_D), F32)
    for k in range(TOP_K):
        hi, lo = _unpack_bf16_pairs(rows_ref[k])
        gk = gate[:, k:k + 1]
        ff_hi = ff_hi + gk * hi
        ff_lo = ff_lo + gk * lo
    ff = jnp.concatenate([ff_hi, ff_lo], axis=1)
    out_ref[...] = _layer_norm_rows(DEEPNORM_ALPHA * h_ref[...] + ff, g_ref[...], b_ref[...])


def _combine(rows_kt, gates, h2d, ln_g, ln_b, token0, prev_out):
    t = h2d.shape[0]
    t_part = rows_kt.shape[1]
    tm = min(COMBINE_TM, t_part)
    first = token0 // tm
    row = lambda i: (first + i, 0)
    const = lambda i: (0, 0)
    in_specs = [
        pl.BlockSpec((TOP_K, tm, HALF_D), lambda i: (0, i, 0)),
        pl.BlockSpec((tm, LANES), row),
        pl.BlockSpec((tm, D_MODEL), row),
        pl.BlockSpec((1, D_MODEL), const),
        pl.BlockSpec((1, D_MODEL), const),
    ]
    args = [rows_kt, gates, h2d, ln_g.reshape(1, -1), ln_b.reshape(1, -1)]
    aliases = {}
    if prev_out is not None:
        in_specs.append(pl.BlockSpec(memory_space=pl.ANY))
        args.append(prev_out)
        aliases = {len(args) - 1: 0}
    return pl.pallas_call(
        _combine_kernel,
        grid=(t_part // tm,),
        in_specs=in_specs,
        out_specs=pl.BlockSpec((tm, D_MODEL), row),
        out_shape=jax.ShapeDtypeStruct((t, D_MODEL), F32),
        input_output_aliases=aliases,
        compiler_params=_cparams("parallel"),
        name="moe_combine",
    )(*args)


def _moe_plan(route, counts_f32, n_tokens):
    eid = route[:TOP_K]
    rank = route[TOP_K:]
    counts = counts_f32[:, 0].astype(jnp.int32)
    padded = ((counts + MOE_BLOCK - 1) // MOE_BLOCK) * MOE_BLOCK
    pend = jnp.cumsum(padded)
    pstart = pend - padded
    experts = jnp.arange(N_EXPERTS, dtype=jnp.int32)[:, None, None]
    dest = rank + jnp.sum(jnp.where(eid[None] == experts, pstart[:, None, None], 0), axis=0)
    n_assign = n_tokens * TOP_K
    n_slots = ((n_assign + MOE_BLOCK - 1) // MOE_BLOCK) * MOE_BLOCK + N_EXPERTS * MOE_BLOCK
    nb = n_slots // MOE_BLOCK
    block_start = jnp.arange(nb, dtype=jnp.int32) * MOE_BLOCK
    block_expert = jnp.sum((pend[None, :] <= block_start[:, None]).astype(jnp.int32), axis=1)
    block_expert = jnp.minimum(block_expert, N_EXPERTS - 1).astype(jnp.int32)
    n_used = (pend[-1:] // MOE_BLOCK).astype(jnp.int32)
    onehot = (block_expert[:, None] == jnp.arange(N_EXPERTS, dtype=jnp.int32)[None, :]).astype(jnp.int32)
    valid_rows = jnp.sum(onehot * (pstart + counts)[None, :], axis=1) - block_start
    part_rows = MOE_BLOCK // MOE_PARTS
    parts = jnp.clip((valid_rows + part_rows - 1) // part_rows, 1, MOE_PARTS).astype(jnp.int32)
    ids = jnp.arange(N_EXPERTS, dtype=jnp.int32)
    later_nonempty = (ids[None, :] > ids[:, None]) & (counts[None, :] > 0)
    next_expert = jnp.min(jnp.where(later_nonempty, ids[None, :], N_EXPERTS), axis=1)
    next_expert = jnp.where(next_expert == N_EXPERTS, ids, next_expert).astype(jnp.int32)
    return dest.astype(jnp.int32), block_expert, n_used, next_expert, parts, n_slots


def _in_stage(x, p):
    t = x.shape[0] * x.shape[1]
    return _in_proj_stage(x.reshape(t, D_MODEL), p["ln_in_g"], p["ln_in_b"], p["lb2"], p["w_in"])


def _mix_and_out_stage(x, in_outs, p):
    batch, seq_len, _ = x.shape
    t = batch * seq_len
    u, qv, lf, kk, og = in_outs
    zt = _fourier_mix(u, batch, seq_len, p["fourier_norm_g"])
    per_seq = lambda a: a.reshape(a.shape[0], batch, seq_len, HEAD_DIM)
    o = _gla(per_seq(qv), per_seq(lf), per_seq(kk))
    return _out_proj_stage(
        zt, o.reshape(HEADS, t, HEAD_DIM), og, x.reshape(t, D_MODEL), p["ln_in_g"], p["ln_in_b"], seq_len,
        p["norm_g6"], p["w_out"], p["ln1_g"], p["ln1_b"], p["rw_hi"], p["rw_lo"], p["rb_col"])


def _moe(x, out_outs, p):
    batch, seq_len, _ = x.shape
    t = batch * seq_len
    h, hpk, route, gates, counts = out_outs
    dest_kt, block_expert, n_used, next_expert, parts, n_slots = _moe_plan(route, counts, t)
    xb = _dispatch(dest_kt, hpk, n_slots)
    yb = _experts(block_expert, n_used, next_expert, parts, xb, p["w_gu"], p["b_gu"], p["w_dn"], p["b_dn"])
    y = None
    part = t // COMBINE_PARTS
    for j in range(COMBINE_PARTS):
        rows = _gather_rows(yb, dest_kt[:, j * part:(j + 1) * part])
        y = _combine(rows, gates, h, p["ln2_g"], p["ln2_b"], j * part, y)
    return y.reshape(batch, seq_len, D_MODEL)


def _prepare_params(ln_in_g, ln_in_b, w_in, fourier_norm_g, lb_gamma, hgrn_norm_g, w_out, ln1_g, ln1_b,
                    router_w, router_b, w_gate_up, b_gate_up, w_down, b_down, ln2_g, ln2_b):
    lb_all = jnp.cumsum(jax.nn.softmax(lb_gamma.astype(F32), axis=1), axis=1)
    rw = router_w[0].astype(F32).T
    rw_hi = rw.astype(BF16)
    rw_lo = (rw - rw_hi.astype(F32)).astype(BF16)
    return dict(
        ln_in_g=ln_in_g, ln_in_b=ln_in_b, w_in=w_in[0].astype(BF16),
        fourier_norm_g=fourier_norm_g[0],
        lb2=jnp.concatenate([lb_all[0, 0], lb_all[1, 0]]).reshape(1, -1),
        norm_g6=jnp.tile(hgrn_norm_g[0].astype(F32), HEADS).reshape(1, -1),
        w_out=w_out[0].astype(BF16), ln1_g=ln1_g[0], ln1_b=ln1_b[0],
        rw_hi=rw_hi, rw_lo=rw_lo, rb_col=router_b[0].astype(F32).reshape(-1, 1),
        w_gu=w_gate_up[0], b_gu=b_gate_up[0].reshape(N_EXPERTS, 1, -1),
        w_dn=w_down[0], b_dn=b_down[0].reshape(N_EXPERTS, 1, -1),
        ln2_g=ln2_g[0], ln2_b=ln2_b[0],
    )


def kernel(x_prompt, x_sample, ln_in_g, ln_in_b, w_in, fourier_norm_g, lb_gamma, hgrn_norm_g, w_out,
           ln1_g, ln1_b, router_w, router_b, w_gate_up, b_gate_up, w_down, b_down, ln2_g, ln2_b):
    p = _prepare_params(ln_in_g, ln_in_b, w_in, fourier_norm_g, lb_gamma, hgrn_norm_g, w_out, ln1_g, ln1_b,
                        router_w, router_b, w_gate_up, b_gate_up, w_down, b_down, ln2_g, ln2_b)
    first, second = x_sample, x_prompt
    (in_first,) = _run_stages([_in_stage(first, p)])
    out_stage_first = _mix_and_out_stage(first, in_first, p)
    in_stage_second = _in_stage(second, p)
    if in_stage_second.grid == out_stage_first.grid:
        in_second, out_first = _run_stages([in_stage_second, out_stage_first])
    else:
        (in_second,) = _run_stages([in_stage_second])
        (out_first,) = _run_stages([out_stage_first])
    (out_second,) = _run_stages([_mix_and_out_stage(second, in_second, p)])
    y_first = _moe(first, out_first, p)
    y_second = _moe(second, out_second, p)
    return (y_second, y_first)
```

```python
import functools
import math
from typing import Callable, NamedTuple

import numpy as np
import jax
import jax.numpy as jnp
from jax import lax
from jax.experimental import pallas as pl
from jax.experimental.pallas import tpu as pltpu
from jax.experimental.pallas import tpu_sc as plsc

D_MODEL = 1024
FOURIER_WIDTH = 256
FOURIER_GROUP_DIM = 64
HGRN_WIDTH = 768
HEAD_DIM = 128
HEADS = 6
CHUNK = 64
N_EXPERTS = 32
TOP_K = 4
D_FF = 1024
SWIGLU_LIMIT = 7.0
SWIGLU_ALPHA = 1.702
LN_EPS = 1e-5
RMS_EPS = 1e-6
DEEPNORM_ALPHA = 2.0 ** 0.25

LANES = 128
VMEM_LIMIT_BYTES = 56 * 1024 * 1024

F32 = jnp.float32
BF16 = jnp.bfloat16


def _cparams(*sem):
    return pltpu.CompilerParams(dimension_semantics=sem, vmem_limit_bytes=VMEM_LIMIT_BYTES)


def _layer_norm_rows(x, g, b):
    mu = jnp.mean(x, axis=-1, keepdims=True)
    xc = x - mu
    var = jnp.mean(xc * xc, axis=-1, keepdims=True)
    return xc * lax.rsqrt(var + LN_EPS) * g + b


IN_TM = 512
IN_TN = 512


def _in_proj_kernel(x_ref, g_ref, b_ref, lb_ref, w_ref, u_ref, qv_ref, lf_ref, kk_ref, og_ref):
    xb = _layer_norm_rows(x_ref[...], g_ref[...], b_ref[...]).astype(BF16)
    hw = HGRN_WIDTH
    yield

    def store_heads(ref, first_head, c0, val):
        for j in range(val.shape[1] // HEAD_DIM):
            ref[first_head + c0 // HEAD_DIM + j] = val[:, j * HEAD_DIM:(j + 1) * HEAD_DIM]

    def chunks(col, width):
        for c0 in range(0, width, IN_TN):
            cw = min(IN_TN, width - c0)
            yield c0, cw, jnp.dot(xb, w_ref[:, col + c0:col + c0 + cw], preferred_element_type=F32)

    for c0, cw, acc in chunks(0, FOURIER_WIDTH):
        u_ref[:, c0:c0 + cw] = acc.astype(BF16)
        yield
    for c0, cw, acc in chunks(FOURIER_WIDTH, hw):
        store_heads(qv_ref, 0, c0, (acc * jax.nn.sigmoid(acc) * (HEAD_DIM ** -0.5)).astype(BF16))
        yield
    for c0, cw, acc in chunks(FOURIER_WIDTH + hw, hw):
        store_heads(qv_ref, HEADS, c0, acc.astype(BF16))
        yield
    for c0, cw, acc in chunks(FOURIER_WIDTH + 2 * hw, 2 * hw):
        lb = lb_ref[:, c0:c0 + cw]
        fg = lb + (1.0 - lb) * jax.nn.sigmoid(acc)
        store_heads(lf_ref, 0, c0, jnp.log(fg))
        store_heads(kk_ref, 0, c0, (1.0 - fg).astype(BF16))
        yield
    for c0, cw, acc in chunks(FOURIER_WIDTH + 4 * hw, hw):
        og_ref[:, c0:c0 + cw] = acc.astype(BF16)
        yield


class _Stage(NamedTuple):
    body: Callable
    init: Callable | None
    grid: tuple
    in_specs: list
    out_specs: list
    out_shape: list
    scratch: list
    args: list
    sequential: bool
    name: str


_DONE = object()


def _interleave(pieces, stagger):
    done = [False] * len(pieces)
    t = 0
    while not all(done):
        for i, g in enumerate(pieces):
            if not done[i] and t >= i * stagger:
                done[i] = next(g, _DONE) is _DONE
        t += 1


def _run_stages(stages):
    assert all(s.grid == stages[0].grid for s in stages)
    n_in = [len(s.in_specs) for s in stages]
    n_out = [len(s.out_specs) for s in stages]
    n_scr = [len(s.scratch) for s in stages]

    def body(*refs):
        scr = sum(n_in) + sum(n_out)
        for s, c in zip(stages, n_scr):
            if s.init is not None:
                pl.when(pl.program_id(0) == 0)(functools.partial(s.init, *refs[scr:scr + c]))
            scr += c
        ins, outs, scr = 0, sum(n_in), sum(n_in) + sum(n_out)
        bodies = []
        for s, a, b, c in zip(stages, n_in, n_out, n_scr):
            bodies.append(s.body(*refs[ins:ins + a], *refs[outs:outs + b], *refs[scr:scr + c]))
            ins, outs, scr = ins + a, outs + b, scr + c
        _interleave(bodies, 0)

    results = pl.pallas_call(
        body,
        grid=stages[0].grid,
        in_specs=[sp for s in stages for sp in s.in_specs],
        out_specs=[sp for s in stages for sp in s.out_specs],
        out_shape=[sh for s in stages for sh in s.out_shape],
        scratch_shapes=[sc for s in stages for sc in s.scratch],
        compiler_params=_cparams("arbitrary" if any(s.sequential for s in stages) else "parallel"),
        name="_".join(s.name for s in stages),
    )(*[a for s in stages for a in s.args])
    split, pos = [], 0
    for b in n_out:
        split.append(results[pos:pos + b])
        pos += b
    return split


def _zero_ref(ref):
    ref[...] = jnp.zeros(ref.shape, ref.dtype)


def _resident(shape):
    return pl.BlockSpec(shape, lambda i: (0,) * len(shape), pipeline_mode=pl.Buffered(1))


def _in_proj_stage(x2d, ln_g, ln_b, lb2, w_bf16):
    t = x2d.shape[0]
    tm = min(IN_TM, t)
    row = lambda i: (i, 0)
    flat = lambda w, dt: (pl.BlockSpec((tm, w), row), jax.ShapeDtypeStruct((t, w), dt))
    head_major = lambda nh, dt: (pl.BlockSpec((nh, tm, HEAD_DIM), lambda i: (0, i, 0)),
                                 jax.ShapeDtypeStruct((nh, t, HEAD_DIM), dt))
    outs = [flat(FOURIER_WIDTH, BF16), head_major(2 * HEADS, BF16),
            head_major(2 * HEADS, F32), head_major(2 * HEADS, BF16), flat(HGRN_WIDTH, BF16)]
    return _Stage(
        body=_in_proj_kernel,
        init=None,
        grid=(t // tm,),
        in_specs=[pl.BlockSpec((tm, D_MODEL), row), _resident((1, D_MODEL)), _resident((1, D_MODEL)),
                  _resident((1, 2 * HGRN_WIDTH)), _resident(w_bf16.shape)],
        out_specs=[o[0] for o in outs],
        out_shape=[o[1] for o in outs],
        scratch=[],
        args=[x2d, ln_g.reshape(1, -1), ln_b.reshape(1, -1), lb2, w_bf16],
        sequential=False,
        name="in_proj",
    )


def _fft_split(seq_len):
    n1 = 1 << ((seq_len.bit_length() - 1 + 1) // 2)
    return n1, seq_len // n1


@functools.lru_cache(maxsize=None)
def _fft_tables(seq_len):
    n1, n2 = _fft_split(seq_len)
    k1 = np.arange(n1)
    ang1 = 2.0 * np.pi * ((k1[:, None] * k1[None, :]) % n1) / n1
    s1 = 1.0 / math.sqrt(n1)
    c1, s1m = np.cos(ang1) * s1, np.sin(ang1) * s1
    l1p = np.arange(n1)[:, None, None]
    l2p = np.arange(n2)[None, :, None]
    l2 = np.arange(n2)[None, None, :]
    ang2 = 2.0 * np.pi * ((l2 * (l1p + n1 * l2p)) % seq_len) / seq_len
    s2 = 1.0 / math.sqrt(n2)
    gc, gs = np.cos(ang2) * s2, np.sin(ang2) * s2
    kc = np.arange(FOURIER_GROUP_DIM)
    angc = 2.0 * np.pi * ((kc[:, None] * kc[None, :]) % FOURIER_GROUP_DIM) / FOURIER_GROUP_DIM
    sc = 1.0 / math.sqrt(FOURIER_GROUP_DIM)
    groups = FOURIER_WIDTH // FOURIER_GROUP_DIM
    bc = np.kron(np.eye(groups), np.cos(angc) * sc)
    bs = np.kron(np.eye(groups), np.sin(angc) * sc)
    as_bf16 = lambda a: jnp.asarray(a, dtype=F32).astype(BF16)
    return tuple(as_bf16(a) for a in (c1, s1m, gc, gs, bc, bs))


def _fft1_kernel(c_ref, s_ref, u_ref, ar_ref, ai_ref):
    u = u_ref[0]
    ar_ref[0] = jnp.dot(c_ref[...], u, preferred_element_type=F32).astype(BF16)
    ai_ref[0] = (-jnp.dot(s_ref[...], u, preferred_element_type=F32)).astype(BF16)


FFT1_TN = 4096


def _fft_stage1(u3, c1, s1):
    b, n1, width = u3.shape
    tn = min(FFT1_TN, width)
    blk = pl.BlockSpec((1, n1, tn), lambda i, j: (i, 0, j))
    mat = pl.BlockSpec((n1, n1), lambda i, j: (0, 0))
    return pl.pallas_call(
        _fft1_kernel,
        grid=(b, width // tn),
        in_specs=[mat, mat, blk],
        out_specs=[blk, blk],
        out_shape=[jax.ShapeDtypeStruct(u3.shape, BF16)] * 2,
        compiler_params=_cparams("parallel", "parallel"),
        name="fft_stage1",
    )(c1, s1, u3)


FFT2_ROWS = 1024


def _fft2_kernel(gc_ref, gs_ref, bc_ref, bs_ref, g_ref, ar_ref, ai_ref, z_ref):
    t1, n2 = ar_ref.shape[1], ar_ref.shape[2]
    xr, xi = [], []
    for j in range(t1):
        ar, ai = ar_ref[0, j], ai_ref[0, j]
        gc, gs = gc_ref[j], gs_ref[j]
        xr.append((jnp.dot(gc, ar, preferred_element_type=F32)
                   + jnp.dot(gs, ai, preferred_element_type=F32)).astype(BF16))
        xi.append((jnp.dot(gc, ai, preferred_element_type=F32)
                   - jnp.dot(gs, ar, preferred_element_type=F32)).astype(BF16))
    z = jnp.dot(jnp.concatenate(xr, axis=0), bc_ref[...], preferred_element_type=F32)
    z += jnp.dot(jnp.concatenate(xi, axis=0), bs_ref[...], preferred_element_type=F32)
    z = (z * lax.rsqrt(jnp.mean(z * z, axis=-1, keepdims=True) + RMS_EPS) * g_ref[...]).astype(BF16)
    for j in range(t1):
        z_ref[0, j] = z[j * n2:(j + 1) * n2]


def _fft_stage2(ar4, ai4, gc, gs, bc, bs, gain):
    b, n1, n2, w = ar4.shape
    t1 = min(max(FFT2_ROWS // n2, 1), n1)
    a_blk = pl.BlockSpec((1, t1, n2, w), lambda i, j: (i, j, 0, 0))
    g_blk = pl.BlockSpec((t1, n2, n2), lambda i, j: (j, 0, 0))
    c_blk = pl.BlockSpec((w, w), lambda i, j: (0, 0))
    return pl.pallas_call(
        _fft2_kernel,
        grid=(b, n1 // t1),
        in_specs=[g_blk, g_blk, c_blk, c_blk, pl.BlockSpec((1, w), lambda i, j: (0, 0)), a_blk, a_blk],
        out_specs=a_blk,
        out_shape=jax.ShapeDtypeStruct(ar4.shape, BF16),
        compiler_params=_cparams("parallel", "parallel"),
        name="fft_stage2",
    )(gc, gs, bc, bs, gain.reshape(1, -1), ar4, ai4)


def _fourier_mix(u2d, batch, seq_len, gain):
    n1, n2 = _fft_split(seq_len)
    c1, s1, gc, gs, bc, bs = _fft_tables(seq_len)
    u3 = u2d.reshape(batch, n1, n2 * FOURIER_WIDTH)
    ar, ai = _fft_stage1(u3, c1, s1)
    shape4 = (batch, n1, n2, FOURIER_WIDTH)
    zt = _fft_stage2(ar.reshape(shape4), ai.reshape(shape4), gc, gs, bc, bs, gain)
    return zt.reshape(batch, n1, n2 * FOURIER_WIDTH)


GLA_LB = 1024
GLA_OUT_BLOCK_BYTES = 32 * 1024 * 1024
GLA_SKEW = 1
GLA_STAGGER = 0
CUMSUM_ROWS = 256


@functools.lru_cache(maxsize=None)
def _cumsum_matrices():
    r = np.arange(CUMSUM_ROWS)
    same_chunk = (r[:, None] // CHUNK) == (r[None, :] // CHUNK)
    prefix = same_chunk & (r[None, :] <= r[:, None])
    suffix = same_chunk & (r[None, :] >= r[:, None])
    return (jnp.asarray(prefix, dtype=F32).astype(BF16), jnp.asarray(suffix, dtype=F32).astype(BF16))


def _chunk_cumsum(x, tri):
    hi = x.astype(BF16)
    lo = (x - hi.astype(F32)).astype(BF16)
    width = x.shape[1]
    parts = []
    for r0 in range(0, x.shape[0], CUMSUM_ROWS):
        rows = slice(r0, r0 + CUMSUM_ROWS)
        both = jnp.dot(tri, jnp.concatenate([hi[rows], lo[rows]], axis=1), preferred_element_type=F32)
        parts.append(both[:, :width] + both[:, width:])
    return parts[0] if len(parts) == 1 else jnp.concatenate(parts, axis=0)


def _gla_direction(q_ref, v_ref, k_ref, lf_ref, tri, st_ref, o_ref, start, reverse):
    n = q_ref.shape[0]
    b = _chunk_cumsum(lf_ref[...], tri)
    yield
    mid = CHUNK // 2 if reverse else CHUNK // 2 - 1
    last = 0 if reverse else CHUNK - 1
    t_idx = lax.broadcasted_iota(jnp.int32, (CHUNK, CHUNK), 0)
    s_idx = lax.broadcasted_iota(jnp.int32, (CHUNK, CHUNK), 1)
    visible = (t_idx <= s_idx) if reverse else (t_idx >= s_idx)
    nchunks = n // CHUNK
    order = range(nchunks - 1, -1, -1) if reverse else range(nchunks)
    nt = (((1,), (1,)), ((), ()))
    tn = (((0,), (0,)), ((), ()))
    rows = [slice(c * CHUNK, (c + 1) * CHUNK) for c in range(nchunks)]
    st = st_ref[...]
    pending = []
    seq = list(order)
    for step in range(nchunks + GLA_SKEW):
        if step < nchunks:
            c = seq[step]
            bc = b[rows[c]]
            b_mid = bc[mid:mid + 1]
            b_last = bc[last:last + 1]
            qe = (q_ref[rows[c]].astype(F32) * jnp.exp(bc - b_mid)).astype(BF16)
            ke = (k_ref[rows[c]].astype(F32) * jnp.exp(b_mid - bc)).astype(BF16)
            s = lax.dot_general(qe, ke, nt, preferred_element_type=F32)
            scores = jnp.where(visible, s, 0.0).astype(BF16)
            delta_t = lax.dot_general(v_ref[rows[c]], ke, tn, preferred_element_type=F32) * jnp.exp(b_last - b_mid)
            pending.append((c, qe, scores, delta_t, jnp.exp(b_mid), jnp.exp(b_last)))
            yield
        if step >= GLA_SKEW:
            c, qe, scores, delta_t, e_mid, e_last = pending.pop(0)
            st_in = (st * e_mid).astype(BF16)
            st = st * e_last + delta_t
            o = jnp.dot(scores, v_ref[rows[c]], preferred_element_type=F32)
            o += lax.dot_general(qe, st_in, nt, preferred_element_type=F32)
            o_ref[pl.ds(start + c * CHUNK, CHUNK), :] += o
            yield
    st_ref[...] = st


def _gla_kernel(trif_ref, trib_ref, qf_ref, vf_ref, kf_ref, lf_ref, qb_ref, vb_ref, kb_ref, lb_ref,
                o_ref, sf_ref, sb_ref):
    j = pl.program_id(2)
    nblk = pl.num_programs(2)
    lb_rows = qf_ref.shape[2]

    @pl.when(j == 0)
    def _():
        o_ref[...] = jnp.zeros(o_ref.shape, o_ref.dtype)
        sf_ref[...] = jnp.zeros(sf_ref.shape, sf_ref.dtype)
        sb_ref[...] = jnp.zeros(sb_ref.shape, sb_ref.dtype)

    start_f = pl.multiple_of(j * lb_rows, lb_rows)
    start_b = pl.multiple_of((nblk - 1 - j) * lb_rows, lb_rows)
    streams = []
    for hh in range(qf_ref.shape[0]):
        streams.append(_gla_direction(qf_ref.at[hh, 0], vf_ref.at[hh, 0], kf_ref.at[hh, 0], lf_ref.at[hh, 0],
                                      trif_ref[...], sf_ref.at[hh], o_ref.at[hh, 0], start_f, False))
        streams.append(_gla_direction(qb_ref.at[hh, 0], vb_ref.at[hh, 0], kb_ref.at[hh, 0], lb_ref.at[hh, 0],
                                      trib_ref[...], sb_ref.at[hh], o_ref.at[hh, 0], start_b, True))
    _interleave(streams, GLA_STAGGER)


def _gla(qv4, lf4, kk4):
    _, b, seq_len, _ = qv4.shape
    lbk = min(GLA_LB, seq_len)
    assert lbk % CUMSUM_ROWS == 0 and seq_len % lbk == 0
    nblk = seq_len // lbk
    nh = max(h for h in (1, 2) if 2 * h * seq_len * HEAD_DIM * 4 <= GLA_OUT_BLOCK_BYTES or h == 1)
    blk = lambda head0, rev: pl.BlockSpec(
        (nh, 1, lbk, HEAD_DIM),
        (lambda i, h, j: (head0 + h, i, nblk - 1 - j, 0)) if rev else (lambda i, h, j: (head0 + h, i, j, 0)))
    tri_spec = pl.BlockSpec((CUMSUM_ROWS, CUMSUM_ROWS), lambda i, h, j: (0, 0))
    tri_f, tri_b = _cumsum_matrices()
    groups = HEADS // nh
    return pl.pallas_call(
        _gla_kernel,
        grid=(b, groups, nblk),
        in_specs=[tri_spec, tri_spec,
                  blk(0, False), blk(groups, False), blk(0, False), blk(0, False),
                  blk(0, True), blk(groups, True), blk(groups, True), blk(groups, True)],
        out_specs=pl.BlockSpec((nh, 1, seq_len, HEAD_DIM), lambda i, h, j: (h, i, 0, 0)),
        out_shape=jax.ShapeDtypeStruct((HEADS, b, seq_len, HEAD_DIM), F32),
        scratch_shapes=[pltpu.VMEM((nh, HEAD_DIM, HEAD_DIM), F32)] * 2,
        compiler_params=_cparams("parallel", "parallel", "arbitrary"),
        name="gla",
    )(tri_f, tri_b, qv4, qv4, kk4, lf4, qv4, qv4, kk4, lf4)


HALF_D = D_MODEL // 2


def _pack_bf16_pairs(x):
    bits = lax.bitcast_convert_type(x.astype(BF16).astype(F32), jnp.uint32)
    return bits[:, :HALF_D] | (bits[:, HALF_D:] >> 16)


def _unpack_bf16_pairs(words):
    hi = lax.bitcast_convert_type(words & jnp.uint32(0xFFFF0000), F32)
    lo = lax.bitcast_convert_type(words << 16, F32)
    return hi, lo


OUT_TM = 512


def _out_proj_kernel(*refs, nz):
    z_refs = refs[:nz]
    (o_ref, g_ref, x_ref, ling_ref, linb_ref, ng_ref, wz_ref, wh_ref, l1g_ref, l1b_ref, rwh_ref, rwl_ref, rb_ref,
     h_ref, hpk_ref, route_ref, gate_ref, cnt_ref, base_ref) = refs[nz:]
    tm = o_ref.shape[1]

    z = jnp.concatenate([r[0] for r in z_refs], axis=0) if nz > 1 else z_refs[0][0]
    g = g_ref[...].astype(F32)
    normed = []
    for hd in range(HEADS):
        oh = o_ref[hd]
        normed.append(oh * lax.rsqrt(jnp.mean(oh * oh, axis=-1, keepdims=True) + RMS_EPS))
        yield
    hg = jnp.concatenate(normed, axis=1) * ng_ref[...] * (g * jax.nn.sigmoid(g))
    yield
    mixed = jnp.dot(z, wz_ref[...], preferred_element_type=F32)
    mixed += jnp.dot(hg.astype(BF16), wh_ref[...], preferred_element_type=F32)
    yield
    xn = _layer_norm_rows(x_ref[...], ling_ref[...], linb_ref[...])
    h = _layer_norm_rows(DEEPNORM_ALPHA * xn + mixed, l1g_ref[...], l1b_ref[...])
    h_ref[...] = h
    yield
    hpk_ref[...] = _pack_bf16_pairs(h)
    yield

    nt = (((1,), (1,)), ((), ()))
    h_hi = h.astype(BF16)
    h_lo = (h - h_hi.astype(F32)).astype(BF16)
    logits = lax.dot_general(rwh_ref[...], h_hi, nt, preferred_element_type=F32)
    logits += lax.dot_general(rwh_ref[...], h_lo, nt, preferred_element_type=F32)
    logits += lax.dot_general(rwl_ref[...], h_hi, nt, preferred_element_type=F32)
    logits += rb_ref[...]
    yield

    eid_f = lax.broadcasted_iota(jnp.int32, (N_EXPERTS, tm), 0).astype(F32)
    work = logits
    vals, idxs, hits = [], [], []
    for _ in range(TOP_K):
        m = jnp.max(work, axis=0, keepdims=True)
        idx = jnp.min(jnp.where(work == m, eid_f, float(N_EXPERTS)), axis=0, keepdims=True)
        hit = eid_f == idx
        work = jnp.where(hit, -jnp.inf, work)
        vals.append(m)
        idxs.append(idx)
        hits.append(hit)
        yield
    exps = [jnp.exp(v - vals[0]) for v in vals]
    denom = exps[0] + exps[1] + exps[2] + exps[3]

    member = jnp.zeros((N_EXPERTS, tm), F32)
    for hit in hits:
        member = member + jnp.where(hit, 1.0, 0.0)
    s_idx = lax.broadcasted_iota(jnp.int32, (tm, tm), 0)
    t_idx = lax.broadcasted_iota(jnp.int32, (tm, tm), 1)
    earlier = jnp.where(s_idx < t_idx, 1.0, 0.0).astype(BF16)
    base = base_ref[...]
    before = jnp.dot(member.astype(BF16), earlier, preferred_element_type=F32) + base
    yield

    row8 = lax.broadcasted_iota(jnp.int32, (2 * TOP_K, tm), 0)
    route = jnp.zeros((2 * TOP_K, tm), F32)
    row128 = lax.broadcasted_iota(jnp.int32, (LANES, tm), 0)
    gate_t = jnp.zeros((LANES, tm), F32)
    for k in range(TOP_K):
        rank = jnp.sum(jnp.where(hits[k], before, 0.0), axis=0, keepdims=True)
        route = route + jnp.where(row8 == k, idxs[k], 0.0) + jnp.where(row8 == TOP_K + k, rank, 0.0)
        gate_t = gate_t + jnp.where(row128 == k, exps[k] / denom, 0.0)
    route_ref[...] = route.astype(jnp.int32)
    gate_ref[...] = gate_t.T
    total = base + jnp.sum(member, axis=1, keepdims=True)
    base_ref[...] = total
    cnt_ref[...] = total


def _out_proj_stage(zt, o3, g2d, x2d, ln_in_g, ln_in_b, seq_len, norm_g6, w_out_bf16, ln_g, ln_b,
                    rw_hi, rw_lo, rb_col):
    t = o3.shape[1]
    n1 = zt.shape[1]
    tm = max(min(OUT_TM, seq_len), n1)
    nz = tm // n1
    row = lambda i: (i, 0)
    const = lambda i: (0, 0)

    def z_spec(k):
        return pl.BlockSpec((1, n1, FOURIER_WIDTH),
                            lambda i: ((i * tm) // seq_len, 0, ((i * tm) % seq_len) // n1 + k))

    in_specs = [z_spec(k) for k in range(nz)] + [
        pl.BlockSpec((HEADS, tm, HEAD_DIM), lambda i: (0, i, 0)),
        pl.BlockSpec((tm, HGRN_WIDTH), row),
        pl.BlockSpec((tm, D_MODEL), row),
        _resident((1, D_MODEL)),
        _resident((1, D_MODEL)),
        _resident((1, HGRN_WIDTH)),
        _resident((FOURIER_WIDTH, D_MODEL)),
        _resident((HGRN_WIDTH, D_MODEL)),
        _resident((1, D_MODEL)),
        _resident((1, D_MODEL)),
        _resident((N_EXPERTS, D_MODEL)),
        _resident((N_EXPERTS, D_MODEL)),
        _resident((N_EXPERTS, 1)),
    ]
    out_specs = [
        pl.BlockSpec((tm, D_MODEL), row),
        pl.BlockSpec((tm, HALF_D), row),
        pl.BlockSpec((2 * TOP_K, tm), lambda i: (0, i)),
        pl.BlockSpec((tm, LANES), row),
        pl.BlockSpec((N_EXPERTS, 1), const),
    ]
    out_shape = [
        jax.ShapeDtypeStruct((t, D_MODEL), F32),
        jax.ShapeDtypeStruct((t, HALF_D), jnp.uint32),
        jax.ShapeDtypeStruct((2 * TOP_K, t), jnp.int32),
        jax.ShapeDtypeStruct((t, LANES), F32),
        jax.ShapeDtypeStruct((N_EXPERTS, 1), F32),
    ]
    return _Stage(
        body=functools.partial(_out_proj_kernel, nz=nz),
        init=_zero_ref,
        grid=(t // tm,),
        in_specs=in_specs,
        out_specs=out_specs,
        out_shape=out_shape,
        scratch=[pltpu.VMEM((N_EXPERTS, 1), F32)],
        args=[zt] * nz + [o3, g2d, x2d, ln_in_g.reshape(1, -1), ln_in_b.reshape(1, -1), norm_g6, w_out_bf16[:FOURIER_WIDTH], w_out_bf16[FOURIER_WIDTH:],
                          ln_g.reshape(1, -1), ln_b.reshape(1, -1), rw_hi, rw_lo, rb_col],
        sequential=True,
        name="out_proj",
    )


MOE_BLOCK = 1024
MOE_PARTS = 4
COMBINE_TM = 512
COMBINE_PARTS = 2
SC_WINDOW = 128


def _sc_mesh():
    return plsc.VectorSubcoreMesh(core_axis_name="core", subcore_axis_name="subcore")


def _dispatch(dest_kt, hpk, n_slots):
    t, d = hpk.shape
    rows = pl.BlockSpec((SC_WINDOW, d), index_map=lambda i: (i, 0), pipeline_mode=pl.Buffered(1))
    idx = pl.BlockSpec((1, SC_WINDOW), index_map=lambda i: (0, i))

    @pl.kernel(out_type=jax.ShapeDtypeStruct((n_slots, d), hpk.dtype), mesh=_sc_mesh(), name="moe_dispatch_sc")
    def scatter(x_hbm, i0_hbm, i1_hbm, i2_hbm, i3_hbm, o_hbm):
        def body(x_vmem, *idx_vmem):
            for iv in idx_vmem:
                pltpu.sync_copy(x_vmem, o_hbm.at[iv.at[0]])

        pltpu.emit_pipeline(
            body, grid=(t // SC_WINDOW,), in_specs=[rows] + [idx] * TOP_K, out_specs=[],
            core_axis_name=("core", "subcore"), dimension_semantics=(pltpu.PARALLEL,),
        )(x_hbm, i0_hbm, i1_hbm, i2_hbm, i3_hbm)

    return scatter(hpk, *[dest_kt[k].reshape(1, t) for k in range(TOP_K)])


def _gather_rows(yb, dest_kt):
    _, t = dest_kt.shape
    d = yb.shape[1]
    n = TOP_K * t

    @pl.kernel(out_type=jax.ShapeDtypeStruct((n, d), yb.dtype), mesh=_sc_mesh(), name="moe_gather_sc")
    def gather(y_hbm, i_hbm, o_hbm):
        def body(i_vmem, o_vmem):
            pltpu.sync_copy(y_hbm.at[i_vmem.at[0]], o_vmem)

        pltpu.emit_pipeline(
            body, grid=(n // SC_WINDOW,),
            in_specs=[pl.BlockSpec((1, SC_WINDOW), index_map=lambda i: (0, i))],
            out_specs=[pl.BlockSpec((SC_WINDOW, d), index_map=lambda i: (i, 0), pipeline_mode=pl.Buffered(1))],
            core_axis_name=("core", "subcore"), dimension_semantics=(pltpu.PARALLEL,),
        )(i_hbm, o_hbm)

    return gather(yb, dest_kt.reshape(1, n)).reshape(TOP_K, t, d)


def _experts_kernel(be_ref, nused_ref, next_ref, parts_ref, xb_ref, wgu_hbm, bgu_ref, wdn_hbm, bdn_ref, yb_ref,
                    wgu_f32, wdn_f32, wgu_bf, wdn_bf, slot_ref, sem):
    i = pl.program_id(0)
    used = i < nused_ref[0]
    e = be_ref[i]

    def weight_copies(expert, slot):
        return (pltpu.make_async_copy(wgu_hbm.at[expert], wgu_f32.at[slot], sem.at[slot, 0]),
                pltpu.make_async_copy(wdn_hbm.at[expert], wdn_f32.at[slot], sem.at[slot, 1]))

    @pl.when(used & (i == 0))
    def _():
        slot_ref[0] = 0
        for c in weight_copies(e, 0):
            c.start()

    @pl.when(used & (i > 0) & (e != be_ref[jnp.maximum(i - 1, 0)]))
    def _():
        slot_ref[0] = 1 - slot_ref[0]

    @pl.when(used & ((i == 0) | (e != be_ref[jnp.maximum(i - 1, 0)])))
    def _():
        slot = slot_ref[0]
        for c in weight_copies(e, slot):
            c.wait()
        wgu_bf[...] = wgu_f32[slot].astype(BF16)
        wdn_bf[...] = wdn_f32[slot].astype(BF16)
        nxt = next_ref[e]

        @pl.when(nxt != e)
        def _():
            for c in weight_copies(nxt, 1 - slot):
                c.start()

    def mlp(rows):
        x_hi, x_lo = _unpack_bf16_pairs(xb_ref[:rows])
        x = jnp.concatenate([x_hi.astype(BF16), x_lo.astype(BF16)], axis=1)
        gu = jnp.dot(x, wgu_bf[...], preferred_element_type=F32) + bgu_ref[0]
        gate = jnp.minimum(gu[:, :D_FF], SWIGLU_LIMIT)
        up = jnp.clip(gu[:, D_FF:], -SWIGLU_LIMIT, SWIGLU_LIMIT)
        act = (up + 1.0) * gate * jax.nn.sigmoid(SWIGLU_ALPHA * gate)
        y = jnp.dot(act.astype(BF16), wdn_bf[...], preferred_element_type=F32) + bdn_ref[0]
        yb_ref[:rows] = _pack_bf16_pairs(y)

    for parts in range(1, MOE_PARTS + 1):
        pl.when(used & (parts_ref[i] == parts))(functools.partial(mlp, parts * (MOE_BLOCK // MOE_PARTS)))


def _experts(block_expert, n_used, next_expert, parts, xb, wgu, bgu, wdn, bdn):
    n_slots = xb.shape[0]
    nb = n_slots // MOE_BLOCK
    blk = pl.BlockSpec((MOE_BLOCK, HALF_D),
                       lambda i, be, nu, nx, hv: (jnp.minimum(i, jnp.maximum(nu[0] - 1, 0)), 0))
    per_expert = lambda shape: pl.BlockSpec((1,) + shape, lambda i, be, nu, nx, hv: (be[i], 0, 0))
    hbm = pl.BlockSpec(memory_space=pl.ANY)
    grid_spec = pltpu.PrefetchScalarGridSpec(
        num_scalar_prefetch=4,
        grid=(nb,),
        in_specs=[blk, hbm, per_expert((1, 2 * D_FF)), hbm, per_expert((1, D_MODEL))],
        out_specs=blk,
        scratch_shapes=[pltpu.VMEM((2, D_MODEL, 2 * D_FF), F32), pltpu.VMEM((2, D_FF, D_MODEL), F32),
                        pltpu.VMEM((D_MODEL, 2 * D_FF), BF16), pltpu.VMEM((D_FF, D_MODEL), BF16),
                        pltpu.SMEM((1,), jnp.int32), pltpu.SemaphoreType.DMA((2, 2))],
    )
    return pl.pallas_call(
        _experts_kernel,
        grid_spec=grid_spec,
        out_shape=jax.ShapeDtypeStruct((n_slots, HALF_D), jnp.uint32),
        compiler_params=_cparams("arbitrary"),
        name="moe_experts",
    )(block_expert, n_used, next_expert, parts, xb, wgu, bgu, wdn, bdn)


def _combine_kernel(rows_ref, gate_ref, h_ref, g_ref, b_ref, *rest):
    out_ref = rest[-1]
    tm = h_ref.shape[0]
    gate = gate_ref[...]
    ff_hi = jnp.zeros((tm, HALF_D), F32)
    ff_lo = jnp.zeros((tm, HALF_D), F32)
    for k in range(TOP_K):
        hi, lo = _unpack_bf16_pairs(rows_ref[k])
        gk = gate[:, k:k + 1]
        ff_hi = ff_hi + gk * hi
        ff_lo = ff_lo + gk * lo
    ff = jnp.concatenate([ff_hi, ff_lo], axis=1)
    out_ref[...] = _layer_norm_rows(DEEPNORM_ALPHA * h_ref[...] + ff, g_ref[...], b_ref[...])


def _combine(rows_kt, gates, h2d, ln_g, ln_b, token0, prev_out):
    t = h2d.shape[0]
    t_part = rows_kt.shape[1]
    tm = min(COMBINE_TM, t_part)
    first = token0 // tm
    row = lambda i: (first + i, 0)
    const = lambda i: (0, 0)
    in_specs = [
        pl.BlockSpec((TOP_K, tm, HALF_D), lambda i: (0, i, 0)),
        pl.BlockSpec((tm, LANES), row),
        pl.BlockSpec((tm, D_MODEL), row),
        pl.BlockSpec((1, D_MODEL), const),
        pl.BlockSpec((1, D_MODEL), const),
    ]
    args = [rows_kt, gates, h2d, ln_g.reshape(1, -1), ln_b.reshape(1, -1)]
    aliases = {}
    if prev_out is not None:
        in_specs.append(pl.BlockSpec(memory_space=pl.ANY))
        args.append(prev_out)
        aliases = {len(args) - 1: 0}
    return pl.pallas_call(
        _combine_kernel,
        grid=(t_part // tm,),
        in_specs=in_specs,
        out_specs=pl.BlockSpec((tm, D_MODEL), row),
        out_shape=jax.ShapeDtypeStruct((t, D_MODEL), F32),
        input_output_aliases=aliases,
        compiler_params=_cparams("parallel"),
        name="moe_combine",
    )(*args)


def _moe_plan(route, counts_f32, n_tokens):
    eid = route[:TOP_K]
    rank = route[TOP_K:]
    counts = counts_f32[:, 0].astype(jnp.int32)
    padded = ((counts + MOE_BLOCK - 1) // MOE_BLOCK) * MOE_BLOCK
    pend = jnp.cumsum(padded)
    pstart = pend - padded
    experts = jnp.arange(N_EXPERTS, dtype=jnp.int32)[:, None, None]
    dest = rank + jnp.sum(jnp.where(eid[None] == experts, pstart[:, None, None], 0), axis=0)
    n_assign = n_tokens * TOP_K
    n_slots = ((n_assign + MOE_BLOCK - 1) // MOE_BLOCK) * MOE_BLOCK + N_EXPERTS * MOE_BLOCK
    nb = n_slots // MOE_BLOCK
    block_start = jnp.arange(nb, dtype=jnp.int32) * MOE_BLOCK
    block_expert = jnp.sum((pend[None, :] <= block_start[:, None]).astype(jnp.int32), axis=1)
    block_expert = jnp.minimum(block_expert, N_EXPERTS - 1).astype(jnp.int32)
    n_used = (pend[-1:] // MOE_BLOCK).astype(jnp.int32)
    onehot = (block_expert[:, None] == jnp.arange(N_EXPERTS, dtype=jnp.int32)[None, :]).astype(jnp.int32)
    valid_rows = jnp.sum(onehot * (pstart + counts)[None, :], axis=1) - block_start
    part_rows = MOE_BLOCK // MOE_PARTS
    parts = jnp.clip((valid_rows + part_rows - 1) // part_rows, 1, MOE_PARTS).astype(jnp.int32)
    ids = jnp.arange(N_EXPERTS, dtype=jnp.int32)
    later_nonempty = (ids[None, :] > ids[:, None]) & (counts[None, :] > 0)
    next_expert = jnp.min(jnp.where(later_nonempty, ids[None, :], N_EXPERTS), axis=1)
    next_expert = jnp.where(next_expert == N_EXPERTS, ids, next_expert).astype(jnp.int32)
    return dest.astype(jnp.int32), block_expert, n_used, next_expert, parts, n_slots


def _in_stage(x, p):
    t = x.shape[0] * x.shape[1]
    return _in_proj_stage(x.reshape(t, D_MODEL), p["ln_in_g"], p["ln_in_b"], p["lb2"], p["w_in"])


def _mix_and_out_stage(x, in_outs, p):
    batch, seq_len, _ = x.shape
    t = batch * seq_len
    u, qv, lf, kk, og = in_outs
    zt = _fourier_mix(u, batch, seq_len, p["fourier_norm_g"])
    per_seq = lambda a: a.reshape(a.shape[0], batch, seq_len, HEAD_DIM)
    o = _gla(per_seq(qv), per_seq(lf), per_seq(kk))
    return _out_proj_stage(
        zt, o.reshape(HEADS, t, HEAD_DIM), og, x.reshape(t, D_MODEL), p["ln_in_g"], p["ln_in_b"], seq_len,
        p["norm_g6"], p["w_out"], p["ln1_g"], p["ln1_b"], p["rw_hi"], p["rw_lo"], p["rb_col"])


def _moe(x, out_outs, p):
    batch, seq_len, _ = x.shape
    t = batch * seq_len
    h, hpk, route, gates, counts = out_outs
    dest_kt, block_expert, n_used, next_expert, parts, n_slots = _moe_plan(route, counts, t)
    xb = _dispatch(dest_kt, hpk, n_slots)
    yb = _experts(block_expert, n_used, next_expert, parts, xb, p["w_gu"], p["b_gu"], p["w_dn"], p["b_dn"])
    y = None
    part = t // COMBINE_PARTS
    for j in range(COMBINE_PARTS):
        rows = _gather_rows(yb, dest_kt[:, j * part:(j + 1) * part])
        y = _combine(rows, gates, h, p["ln2_g"], p["ln2_b"], j * part, y)
    return y.reshape(batch, seq_len, D_MODEL)


def _prepare_params(ln_in_g, ln_in_b, w_in, fourier_norm_g, lb_gamma, hgrn_norm_g, w_out, ln1_g, ln1_b,
                    router_w, router_b, w_gate_up, b_gate_up, w_down, b_down, ln2_g, ln2_b):
    lb_all = jnp.cumsum(jax.nn.softmax(lb_gamma.astype(F32), axis=1), axis=1)
    rw = router_w[0].astype(F32).T
    rw_hi = rw.astype(BF16)
    rw_lo = (rw - rw_hi.astype(F32)).astype(BF16)
    return dict(
        ln_in_g=ln_in_g, ln_in_b=ln_in_b, w_in=w_in[0].astype(BF16),
        fourier_norm_g=fourier_norm_g[0],
        lb2=jnp.concatenate([lb_all[0, 0], lb_all[1, 0]]).reshape(1, -1),
        norm_g6=jnp.tile(hgrn_norm_g[0].astype(F32), HEADS).reshape(1, -1),
        w_out=w_out[0].astype(BF16), ln1_g=ln1_g[0], ln1_b=ln1_b[0],
        rw_hi=rw_hi, rw_lo=rw_lo, rb_col=router_b[0].astype(F32).reshape(-1, 1),
        w_gu=w_gate_up[0], b_gu=b_gate_up[0].reshape(N_EXPERTS, 1, -1),
        w_dn=w_down[0], b_dn=b_down[0].reshape(N_EXPERTS, 1, -1),
        ln2_g=ln2_g[0], ln2_b=ln2_b[0],
    )


def kernel(x_prompt, x_sample, ln_in_g, ln_in_b, w_in, fourier_norm_g, lb_gamma, hgrn_norm_g, w_out,
           ln1_g, ln1_b, router_w, router_b, w_gate_up, b_gate_up, w_down, b_down, ln2_g, ln2_b):
    p = _prepare_params(ln_in_g, ln_in_b, w_in, fourier_norm_g, lb_gamma, hgrn_norm_g, w_out, ln1_g, ln1_b,
                        router_w, router_b, w_gate_up, b_gate_up, w_down, b_down, ln2_g, ln2_b)
    first, second = x_sample, x_prompt
    (in_first,) = _run_stages([_in_stage(first, p)])
    out_stage_first = _mix_and_out_stage(first, in_first, p)
    in_stage_second = _in_stage(second, p)
    if in_stage_second.grid == out_stage_first.grid:
        in_second, out_first = _run_stages([in_stage_second, out_stage_first])
    else:
        (in_second,) = _run_stages([in_stage_second])
        (out_first,) = _run_stages([out_stage_first])
    (out_second,) = _run_stages([_mix_and_out_stage(second, in_second, p)])
    y_first = _moe(first, out_first, p)
    y_second = _moe(second, out_second, p)
    return (y_second, y_first)
```

```python
import functools
import math
from typing import Callable, NamedTuple

import numpy as np
import jax
import jax.numpy as jnp
from jax import lax
from jax.experimental import pallas as pl
from jax.experimental.pallas import tpu as pltpu
from jax.experimental.pallas import tpu_sc as plsc

D_MODEL = 1024
FOURIER_WIDTH = 256
FOURIER_GROUP_DIM = 64
HGRN_WIDTH = 768
HEAD_DIM = 128
HEADS = 6
CHUNK = 64
N_EXPERTS = 32
TOP_K = 4
D_FF = 1024
SWIGLU_LIMIT = 7.0
SWIGLU_ALPHA = 1.702
LN_EPS = 1e-5
RMS_EPS = 1e-6
DEEPNORM_ALPHA = 2.0 ** 0.25

LANES = 128
VMEM_LIMIT_BYTES = 56 * 1024 * 1024

F32 = jnp.float32
BF16 = jnp.bfloat16


def _cparams(*sem):
    return pltpu.CompilerParams(dimension_semantics=sem, vmem_limit_bytes=VMEM_LIMIT_BYTES)


def _layer_norm_rows(x, g, b):
    mu = jnp.mean(x, axis=-1, keepdims=True)
    xc = x - mu
    var = jnp.mean(xc * xc, axis=-1, keepdims=True)
    return xc * lax.rsqrt(var + LN_EPS) * g + b


IN_TM = 512
IN_TN = 512


def _in_proj_kernel(x_ref, g_ref, b_ref, lb_ref, w_ref, u_ref, qv_ref, lf_ref, kk_ref, og_ref):
    xb = _layer_norm_rows(x_ref[...], g_ref[...], b_ref[...]).astype(BF16)
    hw = HGRN_WIDTH
    yield

    def store_heads(ref, first_head, c0, val):
        for j in range(val.shape[1] // HEAD_DIM):
            ref[first_head + c0 // HEAD_DIM + j] = val[:, j * HEAD_DIM:(j + 1) * HEAD_DIM]

    def chunks(col, width):
        for c0 in range(0, width, IN_TN):
            cw = min(IN_TN, width - c0)
            yield c0, cw, jnp.dot(xb, w_ref[:, col + c0:col + c0 + cw], preferred_element_type=F32)

    for c0, cw, acc in chunks(0, FOURIER_WIDTH):
        u_ref[:, c0:c0 + cw] = acc.astype(BF16)
        yield
    for c0, cw, acc in chunks(FOURIER_WIDTH, hw):
        store_heads(qv_ref, 0, c0, (acc * jax.nn.sigmoid(acc) * (HEAD_DIM ** -0.5)).astype(BF16))
        yield
    for c0, cw, acc in chunks(FOURIER_WIDTH + hw, hw):
        store_heads(qv_ref, HEADS, c0, acc.astype(BF16))
        yield
    for c0, cw, acc in chunks(FOURIER_WIDTH + 2 * hw, 2 * hw):
        lb = lb_ref[:, c0:c0 + cw]
        fg = lb + (1.0 - lb) * jax.nn.sigmoid(acc)
        store_heads(lf_ref, 0, c0, jnp.log(fg))
        store_heads(kk_ref, 0, c0, (1.0 - fg).astype(BF16))
        yield
    for c0, cw, acc in chunks(FOURIER_WIDTH + 4 * hw, hw):
        og_ref[:, c0:c0 + cw] = acc.astype(BF16)
        yield


class _Stage(NamedTuple):
    body: Callable
    init: Callable | None
    grid: tuple
    in_specs: list
    out_specs: list
    out_shape: list
    scratch: list
    args: list
    sequential: bool
    name: str


_DONE = object()


def _interleave(pieces, stagger):
    done = [False] * len(pieces)
    t = 0
    while not all(done):
        for i, g in enumerate(pieces):
            if not done[i] and t >= i * stagger:
                done[i] = next(g, _DONE) is _DONE
        t += 1


def _run_stages(stages):
    assert all(s.grid == stages[0].grid for s in stages)
    n_in = [len(s.in_specs) for s in stages]
    n_out = [len(s.out_specs) for s in stages]
    n_scr = [len(s.scratch) for s in stages]

    def body(*refs):
        scr = sum(n_in) + sum(n_out)
        for s, c in zip(stages, n_scr):
            if s.init is not None:
                pl.when(pl.program_id(0) == 0)(functools.partial(s.init, *refs[scr:scr + c]))
            scr += c
        ins, outs, scr = 0, sum(n_in), sum(n_in) + sum(n_out)
        bodies = []
        for s, a, b, c in zip(stages, n_in, n_out, n_scr):
            bodies.append(s.body(*refs[ins:ins + a], *refs[outs:outs + b], *refs[scr:scr + c]))
            ins, outs, scr = ins + a, outs + b, scr + c
        _interleave(bodies, 0)

    results = pl.pallas_call(
        body,
        grid=stages[0].grid,
        in_specs=[sp for s in stages for sp in s.in_specs],
        out_specs=[sp for s in stages for sp in s.out_specs],
        out_shape=[sh for s in stages for sh in s.out_shape],
        scratch_shapes=[sc for s in stages for sc in s.scratch],
        compiler_params=_cparams("arbitrary" if any(s.sequential for s in stages) else "parallel"),
        name="_".join(s.name for s in stages),
    )(*[a for s in stages for a in s.args])
    split, pos = [], 0
    for b in n_out:
        split.append(results[pos:pos + b])
        pos += b
    return split


def _zero_ref(ref):
    ref[...] = jnp.zeros(ref.shape, ref.dtype)


def _resident(shape):
    return pl.BlockSpec(shape, lambda i: (0,) * len(shape), pipeline_mode=pl.Buffered(1))


def _in_proj_stage(x2d, ln_g, ln_b, lb2, w_bf16):
    t = x2d.shape[0]
    tm = min(IN_TM, t)
    row = lambda i: (i, 0)
    flat = lambda w, dt: (pl.BlockSpec((tm, w), row), jax.ShapeDtypeStruct((t, w), dt))
    head_major = lambda nh, dt: (pl.BlockSpec((nh, tm, HEAD_DIM), lambda i: (0, i, 0)),
                                 jax.ShapeDtypeStruct((nh, t, HEAD_DIM), dt))
    outs = [flat(FOURIER_WIDTH, BF16), head_major(2 * HEADS, BF16),
            head_major(2 * HEADS, F32), head_major(2 * HEADS, BF16), flat(HGRN_WIDTH, BF16)]
    return _Stage(
        body=_in_proj_kernel,
        init=None,
        grid=(t // tm,),
        in_specs=[pl.BlockSpec((tm, D_MODEL), row), _resident((1, D_MODEL)), _resident((1, D_MODEL)),
                  _resident((1, 2 * HGRN_WIDTH)), _resident(w_bf16.shape)],
        out_specs=[o[0] for o in outs],
        out_shape=[o[1] for o in outs],
        scratch=[],
        args=[x2d, ln_g.reshape(1, -1), ln_b.reshape(1, -1), lb2, w_bf16],
        sequential=False,
        name="in_proj",
    )


def _fft_split(seq_len):
    n1 = 1 << ((seq_len.bit_length() - 1 + 1) // 2)
    return n1, seq_len // n1


@functools.lru_cache(maxsize=None)
def _fft_tables(seq_len):
    n1, n2 = _fft_split(seq_len)
    k1 = np.arange(n1)
    ang1 = 2.0 * np.pi * ((k1[:, None] * k1[None, :]) % n1) / n1
    s1 = 1.0 / math.sqrt(n1)
    c1, s1m = np.cos(ang1) * s1, np.sin(ang1) * s1
    l1p = np.arange(n1)[:, None, None]
    l2p = np.arange(n2)[None, :, None]
    l2 = np.arange(n2)[None, None, :]
    ang2 = 2.0 * np.pi * ((l2 * (l1p + n1 * l2p)) % seq_len) / seq_len
    s2 = 1.0 / math.sqrt(n2)
    gc, gs = np.cos(ang2) * s2, np.sin(ang2) * s2
    kc = np.arange(FOURIER_GROUP_DIM)
    angc = 2.0 * np.pi * ((kc[:, None] * kc[None, :]) % FOURIER_GROUP_DIM) / FOURIER_GROUP_DIM
    sc = 1.0 / math.sqrt(FOURIER_GROUP_DIM)
    groups = FOURIER_WIDTH // FOURIER_GROUP_DIM
    bc = np.kron(np.eye(groups), np.cos(angc) * sc)
    bs = np.kron(np.eye(groups), np.sin(angc) * sc)
    as_bf16 = lambda a: jnp.asarray(a, dtype=F32).astype(BF16)
    return tuple(as_bf16(a) for a in (c1, s1m, gc, gs, bc, bs))


def _fft1_kernel(c_ref, s_ref, u_ref, ar_ref, ai_ref):
    u = u_ref[0]
    ar_ref[0] = jnp.dot(c_ref[...], u, preferred_element_type=F32).astype(BF16)
    ai_ref[0] = (-jnp.dot(s_ref[...], u, preferred_element_type=F32)).astype(BF16)


FFT1_TN = 4096


def _fft_stage1(u3, c1, s1):
    b, n1, width = u3.shape
    tn = min(FFT1_TN, width)
    blk = pl.BlockSpec((1, n1, tn), lambda i, j: (i, 0, j))
    mat = pl.BlockSpec((n1, n1), lambda i, j: (0, 0))
    return pl.pallas_call(
        _fft1_kernel,
        grid=(b, width // tn),
        in_specs=[mat, mat, blk],
        out_specs=[blk, blk],
        out_shape=[jax.ShapeDtypeStruct(u3.shape, BF16)] * 2,
        compiler_params=_cparams("parallel", "parallel"),
        name="fft_stage1",
    )(c1, s1, u3)


FFT2_ROWS = 1024


def _fft2_kernel(gc_ref, gs_ref, bc_ref, bs_ref, g_ref, ar_ref, ai_ref, z_ref):
    t1, n2 = ar_ref.shape[1], ar_ref.shape[2]
    xr, xi = [], []
    for j in range(t1):
        ar, ai = ar_ref[0, j], ai_ref[0, j]
        gc, gs = gc_ref[j], gs_ref[j]
        xr.append((jnp.dot(gc, ar, preferred_element_type=F32)
                   + jnp.dot(gs, ai, preferred_element_type=F32)).astype(BF16))
        xi.append((jnp.dot(gc, ai, preferred_element_type=F32)
                   - jnp.dot(gs, ar, preferred_element_type=F32)).astype(BF16))
    z = jnp.dot(jnp.concatenate(xr, axis=0), bc_ref[...], preferred_element_type=F32)
    z += jnp.dot(jnp.concatenate(xi, axis=0), bs_ref[...], preferred_element_type=F32)
    z = (z * lax.rsqrt(jnp.mean(z * z, axis=-1, keepdims=True) + RMS_EPS) * g_ref[...]).astype(BF16)
    for j in range(t1):
        z_ref[0, j] = z[j * n2:(j + 1) * n2]


def _fft_stage2(ar4, ai4, gc, gs, bc, bs, gain):
    b, n1, n2, w = ar4.shape
    t1 = min(max(FFT2_ROWS // n2, 1), n1)
    a_blk = pl.BlockSpec((1, t1, n2, w), lambda i, j: (i, j, 0, 0))
    g_blk = pl.BlockSpec((t1, n2, n2), lambda i, j: (j, 0, 0))
    c_blk = pl.BlockSpec((w, w), lambda i, j: (0, 0))
    return pl.pallas_call(
        _fft2_kernel,
        grid=(b, n1 // t1),
        in_specs=[g_blk, g_blk, c_blk, c_blk, pl.BlockSpec((1, w), lambda i, j: (0, 0)), a_blk, a_blk],
        out_specs=a_blk,
        out_shape=jax.ShapeDtypeStruct(ar4.shape, BF16),
        compiler_params=_cparams("parallel", "parallel"),
        name="fft_stage2",
    )(gc, gs, bc, bs, gain.reshape(1, -1), ar4, ai4)


def _fourier_mix(u2d, batch, seq_len, gain):
    n1, n2 = _fft_split(seq_len)
    c1, s1, gc, gs, bc, bs = _fft_tables(seq_len)
    u3 = u2d.reshape(batch, n1, n2 * FOURIER_WIDTH)
    ar, ai = _fft_stage1(u3, c1, s1)
    shape4 = (batch, n1, n2, FOURIER_WIDTH)
    zt = _fft_stage2(ar.reshape(shape4), ai.reshape(shape4), gc, gs, bc, bs, gain)
    return zt.reshape(batch, n1, n2 * FOURIER_WIDTH)


GLA_LB = 1024
GLA_OUT_BLOCK_BYTES = 32 * 1024 * 1024
GLA_SKEW = 1
GLA_STAGGER = 0
CUMSUM_ROWS = 256


@functools.lru_cache(maxsize=None)
def _cumsum_matrices():
    r = np.arange(CUMSUM_ROWS)
    same_chunk = (r[:, None] // CHUNK) == (r[None, :] // CHUNK)
    prefix = same_chunk & (r[None, :] <= r[:, None])
    suffix = same_chunk & (r[None, :] >= r[:, None])
    return (jnp.asarray(prefix, dtype=F32).astype(BF16), jnp.asarray(suffix, dtype=F32).astype(BF16))


def _chunk_cumsum(x, tri):
    hi = x.astype(BF16)
    lo = (x - hi.astype(F32)).astype(BF16)
    width = x.shape[1]
    parts = []
    for r0 in range(0, x.shape[0], CUMSUM_ROWS):
        rows = slice(r0, r0 + CUMSUM_ROWS)
        both = jnp.dot(tri, jnp.concatenate([hi[rows], lo[rows]], axis=1), preferred_element_type=F32)
        parts.append(both[:, :width] + both[:, width:])
    return parts[0] if len(parts) == 1 else jnp.concatenate(parts, axis=0)


def _gla_direction(q_ref, v_ref, k_ref, lf_ref, tri, st_ref, o_ref, start, reverse):
    n = q_ref.shape[0]
    b = _chunk_cumsum(lf_ref[...], tri)
    yield
    mid = CHUNK // 2 if reverse else CHUNK // 2 - 1
    last = 0 if reverse else CHUNK - 1
    t_idx = lax.broadcasted_iota(jnp.int32, (CHUNK, CHUNK), 0)
    s_idx = lax.broadcasted_iota(jnp.int32, (CHUNK, CHUNK), 1)
    visible = (t_idx <= s_idx) if reverse else (t_idx >= s_idx)
    nchunks = n // CHUNK
    order = range(nchunks - 1, -1, -1) if reverse else range(nchunks)
    nt = (((1,), (1,)), ((), ()))
    tn = (((0,), (0,)), ((), ()))
    rows = [slice(c * CHUNK, (c + 1) * CHUNK) for c in range(nchunks)]
    st = st_ref[...]
    pending = []
    seq = list(order)
    for step in range(nchunks + GLA_SKEW):
        if step < nchunks:
            c = seq[step]
            bc = b[rows[c]]
            b_mid = bc[mid:mid + 1]
            b_last = bc[last:last + 1]
            qe = (q_ref[rows[c]].astype(F32) * jnp.exp(bc - b_mid)).astype(BF16)
            ke = (k_ref[rows[c]].astype(F32) * jnp.exp(b_mid - bc)).astype(BF16)
            s = lax.dot_general(qe, ke, nt, preferred_element_type=F32)
            scores = jnp.where(visible, s, 0.0).astype(BF16)
            delta_t = lax.dot_general(v_ref[rows[c]], ke, tn, preferred_element_type=F32) * jnp.exp(b_last - b_mid)
            pending.append((c, qe, scores, delta_t, jnp.exp(b_mid), jnp.exp(b_last)))
            yield
        if step >= GLA_SKEW:
            c, qe, scores, delta_t, e_mid, e_last = pending.pop(0)
            st_in = (st * e_mid).astype(BF16)
            st = st * e_last + delta_t
            o = jnp.dot(scores, v_ref[rows[c]], preferred_element_type=F32)
            o += lax.dot_general(qe, st_in, nt, preferred_element_type=F32)
            o_ref[pl.ds(start + c * CHUNK, CHUNK), :] += o
            yield
    st_ref[...] = st


def _gla_kernel(trif_ref, trib_ref, qf_ref, vf_ref, kf_ref, lf_ref, qb_ref, vb_ref, kb_ref, lb_ref,
                o_ref, sf_ref, sb_ref):
    j = pl.program_id(2)
    nblk = pl.num_programs(2)
    lb_rows = qf_ref.shape[2]

    @pl.when(j == 0)
    def _():
        o_ref[...] = jnp.zeros(o_ref.shape, o_ref.dtype)
        sf_ref[...] = jnp.zeros(sf_ref.shape, sf_ref.dtype)
        sb_ref[...] = jnp.zeros(sb_ref.shape, sb_ref.dtype)

    start_f = pl.multiple_of(j * lb_rows, lb_rows)
    start_b = pl.multiple_of((nblk - 1 - j) * lb_rows, lb_rows)
    streams = []
    for hh in range(qf_ref.shape[0]):
        streams.append(_gla_direction(qf_ref.at[hh, 0], vf_ref.at[hh, 0], kf_ref.at[hh, 0], lf_ref.at[hh, 0],
                                      trif_ref[...], sf_ref.at[hh], o_ref.at[hh, 0], start_f, False))
        streams.append(_gla_direction(qb_ref.at[hh, 0], vb_ref.at[hh, 0], kb_ref.at[hh, 0], lb_ref.at[hh, 0],
                                      trib_ref[...], sb_ref.at[hh], o_ref.at[hh, 0], start_b, True))
    _interleave(streams, GLA_STAGGER)


def _gla(qv4, lf4, kk4):
    _, b, seq_len, _ = qv4.shape
    lbk = min(GLA_LB, seq_len)
    assert lbk % CUMSUM_ROWS == 0 and seq_len % lbk == 0
    nblk = seq_len // lbk
    nh = max(h for h in (1, 2) if 2 * h * seq_len * HEAD_DIM * 4 <= GLA_OUT_BLOCK_BYTES or h == 1)
    blk = lambda head0, rev: pl.BlockSpec(
        (nh, 1, lbk, HEAD_DIM),
        (lambda i, h, j: (head0 + h, i, nblk - 1 - j, 0)) if rev else (lambda i, h, j: (head0 + h, i, j, 0)))
    tri_spec = pl.BlockSpec((CUMSUM_ROWS, CUMSUM_ROWS), lambda i, h, j: (0, 0))
    tri_f, tri_b = _cumsum_matrices()
    groups = HEADS // nh
    return pl.pallas_call(
        _gla_kernel,
        grid=(b, groups, nblk),
        in_specs=[tri_spec, tri_spec,
                  blk(0, False), blk(groups, False), blk(0, False), blk(0, False),
                  blk(0, True), blk(groups, True), blk(groups, True), blk(groups, True)],
        out_specs=pl.BlockSpec((nh, 1, seq_len, HEAD_DIM), lambda i, h, j: (h, i, 0, 0)),
        out_shape=jax.ShapeDtypeStruct((HEADS, b, seq_len, HEAD_DIM), F32),
        scratch_shapes=[pltpu.VMEM((nh, HEAD_DIM, HEAD_DIM), F32)] * 2,
        compiler_params=_cparams("parallel", "parallel", "arbitrary"),
        name="gla",
    )(tri_f, tri_b, qv4, qv4, kk4, lf4, qv4, qv4, kk4, lf4)


HALF_D = D_MODEL // 2


def _pack_bf16_pairs(x):
    bits = lax.bitcast_convert_type(x.astype(BF16).astype(F32), jnp.uint32)
    return bits[:, :HALF_D] | (bits[:, HALF_D:] >> 16)


def _unpack_bf16_pairs(words):
    hi = lax.bitcast_convert_type(words & jnp.uint32(0xFFFF0000), F32)
    lo = lax.bitcast_convert_type(words << 16, F32)
    return hi, lo


OUT_TM = 512


def _out_proj_kernel(*refs, nz):
    z_refs = refs[:nz]
    (o_ref, g_ref, x_ref, ling_ref, linb_ref, ng_ref, wz_ref, wh_ref, l1g_ref, l1b_ref, rwh_ref, rwl_ref, rb_ref,
     h_ref, hpk_ref, route_ref, gate_ref, cnt_ref, base_ref) = refs[nz:]
    tm = o_ref.shape[1]

    z = jnp.concatenate([r[0] for r in z_refs], axis=0) if nz > 1 else z_refs[0][0]
    g = g_ref[...].astype(F32)
    normed = []
    for hd in range(HEADS):
        oh = o_ref[hd]
        normed.append(oh * lax.rsqrt(jnp.mean(oh * oh, axis=-1, keepdims=True) + RMS_EPS))
        yield
    hg = jnp.concatenate(normed, axis=1) * ng_ref[...] * (g * jax.nn.sigmoid(g))
    yield
    mixed = jnp.dot(z, wz_ref[...], preferred_element_type=F32)
    mixed += jnp.dot(hg.astype(BF16), wh_ref[...], preferred_element_type=F32)
    yield
    xn = _layer_norm_rows(x_ref[...], ling_ref[...], linb_ref[...])
    h = _layer_norm_rows(DEEPNORM_ALPHA * xn + mixed, l1g_ref[...], l1b_ref[...])
    h_ref[...] = h
    yield
    hpk_ref[...] = _pack_bf16_pairs(h)
    yield

    nt = (((1,), (1,)), ((), ()))
    h_hi = h.astype(BF16)
    h_lo = (h - h_hi.astype(F32)).astype(BF16)
    logits = lax.dot_general(rwh_ref[...], h_hi, nt, preferred_element_type=F32)
    logits += lax.dot_general(rwh_ref[...], h_lo, nt, preferred_element_type=F32)
    logits += lax.dot_general(rwl_ref[...], h_hi, nt, preferred_element_type=F32)
    logits += rb_ref[...]
    yield

    eid_f = lax.broadcasted_iota(jnp.int32, (N_EXPERTS, tm), 0).astype(F32)
    work = logits
    vals, idxs, hits = [], [], []
    for _ in range(TOP_K):
        m = jnp.max(work, axis=0, keepdims=True)
        idx = jnp.min(jnp.where(work == m, eid_f, float(N_EXPERTS)), axis=0, keepdims=True)
        hit = eid_f == idx
        work = jnp.where(hit, -jnp.inf, work)
        vals.append(m)
        idxs.append(idx)
        hits.append(hit)
        yield
    exps = [jnp.exp(v - vals[0]) for v in vals]
    denom = exps[0] + exps[1] + exps[2] + exps[3]

    member = jnp.zeros((N_EXPERTS, tm), F32)
    for hit in hits:
        member = member + jnp.where(hit, 1.0, 0.0)
    s_idx = lax.broadcasted_iota(jnp.int32, (tm, tm), 0)
    t_idx = lax.broadcasted_iota(jnp.int32, (tm, tm), 1)
    earlier = jnp.where(s_idx < t_idx, 1.0, 0.0).astype(BF16)
    base = base_ref[...]
    before = jnp.dot(member.astype(BF16), earlier, preferred_element_type=F32) + base
    yield

    row8 = lax.broadcasted_iota(jnp.int32, (2 * TOP_K, tm), 0)
    route = jnp.zeros((2 * TOP_K, tm), F32)
    row128 = lax.broadcasted_iota(jnp.int32, (LANES, tm), 0)
    gate_t = jnp.zeros((LANES, tm), F32)
    for k in range(TOP_K):
        rank = jnp.sum(jnp.where(hits[k], before, 0.0), axis=0, keepdims=True)
        route = route + jnp.where(row8 == k, idxs[k], 0.0) + jnp.where(row8 == TOP_K + k, rank, 0.0)
        gate_t = gate_t + jnp.where(row128 == k, exps[k] / denom, 0.0)
    route_ref[...] = route.astype(jnp.int32)
    gate_ref[...] = gate_t.T
    total = base + jnp.sum(member, axis=1, keepdims=True)
    base_ref[...] = total
    cnt_ref[...] = total


def _out_proj_stage(zt, o3, g2d, x2d, ln_in_g, ln_in_b, seq_len, norm_g6, w_out_bf16, ln_g, ln_b,
                    rw_hi, rw_lo, rb_col):
    t = o3.shape[1]
    n1 = zt.shape[1]
    tm = max(min(OUT_TM, seq_len), n1)
    nz = tm // n1
    row = lambda i: (i, 0)
    const = lambda i: (0, 0)

    def z_spec(k):
        return pl.BlockSpec((1, n1, FOURIER_WIDTH),
                            lambda i: ((i * tm) // seq_len, 0, ((i * tm) % seq_len) // n1 + k))

    in_specs = [z_spec(k) for k in range(nz)] + [
        pl.BlockSpec((HEADS, tm, HEAD_DIM), lambda i: (0, i, 0)),
        pl.BlockSpec((tm, HGRN_WIDTH), row),
        pl.BlockSpec((tm, D_MODEL), row),
        _resident((1, D_MODEL)),
        _resident((1, D_MODEL)),
        _resident((1, HGRN_WIDTH)),
        _resident((FOURIER_WIDTH, D_MODEL)),
        _resident((HGRN_WIDTH, D_MODEL)),
        _resident((1, D_MODEL)),
        _resident((1, D_MODEL)),
        _resident((N_EXPERTS, D_MODEL)),
        _resident((N_EXPERTS, D_MODEL)),
        _resident((N_EXPERTS, 1)),
    ]
    out_specs = [
        pl.BlockSpec((tm, D_MODEL), row),
        pl.BlockSpec((tm, HALF_D), row),
        pl.BlockSpec((2 * TOP_K, tm), lambda i: (0, i)),
        pl.BlockSpec((tm, LANES), row),
        pl.BlockSpec((N_EXPERTS, 1), const),
    ]
    out_shape = [
        jax.ShapeDtypeStruct((t, D_MODEL), F32),
        jax.ShapeDtypeStruct((t, HALF_D), jnp.uint32),
        jax.ShapeDtypeStruct((2 * TOP_K, t), jnp.int32),
        jax.ShapeDtypeStruct((t, LANES), F32),
        jax.ShapeDtypeStruct((N_EXPERTS, 1), F32),
    ]
    return _Stage(
        body=functools.partial(_out_proj_kernel, nz=nz),
        init=_zero_ref,
        grid=(t // tm,),
        in_specs=in_specs,
        out_specs=out_specs,
        out_shape=out_shape,
        scratch=[pltpu.VMEM((N_EXPERTS, 1), F32)],
        args=[zt] * nz + [o3, g2d, x2d, ln_in_g.reshape(1, -1), ln_in_b.reshape(1, -1), norm_g6, w_out_bf16[:FOURIER_WIDTH], w_out_bf16[FOURIER_WIDTH:],
                          ln_g.reshape(1, -1), ln_b.reshape(1, -1), rw_hi, rw_lo, rb_col],
        sequential=True,
        name="out_proj",
    )


MOE_BLOCK = 1024
MOE_PARTS = 4
COMBINE_TM = 512
COMBINE_PARTS = 2
SC_WINDOW = 128


def _sc_mesh():
    return plsc.VectorSubcoreMesh(core_axis_name="core", subcore_axis_name="subcore")


def _dispatch(dest_kt, hpk, n_slots):
    t, d = hpk.shape
    rows = pl.BlockSpec((SC_WINDOW, d), index_map=lambda i: (i, 0), pipeline_mode=pl.Buffered(1))
    idx = pl.BlockSpec((1, SC_WINDOW), index_map=lambda i: (0, i))

    @pl.kernel(out_type=jax.ShapeDtypeStruct((n_slots, d), hpk.dtype), mesh=_sc_mesh(), name="moe_dispatch_sc")
    def scatter(x_hbm, i0_hbm, i1_hbm, i2_hbm, i3_hbm, o_hbm):
        def body(x_vmem, *idx_vmem):
            for iv in idx_vmem:
                pltpu.sync_copy(x_vmem, o_hbm.at[iv.at[0]])

        pltpu.emit_pipeline(
            body, grid=(t // SC_WINDOW,), in_specs=[rows] + [idx] * TOP_K, out_specs=[],
            core_axis_name=("core", "subcore"), dimension_semantics=(pltpu.PARALLEL,),
        )(x_hbm, i0_hbm, i1_hbm, i2_hbm, i3_hbm)

    return scatter(hpk, *[dest_kt[k].reshape(1, t) for k in range(TOP_K)])


def _gather_rows(yb, dest_kt):
    _, t = dest_kt.shape
    d = yb.shape[1]
    n = TOP_K * t

    @pl.kernel(out_type=jax.ShapeDtypeStruct((n, d), yb.dtype), mesh=_sc_mesh(), name="moe_gather_sc")
    def gather(y_hbm, i_hbm, o_hbm):
        def body(i_vmem, o_vmem):
            pltpu.sync_copy(y_hbm.at[i_vmem.at[0]], o_vmem)

        pltpu.emit_pipeline(
            body, grid=(n // SC_WINDOW,),
            in_specs=[pl.BlockSpec((1, SC_WINDOW), index_map=lambda i: (0, i))],
            out_specs=[pl.BlockSpec((SC_WINDOW, d), index_map=lambda i: (i, 0), pipeline_mode=pl.Buffered(1))],
            core_axis_name=("core", "subcore"), dimension_semantics=(pltpu.PARALLEL,),
        )(i_hbm, o_hbm)

    return gather(yb, dest_kt.reshape(1, n)).reshape(TOP_K, t, d)


def _experts_kernel(be_ref, nused_ref, next_ref, parts_ref, xb_ref, wgu_hbm, bgu_ref, wdn_hbm, bdn_ref, yb_ref,
                    wgu_f32, wdn_f32, wgu_bf, wdn_bf, slot_ref, sem):
    i = pl.program_id(0)
    used = i < nused_ref[0]
    e = be_ref[i]

    def weight_copies(expert, slot):
        return (pltpu.make_async_copy(wgu_hbm.at[expert], wgu_f32.at[slot], sem.at[slot, 0]),
                pltpu.make_async_copy(wdn_hbm.at[expert], wdn_f32.at[slot], sem.at[slot, 1]))

    @pl.when(used & (i == 0))
    def _():
        slot_ref[0] = 0
        for c in weight_copies(e, 0):
            c.start()

    @pl.when(used & (i > 0) & (e != be_ref[jnp.maximum(i - 1, 0)]))
    def _():
        slot_ref[0] = 1 - slot_ref[0]

    @pl.when(used & ((i == 0) | (e != be_ref[jnp.maximum(i - 1, 0)])))
    def _():
        slot = slot_ref[0]
        for c in weight_copies(e, slot):
            c.wait()
        wgu_bf[...] = wgu_f32[slot].astype(BF16)
        wdn_bf[...] = wdn_f32[slot].astype(BF16)
        nxt = next_ref[e]

        @pl.when(nxt != e)
        def _():
            for c in weight_copies(nxt, 1 - slot):
                c.start()

    def mlp(rows):
        x_hi, x_lo = _unpack_bf16_pairs(xb_ref[:rows])
        x = jnp.concatenate([x_hi.astype(BF16), x_lo.astype(BF16)], axis=1)
        gu = jnp.dot(x, wgu_bf[...], preferred_element_type=F32) + bgu_ref[0]
        gate = jnp.minimum(gu[:, :D_FF], SWIGLU_LIMIT)
        up = jnp.clip(gu[:, D_FF:], -SWIGLU_LIMIT, SWIGLU_LIMIT)
        act = (up + 1.0) * gate * jax.nn.sigmoid(SWIGLU_ALPHA * gate)
        y = jnp.dot(act.astype(BF16), wdn_bf[...], preferred_element_type=F32) + bdn_ref[0]
        yb_ref[:rows] = _pack_bf16_pairs(y)

    for parts in range(1, MOE_PARTS + 1):
        pl.when(used & (parts_ref[i] == parts))(functools.partial(mlp, parts * (MOE_BLOCK // MOE_PARTS)))


def _experts(block_expert, n_used, next_expert, parts, xb, wgu, bgu, wdn, bdn):
    n_slots = xb.shape[0]
    nb = n_slots // MOE_BLOCK
    blk = pl.BlockSpec((MOE_BLOCK, HALF_D),
                       lambda i, be, nu, nx, hv: (jnp.minimum(i, jnp.maximum(nu[0] - 1, 0)), 0))
    per_expert = lambda shape: pl.BlockSpec((1,) + shape, lambda i, be, nu, nx, hv: (be[i], 0, 0))
    hbm = pl.BlockSpec(memory_space=pl.ANY)
    grid_spec = pltpu.PrefetchScalarGridSpec(
        num_scalar_prefetch=4,
        grid=(nb,),
        in_specs=[blk, hbm, per_expert((1, 2 * D_FF)), hbm, per_expert((1, D_MODEL))],
        out_specs=blk,
        scratch_shapes=[pltpu.VMEM((2, D_MODEL, 2 * D_FF), F32), pltpu.VMEM((2, D_FF, D_MODEL), F32),
                        pltpu.VMEM((D_MODEL, 2 * D_FF), BF16), pltpu.VMEM((D_FF, D_MODEL), BF16),
                        pltpu.SMEM((1,), jnp.int32), pltpu.SemaphoreType.DMA((2, 2))],
    )
    return pl.pallas_call(
        _experts_kernel,
        grid_spec=grid_spec,
        out_shape=jax.ShapeDtypeStruct((n_slots, HALF_D), jnp.uint32),
        compiler_params=_cparams("arbitrary"),
        name="moe_experts",
    )(block_expert, n_used, next_expert, parts, xb, wgu, bgu, wdn, bdn)


def _combine_kernel(rows_ref, gate_ref, h_ref, g_ref, b_ref, *rest):
    out_ref = rest[-1]
    tm = h_ref.shape[0]
    gate = gate_ref[...]
    ff_hi = jnp.zeros((tm, HALF_D), F32)
    ff_lo = jnp.zeros((tm, HALF_D), F32)
    for k in range(TOP_K):
        hi, lo = _unpack_bf16_pairs(rows_ref[k])
        gk = gate[:, k:k + 1]
        ff_hi = ff_hi + gk * hi
        ff_lo = ff_lo + gk * lo
    ff = jnp.concatenate([ff_hi, ff_lo], axis=1)
    out_ref[...] = _layer_norm_rows(DEEPNORM_ALPHA * h_ref[...] + ff, g_ref[...], b_ref[...])


def _combine(rows_kt, gates, h2d, ln_g, ln_b, token0, prev_out):
    t = h2d.shape[0]
    t_part = rows_kt.shape[1]
    tm = min(COMBINE_TM, t_part)
    first = token0 // tm
    row = lambda i: (first + i, 0)
    const = lambda i: (0, 0)
    in_specs = [
        pl.BlockSpec((TOP_K, tm, HALF_D), lambda i: (0, i, 0)),
        pl.BlockSpec((tm, LANES), row),
        pl.BlockSpec((tm, D_MODEL), row),
        pl.BlockSpec((1, D_MODEL), const),
        pl.BlockSpec((1, D_MODEL), const),
    ]
    args = [rows_kt, gates, h2d, ln_g.reshape(1, -1), ln_b.reshape(1, -1)]
    aliases = {}
    if prev_out is not None:
        in_specs.append(pl.BlockSpec(memory_space=pl.ANY))
        args.append(prev_out)
        aliases = {len(args) - 1: 0}
    return pl.pallas_call(
        _combine_kernel,
        grid=(t_part // tm,),
        in_specs=in_specs,
        out_specs=pl.BlockSpec((tm, D_MODEL), row),
        out_shape=jax.ShapeDtypeStruct((t, D_MODEL), F32),
        input_output_aliases=aliases,
        compiler_params=_cparams("parallel"),
        name="moe_combine",
    )(*args)


def _moe_plan(route, counts_f32, n_tokens):
    eid = route[:TOP_K]
    rank = route[TOP_K:]
    counts = counts_f32[:, 0].astype(jnp.int32)
    padded = ((counts + MOE_BLOCK - 1) // MOE_BLOCK) * MOE_BLOCK
    pend = jnp.cumsum(padded)
    pstart = pend - padded
    experts = jnp.arange(N_EXPERTS, dtype=jnp.int32)[:, None, None]
    dest = rank + jnp.sum(jnp.where(eid[None] == experts, pstart[:, None, None], 0), axis=0)
    n_assign = n_tokens * TOP_K
    n_slots = ((n_assign + MOE_BLOCK - 1) // MOE_BLOCK) * MOE_BLOCK + N_EXPERTS * MOE_BLOCK
    nb = n_slots // MOE_BLOCK
    block_start = jnp.arange(nb, dtype=jnp.int32) * MOE_BLOCK
    block_expert = jnp.sum((pend[None, :] <= block_start[:, None]).astype(jnp.int32), axis=1)
    block_expert = jnp.minimum(block_expert, N_EXPERTS - 1).astype(jnp.int32)
    n_used = (pend[-1:] // MOE_BLOCK).astype(jnp.int32)
    onehot = (block_expert[:, None] == jnp.arange(N_EXPERTS, dtype=jnp.int32)[None, :]).astype(jnp.int32)
    valid_rows = jnp.sum(onehot * (pstart + counts)[None, :], axis=1) - block_start
    part_rows = MOE_BLOCK // MOE_PARTS
    parts = jnp.clip((valid_rows + part_rows - 1) // part_rows, 1, MOE_PARTS).astype(jnp.int32)
    ids = jnp.arange(N_EXPERTS, dtype=jnp.int32)
    later_nonempty = (ids[None, :] > ids[:, None]) & (counts[None, :] > 0)
    next_expert = jnp.min(jnp.where(later_nonempty, ids[None, :], N_EXPERTS), axis=1)
    next_expert = jnp.where(next_expert == N_EXPERTS, ids, next_expert).astype(jnp.int32)
    return dest.astype(jnp.int32), block_expert, n_used, next_expert, parts, n_slots


def _in_stage(x, p):
    t = x.shape[0] * x.shape[1]
    return _in_proj_stage(x.reshape(t, D_MODEL), p["ln_in_g"], p["ln_in_b"], p["lb2"], p["w_in"])


def _mix_and_out_stage(x, in_outs, p):
    batch, seq_len, _ = x.shape
    t = batch * seq_len
    u, qv, lf, kk, og = in_outs
    zt = _fourier_mix(u, batch, seq_len, p["fourier_norm_g"])
    per_seq = lambda a: a.reshape(a.shape[0], batch, seq_len, HEAD_DIM)
    o = _gla(per_seq(qv), per_seq(lf), per_seq(kk))
    return _out_proj_stage(
        zt, o.reshape(HEADS, t, HEAD_DIM), og, x.reshape(t, D_MODEL), p["ln_in_g"], p["ln_in_b"], seq_len,
        p["norm_g6"], p["w_out"], p["ln1_g"], p["ln1_b"], p["rw_hi"], p["rw_lo"], p["rb_col"])


def _moe(x, out_outs, p):
    batch, seq_len, _ = x.shape
    t = batch * seq_len
    h, hpk, route, gates, counts = out_outs
    dest_kt, block_expert, n_used, next_expert, parts, n_slots = _moe_plan(route, counts, t)
    xb = _dispatch(dest_kt, hpk, n_slots)
    yb = _experts(block_expert, n_used, next_expert, parts, xb, p["w_gu"], p["b_gu"], p["w_dn"], p["b_dn"])
    y = None
    part = t // COMBINE_PARTS
    for j in range(COMBINE_PARTS):
        rows = _gather_rows(yb, dest_kt[:, j * part:(j + 1) * part])
        y = _combine(rows, gates, h, p["ln2_g"], p["ln2_b"], j * part, y)
    return y.reshape(batch, seq_len, D_MODEL)


def _prepare_params(ln_in_g, ln_in_b, w_in, fourier_norm_g, lb_gamma, hgrn_norm_g, w_out, ln1_g, ln1_b,
                    router_w, router_b, w_gate_up, b_gate_up, w_down, b_down, ln2_g, ln2_b):
    lb_all = jnp.cumsum(jax.nn.softmax(lb_gamma.astype(F32), axis=1), axis=1)
    rw = router_w[0].astype(F32).T
    rw_hi = rw.astype(BF16)
    rw_lo = (rw - rw_hi.astype(F32)).astype(BF16)
    return dict(
        ln_in_g=ln_in_g, ln_in_b=ln_in_b, w_in=w_in[0].astype(BF16),
        fourier_norm_g=fourier_norm_g[0],
        lb2=jnp.concatenate([lb_all[0, 0], lb_all[1, 0]]).reshape(1, -1),
        norm_g6=jnp.tile(hgrn_norm_g[0].astype(F32), HEADS).reshape(1, -1),
        w_out=w_out[0].astype(BF16), ln1_g=ln1_g[0], ln1_b=ln1_b[0],
        rw_hi=rw_hi, rw_lo=rw_lo, rb_col=router_b[0].astype(F32).reshape(-1, 1),
        w_gu=w_gate_up[0], b_gu=b_gate_up[0].reshape(N_EXPERTS, 1, -1),
        w_dn=w_down[0], b_dn=b_down[0].reshape(N_EXPERTS, 1, -1),
        ln2_g=ln2_g[0], ln2_b=ln2_b[0],
    )


def kernel(x_prompt, x_sample, ln_in_g, ln_in_b, w_in, fourier_norm_g, lb_gamma, hgrn_norm_g, w_out,
           ln1_g, ln1_b, router_w, router_b, w_gate_up, b_gate_up, w_down, b_down, ln2_g, ln2_b):
    p = _prepare_params(ln_in_g, ln_in_b, w_in, fourier_norm_g, lb_gamma, hgrn_norm_g, w_out, ln1_g, ln1_b,
                        router_w, router_b, w_gate_up, b_gate_up, w_down, b_down, ln2_g, ln2_b)
    first, second = x_prompt, x_sample
    (in_first,) = _run_stages([_in_stage(first, p)])
    out_stage_first = _mix_and_out_stage(first, in_first, p)
    in_stage_second = _in_stage(second, p)
    if in_stage_second.grid == out_stage_first.grid:
        in_second, out_first = _run_stages([in_stage_second, out_stage_first])
    else:
        (in_second,) = _run_stages([in_stage_second])
        (out_first,) = _run_stages([out_stage_first])
    (out_second,) = _run_stages([_mix_and_out_stage(second, in_second, p)])
    y_first = _moe(first, out_first, p)
    y_second = _moe(second, out_second, p)
    return (y_first, y_second)
```

```python
import functools
import math
from typing import Callable, NamedTuple

import numpy as np
import jax
import jax.numpy as jnp
from jax import lax
from jax.experimental import pallas as pl
from jax.experimental.pallas import tpu as pltpu
from jax.experimental.pallas import tpu_sc as plsc

D_MODEL = 1024
FOURIER_WIDTH = 256
FOURIER_GROUP_DIM = 64
HGRN_WIDTH = 768
HEAD_DIM = 128
HEADS = 6
CHUNK = 64
N_EXPERTS = 32
TOP_K = 4
D_FF = 1024
SWIGLU_LIMIT = 7.0
SWIGLU_ALPHA = 1.702
LN_EPS = 1e-5
RMS_EPS = 1e-6
DEEPNORM_ALPHA = 2.0 ** 0.25

LANES = 128
VMEM_LIMIT_BYTES = 56 * 1024 * 1024

F32 = jnp.float32
BF16 = jnp.bfloat16


def _cparams(*sem):
    return pltpu.CompilerParams(dimension_semantics=sem, vmem_limit_bytes=VMEM_LIMIT_BYTES)


def _layer_norm_rows(x, g, b):
    mu = jnp.mean(x, axis=-1, keepdims=True)
    xc = x - mu
    var = jnp.mean(xc * xc, axis=-1, keepdims=True)
    return xc * lax.rsqrt(var + LN_EPS) * g + b


IN_TM = 512
IN_TN = 512


def _in_proj_kernel(x_ref, g_ref, b_ref, lb_ref, w_ref, u_ref, qv_ref, lf_ref, kk_ref, og_ref):
    xb = _layer_norm_rows(x_ref[...], g_ref[...], b_ref[...]).astype(BF16)
    hw = HGRN_WIDTH
    yield

    def store_heads(ref, first_head, c0, val):
        for j in range(val.shape[1] // HEAD_DIM):
            ref[first_head + c0 // HEAD_DIM + j] = val[:, j * HEAD_DIM:(j + 1) * HEAD_DIM]

    def chunks(col, width):
        for c0 in range(0, width, IN_TN):
            cw = min(IN_TN, width - c0)
            yield c0, cw, jnp.dot(xb, w_ref[:, col + c0:col + c0 + cw], preferred_element_type=F32)

    for c0, cw, acc in chunks(0, FOURIER_WIDTH):
        u_ref[:, c0:c0 + cw] = acc.astype(BF16)
        yield
    for c0, cw, acc in chunks(FOURIER_WIDTH, hw):
        store_heads(qv_ref, 0, c0, (acc * jax.nn.sigmoid(acc) * (HEAD_DIM ** -0.5)).astype(BF16))
        yield
    for c0, cw, acc in chunks(FOURIER_WIDTH + hw, hw):
        store_heads(qv_ref, HEADS, c0, acc.astype(BF16))
        yield
    for c0, cw, acc in chunks(FOURIER_WIDTH + 2 * hw, 2 * hw):
        lb = lb_ref[:, c0:c0 + cw]
        fg = lb + (1.0 - lb) * jax.nn.sigmoid(acc)
        store_heads(lf_ref, 0, c0, jnp.log(fg))
        store_heads(kk_ref, 0, c0, (1.0 - fg).astype(BF16))
        yield
    for c0, cw, acc in chunks(FOURIER_WIDTH + 4 * hw, hw):
        og_ref[:, c0:c0 + cw] = acc.astype(BF16)
        yield


class _Stage(NamedTuple):
    body: Callable
    init: Callable | None
    grid: tuple
    in_specs: list
    out_specs: list
    out_shape: list
    scratch: list
    args: list
    sequential: bool
    name: str


_DONE = object()


def _interleave(pieces, stagger):
    done = [False] * len(pieces)
    t = 0
    while not all(done):
        for i, g in enumerate(pieces):
            if not done[i] and t >= i * stagger:
                done[i] = next(g, _DONE) is _DONE
        t += 1


def _run_stages(stages):
    assert all(s.grid == stages[0].grid for s in stages)
    n_in = [len(s.in_specs) for s in stages]
    n_out = [len(s.out_specs) for s in stages]
    n_scr = [len(s.scratch) for s in stages]

    def body(*refs):
        scr = sum(n_in) + sum(n_out)
        for s, c in zip(stages, n_scr):
            if s.init is not None:
                pl.when(pl.program_id(0) == 0)(functools.partial(s.init, *refs[scr:scr + c]))
            scr += c
        ins, outs, scr = 0, sum(n_in), sum(n_in) + sum(n_out)
        bodies = []
        for s, a, b, c in zip(stages, n_in, n_out, n_scr):
            bodies.append(s.body(*refs[ins:ins + a], *refs[outs:outs + b], *refs[scr:scr + c]))
            ins, outs, scr = ins + a, outs + b, scr + c
        _interleave(bodies, 0)

    results = pl.pallas_call(
        body,
        grid=stages[0].grid,
        in_specs=[sp for s in stages for sp in s.in_specs],
        out_specs=[sp for s in stages for sp in s.out_specs],
        out_shape=[sh for s in stages for sh in s.out_shape],
        scratch_shapes=[sc for s in stages for sc in s.scratch],
        compiler_params=_cparams("arbitrary" if any(s.sequential for s in stages) else "parallel"),
        name="_".join(s.name for s in stages),
    )(*[a for s in stages for a in s.args])
    split, pos = [], 0
    for b in n_out:
        split.append(results[pos:pos + b])
        pos += b
    return split


def _zero_ref(ref):
    ref[...] = jnp.zeros(ref.shape, ref.dtype)


def _resident(shape):
    return pl.BlockSpec(shape, lambda i: (0,) * len(shape), pipeline_mode=pl.Buffered(1))


def _in_proj_stage(x2d, ln_g, ln_b, lb2, w_bf16):
    t = x2d.shape[0]
    tm = min(IN_TM, t)
    row = lambda i: (i, 0)
    flat = lambda w, dt: (pl.BlockSpec((tm, w), row), jax.ShapeDtypeStruct((t, w), dt))
    head_major = lambda nh, dt: (pl.BlockSpec((nh, tm, HEAD_DIM), lambda i: (0, i, 0)),
                                 jax.ShapeDtypeStruct((nh, t, HEAD_DIM), dt))
    outs = [flat(FOURIER_WIDTH, BF16), head_major(2 * HEADS, BF16),
            head_major(2 * HEADS, F32), head_major(2 * HEADS, BF16), flat(HGRN_WIDTH, BF16)]
    return _Stage(
        body=_in_proj_kernel,
        init=None,
        grid=(t // tm,),
        in_specs=[pl.BlockSpec((tm, D_MODEL), row), _resident((1, D_MODEL)), _resident((1, D_MODEL)),
                  _resident((1, 2 * HGRN_WIDTH)), _resident(w_bf16.shape)],
        out_specs=[o[0] for o in outs],
        out_shape=[o[1] for o in outs],
        scratch=[],
        args=[x2d, ln_g.reshape(1, -1), ln_b.reshape(1, -1), lb2, w_bf16],
        sequential=False,
        name="in_proj",
    )


def _fft_split(seq_len):
    n1 = 1 << ((seq_len.bit_length() - 1 + 1) // 2)
    return n1, seq_len // n1


@functools.lru_cache(maxsize=None)
def _fft_tables(seq_len):
    n1, n2 = _fft_split(seq_len)
    k1 = np.arange(n1)
    ang1 = 2.0 * np.pi * ((k1[:, None] * k1[None, :]) % n1) / n1
    s1 = 1.0 / math.sqrt(n1)
    c1, s1m = np.cos(ang1) * s1, np.sin(ang1) * s1
    l1p = np.arange(n1)[:, None, None]
    l2p = np.arange(n2)[None, :, None]
    l2 = np.arange(n2)[None, None, :]
    ang2 = 2.0 * np.pi * ((l2 * (l1p + n1 * l2p)) % seq_len) / seq_len
    s2 = 1.0 / math.sqrt(n2)
    gc, gs = np.cos(ang2) * s2, np.sin(ang2) * s2
    kc = np.arange(FOURIER_GROUP_DIM)
    angc = 2.0 * np.pi * ((kc[:, None] * kc[None, :]) % FOURIER_GROUP_DIM) / FOURIER_GROUP_DIM
    sc = 1.0 / math.sqrt(FOURIER_GROUP_DIM)
    groups = FOURIER_WIDTH // FOURIER_GROUP_DIM
    bc = np.kron(np.eye(groups), np.cos(angc) * sc)
    bs = np.kron(np.eye(groups), np.sin(angc) * sc)
    as_bf16 = lambda a: jnp.asarray(a, dtype=F32).astype(BF16)
    return tuple(as_bf16(a) for a in (c1, s1m, gc, gs, bc, bs))


def _fft1_kernel(c_ref, s_ref, u_ref, ar_ref, ai_ref):
    u = u_ref[0]
    ar_ref[0] = jnp.dot(c_ref[...], u, preferred_element_type=F32).astype(BF16)
    ai_ref[0] = (-jnp.dot(s_ref[...], u, preferred_element_type=F32)).astype(BF16)


FFT1_TN = 4096


def _fft_stage1(u3, c1, s1):
    b, n1, width = u3.shape
    tn = min(FFT1_TN, width)
    blk = pl.BlockSpec((1, n1, tn), lambda i, j: (i, 0, j))
    mat = pl.BlockSpec((n1, n1), lambda i, j: (0, 0))
    return pl.pallas_call(
        _fft1_kernel,
        grid=(b, width // tn),
        in_specs=[mat, mat, blk],
        out_specs=[blk, blk],
        out_shape=[jax.ShapeDtypeStruct(u3.shape, BF16)] * 2,
        compiler_params=_cparams("parallel", "parallel"),
        name="fft_stage1",
    )(c1, s1, u3)


FFT2_ROWS = 1024


def _fft2_kernel(gc_ref, gs_ref, bc_ref, bs_ref, g_ref, ar_ref, ai_ref, z_ref):
    t1, n2 = ar_ref.shape[1], ar_ref.shape[2]
    xr, xi = [], []
    for j in range(t1):
        ar, ai = ar_ref[0, j], ai_ref[0, j]
        gc, gs = gc_ref[j], gs_ref[j]
        xr.append((jnp.dot(gc, ar, preferred_element_type=F32)
                   + jnp.dot(gs, ai, preferred_element_type=F32)).astype(BF16))
        xi.append((jnp.dot(gc, ai, preferred_element_type=F32)
                   - jnp.dot(gs, ar, preferred_element_type=F32)).astype(BF16))
    z = jnp.dot(jnp.concatenate(xr, axis=0), bc_ref[...], preferred_element_type=F32)
    z += jnp.dot(jnp.concatenate(xi, axis=0), bs_ref[...], preferred_element_type=F32)
    z = (z * lax.rsqrt(jnp.mean(z * z, axis=-1, keepdims=True) + RMS_EPS) * g_ref[...]).astype(BF16)
    for j in range(t1):
        z_ref[0, j] = z[j * n2:(j + 1) * n2]


def _fft_stage2(ar4, ai4, gc, gs, bc, bs, gain):
    b, n1, n2, w = ar4.shape
    t1 = min(max(FFT2_ROWS // n2, 1), n1)
    a_blk = pl.BlockSpec((1, t1, n2, w), lambda i, j: (i, j, 0, 0))
    g_blk = pl.BlockSpec((t1, n2, n2), lambda i, j: (j, 0, 0))
    c_blk = pl.BlockSpec((w, w), lambda i, j: (0, 0))
    return pl.pallas_call(
        _fft2_kernel,
        grid=(b, n1 // t1),
        in_specs=[g_blk, g_blk, c_blk, c_blk, pl.BlockSpec((1, w), lambda i, j: (0, 0)), a_blk, a_blk],
        out_specs=a_blk,
        out_shape=jax.ShapeDtypeStruct(ar4.shape, BF16),
        compiler_params=_cparams("parallel", "parallel"),
        name="fft_stage2",
    )(gc, gs, bc, bs, gain.reshape(1, -1), ar4, ai4)


def _fourier_mix(u2d, batch, seq_len, gain):
    n1, n2 = _fft_split(seq_len)
    c1, s1, gc, gs, bc, bs = _fft_tables(seq_len)
    u3 = u2d.reshape(batch, n1, n2 * FOURIER_WIDTH)
    ar, ai = _fft_stage1(u3, c1, s1)
    shape4 = (batch, n1, n2, FOURIER_WIDTH)
    zt = _fft_stage2(ar.reshape(shape4), ai.reshape(shape4), gc, gs, bc, bs, gain)
    return zt.reshape(batch, n1, n2 * FOURIER_WIDTH)


GLA_LB = (2048, 1024, 512)
GLA_OUT_BLOCK_BYTES = 32 * 1024 * 1024
GLA_VMEM_BUDGET = 46 * 1024 * 1024
GLA_SKEW = 1
GLA_STAGGER = 0
CUMSUM_ROWS = 256


@functools.lru_cache(maxsize=None)
def _cumsum_matrices():
    r = np.arange(CUMSUM_ROWS)
    same_chunk = (r[:, None] // CHUNK) == (r[None, :] // CHUNK)
    prefix = same_chunk & (r[None, :] <= r[:, None])
    suffix = same_chunk & (r[None, :] >= r[:, None])
    return (jnp.asarray(prefix, dtype=F32).astype(BF16), jnp.asarray(suffix, dtype=F32).astype(BF16))


def _chunk_cumsum(x, tri):
    hi = x.astype(BF16)
    lo = (x - hi.astype(F32)).astype(BF16)
    width = x.shape[1]
    parts = []
    for r0 in range(0, x.shape[0], CUMSUM_ROWS):
        rows = slice(r0, r0 + CUMSUM_ROWS)
        both = jnp.dot(tri, jnp.concatenate([hi[rows], lo[rows]], axis=1), preferred_element_type=F32)
        parts.append(both[:, :width] + both[:, width:])
    return parts[0] if len(parts) == 1 else jnp.concatenate(parts, axis=0)


def _gla_direction(q_ref, v_ref, k_ref, lf_ref, tri, st_ref, o_ref, start, reverse):
    n = q_ref.shape[0]
    b = _chunk_cumsum(lf_ref[...], tri)
    yield
    mid = CHUNK // 2 if reverse else CHUNK // 2 - 1
    last = 0 if reverse else CHUNK - 1
    t_idx = lax.broadcasted_iota(jnp.int32, (CHUNK, CHUNK), 0)
    s_idx = lax.broadcasted_iota(jnp.int32, (CHUNK, CHUNK), 1)
    visible = (t_idx <= s_idx) if reverse else (t_idx >= s_idx)
    nchunks = n // CHUNK
    order = range(nchunks - 1, -1, -1) if reverse else range(nchunks)
    nt = (((1,), (1,)), ((), ()))
    tn = (((0,), (0,)), ((), ()))
    rows = [slice(c * CHUNK, (c + 1) * CHUNK) for c in range(nchunks)]
    st = st_ref[...]
    pending = []
    seq = list(order)
    for step in range(nchunks + GLA_SKEW):
        if step < nchunks:
            c = seq[step]
            bc = b[rows[c]]
            b_mid = bc[mid:mid + 1]
            b_last = bc[last:last + 1]
            qe = (q_ref[rows[c]].astype(F32) * jnp.exp(bc - b_mid)).astype(BF16)
            ke = (k_ref[rows[c]].astype(F32) * jnp.exp(b_mid - bc)).astype(BF16)
            s = lax.dot_general(qe, ke, nt, preferred_element_type=F32)
            scores = jnp.where(visible, s, 0.0).astype(BF16)
            delta_t = lax.dot_general(v_ref[rows[c]], ke, tn, preferred_element_type=F32) * jnp.exp(b_last - b_mid)
            pending.append((c, qe, scores, delta_t, jnp.exp(b_mid), jnp.exp(b_last)))
            yield
        if step >= GLA_SKEW:
            c, qe, scores, delta_t, e_mid, e_last = pending.pop(0)
            st_in = (st * e_mid).astype(BF16)
            st = st * e_last + delta_t
            o = jnp.dot(scores, v_ref[rows[c]], preferred_element_type=F32)
            o += lax.dot_general(qe, st_in, nt, preferred_element_type=F32)
            o_ref[pl.ds(start + c * CHUNK, CHUNK), :] += o
            yield
    st_ref[...] = st


def _gla_kernel(trif_ref, trib_ref, qf_ref, vf_ref, kf_ref, lf_ref, qb_ref, vb_ref, kb_ref, lb_ref,
                o_ref, sf_ref, sb_ref):
    j = pl.program_id(2)
    nblk = pl.num_programs(2)
    lb_rows = qf_ref.shape[2]

    @pl.when(j == 0)
    def _():
        o_ref[...] = jnp.zeros(o_ref.shape, o_ref.dtype)
        sf_ref[...] = jnp.zeros(sf_ref.shape, sf_ref.dtype)
        sb_ref[...] = jnp.zeros(sb_ref.shape, sb_ref.dtype)

    start_f = pl.multiple_of(j * lb_rows, lb_rows)
    start_b = pl.multiple_of((nblk - 1 - j) * lb_rows, lb_rows)
    streams = []
    for hh in range(qf_ref.shape[0]):
        streams.append(_gla_direction(qf_ref.at[hh, 0], vf_ref.at[hh, 0], kf_ref.at[hh, 0], lf_ref.at[hh, 0],
                                      trif_ref[...], sf_ref.at[hh], o_ref.at[hh, 0], start_f, False))
        streams.append(_gla_direction(qb_ref.at[hh, 0], vb_ref.at[hh, 0], kb_ref.at[hh, 0], lb_ref.at[hh, 0],
                                      trib_ref[...], sb_ref.at[hh], o_ref.at[hh, 0], start_b, True))
    _interleave(streams, GLA_STAGGER)


def _gla(qv4, lf4, kk4):
    _, b, seq_len, _ = qv4.shape
    nh = max(h for h in (1, 2) if 2 * h * seq_len * HEAD_DIM * 4 <= GLA_OUT_BLOCK_BYTES or h == 1)
    out_bytes = 2 * nh * seq_len * HEAD_DIM * 4
    in_bytes = lambda rows: 2 * nh * rows * HEAD_DIM * (6 * 2 + 2 * 4)
    lbk = max([r for r in GLA_LB if seq_len % r == 0 and out_bytes + in_bytes(r) <= GLA_VMEM_BUDGET]
              or [CUMSUM_ROWS])
    assert lbk % CUMSUM_ROWS == 0 and seq_len % lbk == 0
    nblk = seq_len // lbk
    blk = lambda head0, rev: pl.BlockSpec(
        (nh, 1, lbk, HEAD_DIM),
        (lambda i, h, j: (head0 + h, i, nblk - 1 - j, 0)) if rev else (lambda i, h, j: (head0 + h, i, j, 0)))
    tri_spec = pl.BlockSpec((CUMSUM_ROWS, CUMSUM_ROWS), lambda i, h, j: (0, 0))
    tri_f, tri_b = _cumsum_matrices()
    groups = HEADS // nh
    return pl.pallas_call(
        _gla_kernel,
        grid=(b, groups, nblk),
        in_specs=[tri_spec, tri_spec,
                  blk(0, False), blk(groups, False), blk(0, False), blk(0, False),
                  blk(0, True), blk(groups, True), blk(groups, True), blk(groups, True)],
        out_specs=pl.BlockSpec((nh, 1, seq_len, HEAD_DIM), lambda i, h, j: (h, i, 0, 0)),
        out_shape=jax.ShapeDtypeStruct((HEADS, b, seq_len, HEAD_DIM), F32),
        scratch_shapes=[pltpu.VMEM((nh, HEAD_DIM, HEAD_DIM), F32)] * 2,
        compiler_params=_cparams("parallel", "parallel", "arbitrary"),
        name="gla",
    )(tri_f, tri_b, qv4, qv4, kk4, lf4, qv4, qv4, kk4, lf4)


HALF_D = D_MODEL // 2


def _pack_bf16_pairs(x):
    bits = lax.bitcast_convert_type(x.astype(BF16).astype(F32), jnp.uint32)
    return bits[:, :HALF_D] | (bits[:, HALF_D:] >> 16)


def _unpack_bf16_pairs(words):
    hi = lax.bitcast_convert_type(words & jnp.uint32(0xFFFF0000), F32)
    lo = lax.bitcast_convert_type(words << 16, F32)
    return hi, lo


OUT_TM = 512


def _out_proj_kernel(*refs, nz):
    z_refs = refs[:nz]
    (o_ref, g_ref, x_ref, ling_ref, linb_ref, ng_ref, wz_ref, wh_ref, l1g_ref, l1b_ref, rwh_ref, rwl_ref, rb_ref,
     h_ref, hpk_ref, route_ref, gate_ref, cnt_ref, base_ref) = refs[nz:]
    tm = o_ref.shape[1]

    z = jnp.concatenate([r[0] for r in z_refs], axis=0) if nz > 1 else z_refs[0][0]
    g = g_ref[...].astype(F32)
    normed = []
    for hd in range(HEADS):
        oh = o_ref[hd]
        normed.append(oh * lax.rsqrt(jnp.mean(oh * oh, axis=-1, keepdims=True) + RMS_EPS))
        yield
    hg = jnp.concatenate(normed, axis=1) * ng_ref[...] * (g * jax.nn.sigmoid(g))
    yield
    mixed = jnp.dot(z, wz_ref[...], preferred_element_type=F32)
    mixed += jnp.dot(hg.astype(BF16), wh_ref[...], preferred_element_type=F32)
    yield
    xn = _layer_norm_rows(x_ref[...], ling_ref[...], linb_ref[...])
    h = _layer_norm_rows(DEEPNORM_ALPHA * xn + mixed, l1g_ref[...], l1b_ref[...])
    h_ref[...] = h
    yield
    hpk_ref[...] = _pack_bf16_pairs(h)
    yield

    nt = (((1,), (1,)), ((), ()))
    h_hi = h.astype(BF16)
    h_lo = (h - h_hi.astype(F32)).astype(BF16)
    logits = lax.dot_general(rwh_ref[...], h_hi, nt, preferred_element_type=F32)
    logits += lax.dot_general(rwh_ref[...], h_lo, nt, preferred_element_type=F32)
    logits += lax.dot_general(rwl_ref[...], h_hi, nt, preferred_element_type=F32)
    logits += rb_ref[...]
    yield

    eid_f = lax.broadcasted_iota(jnp.int32, (N_EXPERTS, tm), 0).astype(F32)
    work = logits
    vals, idxs, hits = [], [], []
    for _ in range(TOP_K):
        m = jnp.max(work, axis=0, keepdims=True)
        idx = jnp.min(jnp.where(work == m, eid_f, float(N_EXPERTS)), axis=0, keepdims=True)
        hit = eid_f == idx
        work = jnp.where(hit, -jnp.inf, work)
        vals.append(m)
        idxs.append(idx)
        hits.append(hit)
        yield
    exps = [jnp.exp(v - vals[0]) for v in vals]
    denom = exps[0] + exps[1] + exps[2] + exps[3]

    member = jnp.zeros((N_EXPERTS, tm), F32)
    for hit in hits:
        member = member + jnp.where(hit, 1.0, 0.0)
    s_idx = lax.broadcasted_iota(jnp.int32, (tm, tm), 0)
    t_idx = lax.broadcasted_iota(jnp.int32, (tm, tm), 1)
    earlier = jnp.where(s_idx < t_idx, 1.0, 0.0).astype(BF16)
    base = base_ref[...]
    before = jnp.dot(member.astype(BF16), earlier, preferred_element_type=F32) + base
    yield

    row8 = lax.broadcasted_iota(jnp.int32, (2 * TOP_K, tm), 0)
    route = jnp.zeros((2 * TOP_K, tm), F32)
    row128 = lax.broadcasted_iota(jnp.int32, (LANES, tm), 0)
    gate_t = jnp.zeros((LANES, tm), F32)
    for k in range(TOP_K):
        rank = jnp.sum(jnp.where(hits[k], before, 0.0), axis=0, keepdims=True)
        route = route + jnp.where(row8 == k, idxs[k], 0.0) + jnp.where(row8 == TOP_K + k, rank, 0.0)
        gate_t = gate_t + jnp.where(row128 == k, exps[k] / denom, 0.0)
    route_ref[...] = route.astype(jnp.int32)
    gate_ref[...] = gate_t.T
    total = base + jnp.sum(member, axis=1, keepdims=True)
    base_ref[...] = total
    cnt_ref[...] = total


def _out_proj_stage(zt, o3, g2d, x2d, ln_in_g, ln_in_b, seq_len, norm_g6, w_out_bf16, ln_g, ln_b,
                    rw_hi, rw_lo, rb_col):
    t = o3.shape[1]
    n1 = zt.shape[1]
    tm = max(min(OUT_TM, seq_len), n1)
    nz = tm // n1
    row = lambda i: (i, 0)
    const = lambda i: (0, 0)

    def z_spec(k):
        return pl.BlockSpec((1, n1, FOURIER_WIDTH),
                            lambda i: ((i * tm) // seq_len, 0, ((i * tm) % seq_len) // n1 + k))

    in_specs = [z_spec(k) for k in range(nz)] + [
        pl.BlockSpec((HEADS, tm, HEAD_DIM), lambda i: (0, i, 0)),
        pl.BlockSpec((tm, HGRN_WIDTH), row),
        pl.BlockSpec((tm, D_MODEL), row),
        _resident((1, D_MODEL)),
        _resident((1, D_MODEL)),
        _resident((1, HGRN_WIDTH)),
        _resident((FOURIER_WIDTH, D_MODEL)),
        _resident((HGRN_WIDTH, D_MODEL)),
        _resident((1, D_MODEL)),
        _resident((1, D_MODEL)),
        _resident((N_EXPERTS, D_MODEL)),
        _resident((N_EXPERTS, D_MODEL)),
        _resident((N_EXPERTS, 1)),
    ]
    out_specs = [
        pl.BlockSpec((tm, D_MODEL), row),
        pl.BlockSpec((tm, HALF_D), row),
        pl.BlockSpec((2 * TOP_K, tm), lambda i: (0, i)),
        pl.BlockSpec((tm, LANES), row),
        pl.BlockSpec((N_EXPERTS, 1), const),
    ]
    out_shape = [
        jax.ShapeDtypeStruct((t, D_MODEL), F32),
        jax.ShapeDtypeStruct((t, HALF_D), jnp.uint32),
        jax.ShapeDtypeStruct((2 * TOP_K, t), jnp.int32),
        jax.ShapeDtypeStruct((t, LANES), F32),
        jax.ShapeDtypeStruct((N_EXPERTS, 1), F32),
    ]
    return _Stage(
        body=functools.partial(_out_proj_kernel, nz=nz),
        init=_zero_ref,
        grid=(t // tm,),
        in_specs=in_specs,
        out_specs=out_specs,
        out_shape=out_shape,
        scratch=[pltpu.VMEM((N_EXPERTS, 1), F32)],
        args=[zt] * nz + [o3, g2d, x2d, ln_in_g.reshape(1, -1), ln_in_b.reshape(1, -1), norm_g6, w_out_bf16[:FOURIER_WIDTH], w_out_bf16[FOURIER_WIDTH:],
                          ln_g.reshape(1, -1), ln_b.reshape(1, -1), rw_hi, rw_lo, rb_col],
        sequential=True,
        name="out_proj",
    )


MOE_BLOCK = 1024
MOE_PARTS = 4
COMBINE_TM = 512
COMBINE_PARTS = 2
SC_WINDOW = 128


def _sc_mesh():
    return plsc.VectorSubcoreMesh(core_axis_name="core", subcore_axis_name="subcore")


def _dispatch(dest_kt, hpk, n_slots):
    t, d = hpk.shape
    rows = pl.BlockSpec((SC_WINDOW, d), index_map=lambda i: (i, 0), pipeline_mode=pl.Buffered(1))
    idx = pl.BlockSpec((1, SC_WINDOW), index_map=lambda i: (0, i))

    @pl.kernel(out_type=jax.ShapeDtypeStruct((n_slots, d), hpk.dtype), mesh=_sc_mesh(), name="moe_dispatch_sc")
    def scatter(x_hbm, i0_hbm, i1_hbm, i2_hbm, i3_hbm, o_hbm):
        def body(x_vmem, *idx_vmem):
            for iv in idx_vmem:
                pltpu.sync_copy(x_vmem, o_hbm.at[iv.at[0]])

        pltpu.emit_pipeline(
            body, grid=(t // SC_WINDOW,), in_specs=[rows] + [idx] * TOP_K, out_specs=[],
            core_axis_name=("core", "subcore"), dimension_semantics=(pltpu.PARALLEL,),
        )(x_hbm, i0_hbm, i1_hbm, i2_hbm, i3_hbm)

    return scatter(hpk, *[dest_kt[k].reshape(1, t) for k in range(TOP_K)])


def _gather_rows(yb, dest_kt):
    _, t = dest_kt.shape
    d = yb.shape[1]
    n = TOP_K * t

    @pl.kernel(out_type=jax.ShapeDtypeStruct((n, d), yb.dtype), mesh=_sc_mesh(), name="moe_gather_sc")
    def gather(y_hbm, i_hbm, o_hbm):
        def body(i_vmem, o_vmem):
            pltpu.sync_copy(y_hbm.at[i_vmem.at[0]], o_vmem)

        pltpu.emit_pipeline(
            body, grid=(n // SC_WINDOW,),
            in_specs=[pl.BlockSpec((1, SC_WINDOW), index_map=lambda i: (0, i))],
            out_specs=[pl.BlockSpec((SC_WINDOW, d), index_map=lambda i: (i, 0), pipeline_mode=pl.Buffered(1))],
            core_axis_name=("core", "subcore"), dimension_semantics=(pltpu.PARALLEL,),
        )(i_hbm, o_hbm)

    return gather(yb, dest_kt.reshape(1, n)).reshape(TOP_K, t, d)


def _experts_kernel(be_ref, nused_ref, next_ref, parts_ref, xb_ref, wgu_hbm, bgu_ref, wdn_hbm, bdn_ref, yb_ref,
                    wgu_f32, wdn_f32, wgu_bf, wdn_bf, slot_ref, sem):
    i = pl.program_id(0)
    used = i < nused_ref[0]
    e = be_ref[i]

    def weight_copies(expert, slot):
        return (pltpu.make_async_copy(wgu_hbm.at[expert], wgu_f32.at[slot], sem.at[slot, 0]),
                pltpu.make_async_copy(wdn_hbm.at[expert], wdn_f32.at[slot], sem.at[slot, 1]))

    @pl.when(used & (i == 0))
    def _():
        slot_ref[0] = 0
        for c in weight_copies(e, 0):
            c.start()

    @pl.when(used & (i > 0) & (e != be_ref[jnp.maximum(i - 1, 0)]))
    def _():
        slot_ref[0] = 1 - slot_ref[0]

    @pl.when(used & ((i == 0) | (e != be_ref[jnp.maximum(i - 1, 0)])))
    def _():
        slot = slot_ref[0]
        for c in weight_copies(e, slot):
            c.wait()
        wgu_bf[...] = wgu_f32[slot].astype(BF16)
        wdn_bf[...] = wdn_f32[slot].astype(BF16)
        nxt = next_ref[e]

        @pl.when(nxt != e)
        def _():
            for c in weight_copies(nxt, 1 - slot):
                c.start()

    def mlp(rows):
        x_hi, x_lo = _unpack_bf16_pairs(xb_ref[:rows])
        x = jnp.concatenate([x_hi.astype(BF16), x_lo.astype(BF16)], axis=1)
        gu = jnp.dot(x, wgu_bf[...], preferred_element_type=F32) + bgu_ref[0]
        gate = jnp.minimum(gu[:, :D_FF], SWIGLU_LIMIT)
        up = jnp.clip(gu[:, D_FF:], -SWIGLU_LIMIT, SWIGLU_LIMIT)
        act = (up + 1.0) * gate * jax.nn.sigmoid(SWIGLU_ALPHA * gate)
        y = jnp.dot(act.astype(BF16), wdn_bf[...], preferred_element_type=F32) + bdn_ref[0]
        yb_ref[:rows] = _pack_bf16_pairs(y)

    for parts in range(1, MOE_PARTS + 1):
        pl.when(used & (parts_ref[i] == parts))(functools.partial(mlp, parts * (MOE_BLOCK // MOE_PARTS)))


def _experts(block_expert, n_used, next_expert, parts, xb, wgu, bgu, wdn, bdn):
    n_slots = xb.shape[0]
    nb = n_slots // MOE_BLOCK
    blk = pl.BlockSpec((MOE_BLOCK, HALF_D),
                       lambda i, be, nu, nx, hv: (jnp.minimum(i, jnp.maximum(nu[0] - 1, 0)), 0))
    per_expert = lambda shape: pl.BlockSpec((1,) + shape, lambda i, be, nu, nx, hv: (be[i], 0, 0))
    hbm = pl.BlockSpec(memory_space=pl.ANY)
    grid_spec = pltpu.PrefetchScalarGridSpec(
        num_scalar_prefetch=4,
        grid=(nb,),
        in_specs=[blk, hbm, per_expert((1, 2 * D_FF)), hbm, per_expert((1, D_MODEL))],
        out_specs=blk,
        scratch_shapes=[pltpu.VMEM((2, D_MODEL, 2 * D_FF), F32), pltpu.VMEM((2, D_FF, D_MODEL), F32),
                        pltpu.VMEM((D_MODEL, 2 * D_FF), BF16), pltpu.VMEM((D_FF, D_MODEL), BF16),
                        pltpu.SMEM((1,), jnp.int32), pltpu.SemaphoreType.DMA((2, 2))],
    )
    return pl.pallas_call(
        _experts_kernel,
        grid_spec=grid_spec,
        out_shape=jax.ShapeDtypeStruct((n_slots, HALF_D), jnp.uint32),
        compiler_params=_cparams("arbitrary"),
        name="moe_experts",
    )(block_expert, n_used, next_expert, parts, xb, wgu, bgu, wdn, bdn)


def _combine_kernel(rows_ref, gate_ref, h_ref, g_ref, b_ref, *rest):
    out_ref = rest[-1]
    tm = h_ref.shape[0]
    gate = gate_ref[...]
    ff_hi = jnp.zeros((tm, HALF_D), F32)
    ff_lo = jnp.zeros((tm, HALF_D), F32)
    for k in range(TOP_K):
        hi, lo = _unpack_bf16_pairs(rows_ref[k])
        gk = gate[:, k:k + 1]
        ff_hi = ff_hi + gk * hi
        ff_lo = ff_lo + gk * lo
    ff = jnp.concatenate([ff_hi, ff_lo], axis=1)
    out_ref[...] = _layer_norm_rows(DEEPNORM_ALPHA * h_ref[...] + ff, g_ref[...], b_ref[...])


def _combine(rows_kt, gates, h2d, ln_g, ln_b, token0, prev_out):
    t = h2d.shape[0]
    t_part = rows_kt.shape[1]
    tm = min(COMBINE_TM, t_part)
    first = token0 // tm
    row = lambda i: (first + i, 0)
    const = lambda i: (0, 0)
    in_specs = [
        pl.BlockSpec((TOP_K, tm, HALF_D), lambda i: (0, i, 0)),
        pl.BlockSpec((tm, LANES), row),
        pl.BlockSpec((tm, D_MODEL), row),
        pl.BlockSpec((1, D_MODEL), const),
        pl.BlockSpec((1, D_MODEL), const),
    ]
    args = [rows_kt, gates, h2d, ln_g.reshape(1, -1), ln_b.reshape(1, -1)]
    aliases = {}
    if prev_out is not None:
        in_specs.append(pl.BlockSpec(memory_space=pl.ANY))
        args.append(prev_out)
        aliases = {len(args) - 1: 0}
    return pl.pallas_call(
        _combine_kernel,
        grid=(t_part // tm,),
        in_specs=in_specs,
        out_specs=pl.BlockSpec((tm, D_MODEL), row),
        out_shape=jax.ShapeDtypeStruct((t, D_MODEL), F32),
        input_output_aliases=aliases,
        compiler_params=_cparams("parallel"),
        name="moe_combine",
    )(*args)


def _moe_plan(route, counts_f32, n_tokens):
    eid = route[:TOP_K]
    rank = route[TOP_K:]
    counts = counts_f32[:, 0].astype(jnp.int32)
    padded = ((counts + MOE_BLOCK - 1) // MOE_BLOCK) * MOE_BLOCK
    pend = jnp.cumsum(padded)
    pstart = pend - padded
    experts = jnp.arange(N_EXPERTS, dtype=jnp.int32)[:, None, None]
    dest = rank + jnp.sum(jnp.where(eid[None] == experts, pstart[:, None, None], 0), axis=0)
    n_assign = n_tokens * TOP_K
    n_slots = ((n_assign + MOE_BLOCK - 1) // MOE_BLOCK) * MOE_BLOCK + N_EXPERTS * MOE_BLOCK
    nb = n_slots // MOE_BLOCK
    block_start = jnp.arange(nb, dtype=jnp.int32) * MOE_BLOCK
    block_expert = jnp.sum((pend[None, :] <= block_start[:, None]).astype(jnp.int32), axis=1)
    block_expert = jnp.minimum(block_expert, N_EXPERTS - 1).astype(jnp.int32)
    n_used = (pend[-1:] // MOE_BLOCK).astype(jnp.int32)
    onehot = (block_expert[:, None] == jnp.arange(N_EXPERTS, dtype=jnp.int32)[None, :]).astype(jnp.int32)
    valid_rows = jnp.sum(onehot * (pstart + counts)[None, :], axis=1) - block_start
    part_rows = MOE_BLOCK // MOE_PARTS
    parts = jnp.clip((valid_rows + part_rows - 1) // part_rows, 1, MOE_PARTS).astype(jnp.int32)
    ids = jnp.arange(N_EXPERTS, dtype=jnp.int32)
    later_nonempty = (ids[None, :] > ids[:, None]) & (counts[None, :] > 0)
    next_expert = jnp.min(jnp.where(later_nonempty, ids[None, :], N_EXPERTS), axis=1)
    next_expert = jnp.where(next_expert == N_EXPERTS, ids, next_expert).astype(jnp.int32)
    return dest.astype(jnp.int32), block_expert, n_used, next_expert, parts, n_slots


def _in_stage(x, p):
    t = x.shape[0] * x.shape[1]
    return _in_proj_stage(x.reshape(t, D_MODEL), p["ln_in_g"], p["ln_in_b"], p["lb2"], p["w_in"])


def _mix_and_out_stage(x, in_outs, p):
    batch, seq_len, _ = x.shape
    t = batch * seq_len
    u, qv, lf, kk, og = in_outs
    zt = _fourier_mix(u, batch, seq_len, p["fourier_norm_g"])
    per_seq = lambda a: a.reshape(a.shape[0], batch, seq_len, HEAD_DIM)
    o = _gla(per_seq(qv), per_seq(lf), per_seq(kk))
    return _out_proj_stage(
        zt, o.reshape(HEADS, t, HEAD_DIM), og, x.reshape(t, D_MODEL), p["ln_in_g"], p["ln_in_b"], seq_len,
        p["norm_g6"], p["w_out"], p["ln1_g"], p["ln1_b"], p["rw_hi"], p["rw_lo"], p["rb_col"])


def _moe(x, out_outs, p):
    batch, seq_len, _ = x.shape
    t = batch * seq_len
    h, hpk, route, gates, counts = out_outs
    dest_kt, block_expert, n_used, next_expert, parts, n_slots = _moe_plan(route, counts, t)
    xb = _dispatch(dest_kt, hpk, n_slots)
    yb = _experts(block_expert, n_used, next_expert, parts, xb, p["w_gu"], p["b_gu"], p["w_dn"], p["b_dn"])
    y = None
    part = t // COMBINE_PARTS
    for j in range(COMBINE_PARTS):
        rows = _gather_rows(yb, dest_kt[:, j * part:(j + 1) * part])
        y = _combine(rows, gates, h, p["ln2_g"], p["ln2_b"], j * part, y)
    return y.reshape(batch, seq_len, D_MODEL)


def _prepare_params(ln_in_g, ln_in_b, w_in, fourier_norm_g, lb_gamma, hgrn_norm_g, w_out, ln1_g, ln1_b,
                    router_w, router_b, w_gate_up, b_gate_up, w_down, b_down, ln2_g, ln2_b):
    lb_all = jnp.cumsum(jax.nn.softmax(lb_gamma.astype(F32), axis=1), axis=1)
    rw = router_w[0].astype(F32).T
    rw_hi = rw.astype(BF16)
    rw_lo = (rw - rw_hi.astype(F32)).astype(BF16)
    return dict(
        ln_in_g=ln_in_g, ln_in_b=ln_in_b, w_in=w_in[0].astype(BF16),
        fourier_norm_g=fourier_norm_g[0],
        lb2=jnp.concatenate([lb_all[0, 0], lb_all[1, 0]]).reshape(1, -1),
        norm_g6=jnp.tile(hgrn_norm_g[0].astype(F32), HEADS).reshape(1, -1),
        w_out=w_out[0].astype(BF16), ln1_g=ln1_g[0], ln1_b=ln1_b[0],
        rw_hi=rw_hi, rw_lo=rw_lo, rb_col=router_b[0].astype(F32).reshape(-1, 1),
        w_gu=w_gate_up[0], b_gu=b_gate_up[0].reshape(N_EXPERTS, 1, -1),
        w_dn=w_down[0], b_dn=b_down[0].reshape(N_EXPERTS, 1, -1),
        ln2_g=ln2_g[0], ln2_b=ln2_b[0],
    )


def kernel(x_prompt, x_sample, ln_in_g, ln_in_b, w_in, fourier_norm_g, lb_gamma, hgrn_norm_g, w_out,
           ln1_g, ln1_b, router_w, router_b, w_gate_up, b_gate_up, w_down, b_down, ln2_g, ln2_b):
    p = _prepare_params(ln_in_g, ln_in_b, w_in, fourier_norm_g, lb_gamma, hgrn_norm_g, w_out, ln1_g, ln1_b,
                        router_w, router_b, w_gate_up, b_gate_up, w_down, b_down, ln2_g, ln2_b)
    first, second = x_sample, x_prompt
    (in_first,) = _run_stages([_in_stage(first, p)])
    out_stage_first = _mix_and_out_stage(first, in_first, p)
    in_stage_second = _in_stage(second, p)
    if in_stage_second.grid == out_stage_first.grid:
        in_second, out_first = _run_stages([in_stage_second, out_stage_first])
    else:
        (in_second,) = _run_stages([in_stage_second])
        (out_first,) = _run_stages([out_stage_first])
    (out_second,) = _run_stages([_mix_and_out_stage(second, in_second, p)])
    y_first = _moe(first, out_first, p)
    y_second = _moe(second, out_second, p)
    return (y_second, y_first)
```

```python
import functools
import math
from typing import Callable, NamedTuple

import numpy as np
import jax
import jax.numpy as jnp
from jax import lax
from jax.experimental import pallas as pl
from jax.experimental.pallas import tpu as pltpu
from jax.experimental.pallas import tpu_sc as plsc

D_MODEL = 1024
FOURIER_WIDTH = 256
FOURIER_GROUP_DIM = 64
HGRN_WIDTH = 768
HEAD_DIM = 128
HEADS = 6
CHUNK = 64
N_EXPERTS = 32
TOP_K = 4
D_FF = 1024
SWIGLU_LIMIT = 7.0
SWIGLU_ALPHA = 1.702
LN_EPS = 1e-5
RMS_EPS = 1e-6
DEEPNORM_ALPHA = 2.0 ** 0.25

LANES = 128
VMEM_LIMIT_BYTES = 56 * 1024 * 1024

F32 = jnp.float32
BF16 = jnp.bfloat16


def _cparams(*sem):
    return pltpu.CompilerParams(dimension_semantics=sem, vmem_limit_bytes=VMEM_LIMIT_BYTES)


def _layer_norm_rows(x, g, b):
    mu = jnp.mean(x, axis=-1, keepdims=True)
    xc = x - mu
    var = jnp.mean(xc * xc, axis=-1, keepdims=True)
    return xc * lax.rsqrt(var + LN_EPS) * g + b


IN_TM = 512
IN_TN = 512


def _in_proj_kernel(x_ref, g_ref, b_ref, lb_ref, w_ref, u_ref, qv_ref, lf_ref, kk_ref, og_ref):
    xb = _layer_norm_rows(x_ref[...], g_ref[...], b_ref[...]).astype(BF16)
    hw = HGRN_WIDTH
    yield

    def store_heads(ref, first_head, c0, val):
        for j in range(val.shape[1] // HEAD_DIM):
            ref[first_head + c0 // HEAD_DIM + j] = val[:, j * HEAD_DIM:(j + 1) * HEAD_DIM]

    def chunks(col, width):
        for c0 in range(0, width, IN_TN):
            cw = min(IN_TN, width - c0)
            yield c0, cw, jnp.dot(xb, w_ref[:, col + c0:col + c0 + cw], preferred_element_type=F32)

    for c0, cw, acc in chunks(0, FOURIER_WIDTH):
        u_ref[:, c0:c0 + cw] = acc.astype(BF16)
        yield
    for c0, cw, acc in chunks(FOURIER_WIDTH, hw):
        store_heads(qv_ref, 0, c0, (acc * jax.nn.sigmoid(acc) * (HEAD_DIM ** -0.5)).astype(BF16))
        yield
    for c0, cw, acc in chunks(FOURIER_WIDTH + hw, hw):
        store_heads(qv_ref, HEADS, c0, acc.astype(BF16))
        yield
    for c0, cw, acc in chunks(FOURIER_WIDTH + 2 * hw, 2 * hw):
        lb = lb_ref[:, c0:c0 + cw]
        fg = lb + (1.0 - lb) * jax.nn.sigmoid(acc)
        store_heads(lf_ref, 0, c0, jnp.log(fg))
        store_heads(kk_ref, 0, c0, (1.0 - fg).astype(BF16))
        yield
    for c0, cw, acc in chunks(FOURIER_WIDTH + 4 * hw, hw):
        og_ref[:, c0:c0 + cw] = acc.astype(BF16)
        yield


class _Stage(NamedTuple):
    body: Callable
    init: Callable | None
    grid: tuple
    in_specs: list
    out_specs: list
    out_shape: list
    scratch: list
    args: list
    sequential: bool
    name: str


_DONE = object()


def _interleave(pieces, stagger):
    done = [False] * len(pieces)
    t = 0
    while not all(done):
        for i, g in enumerate(pieces):
            if not done[i] and t >= i * stagger:
                done[i] = next(g, _DONE) is _DONE
        t += 1


def _run_stages(stages):
    assert all(s.grid == stages[0].grid for s in stages)
    n_in = [len(s.in_specs) for s in stages]
    n_out = [len(s.out_specs) for s in stages]
    n_scr = [len(s.scratch) for s in stages]

    def body(*refs):
        scr = sum(n_in) + sum(n_out)
        for s, c in zip(stages, n_scr):
            if s.init is not None:
                pl.when(pl.program_id(0) == 0)(functools.partial(s.init, *refs[scr:scr + c]))
            scr += c
        ins, outs, scr = 0, sum(n_in), sum(n_in) + sum(n_out)
        bodies = []
        for s, a, b, c in zip(stages, n_in, n_out, n_scr):
            bodies.append(s.body(*refs[ins:ins + a], *refs[outs:outs + b], *refs[scr:scr + c]))
            ins, outs, scr = ins + a, outs + b, scr + c
        _interleave(bodies, 0)

    results = pl.pallas_call(
        body,
        grid=stages[0].grid,
        in_specs=[sp for s in stages for sp in s.in_specs],
        out_specs=[sp for s in stages for sp in s.out_specs],
        out_shape=[sh for s in stages for sh in s.out_shape],
        scratch_shapes=[sc for s in stages for sc in s.scratch],
        compiler_params=_cparams("arbitrary" if any(s.sequential for s in stages) else "parallel"),
        name="_".join(s.name for s in stages),
    )(*[a for s in stages for a in s.args])
    split, pos = [], 0
    for b in n_out:
        split.append(results[pos:pos + b])
        pos += b
    return split


def _zero_ref(ref):
    ref[...] = jnp.zeros(ref.shape, ref.dtype)


def _resident(shape):
    return pl.BlockSpec(shape, lambda i: (0,) * len(shape), pipeline_mode=pl.Buffered(1))


def _in_proj_stage(x2d, ln_g, ln_b, lb2, w_bf16):
    t = x2d.shape[0]
    tm = min(IN_TM, t)
    row = lambda i: (i, 0)
    flat = lambda w, dt: (pl.BlockSpec((tm, w), row), jax.ShapeDtypeStruct((t, w), dt))
    head_major = lambda nh, dt: (pl.BlockSpec((nh, tm, HEAD_DIM), lambda i: (0, i, 0)),
                                 jax.ShapeDtypeStruct((nh, t, HEAD_DIM), dt))
    outs = [flat(FOURIER_WIDTH, BF16), head_major(2 * HEADS, BF16),
            head_major(2 * HEADS, F32), head_major(2 * HEADS, BF16), flat(HGRN_WIDTH, BF16)]
    return _Stage(
        body=_in_proj_kernel,
        init=None,
        grid=(t // tm,),
        in_specs=[pl.BlockSpec((tm, D_MODEL), row), _resident((1, D_MODEL)), _resident((1, D_MODEL)),
                  _resident((1, 2 * HGRN_WIDTH)), _resident(w_bf16.shape)],
        out_specs=[o[0] for o in outs],
        out_shape=[o[1] for o in outs],
        scratch=[],
        args=[x2d, ln_g.reshape(1, -1), ln_b.reshape(1, -1), lb2, w_bf16],
        sequential=False,
        name="in_proj",
    )


def _fft_split(seq_len):
    n1 = 1 << ((seq_len.bit_length() - 1 + 1) // 2)
    return n1, seq_len // n1


@functools.lru_cache(maxsize=None)
def _fft_tables(seq_len):
    n1, n2 = _fft_split(seq_len)
    k1 = np.arange(n1)
    ang1 = 2.0 * np.pi * ((k1[:, None] * k1[None, :]) % n1) / n1
    s1 = 1.0 / math.sqrt(n1)
    c1, s1m = np.cos(ang1) * s1, np.sin(ang1) * s1
    l1p = np.arange(n1)[:, None, None]
    l2p = np.arange(n2)[None, :, None]
    l2 = np.arange(n2)[None, None, :]
    ang2 = 2.0 * np.pi * ((l2 * (l1p + n1 * l2p)) % seq_len) / seq_len
    s2 = 1.0 / math.sqrt(n2)
    gc, gs = np.cos(ang2) * s2, np.sin(ang2) * s2
    kc = np.arange(FOURIER_GROUP_DIM)
    angc = 2.0 * np.pi * ((kc[:, None] * kc[None, :]) % FOURIER_GROUP_DIM) / FOURIER_GROUP_DIM
    sc = 1.0 / math.sqrt(FOURIER_GROUP_DIM)
    groups = FOURIER_WIDTH // FOURIER_GROUP_DIM
    bc = np.kron(np.eye(groups), np.cos(angc) * sc)
    bs = np.kron(np.eye(groups), np.sin(angc) * sc)
    as_bf16 = lambda a: jnp.asarray(a, dtype=F32).astype(BF16)
    return tuple(as_bf16(a) for a in (c1, s1m, gc, gs, bc, bs))


def _fft1_kernel(c_ref, s_ref, u_ref, ar_ref, ai_ref):
    u = u_ref[0]
    ar_ref[0] = jnp.dot(c_ref[...], u, preferred_element_type=F32).astype(BF16)
    ai_ref[0] = (-jnp.dot(s_ref[...], u, preferred_element_type=F32)).astype(BF16)


FFT1_TN = 4096


def _fft_stage1(u3, c1, s1):
    b, n1, width = u3.shape
    tn = min(FFT1_TN, width)
    blk = pl.BlockSpec((1, n1, tn), lambda i, j: (i, 0, j))
    mat = pl.BlockSpec((n1, n1), lambda i, j: (0, 0))
    return pl.pallas_call(
        _fft1_kernel,
        grid=(b, width // tn),
        in_specs=[mat, mat, blk],
        out_specs=[blk, blk],
        out_shape=[jax.ShapeDtypeStruct(u3.shape, BF16)] * 2,
        compiler_params=_cparams("parallel", "parallel"),
        name="fft_stage1",
    )(c1, s1, u3)


FFT2_ROWS = 1024


def _fft2_kernel(gc_ref, gs_ref, bc_ref, bs_ref, g_ref, ar_ref, ai_ref, z_ref):
    t1, n2 = ar_ref.shape[1], ar_ref.shape[2]
    xr, xi = [], []
    for j in range(t1):
        ar, ai = ar_ref[0, j], ai_ref[0, j]
        gc, gs = gc_ref[j], gs_ref[j]
        xr.append((jnp.dot(gc, ar, preferred_element_type=F32)
                   + jnp.dot(gs, ai, preferred_element_type=F32)).astype(BF16))
        xi.append((jnp.dot(gc, ai, preferred_element_type=F32)
                   - jnp.dot(gs, ar, preferred_element_type=F32)).astype(BF16))
    z = jnp.dot(jnp.concatenate(xr, axis=0), bc_ref[...], preferred_element_type=F32)
    z += jnp.dot(jnp.concatenate(xi, axis=0), bs_ref[...], preferred_element_type=F32)
    z = (z * lax.rsqrt(jnp.mean(z * z, axis=-1, keepdims=True) + RMS_EPS) * g_ref[...]).astype(BF16)
    for j in range(t1):
        z_ref[0, j] = z[j * n2:(j + 1) * n2]


def _fft_stage2(ar4, ai4, gc, gs, bc, bs, gain):
    b, n1, n2, w = ar4.shape
    t1 = min(max(FFT2_ROWS // n2, 1), n1)
    a_blk = pl.BlockSpec((1, t1, n2, w), lambda i, j: (i, j, 0, 0))
    g_blk = pl.BlockSpec((t1, n2, n2), lambda i, j: (j, 0, 0))
    c_blk = pl.BlockSpec((w, w), lambda i, j: (0, 0))
    return pl.pallas_call(
        _fft2_kernel,
        grid=(b, n1 // t1),
        in_specs=[g_blk, g_blk, c_blk, c_blk, pl.BlockSpec((1, w), lambda i, j: (0, 0)), a_blk, a_blk],
        out_specs=a_blk,
        out_shape=jax.ShapeDtypeStruct(ar4.shape, BF16),
        compiler_params=_cparams("parallel", "parallel"),
        name="fft_stage2",
    )(gc, gs, bc, bs, gain.reshape(1, -1), ar4, ai4)


def _fourier_mix(u2d, batch, seq_len, gain):
    n1, n2 = _fft_split(seq_len)
    c1, s1, gc, gs, bc, bs = _fft_tables(seq_len)
    u3 = u2d.reshape(batch, n1, n2 * FOURIER_WIDTH)
    ar, ai = _fft_stage1(u3, c1, s1)
    shape4 = (batch, n1, n2, FOURIER_WIDTH)
    zt = _fft_stage2(ar.reshape(shape4), ai.reshape(shape4), gc, gs, bc, bs, gain)
    return zt.reshape(batch, n1, n2 * FOURIER_WIDTH)


GLA_LB = (2048, 1024, 512)
GLA_OUT_BLOCK_BYTES = 32 * 1024 * 1024
GLA_VMEM_BUDGET = 46 * 1024 * 1024
GLA_SKEW = 1
GLA_STAGGER = 0
CUMSUM_ROWS = 256


@functools.lru_cache(maxsize=None)
def _cumsum_matrices():
    r = np.arange(CUMSUM_ROWS)
    same_chunk = (r[:, None] // CHUNK) == (r[None, :] // CHUNK)
    prefix = same_chunk & (r[None, :] <= r[:, None])
    suffix = same_chunk & (r[None, :] >= r[:, None])
    return (jnp.asarray(prefix, dtype=F32).astype(BF16), jnp.asarray(suffix, dtype=F32).astype(BF16))


def _chunk_cumsum(x, tri):
    hi = x.astype(BF16)
    lo = (x - hi.astype(F32)).astype(BF16)
    width = x.shape[1]
    parts = []
    for r0 in range(0, x.shape[0], CUMSUM_ROWS):
        rows = slice(r0, r0 + CUMSUM_ROWS)
        both = jnp.dot(tri, jnp.concatenate([hi[rows], lo[rows]], axis=1), preferred_element_type=F32)
        parts.append(both[:, :width] + both[:, width:])
    return parts[0] if len(parts) == 1 else jnp.concatenate(parts, axis=0)


def _gla_direction(q_ref, v_ref, k_ref, lf_ref, tri, st_ref, o_ref, start, reverse):
    n = q_ref.shape[0]
    b = _chunk_cumsum(lf_ref[...], tri)
    yield
    mid = CHUNK // 2 if reverse else CHUNK // 2 - 1
    last = 0 if reverse else CHUNK - 1
    t_idx = lax.broadcasted_iota(jnp.int32, (CHUNK, CHUNK), 0)
    s_idx = lax.broadcasted_iota(jnp.int32, (CHUNK, CHUNK), 1)
    visible = (t_idx <= s_idx) if reverse else (t_idx >= s_idx)
    nchunks = n // CHUNK
    order = range(nchunks - 1, -1, -1) if reverse else range(nchunks)
    nt = (((1,), (1,)), ((), ()))
    tn = (((0,), (0,)), ((), ()))
    rows = [slice(c * CHUNK, (c + 1) * CHUNK) for c in range(nchunks)]
    st = st_ref[...]
    pending = []
    seq = list(order)
    for step in range(nchunks + GLA_SKEW):
        if step < nchunks:
            c = seq[step]
            bc = b[rows[c]]
            b_mid = bc[mid:mid + 1]
            b_last = bc[last:last + 1]
            qe = (q_ref[rows[c]].astype(F32) * jnp.exp(bc - b_mid)).astype(BF16)
            ke = (k_ref[rows[c]].astype(F32) * jnp.exp(b_mid - bc)).astype(BF16)
            s = lax.dot_general(qe, ke, nt, preferred_element_type=F32)
            scores = jnp.where(visible, s, 0.0).astype(BF16)
            delta_t = lax.dot_general(v_ref[rows[c]], ke, tn, preferred_element_type=F32) * jnp.exp(b_last - b_mid)
            pending.append((c, qe, scores, delta_t, jnp.exp(b_mid), jnp.exp(b_last)))
            yield
        if step >= GLA_SKEW:
            c, qe, scores, delta_t, e_mid, e_last = pending.pop(0)
            st_in = (st * e_mid).astype(BF16)
            st = st * e_last + delta_t
            o = jnp.dot(scores, v_ref[rows[c]], preferred_element_type=F32)
            o += lax.dot_general(qe, st_in, nt, preferred_element_type=F32)
            o_ref[pl.ds(start + c * CHUNK, CHUNK), :] += o
            yield
    st_ref[...] = st


def _gla_kernel(trif_ref, trib_ref, qf_ref, vf_ref, kf_ref, lf_ref, qb_ref, vb_ref, kb_ref, lb_ref,
                o_ref, sf_ref, sb_ref):
    j = pl.program_id(2)
    nblk = pl.num_programs(2)
    lb_rows = qf_ref.shape[2]

    @pl.when(j == 0)
    def _():
        o_ref[...] = jnp.zeros(o_ref.shape, o_ref.dtype)
        sf_ref[...] = jnp.zeros(sf_ref.shape, sf_ref.dtype)
        sb_ref[...] = jnp.zeros(sb_ref.shape, sb_ref.dtype)

    start_f = pl.multiple_of(j * lb_rows, lb_rows)
    start_b = pl.multiple_of((nblk - 1 - j) * lb_rows, lb_rows)
    streams = []
    for hh in range(qf_ref.shape[0]):
        streams.append(_gla_direction(qf_ref.at[hh, 0], vf_ref.at[hh, 0], kf_ref.at[hh, 0], lf_ref.at[hh, 0],
                                      trif_ref[...], sf_ref.at[hh], o_ref.at[hh, 0], start_f, False))
        streams.append(_gla_direction(qb_ref.at[hh, 0], vb_ref.at[hh, 0], kb_ref.at[hh, 0], lb_ref.at[hh, 0],
                                      trib_ref[...], sb_ref.at[hh], o_ref.at[hh, 0], start_b, True))
    _interleave(streams, GLA_STAGGER)


def _gla(qv4, lf4, kk4):
    _, b, seq_len, _ = qv4.shape
    nh = max(h for h in (1, 2) if 2 * h * seq_len * HEAD_DIM * 4 <= GLA_OUT_BLOCK_BYTES or h == 1)
    out_bytes = 2 * nh * seq_len * HEAD_DIM * 4
    in_bytes = lambda rows: 2 * nh * rows * HEAD_DIM * (6 * 2 + 2 * 4)
    lbk = max([r for r in GLA_LB if seq_len % r == 0 and out_bytes + in_bytes(r) <= GLA_VMEM_BUDGET]
              or [CUMSUM_ROWS])
    assert lbk % CUMSUM_ROWS == 0 and seq_len % lbk == 0
    nblk = seq_len // lbk
    blk = lambda head0, rev: pl.BlockSpec(
        (nh, 1, lbk, HEAD_DIM),
        (lambda i, h, j: (head0 + h, i, nblk - 1 - j, 0)) if rev else (lambda i, h, j: (head0 + h, i, j, 0)))
    tri_spec = pl.BlockSpec((CUMSUM_ROWS, CUMSUM_ROWS), lambda i, h, j: (0, 0))
    tri_f, tri_b = _cumsum_matrices()
    groups = HEADS // nh
    return pl.pallas_call(
        _gla_kernel,
        grid=(b, groups, nblk),
        in_specs=[tri_spec, tri_spec,
                  blk(0, False), blk(groups, False), blk(0, False), blk(0, False),
                  blk(0, True), blk(groups, True), blk(groups, True), blk(groups, True)],
        out_specs=pl.BlockSpec((nh, 1, seq_len, HEAD_DIM), lambda i, h, j: (h, i, 0, 0)),
        out_shape=jax.ShapeDtypeStruct((HEADS, b, seq_len, HEAD_DIM), F32),
        scratch_shapes=[pltpu.VMEM((nh, HEAD_DIM, HEAD_DIM), F32)] * 2,
        compiler_params=_cparams("parallel", "parallel", "arbitrary"),
        name="gla",
    )(tri_f, tri_b, qv4, qv4, kk4, lf4, qv4, qv4, kk4, lf4)


HALF_D = D_MODEL // 2


def _pack_bf16_pairs(x):
    bits = lax.bitcast_convert_type(x.astype(BF16).astype(F32), jnp.uint32)
    return bits[:, :HALF_D] | (bits[:, HALF_D:] >> 16)


def _unpack_bf16_pairs(words):
    hi = lax.bitcast_convert_type(words & jnp.uint32(0xFFFF0000), F32)
    lo = lax.bitcast_convert_type(words << 16, F32)
    return hi, lo


OUT_TM = 512


def _out_proj_kernel(*refs, nz):
    z_refs = refs[:nz]
    (o_ref, g_ref, x_ref, ling_ref, linb_ref, ng_ref, wz_ref, wh_ref, l1g_ref, l1b_ref, rwh_ref, rwl_ref, rb_ref,
     h_ref, hpk_ref, route_ref, gate_ref, cnt_ref, base_ref) = refs[nz:]
    tm = o_ref.shape[1]

    z = jnp.concatenate([r[0] for r in z_refs], axis=0) if nz > 1 else z_refs[0][0]
    g = g_ref[...].astype(F32)
    normed = []
    for hd in range(HEADS):
        oh = o_ref[hd]
        normed.append(oh * lax.rsqrt(jnp.mean(oh * oh, axis=-1, keepdims=True) + RMS_EPS))
        yield
    hg = jnp.concatenate(normed, axis=1) * ng_ref[...] * (g * jax.nn.sigmoid(g))
    yield
    mixed = jnp.dot(z, wz_ref[...], preferred_element_type=F32)
    mixed += jnp.dot(hg.astype(BF16), wh_ref[...], preferred_element_type=F32)
    yield
    xn = _layer_norm_rows(x_ref[...], ling_ref[...], linb_ref[...])
    h = _layer_norm_rows(DEEPNORM_ALPHA * xn + mixed, l1g_ref[...], l1b_ref[...])
    h_ref[...] = h
    yield
    hpk_ref[...] = _pack_bf16_pairs(h)
    yield

    nt = (((1,), (1,)), ((), ()))
    h_hi = h.astype(BF16)
    h_lo = (h - h_hi.astype(F32)).astype(BF16)
    logits = lax.dot_general(rwh_ref[...], h_hi, nt, preferred_element_type=F32)
    logits += lax.dot_general(rwh_ref[...], h_lo, nt, preferred_element_type=F32)
    logits += lax.dot_general(rwl_ref[...], h_hi, nt, preferred_element_type=F32)
    logits += rb_ref[...]
    yield

    eid_f = lax.broadcasted_iota(jnp.int32, (N_EXPERTS, tm), 0).astype(F32)
    work = logits
    vals, idxs, hits = [], [], []
    for _ in range(TOP_K):
        m = jnp.max(work, axis=0, keepdims=True)
        idx = jnp.min(jnp.where(work == m, eid_f, float(N_EXPERTS)), axis=0, keepdims=True)
        hit = eid_f == idx
        work = jnp.where(hit, -jnp.inf, work)
        vals.append(m)
        idxs.append(idx)
        hits.append(hit)
        yield
    exps = [jnp.exp(v - vals[0]) for v in vals]
    denom = exps[0] + exps[1] + exps[2] + exps[3]

    member = jnp.zeros((N_EXPERTS, tm), F32)
    for hit in hits:
        member = member + jnp.where(hit, 1.0, 0.0)
    s_idx = lax.broadcasted_iota(jnp.int32, (tm, tm), 0)
    t_idx = lax.broadcasted_iota(jnp.int32, (tm, tm), 1)
    earlier = jnp.where(s_idx < t_idx, 1.0, 0.0).astype(BF16)
    base = base_ref[...]
    before = jnp.dot(member.astype(BF16), earlier, preferred_element_type=F32) + base
    yield

    row8 = lax.broadcasted_iota(jnp.int32, (2 * TOP_K, tm), 0)
    route = jnp.zeros((2 * TOP_K, tm), F32)
    row128 = lax.broadcasted_iota(jnp.int32, (LANES, tm), 0)
    gate_t = jnp.zeros((LANES, tm), F32)
    for k in range(TOP_K):
        rank = jnp.sum(jnp.where(hits[k], before, 0.0), axis=0, keepdims=True)
        route = route + jnp.where(row8 == k, idxs[k], 0.0) + jnp.where(row8 == TOP_K + k, rank, 0.0)
        gate_t = gate_t + jnp.where(row128 == k, exps[k] / denom, 0.0)
    route_ref[...] = route.astype(jnp.int32)
    gate_ref[...] = gate_t.T
    total = base + jnp.sum(member, axis=1, keepdims=True)
    base_ref[...] = total
    cnt_ref[...] = total


def _out_proj_stage(zt, o3, g2d, x2d, ln_in_g, ln_in_b, seq_len, norm_g6, w_out_bf16, ln_g, ln_b,
                    rw_hi, rw_lo, rb_col):
    t = o3.shape[1]
    n1 = zt.shape[1]
    tm = max(min(OUT_TM, seq_len), n1)
    nz = tm // n1
    row = lambda i: (i, 0)
    const = lambda i: (0, 0)

    def z_spec(k):
        return pl.BlockSpec((1, n1, FOURIER_WIDTH),
                            lambda i: ((i * tm) // seq_len, 0, ((i * tm) % seq_len) // n1 + k))

    in_specs = [z_spec(k) for k in range(nz)] + [
        pl.BlockSpec((HEADS, tm, HEAD_DIM), lambda i: (0, i, 0)),
        pl.BlockSpec((tm, HGRN_WIDTH), row),
        pl.BlockSpec((tm, D_MODEL), row),
        _resident((1, D_MODEL)),
        _resident((1, D_MODEL)),
        _resident((1, HGRN_WIDTH)),
        _resident((FOURIER_WIDTH, D_MODEL)),
        _resident((HGRN_WIDTH, D_MODEL)),
        _resident((1, D_MODEL)),
        _resident((1, D_MODEL)),
        _resident((N_EXPERTS, D_MODEL)),
        _resident((N_EXPERTS, D_MODEL)),
        _resident((N_EXPERTS, 1)),
    ]
    out_specs = [
        pl.BlockSpec((tm, D_MODEL), row),
        pl.BlockSpec((tm, HALF_D), row),
        pl.BlockSpec((2 * TOP_K, tm), lambda i: (0, i)),
        pl.BlockSpec((tm, LANES), row),
        pl.BlockSpec((N_EXPERTS, 1), const),
    ]
    out_shape = [
        jax.ShapeDtypeStruct((t, D_MODEL), F32),
        jax.ShapeDtypeStruct((t, HALF_D), jnp.uint32),
        jax.ShapeDtypeStruct((2 * TOP_K, t), jnp.int32),
        jax.ShapeDtypeStruct((t, LANES), F32),
        jax.ShapeDtypeStruct((N_EXPERTS, 1), F32),
    ]
    return _Stage(
        body=functools.partial(_out_proj_kernel, nz=nz),
        init=_zero_ref,
        grid=(t // tm,),
        in_specs=in_specs,
        out_specs=out_specs,
        out_shape=out_shape,
        scratch=[pltpu.VMEM((N_EXPERTS, 1), F32)],
        args=[zt] * nz + [o3, g2d, x2d, ln_in_g.reshape(1, -1), ln_in_b.reshape(1, -1), norm_g6, w_out_bf16[:FOURIER_WIDTH], w_out_bf16[FOURIER_WIDTH:],
                          ln_g.reshape(1, -1), ln_b.reshape(1, -1), rw_hi, rw_lo, rb_col],
        sequential=True,
        name="out_proj",
    )


MOE_BLOCK = 1024
MOE_PARTS = 4
COMBINE_TM = 1024
COMBINE_PARTS = 2
SC_WINDOW = 128


def _sc_mesh():
    return plsc.VectorSubcoreMesh(core_axis_name="core", subcore_axis_name="subcore")


def _dispatch(dest_kt, hpk, n_slots):
    t, d = hpk.shape
    rows = pl.BlockSpec((SC_WINDOW, d), index_map=lambda i: (i, 0), pipeline_mode=pl.Buffered(1))
    idx = pl.BlockSpec((1, SC_WINDOW), index_map=lambda i: (0, i))

    @pl.kernel(out_type=jax.ShapeDtypeStruct((n_slots, d), hpk.dtype), mesh=_sc_mesh(), name="moe_dispatch_sc")
    def scatter(x_hbm, i0_hbm, i1_hbm, i2_hbm, i3_hbm, o_hbm):
        def body(x_vmem, *idx_vmem):
            for iv in idx_vmem:
                pltpu.sync_copy(x_vmem, o_hbm.at[iv.at[0]])

        pltpu.emit_pipeline(
            body, grid=(t // SC_WINDOW,), in_specs=[rows] + [idx] * TOP_K, out_specs=[],
            core_axis_name=("core", "subcore"), dimension_semantics=(pltpu.PARALLEL,),
        )(x_hbm, i0_hbm, i1_hbm, i2_hbm, i3_hbm)

    return scatter(hpk, *[dest_kt[k].reshape(1, t) for k in range(TOP_K)])


def _gather_rows(yb, dest_kt):
    _, t = dest_kt.shape
    d = yb.shape[1]
    n = TOP_K * t

    @pl.kernel(out_type=jax.ShapeDtypeStruct((n, d), yb.dtype), mesh=_sc_mesh(), name="moe_gather_sc")
    def gather(y_hbm, i_hbm, o_hbm):
        def body(i_vmem, o_vmem):
            pltpu.sync_copy(y_hbm.at[i_vmem.at[0]], o_vmem)

        pltpu.emit_pipeline(
            body, grid=(n // SC_WINDOW,),
            in_specs=[pl.BlockSpec((1, SC_WINDOW), index_map=lambda i: (0, i))],
            out_specs=[pl.BlockSpec((SC_WINDOW, d), index_map=lambda i: (i, 0), pipeline_mode=pl.Buffered(1))],
            core_axis_name=("core", "subcore"), dimension_semantics=(pltpu.PARALLEL,),
        )(i_hbm, o_hbm)

    return gather(yb, dest_kt.reshape(1, n)).reshape(TOP_K, t, d)


def _experts_kernel(be_ref, nused_ref, next_ref, parts_ref, xb_ref, wgu_hbm, bgu_ref, wdn_hbm, bdn_ref, yb_ref,
                    wgu_f32, wdn_f32, wgu_bf, wdn_bf, slot_ref, sem):
    i = pl.program_id(0)
    used = i < nused_ref[0]
    e = be_ref[i]

    def weight_copies(expert, slot):
        return (pltpu.make_async_copy(wgu_hbm.at[expert], wgu_f32.at[slot], sem.at[slot, 0]),
                pltpu.make_async_copy(wdn_hbm.at[expert], wdn_f32.at[slot], sem.at[slot, 1]))

    @pl.when(used & (i == 0))
    def _():
        slot_ref[0] = 0
        for c in weight_copies(e, 0):
            c.start()

    @pl.when(used & (i > 0) & (e != be_ref[jnp.maximum(i - 1, 0)]))
    def _():
        slot_ref[0] = 1 - slot_ref[0]

    @pl.when(used & ((i == 0) | (e != be_ref[jnp.maximum(i - 1, 0)])))
    def _():
        slot = slot_ref[0]
        for c in weight_copies(e, slot):
            c.wait()
        wgu_bf[...] = wgu_f32[slot].astype(BF16)
        wdn_bf[...] = wdn_f32[slot].astype(BF16)
        nxt = next_ref[e]

        @pl.when(nxt != e)
        def _():
            for c in weight_copies(nxt, 1 - slot):
                c.start()

    def mlp(rows):
        x_hi, x_lo = _unpack_bf16_pairs(xb_ref[:rows])
        x = jnp.concatenate([x_hi.astype(BF16), x_lo.astype(BF16)], axis=1)
        gu = jnp.dot(x, wgu_bf[...], preferred_element_type=F32) + bgu_ref[0]
        gate = jnp.minimum(gu[:, :D_FF], SWIGLU_LIMIT)
        up = jnp.clip(gu[:, D_FF:], -SWIGLU_LIMIT, SWIGLU_LIMIT)
        act = (up + 1.0) * gate * jax.nn.sigmoid(SWIGLU_ALPHA * gate)
        y = jnp.dot(act.astype(BF16), wdn_bf[...], preferred_element_type=F32) + bdn_ref[0]
        yb_ref[:rows] = _pack_bf16_pairs(y)

    for parts in range(1, MOE_PARTS + 1):
        pl.when(used & (parts_ref[i] == parts))(functools.partial(mlp, parts * (MOE_BLOCK // MOE_PARTS)))


def _experts(block_expert, n_used, next_expert, parts, xb, wgu, bgu, wdn, bdn):
    n_slots = xb.shape[0]
    nb = n_slots // MOE_BLOCK
    blk = pl.BlockSpec((MOE_BLOCK, HALF_D),
                       lambda i, be, nu, nx, hv: (jnp.minimum(i, jnp.maximum(nu[0] - 1, 0)), 0))
    per_expert = lambda shape: pl.BlockSpec((1,) + shape, lambda i, be, nu, nx, hv: (be[i], 0, 0))
    hbm = pl.BlockSpec(memory_space=pl.ANY)
    grid_spec = pltpu.PrefetchScalarGridSpec(
        num_scalar_prefetch=4,
        grid=(nb,),
        in_specs=[blk, hbm, per_expert((1, 2 * D_FF)), hbm, per_expert((1, D_MODEL))],
        out_specs=blk,
        scratch_shapes=[pltpu.VMEM((2, D_MODEL, 2 * D_FF), F32), pltpu.VMEM((2, D_FF, D_MODEL), F32),
                        pltpu.VMEM((D_MODEL, 2 * D_FF), BF16), pltpu.VMEM((D_FF, D_MODEL), BF16),
                        pltpu.SMEM((1,), jnp.int32), pltpu.SemaphoreType.DMA((2, 2))],
    )
    return pl.pallas_call(
        _experts_kernel,
        grid_spec=grid_spec,
        out_shape=jax.ShapeDtypeStruct((n_slots, HALF_D), jnp.uint32),
        compiler_params=_cparams("arbitrary"),
        name="moe_experts",
    )(block_expert, n_used, next_expert, parts, xb, wgu, bgu, wdn, bdn)


def _combine_kernel(rows_ref, gate_ref, h_ref, g_ref, b_ref, *rest):
    out_ref = rest[-1]
    tm = h_ref.shape[0]
    gate = gate_ref[...]
    ff_hi = jnp.zeros((tm, HALF_D), F32)
    ff_lo = jnp.zeros((tm, HALF_D), F32)
    for k in range(TOP_K):
        hi, lo = _unpack_bf16_pairs(rows_ref[k])
        gk = gate[:, k:k + 1]
        ff_hi = ff_hi + gk * hi
        ff_lo = ff_lo + gk * lo
    ff = jnp.concatenate([ff_hi, ff_lo], axis=1)
    out_ref[...] = _layer_norm_rows(DEEPNORM_ALPHA * h_ref[...] + ff, g_ref[...], b_ref[...])


def _combine(rows_kt, gates, h2d, ln_g, ln_b, token0, prev_out):
    t = h2d.shape[0]
    t_part = rows_kt.shape[1]
    tm = min(COMBINE_TM, t_part)
    first = token0 // tm
    row = lambda i: (first + i, 0)
    const = lambda i: (0, 0)
    in_specs = [
        pl.BlockSpec((TOP_K, tm, HALF_D), lambda i: (0, i, 0)),
        pl.BlockSpec((tm, LANES), row),
        pl.BlockSpec((tm, D_MODEL), row),
        pl.BlockSpec((1, D_MODEL), const),
        pl.BlockSpec((1, D_MODEL), const),
    ]
    args = [rows_kt, gates, h2d, ln_g.reshape(1, -1), ln_b.reshape(1, -1)]
    aliases = {}
    if prev_out is not None:
        in_specs.append(pl.BlockSpec(memory_space=pl.ANY))
        args.append(prev_out)
        aliases = {len(args) - 1: 0}
    return pl.pallas_call(
        _combine_kernel,
        grid=(t_part // tm,),
        in_specs=in_specs,
        out_specs=pl.BlockSpec((tm, D_MODEL), row),
        out_shape=jax.ShapeDtypeStruct((t, D_MODEL), F32),
        input_output_aliases=aliases,
        compiler_params=_cparams("parallel"),
        name="moe_combine",
    )(*args)


def _moe_plan(route, counts_f32, n_tokens):
    eid = route[:TOP_K]
    rank = route[TOP_K:]
    counts = counts_f32[:, 0].astype(jnp.int32)
    padded = ((counts + MOE_BLOCK - 1) // MOE_BLOCK) * MOE_BLOCK
    pend = jnp.cumsum(padded)
    pstart = pend - padded
    experts = jnp.arange(N_EXPERTS, dtype=jnp.int32)[:, None, None]
    dest = rank + jnp.sum(jnp.where(eid[None] == experts, pstart[:, None, None], 0), axis=0)
    n_assign = n_tokens * TOP_K
    n_slots = ((n_assign + MOE_BLOCK - 1) // MOE_BLOCK) * MOE_BLOCK + N_EXPERTS * MOE_BLOCK
    nb = n_slots // MOE_BLOCK
    block_start = jnp.arange(nb, dtype=jnp.int32) * MOE_BLOCK
    block_expert = jnp.sum((pend[None, :] <= block_start[:, None]).astype(jnp.int32), axis=1)
    block_expert = jnp.minimum(block_expert, N_EXPERTS - 1).astype(jnp.int32)
    n_used = (pend[-1:] // MOE_BLOCK).astype(jnp.int32)
    onehot = (block_expert[:, None] == jnp.arange(N_EXPERTS, dtype=jnp.int32)[None, :]).astype(jnp.int32)
    valid_rows = jnp.sum(onehot * (pstart + counts)[None, :], axis=1) - block_start
    part_rows = MOE_BLOCK // MOE_PARTS
    parts = jnp.clip((valid_rows + part_rows - 1) // part_rows, 1, MOE_PARTS).astype(jnp.int32)
    ids = jnp.arange(N_EXPERTS, dtype=jnp.int32)
    later_nonempty = (ids[None, :] > ids[:, None]) & (counts[None, :] > 0)
    next_expert = jnp.min(jnp.where(later_nonempty, ids[None, :], N_EXPERTS), axis=1)
    next_expert = jnp.where(next_expert == N_EXPERTS, ids, next_expert).astype(jnp.int32)
    return dest.astype(jnp.int32), block_expert, n_used, next_expert, parts, n_slots


def _in_stage(x, p):
    t = x.shape[0] * x.shape[1]
    return _in_proj_stage(x.reshape(t, D_MODEL), p["ln_in_g"], p["ln_in_b"], p["lb2"], p["w_in"])


def _mix_and_out_stage(x, in_outs, p):
    batch, seq_len, _ = x.shape
    t = batch * seq_len
    u, qv, lf, kk, og = in_outs
    zt = _fourier_mix(u, batch, seq_len, p["fourier_norm_g"])
    per_seq = lambda a: a.reshape(a.shape[0], batch, seq_len, HEAD_DIM)
    o = _gla(per_seq(qv), per_seq(lf), per_seq(kk))
    return _out_proj_stage(
        zt, o.reshape(HEADS, t, HEAD_DIM), og, x.reshape(t, D_MODEL), p["ln_in_g"], p["ln_in_b"], seq_len,
        p["norm_g6"], p["w_out"], p["ln1_g"], p["ln1_b"], p["rw_hi"], p["rw_lo"], p["rb_col"])


def _moe(x, out_outs, p):
    batch, seq_len, _ = x.shape
    t = batch * seq_len
    h, hpk, route, gates, counts = out_outs
    dest_kt, block_expert, n_used, next_expert, parts, n_slots = _moe_plan(route, counts, t)
    xb = _dispatch(dest_kt, hpk, n_slots)
    yb = _experts(block_expert, n_used, next_expert, parts, xb, p["w_gu"], p["b_gu"], p["w_dn"], p["b_dn"])
    y = None
    part = t // COMBINE_PARTS
    for j in range(COMBINE_PARTS):
        rows = _gather_rows(yb, dest_kt[:, j * part:(j + 1) * part])
        y = _combine(rows, gates, h, p["ln2_g"], p["ln2_b"], j * part, y)
    return y.reshape(batch, seq_len, D_MODEL)


def _prepare_params(ln_in_g, ln_in_b, w_in, fourier_norm_g, lb_gamma, hgrn_norm_g, w_out, ln1_g, ln1_b,
                    router_w, router_b, w_gate_up, b_gate_up, w_down, b_down, ln2_g, ln2_b):
    lb_all = jnp.cumsum(jax.nn.softmax(lb_gamma.astype(F32), axis=1), axis=1)
    rw = router_w[0].astype(F32).T
    rw_hi = rw.astype(BF16)
    rw_lo = (rw - rw_hi.astype(F32)).astype(BF16)
    return dict(
        ln_in_g=ln_in_g, ln_in_b=ln_in_b, w_in=w_in[0].astype(BF16),
        fourier_norm_g=fourier_norm_g[0],
        lb2=jnp.concatenate([lb_all[0, 0], lb_all[1, 0]]).reshape(1, -1),
        norm_g6=jnp.tile(hgrn_norm_g[0].astype(F32), HEADS).reshape(1, -1),
        w_out=w_out[0].astype(BF16), ln1_g=ln1_g[0], ln1_b=ln1_b[0],
        rw_hi=rw_hi, rw_lo=rw_lo, rb_col=router_b[0].astype(F32).reshape(-1, 1),
        w_gu=w_gate_up[0], b_gu=b_gate_up[0].reshape(N_EXPERTS, 1, -1),
        w_dn=w_down[0], b_dn=b_down[0].reshape(N_EXPERTS, 1, -1),
        ln2_g=ln2_g[0], ln2_b=ln2_b[0],
    )


def kernel(x_prompt, x_sample, ln_in_g, ln_in_b, w_in, fourier_norm_g, lb_gamma, hgrn_norm_g, w_out,
           ln1_g, ln1_b, router_w, router_b, w_gate_up, b_gate_up, w_down, b_down, ln2_g, ln2_b):
    p = _prepare_params(ln_in_g, ln_in_b, w_in, fourier_norm_g, lb_gamma, hgrn_norm_g, w_out, ln1_g, ln1_b,
                        router_w, router_b, w_gate_up, b_gate_up, w_down, b_down, ln2_g, ln2_b)
    first, second = x_sample, x_prompt
    (in_first,) = _run_stages([_in_stage(first, p)])
    out_stage_first = _mix_and_out_stage(first, in_first, p)
    in_stage_second = _in_stage(second, p)
    if in_stage_second.grid == out_stage_first.grid:
        in_second, out_first = _run_stages([in_stage_second, out_stage_first])
    else:
        (in_second,) = _run_stages([in_stage_second])
        (out_first,) = _run_stages([out_stage_first])
    (out_second,) = _run_stages([_mix_and_out_stage(second, in_second, p)])
    y_first = _moe(first, out_first, p)
    y_second = _moe(second, out_second, p)
    return (y_second, y_first)
```
